```python
import jax, jax.numpy as jnp
from jax import lax
import numpy as np

D_MODEL = 1024
BATCH = 8
SEQ = 2048
DEPTH = 1
DEC_BATCH = 128
DEC_SEQ = 4
PAST_LEN = 16384
PAGE_SIZE = 128

H_A = 4
D_A = D_MODEL
DK_A = D_A // H_A
MLSTM_CHUNK = 64
CONV_W = 4
D_B = D_MODEL
HEAD_B = 64
H_B = D_B // HEAD_B
LORA_W = 64
LORA_A = 64
LORA_G = 128
D_FF = 4 * D_MODEL
N_COND = 6 * D_MODEL
OFF_V_A = 2 * D_A
OFF_I = 3 * D_A
OFF_F = OFF_I + H_A
OFF_RWKV = OFF_F + H_A
N_RWKV = 3 * D_B + LORA_W + LORA_A + LORA_G
OFF_GATE = OFF_RWKV + N_RWKV
N_IN = OFF_GATE + 2 * D_MODEL
ALPHA = (2.0 * DEPTH) ** 0.25
BETA = (8.0 * DEPTH) ** -0.25
LN_EPS = 1e-5
MLSTM_NORM_EPS = 1e-6
RWKV_NORM_EPS = 64e-5

kernel_name = 'hybrid_mlstm_rwkv7_decoder_step'


def layer_norm(x, g, b):
    xf = x.astype(jnp.float32)
    mu = xf.mean(-1, keepdims=True)
    var = jnp.square(xf - mu).mean(-1, keepdims=True)
    return ((xf - mu) * lax.rsqrt(var + LN_EPS) * g + b).astype(x.dtype)


def head_norm(o, eps):
    mu = o.mean(-1, keepdims=True)
    var = jnp.square(o - mu).mean(-1, keepdims=True)
    return (o - mu) * lax.rsqrt(var + eps)


def mlstm_chunk_step(carry, inp):
    C, n, m = carry
    q, k, v, ig, lf = inp
    L = q.shape[2]
    causal = jnp.tril(jnp.ones((L, L), dtype=bool))
    b = jnp.cumsum(lf, axis=-1)
    g = b + m[..., None]
    dlog = jnp.where(causal, b[..., :, None] - b[..., None, :] + ig[..., None, :], -jnp.inf)
    m_t = jnp.maximum(g, dlog.max(-1))
    w_inter = jnp.exp(g - m_t)
    s = jnp.einsum('bhtk,bhsk->bhts', q, k) * jnp.exp(dlog - m_t[..., None])
    num = w_inter[..., None] * jnp.einsum('bhvk,bhtk->bhtv', C, q) + jnp.einsum('bhts,bhsv->bhtv', s, v)
    den = w_inter * jnp.einsum('bhk,bhtk->bht', n, q) + s.sum(-1)
    h = num / jnp.maximum(jnp.abs(den), jnp.exp(-m_t))[..., None]
    b_last = b[..., -1]
    wlog = b_last[..., None] - b + ig
    m_new = jnp.maximum(b_last + m, wlog.max(-1))
    decay = jnp.exp(b_last + m - m_new)
    wts = jnp.exp(wlog - m_new[..., None])
    C_new = decay[..., None, None] * C + jnp.einsum('bhs,bhsv,bhsk->bhvk', wts, v, k)
    n_new = decay[..., None] * n + jnp.einsum('bhs,bhsk->bhk', wts, k)
    return (C_new, n_new, m_new), h


def mlstm_scan(q, k, v, ig, lf, C0, n0, m0):
    B, T = q.shape[:2]
    L = MLSTM_CHUNK if T % MLSTM_CHUNK == 0 else T
    nC = T // L

    def to_chunks(a):
        a = a.reshape((B, nC, L) + a.shape[2:])
        return jnp.moveaxis(jnp.moveaxis(a, 1, 0), 3, 2)

    xs = tuple(to_chunks(a) for a in (q, k, v, ig, lf))
    (C, n, m), h = lax.scan(mlstm_chunk_step, (C0, n0, m0), xs)
    h = jnp.moveaxis(jnp.moveaxis(h, 2, 3), 0, 1).reshape(B, T, H_A, DK_A)
    return h, C, n, m


def rwkv_step(S, inp):
    r, w, k, v, kk, a = inp
    sa = jnp.einsum('bhvk,bhk->bhv', S, -kk)
    S = S * w[:, :, None, :] + sa[..., None] * (kk * a)[:, :, None, :] + v[..., None] * k[:, :, None, :]
    return S, jnp.einsum('bhvk,bhk->bhv', S, r)


def rwkv_scan(r, w, k, v, kk, a, S0):
    xs = tuple(jnp.moveaxis(t, 1, 0) for t in (r, w, k, v, kk, a))
    S, o = lax.scan(rwkv_step, S0, xs)
    return jnp.moveaxis(o, 0, 1), S


def token_mixer(h, st, p):
    C0, n0, m0, conv0, S0, shift0 = st
    B, T, _ = h.shape
    f32 = jnp.float32
    proj = h @ p['w_in']
    qk_pad = jnp.concatenate([conv0.astype(proj.dtype), proj[..., :OFF_V_A]], axis=1)
    qk = p['conv_b'] + sum(qk_pad[:, j:j + T] * p['conv_w'][j] for j in range(CONV_W))
    qk = jax.nn.silu(qk.astype(f32))
    q = qk[..., :D_A].reshape(B, T, H_A, DK_A)
    k = qk[..., D_A:].reshape(B, T, H_A, DK_A) * DK_A ** -0.5
    v = proj[..., OFF_V_A:OFF_I].astype(f32).reshape(B, T, H_A, DK_A)
    ig = (proj[..., OFF_I:OFF_F] + p['mlstm_i_bias']).astype(f32)
    lf = jax.nn.log_sigmoid((proj[..., OFF_F:OFF_RWKV] + p['mlstm_f_bias']).astype(f32))
    h_a, C1, n1, m1 = mlstm_scan(q, k, v, ig, lf, C0.astype(f32), n0.astype(f32), m0.astype(f32))
    gate_a = jax.nn.sigmoid(proj[..., OFF_GATE:OFF_GATE + D_MODEL].astype(f32))
    y_a = gate_a * head_norm(h_a, MLSTM_NORM_EPS).reshape(B, T, D_A) * p['mlstm_norm_w']
    pr = proj[..., OFF_RWKV:OFF_GATE]
    prev_row = shift0.astype(h.dtype) @ p['w_in'][:, OFF_RWKV:OFF_GATE]
    pr_prev = jnp.concatenate([prev_row[:, None], pr[:, :-1]], axis=1)
    xs = pr + (pr_prev - pr) * p['rwkv_mu']
    o1 = 3 * D_B
    o2 = o1 + LORA_W
    o3 = o2 + LORA_A
    r = xs[..., :D_B]
    kr = xs[..., D_B:2 * D_B]
    vr = xs[..., 2 * D_B:o1]
    w_log = -jax.nn.softplus(-(p['rwkv_w0'] + jnp.tanh(xs[..., o1:o2]) @ p['rwkv_w2']).astype(f32)) - 0.5
    decay = jnp.exp(-jnp.exp(w_log))
    a = jax.nn.sigmoid((p['rwkv_a0'] + xs[..., o2:o3] @ p['rwkv_a2']).astype(f32))
    g = jax.nn.sigmoid(xs[..., o3:]) @ p['rwkv_g2']
    kk = kr * p['rwkv_k_k']
    kmod = kr * (1.0 + (a - 1.0) * p['rwkv_k_a'])

    def split(t):
        return t.astype(f32).reshape(B, T, H_B, HEAD_B)

    rh, kh, vh, wh, ah, kkh = split(r), split(kmod), split(vr), split(decay), split(a), split(kk)
    kkh = kkh / jnp.maximum(jnp.linalg.norm(kkh, axis=-1, keepdims=True), 1e-12)
    o, S1 = rwkv_scan(rh, wh, kh, vh, kkh, ah, S0.astype(f32))
    o = head_norm(o, RWKV_NORM_EPS).reshape(B, T, D_B) * p['rwkv_lnx_w'] + p['rwkv_lnx_b']
    o = o + ((rh * kh * p['rwkv_r_k']).sum(-1, keepdims=True) * vh).reshape(B, T, D_B)
    y_b = o * g
    gate_b = jax.nn.sigmoid(proj[..., OFF_GATE + D_MODEL:].astype(f32))
    u = y_a + gate_b * y_b
    y = u.astype(h.dtype) @ p['w_out']
    new_st = (C1.astype(C0.dtype), n1.astype(n0.dtype), m1.astype(m0.dtype),
              qk_pad[:, T:].astype(conv0.dtype), S1.astype(S0.dtype), h[:, -1].astype(shift0.dtype))
    return y, new_st


def decoder_layer(x, c, st, p):
    mod = jax.nn.silu(c) @ p['w_cond'] + p['b_cond']
    sh1, sc1, g1, sh2, sc2, g2 = [t[:, None, :] for t in jnp.split(mod, 6, axis=-1)]
    y_mix, new_st = token_mixer(x * (1.0 + sc1) + sh1, st, p)
    x = layer_norm(ALPHA * x + g1 * y_mix, p['ln1_g'], p['ln1_b'])
    h2 = x * (1.0 + sc2) + sh2
    y_ff = jnp.square(jax.nn.relu(h2 @ p['w_up'])) @ p['w_down']
    x = layer_norm(ALPHA * x + g2 * y_ff, p['ln2_g'], p['ln2_b'])
    return x, new_st


def setup_inputs(seed: int = 0) -> dict:
    key = jax.random.key(seed)
    ks = jax.random.split(key, 40)
    f32 = jnp.float32

    def nrm(i, shape, s):
        return jax.random.normal(ks[i], shape, f32) * s

    L = DEPTH
    return {
        'x_prompt': nrm(0, (BATCH, SEQ, D_MODEL), 1.0),
        'x_sample': nrm(1, (DEC_BATCH, DEC_SEQ, D_MODEL), 1.0),
        'c_prompt': nrm(2, (BATCH, D_MODEL), 1.0),
        'c_sample': nrm(3, (DEC_BATCH, D_MODEL), 1.0),
        'state_mlstm_C': nrm(4, (L, DEC_BATCH, H_A, DK_A, DK_A), 0.05),
        'state_mlstm_n': nrm(5, (L, DEC_BATCH, H_A, DK_A), 0.1),
        'state_mlstm_m': nrm(6, (L, DEC_BATCH, H_A), 0.5),
        'state_mlstm_conv': nrm(7, (L, DEC_BATCH, CONV_W - 1, 2 * D_A), 1.0),
        'state_rwkv_S': nrm(8, (L, DEC_BATCH, H_B, HEAD_B, HEAD_B), 0.1),
        'state_rwkv_shift': nrm(9, (L, DEC_BATCH, D_MODEL), 1.0),
        'w_cond': nrm(10, (L, D_MODEL, N_COND), D_MODEL ** -0.5),
        'b_cond': nrm(11, (L, N_COND), 0.02),
        'w_in': nrm(12, (L, D_MODEL, N_IN), D_MODEL ** -0.5),
        'mlstm_i_bias': nrm(13, (L, H_A), 0.1),
        'mlstm_f_bias': jnp.linspace(3.0, 6.0, H_A, dtype=f32)[None] + nrm(14, (L, H_A), 0.1),
        'conv_w': nrm(15, (L, CONV_W, 2 * D_A), CONV_W ** -0.5),
        'conv_b': nrm(16, (L, 2 * D_A), 0.02),
        'mlstm_norm_w': 1.0 + nrm(17, (L, D_A), 0.02),
        'rwkv_mu': jax.random.uniform(ks[18], (L, N_RWKV), f32),
        'rwkv_w0': jax.random.uniform(ks[19], (L, D_B), f32, -6.0, -1.0),
        'rwkv_w2': nrm(20, (L, LORA_W, D_B), 0.1 * LORA_W ** -0.5),
        'rwkv_a0': nrm(21, (L, D_B), 0.1),
        'rwkv_a2': nrm(22, (L, LORA_A, D_B), LORA_A ** -0.5),
        'rwkv_g2': nrm(23, (L, LORA_G, D_B), LORA_G ** -0.5),
        'rwkv_k_k': 0.85 + nrm(24, (L, D_B), 0.02),
        'rwkv_k_a': 1.0 + nrm(25, (L, D_B), 0.02),
        'rwkv_r_k': nrm(26, (L, H_B, HEAD_B), 0.1),
        'rwkv_lnx_w': 1.0 + nrm(27, (L, D_B), 0.02),
        'rwkv_lnx_b': nrm(28, (L, D_B), 0.02),
        'w_out': nrm(29, (L, D_MODEL, D_MODEL), BETA * D_MODEL ** -0.5),
        'ln1_g': 1.0 + nrm(30, (L, D_MODEL), 0.02),
        'ln1_b': nrm(31, (L, D_MODEL), 0.02),
        'w_up': nrm(32, (L, D_MODEL, D_FF), D_MODEL ** -0.5),
        'w_down': nrm(33, (L, D_FF, D_MODEL), BETA * D_FF ** -0.5),
        'ln2_g': 1.0 + nrm(34, (L, D_MODEL), 0.02),
        'ln2_b': nrm(35, (L, D_MODEL), 0.02),
    }


def reference(x_prompt, x_sample, c_prompt, c_sample, state_mlstm_C, state_mlstm_n, state_mlstm_m,
              state_mlstm_conv, state_rwkv_S, state_rwkv_shift, w_cond, b_cond, w_in, mlstm_i_bias,
              mlstm_f_bias, conv_w, conv_b, mlstm_norm_w, rwkv_mu, rwkv_w0, rwkv_w2, rwkv_a0, rwkv_a2,
              rwkv_g2, rwkv_k_k, rwkv_k_a, rwkv_r_k, rwkv_lnx_w, rwkv_lnx_b, w_out, ln1_g, ln1_b, w_up,
              w_down, ln2_g, ln2_b):
    bp = x_prompt.shape[0]
    dt = x_prompt.dtype
    zero_st = (jnp.zeros((bp, H_A, DK_A, DK_A), dt), jnp.zeros((bp, H_A, DK_A), dt),
               jnp.zeros((bp, H_A), dt), jnp.zeros((bp, CONV_W - 1, 2 * D_A), dt),
               jnp.zeros((bp, H_B, HEAD_B, HEAD_B), dt), jnp.zeros((bp, D_MODEL), dt))
    yp = x_prompt
    ys = x_sample
    new_p = [[] for _ in range(6)]
    new_s = [[] for _ in range(6)]
    for l in range(DEPTH):
        p = {'w_cond': w_cond[l], 'b_cond': b_cond[l], 'w_in': w_in[l], 'mlstm_i_bias': mlstm_i_bias[l],
             'mlstm_f_bias': mlstm_f_bias[l], 'conv_w': conv_w[l], 'conv_b': conv_b[l],
             'mlstm_norm_w': mlstm_norm_w[l], 'rwkv_mu': rwkv_mu[l], 'rwkv_w0': rwkv_w0[l],
             'rwkv_w2': rwkv_w2[l], 'rwkv_a0': rwkv_a0[l], 'rwkv_a2': rwkv_a2[l], 'rwkv_g2': rwkv_g2[l],
             'rwkv_k_k': rwkv_k_k[l], 'rwkv_k_a': rwkv_k_a[l], 'rwkv_r_k': rwkv_r_k[l],
             'rwkv_lnx_w': rwkv_lnx_w[l], 'rwkv_lnx_b': rwkv_lnx_b[l], 'w_out': w_out[l],
             'ln1_g': ln1_g[l], 'ln1_b': ln1_b[l], 'w_up': w_up[l], 'w_down': w_down[l],
             'ln2_g': ln2_g[l], 'ln2_b': ln2_b[l]}
        yp, st_p = decoder_layer(yp, c_prompt, zero_st, p)
        st_in = (state_mlstm_C[l], state_mlstm_n[l], state_mlstm_m[l], state_mlstm_conv[l],
                 state_rwkv_S[l], state_rwkv_shift[l])
        ys, st_s = decoder_layer(ys, c_sample, st_in, p)
        for lst, t in zip(new_p, st_p):
            lst.append(t)
        for lst, t in zip(new_s, st_s):
            lst.append(t)
    C_p, n_p, m_p, conv_p, S_p, shift_p = [jnp.stack(t) for t in new_p]
    C_s, n_s, m_s, conv_s, S_s, shift_s = [jnp.stack(t) for t in new_s]
    return (yp, ys, C_p, n_p, m_p, conv_p, S_p, shift_p, C_s, n_s, m_s, conv_s, S_s, shift_s)
```

```python
import functools

import jax
import jax.numpy as jnp
from jax import lax
from jax.experimental import pallas as pl
from jax.experimental.pallas import tpu as pltpu

F32 = jnp.float32
BF16 = jnp.bfloat16
HIGHEST = lax.Precision.HIGHEST

D = 1024
H_A = 4
DK = 256
CONV_W = 4
H_B = 16
HB = 64
N_PAIR = H_B // 2
D_FF = 4096
N_COND = 6 * D
ALPHA = 2.0 ** 0.25
LN_EPS = 1e-5
MLSTM_EPS = 1e-6
RWKV_EPS = 64e-5

COL_LORA = 8 * D
COL_IF = COL_LORA + 256
NP = 8704
LORA = 256
TN_IN = 512

RW_L = 64
NEG = -1e30
VMEM_LIMIT = 56 * 1024 * 1024


def _cp(sem):
    return pltpu.CompilerParams(dimension_semantics=sem, vmem_limit_bytes=VMEM_LIMIT)


def _dot(a, b, prec=None):
    return jnp.dot(a, b, preferred_element_type=F32, precision=prec)


def _dot_nt(a, b, prec=None):
    return lax.dot_general(a, b, (((1,), (1,)), ((), ())), preferred_element_type=F32, precision=prec)


def _dot_tn(a, b, prec=None):
    return lax.dot_general(a, b, (((0,), (0,)), ((), ())), preferred_element_type=F32, precision=prec)


def _log_sigmoid(x):
    return jnp.minimum(x, 0.0) - jnp.log1p(jnp.exp(-jnp.abs(x)))


def _silu(x):
    return x * jax.nn.sigmoid(x)


def _layer_norm(z, g, b):
    mu = jnp.mean(z, axis=-1, keepdims=True)
    zc = z - mu
    var = jnp.mean(zc * zc, axis=-1, keepdims=True)
    return zc * lax.rsqrt(var + LN_EPS) * g + b


def _cond_kernel(c_ref, w_ref, b_ref, o_ref):
    s = _silu(c_ref[...]).astype(BF16)
    o_ref[...] = _dot(s, w_ref[...].astype(BF16)) + b_ref[...]


def _cond(c, w_cond, b_cond):
    n = c.shape[0]
    tn = 512
    return pl.pallas_call(
        _cond_kernel,
        grid=(N_COND // tn,),
        in_specs=[pl.BlockSpec((n, D), lambda j: (0, 0)),
                  pl.BlockSpec((D, tn), lambda j: (0, j)),
                  pl.BlockSpec((1, tn), lambda j: (0, j))],
        out_specs=pl.BlockSpec((n, tn), lambda j: (0, j)),
        out_shape=jax.ShapeDtypeStruct((n, N_COND), F32),
        compiler_params=_cp(("arbitrary",)),
        name="cond",
    )(c, w_cond, b_cond.reshape(1, N_COND))


def _inproj_kernel(x_ref, sh_ref, sc_ref, w_ref, o_ref, h_scr):
    bb, tt, _ = x_ref.shape

    @pl.when(pl.program_id(2) == 0)
    def _():
        h = x_ref[...] * (1.0 + sc_ref[...]) + sh_ref[...]
        h_scr[...] = h.reshape(bb * tt, D).astype(BF16)

    o_ref[...] = _dot(h_scr[...], w_ref[...]).reshape(bb, tt, TN_IN)


def _inproj(x3, mod3, w_in_r, bb, tt):
    b, tp, _ = x3.shape
    return pl.pallas_call(
        _inproj_kernel,
        grid=(b // bb, tp // tt, NP // TN_IN),
        in_specs=[pl.BlockSpec((bb, tt, D), lambda i, t, j: (i, t, 0)),
                  pl.BlockSpec((bb, 1, D), lambda i, t, j: (i, 0, 0)),
                  pl.BlockSpec((bb, 1, D), lambda i, t, j: (i, 0, 1)),
                  pl.BlockSpec((D, TN_IN), lambda i, t, j: (0, j))],
        out_specs=pl.BlockSpec((bb, tt, TN_IN), lambda i, t, j: (i, t, j)),
        out_shape=jax.ShapeDtypeStruct((b, tp, NP), F32),
        scratch_shapes=[pltpu.VMEM((bb * tt, D), BF16)],
        compiler_params=_cp(("parallel", "parallel", "arbitrary")),
        name="inproj",
    )(x3, mod3, mod3, w_in_r)


def _modulate_kernel(x_ref, sh_ref, sc_ref, o_ref):
    o_ref[...] = x_ref[...] * (1.0 + sc_ref[...]) + sh_ref[...]


def _modulate_rows(x2, mod2):
    n = x2.shape[0]
    return pl.pallas_call(
        _modulate_kernel,
        grid=(1,),
        in_specs=[pl.BlockSpec((n, D), lambda i: (0, 0)),
                  pl.BlockSpec((n, D), lambda i: (0, 0)),
                  pl.BlockSpec((n, D), lambda i: (0, 1))],
        out_specs=pl.BlockSpec((n, D), lambda i: (0, 0)),
        out_shape=jax.ShapeDtypeStruct((n, D), F32),
        name="modulate_last",
    )(x2, mod2, mod2)


def _conv4(pad_ref, n_rows, cw, cb):
    acc = cb + pad_ref[8:8 + n_rows, :] * cw[3:4, :]
    acc = acc + pad_ref[7:7 + n_rows, :] * cw[2:3, :]
    acc = acc + pad_ref[6:6 + n_rows, :] * cw[1:2, :]
    acc = acc + pad_ref[5:5 + n_rows, :] * cw[0:1, :]
    return acc


def _head_norm_rows(h, eps):
    mu = jnp.mean(h, axis=-1, keepdims=True)
    hc = h - mu
    var = jnp.mean(hc * hc, axis=-1, keepdims=True)
    return hc * lax.rsqrt(var + eps)


def _mlstm_seq_kernel(qp_ref, kp_ref, v_ref, ga_ref, if_ref, cw_ref, cb_ref, gb_ref, nw_ref,
                      ya_ref, c_ref, n_ref, m_ref, padq, padk):
    L = qp_ref.shape[1]

    @pl.when(pl.program_id(1) == 0)
    def _():
        c_ref[...] = jnp.zeros_like(c_ref)
        n_ref[...] = jnp.zeros_like(n_ref)
        m_ref[...] = jnp.zeros_like(m_ref)
        padq[0:8, :] = jnp.zeros((8, D), F32)
        padk[0:8, :] = jnp.zeros((8, D), F32)

    padq[8:8 + L, :] = qp_ref[0]
    padk[8:8 + L, :] = kp_ref[0]
    cw = cw_ref[...]
    cb = cb_ref[...]
    q = _silu(_conv4(padq, L, cw[:, 0:D], cb[:, 0:D]))
    k = _silu(_conv4(padk, L, cw[:, D:2 * D], cb[:, D:2 * D])) * (DK ** -0.5)
    padq[0:8, :] = padq[L:L + 8, :]
    padk[0:8, :] = padk[L:L + 8, :]

    gpre = if_ref[0] + gb_ref[...]
    lsg = _log_sigmoid(gpre)
    row = lax.broadcasted_iota(jnp.int32, (L, L), 0)
    col = lax.broadcasted_iota(jnp.int32, (L, L), 1)
    causal = col <= row
    tril = jnp.where(causal, 1.0, 0.0).astype(F32)
    bcum = _dot(tril, lsg, HIGHEST)
    gpre_t = gpre.T
    bcum_t = bcum.T

    v_all = v_ref[0]
    ga = jax.nn.sigmoid(ga_ref[0])
    nw = nw_ref[...]
    m_all = m_ref[0]
    for h in range(H_A):
        sl = slice(h * DK, (h + 1) * DK)
        ig_col = gpre[:, h:h + 1]
        b_col = bcum[:, H_A + h:H_A + h + 1]
        ig_row = gpre_t[h:h + 1, :]
        b_row = bcum_t[H_A + h:H_A + h + 1, :]
        m_prev = m_all[:, h:h + 1]
        g_col = b_col + m_prev
        dlog = jnp.where(causal, b_col - b_row + ig_row, NEG)
        m_t = jnp.maximum(g_col, jnp.max(dlog, axis=1, keepdims=True))
        w_inter = jnp.exp(g_col - m_t)
        p = jnp.exp(dlog - m_t)
        qh = q[:, sl]
        kh = k[:, sl]
        vh = v_all[:, sl]
        qb = qh.astype(BF16)
        kb = kh.astype(BF16)
        s = _dot_nt(qb, kb) * p
        ch = c_ref[0, h]
        nh = n_ref[0, h:h + 1, :]
        num = w_inter * _dot_nt(qb, ch.astype(BF16)) + _dot(s.astype(BF16), vh.astype(BF16))
        den = w_inter * jnp.sum(qh * nh, axis=1, keepdims=True) + jnp.sum(s, axis=1, keepdims=True)
        hh = num / jnp.maximum(jnp.abs(den), jnp.exp(-m_t))
        b_last = b_col[L - 1:L, :]
        wlog = b_last - b_col + ig_col
        m_new = jnp.maximum(b_last + m_prev, jnp.max(wlog, axis=0, keepdims=True))
        decay = jnp.exp(b_last + m_prev - m_new)
        wts = jnp.exp(wlog - m_new)
        c_ref[0, h] = decay * ch + _dot_tn((wts * vh).astype(BF16), kb)
        n_ref[0, h:h + 1, :] = decay * nh + jnp.sum(wts * kh, axis=0, keepdims=True)
        m_ref[0, :, h:h + 1] = m_new
        ya_ref[0, :, sl] = ga[:, sl] * _head_norm_rows(hh, MLSTM_EPS) * nw[:, sl]


def _mlstm_seq(proj3, conv_w, conv_b, gbias, norm_w, L):
    b, tp, _ = proj3.shape
    blk = lambda j: pl.BlockSpec((1, L, D), lambda i, c, j=j: (i, c, j))
    full = lambda shp: pl.BlockSpec(shp, lambda i, c: (0,) * len(shp))
    return pl.pallas_call(
        _mlstm_seq_kernel,
        grid=(b, tp // L),
        in_specs=[blk(0), blk(1), blk(2), blk(6),
                  pl.BlockSpec((1, L, 128), lambda i, c: (i, c, COL_IF // 128)),
                  full((CONV_W, 2 * D)), full((1, 2 * D)), full((1, 128)), full((1, D))],
        out_specs=[pl.BlockSpec((1, L, D), lambda i, c: (i, c, 0)),
                   pl.BlockSpec((1, H_A, DK, DK), lambda i, c: (i, 0, 0, 0)),
                   pl.BlockSpec((1, 8, DK), lambda i, c: (i, 0, 0)),
                   pl.BlockSpec((1, 1, 128), lambda i, c: (i, 0, 0))],
        out_shape=[jax.ShapeDtypeStruct((b, tp, D), F32),
                   jax.ShapeDtypeStruct((b, H_A, DK, DK), F32),
                   jax.ShapeDtypeStruct((b, 8, DK), F32),
                   jax.ShapeDtypeStruct((b, 1, 128), F32)],
        scratch_shapes=[pltpu.VMEM((L + 8, D), F32), pltpu.VMEM((L + 8, D), F32)],
        compiler_params=_cp(("parallel", "arbitrary")),
        name="mlstm_seq",
    )(proj3, proj3, proj3, proj3, proj3, conv_w, conv_b, gbias, norm_w)


def _mlstm_step_kernel(tv, qp_ref, kp_ref, v_ref, ga_ref, if_ref, conv0_ref, c0_ref, n0_ref, m0_ref,
                       cw_ref, cb_ref, gb_ref, nw_ref,
                       ya_ref, c_ref, n_ref, m_ref, padq, padk, wv_pad, k_pad):
    @pl.when(pl.program_id(0) == 0)
    def _():
        wv_pad[...] = jnp.zeros_like(wv_pad)
        k_pad[...] = jnp.zeros_like(k_pad)

    padq[0:8, :] = conv0_ref[0, :, 0:D]
    padk[0:8, :] = conv0_ref[0, :, D:2 * D]
    padq[8:16, :] = qp_ref[0]
    padk[8:16, :] = kp_ref[0]
    cw = cw_ref[...]
    cb = cb_ref[...]
    q = _silu(_conv4(padq, 8, cw[:, 0:D], cb[:, 0:D]))
    k = _silu(_conv4(padk, 8, cw[:, D:2 * D], cb[:, D:2 * D])) * (DK ** -0.5)

    gpre = if_ref[0] + gb_ref[...]
    lsg = _log_sigmoid(gpre)
    v_all = v_ref[0]
    ga = jax.nn.sigmoid(ga_ref[0])
    nw = nw_ref[...]
    m_all = m0_ref[0]
    ya_ref[...] = jnp.zeros_like(ya_ref)
    n_ref[...] = jnp.zeros_like(n_ref)
    m_ref[...] = jnp.zeros_like(m_ref)
    for h in range(H_A):
        sl = slice(h * DK, (h + 1) * DK)
        ig = [gpre[t:t + 1, h:h + 1] for t in range(tv)]
        lf = [lsg[t:t + 1, H_A + h:H_A + h + 1] for t in range(tv)]
        b = [lf[0]]
        for t in range(1, tv):
            b.append(b[t - 1] + lf[t])
        m_prev = m_all[:, h:h + 1]
        qh = q[:, sl]
        kh = k[:, sl]
        vh = v_all[:, sl]
        ch = c0_ref[0, h]
        nh = n0_ref[0, h:h + 1, :]
        qc = _dot_nt(qh.astype(BF16), ch.astype(BF16))
        for t in range(tv):
            g_t = b[t] + m_prev
            dl = [b[t] - b[s] + ig[s] for s in range(t + 1)]
            m_t = g_t
            for s in range(t + 1):
                m_t = jnp.maximum(m_t, dl[s])
            w_inter = jnp.exp(g_t - m_t)
            q_t = qh[t:t + 1, :]
            num = w_inter * qc[t:t + 1, :]
            den = w_inter * jnp.sum(q_t * nh, axis=1, keepdims=True)
            for s in range(t + 1):
                sc = jnp.sum(q_t * kh[s:s + 1, :], axis=1, keepdims=True) * jnp.exp(dl[s] - m_t)
                num = num + sc * vh[s:s + 1, :]
                den = den + sc
            hh = num / jnp.maximum(jnp.abs(den), jnp.exp(-m_t))
            ya_ref[0, t:t + 1, sl] = ga[t:t + 1, sl] * _head_norm_rows(hh, MLSTM_EPS) * nw[:, sl]
        b_last = b[tv - 1]
        wlog = [b_last - b[s] + ig[s] for s in range(tv)]
        m_new = b_last + m_prev
        for s in range(tv):
            m_new = jnp.maximum(m_new, wlog[s])
        decay = jnp.exp(b_last + m_prev - m_new)
        n_new = decay * nh
        for s in range(tv):
            wts = jnp.exp(wlog[s] - m_new)
            wv_pad[s:s + 1, :] = wts * vh[s:s + 1, :]
            n_new = n_new + wts * kh[s:s + 1, :]
        k_pad[0:8, :] = kh
        upd = _dot(wv_pad[...].T.astype(BF16), k_pad[...].astype(BF16))
        c_ref[0, h] = decay * ch + upd
        n_ref[0, h:h + 1, :] = n_new
        m_ref[0, :, h:h + 1] = m_new


def _mlstm_step(proj3, conv0p, c0, n0p, m0p, conv_w, conv_b, gbias, norm_w, tv):
    b = proj3.shape[0]
    blk = lambda j: pl.BlockSpec((1, 8, D), lambda i, j=j: (i, 0, j))
    full = lambda shp: pl.BlockSpec(shp, lambda i: (0,) * len(shp))
    return pl.pallas_call(
        functools.partial(_mlstm_step_kernel, tv),
        grid=(b,),
        in_specs=[blk(0), blk(1), blk(2), blk(6),
                  pl.BlockSpec((1, 8, 128), lambda i: (i, 0, COL_IF // 128)),
                  pl.BlockSpec((1, 8, 2 * D), lambda i: (i, 0, 0)),
                  pl.BlockSpec((1, H_A, DK, DK), lambda i: (i, 0, 0, 0)),
                  pl.BlockSpec((1, 8, DK), lambda i: (i, 0, 0)),
                  pl.BlockSpec((1, 1, 128), lambda i: (i, 0, 0)),
                  full((CONV_W, 2 * D)), full((1, 2 * D)), full((1, 128)), full((1, D))],
        out_specs=[pl.BlockSpec((1, 8, D), lambda i: (i, 0, 0)),
                   pl.BlockSpec((1, H_A, DK, DK), lambda i: (i, 0, 0, 0)),
                   pl.BlockSpec((1, 8, DK), lambda i: (i, 0, 0)),
                   pl.BlockSpec((1, 1, 128), lambda i: (i, 0, 0))],
        out_shape=[jax.ShapeDtypeStruct((b, 8, D), F32),
                   jax.ShapeDtypeStruct((b, H_A, DK, DK), F32),
                   jax.ShapeDtypeStruct((b, 8, DK), F32),
                   jax.ShapeDtypeStruct((b, 1, 128), F32)],
        scratch_shapes=[pltpu.VMEM((16, D), F32), pltpu.VMEM((16, D), F32),
                        pltpu.VMEM((128, DK), F32), pltpu.VMEM((128, DK), F32)],
        compiler_params=_cp(("arbitrary",)),
        name="mlstm_step",
    )(proj3, proj3, proj3, proj3, proj3, conv0p, c0, n0p, m0p, conv_w, conv_b, gbias, norm_w)


def _bd(x, lo):
    return jnp.concatenate([jnp.where(lo, x, 0.0), jnp.where(lo, 0.0, x)], axis=0)


def _pair_sum(x, lo):
    s_lo = jnp.sum(jnp.where(lo, x, 0.0), axis=1, keepdims=True)
    s_hi = jnp.sum(jnp.where(lo, 0.0, x), axis=1, keepdims=True)
    return jnp.where(lo, s_lo, s_hi)


def _rwkv_kernel(nbg, lb, tv,
                 r_ref, k_ref, v_ref, l_ref, gb_ref, pr_ref, pk_ref, pv_ref, pl_ref, s0_ref,
                 mur_ref, muk_ref, muv_ref, mul_ref, w0_ref, a0_ref, kk_ref, ka_ref, rk_ref,
                 lw_ref, lb_ref, w2_ref, a2_ref, g2_ref,
                 y_ref, s_ref, cr, ck, cv, cl):
    L = nbg * lb

    @pl.when(pl.program_id(1) == 0)
    def _():
        s_ref[...] = s0_ref[...]
        cr[...] = pr_ref[...]
        ck[...] = pk_ref[...]
        cv[...] = pv_ref[...]
        cl[...] = pl_ref[...]

    def shift_mix(x_ref, carry, mu_ref):
        x3 = x_ref[...]
        width = x3.shape[-1]
        tpos = lax.broadcasted_iota(jnp.int32, x3.shape, 1)
        prev = jnp.where(tpos == 0, carry[...], pltpu.roll(x3, 1, 1))
        carry[...] = x3[:, lb - 1:lb, :]
        return (x3 + (prev - x3) * mu_ref[...]).reshape(L, width)

    xr = shift_mix(r_ref, cr, mur_ref)
    xk = shift_mix(k_ref, ck, muk_ref)
    xv = shift_mix(v_ref, cv, muv_ref)
    xl = shift_mix(l_ref, cl, mul_ref)

    lane_l = lax.broadcasted_iota(jnp.int32, (L, LORA), 1)
    act = jnp.where(lane_l < 64, jnp.tanh(xl), jnp.where(lane_l < 128, xl, jax.nn.sigmoid(xl))).astype(BF16)
    z = w0_ref[...] + _dot(act, w2_ref[...])
    w_log = -(jnp.maximum(-z, 0.0) + jnp.log1p(jnp.exp(-jnp.abs(z)))) - 0.5
    lw = -jnp.exp(w_log)
    a = jax.nn.sigmoid(a0_ref[...] + _dot(act, a2_ref[...]))
    g = _dot(act, g2_ref[...])
    kk = xk * kk_ref[...]
    kmod = xk * (1.0 + (a - 1.0) * ka_ref[...])
    gate_b = jax.nn.sigmoid(gb_ref[...].reshape(L, D))

    t_idx = lax.broadcasted_iota(jnp.int32, (L, 1), 0)
    if tv < lb:
        valid = (t_idx % lb) < tv
        lw = jnp.where(valid, lw, 0.0)
        kk = jnp.where(valid, kk, 0.0)
        kmod = jnp.where(valid, kmod, 0.0)
        xv = jnp.where(valid, xv, 0.0)

    row = lax.broadcasted_iota(jnp.int32, (L, L), 0)
    col = lax.broadcasted_iota(jnp.int32, (L, L), 1)
    tril = jnp.where((col <= row) & (col // lb == row // lb), 1.0, 0.0).astype(F32)
    cum = _dot(tril, lw, HIGHEST)

    lane = lax.broadcasted_iota(jnp.int32, (L, 128), 1)
    lo = lane < HB
    src = lane % HB
    trow = lax.broadcasted_iota(jnp.int32, (L, 128), 0)
    same = (src // lb) == (trow // lb)
    strict = same & (src < trow)
    incl = same & (src <= trow)
    r128 = lax.broadcasted_iota(jnp.int32, (128, 128), 0)
    c128 = lax.broadcasted_iota(jnp.int32, (128, 128), 1)
    blockdiag = (r128 < HB) == (c128 < HB)
    eye_pair = jnp.where(src == trow, 1.0, 0.0).astype(F32)

    for p in range(N_PAIR):
        sl = slice(p * 128, (p + 1) * 128)
        kkp = kk[:, sl]
        nrm = jnp.sqrt(_pair_sum(kkp * kkp, lo))
        kap = kkp / jnp.maximum(nrm, 1e-12)
        cum_p = cum[:, sl]
        lw_p = lw[:, sl]
        w_in = jnp.exp(cum_p)
        w_ex = jnp.exp(cum_p - lw_p)
        w_inv = jnp.exp(-cum_p)
        rp = xr[:, sl]
        kmp = kmod[:, sl]
        vp = xv[:, sl]
        at = -kap * w_ex
        rt = rp * w_in
        bt = kap * a[:, sl] * w_inv
        kt = kmp * w_inv
        x_b = jnp.concatenate([at, rt], axis=0).astype(BF16)
        y_b = jnp.concatenate([_bd(bt, lo), _bd(kt, lo)], axis=0).astype(BF16)
        gm = _dot_nt(x_b, y_b)
        n_m = jnp.where(strict, gm[0:L, 0:128], 0.0)
        a_ak = jnp.where(strict, gm[0:L, 128:256], 0.0)
        a_rb = jnp.where(incl, gm[L:2 * L, 0:128], 0.0)
        a_rk = jnp.where(incl, gm[L:2 * L, 128:256], 0.0)

        as_parts, rs_parts = [], []
        for gi in range(nbg):
            rs_ = slice(gi * lb, (gi + 1) * lb)
            xg = jnp.concatenate([at[rs_], rt[rs_]], axis=0).astype(BF16)
            xs = _dot_nt(xg, s_ref[gi, p].astype(BF16))
            as_parts.append(xs[0:lb])
            rs_parts.append(xs[lb:2 * lb])
        a_s = as_parts[0] if nbg == 1 else jnp.concatenate(as_parts, axis=0)
        r_s = rs_parts[0] if nbg == 1 else jnp.concatenate(rs_parts, axis=0)

        bdv = _bd(vp, lo).astype(BF16)
        y0 = a_s + _dot(a_ak.astype(BF16), bdv)

        dm = eye_pair
        s_blk = 1
        while 2 * s_blk <= lb:
            lvl = ((trow // (2 * s_blk)) == (src // (2 * s_blk))) & ((trow % (2 * s_blk)) >= s_blk) \
                & ((src % (2 * s_blk)) < s_blk)
            m_s = jnp.where(lvl, n_m, 0.0)
            if s_blk == 1:
                dm = dm + m_s
            else:
                t1 = _dot(m_s.astype(BF16), _bd(dm, lo).astype(BF16))
                dm = dm + _dot(dm.astype(BF16), _bd(t1, lo).astype(BF16))
            s_blk *= 2
        u = _dot(dm.astype(BF16), _bd(y0, lo).astype(BF16))

        bdu = _bd(u, lo).astype(BF16)
        o = r_s + _dot(jnp.concatenate([a_rb, a_rk], axis=1).astype(BF16),
                       jnp.concatenate([bdu, bdv], axis=0))

        w3 = w_in.reshape(nbg, lb, 128)[:, lb - 1:lb, :]
        w_last = jnp.broadcast_to(w3, (nbg, lb, 128)).reshape(L, 128)
        rhs = jnp.concatenate([bt * w_last, kt * w_last], axis=0).astype(BF16)
        uv = jnp.concatenate([u, vp], axis=0)
        if nbg == 1:
            upd = _dot_tn(uv.astype(BF16), rhs)
            s_ref[0, p] = s_ref[0, p] * w3[0] + jnp.where(blockdiag, upd, 0.0)
        else:
            uvt = uv.T
            cgrp = (c128 % L) // lb
            for gi in range(nbg):
                lhs = jnp.where(cgrp == gi, uvt, 0.0).astype(BF16)
                upd = _dot(lhs, rhs)
                s_ref[gi, p] = s_ref[gi, p] * w3[gi] + jnp.where(blockdiag, upd, 0.0)

        mu = _pair_sum(o, lo) * (1.0 / HB)
        oc = o - mu
        var = _pair_sum(oc * oc, lo) * (1.0 / HB)
        on = oc * lax.rsqrt(var + RWKV_EPS) * lw_ref[:, sl] + lb_ref[:, sl]
        bonus = _pair_sum(rp * kmp * rk_ref[:, sl], lo) * vp
        yb = (on + bonus) * g[:, sl]
        y_ref[:, :, sl] = (gate_b[:, sl] * yb).reshape(nbg, lb, 128)


def _rwkv(proj3, prev3, s0bd, prm, nbg, lb, tv):
    b, tp, _ = proj3.shape
    blk = lambda j: pl.BlockSpec((nbg, lb, D), lambda i, c, j=j: (i, c, j))
    pblk = lambda j: pl.BlockSpec((nbg, 1, D), lambda i, c, j=j: (i, 0, j))
    full = lambda a: pl.BlockSpec(a.shape, lambda i, c: (0,) * a.ndim)
    sblk = pl.BlockSpec((nbg, N_PAIR, 128, 128), lambda i, c: (i, 0, 0, 0))
    return pl.pallas_call(
        functools.partial(_rwkv_kernel, nbg, lb, tv),
        grid=(b // nbg, tp // lb),
        in_specs=[blk(3), blk(4), blk(5),
                  pl.BlockSpec((nbg, lb, LORA), lambda i, c: (i, c, COL_LORA // LORA)),
                  blk(7),
                  pblk(3), pblk(4), pblk(5),
                  pl.BlockSpec((nbg, 1, LORA), lambda i, c: (i, 0, COL_LORA // LORA)),
                  sblk] + [full(a) for a in prm],
        out_specs=[pl.BlockSpec((nbg, lb, D), lambda i, c: (i, c, 0)), sblk],
        out_shape=[jax.ShapeDtypeStruct((b, tp, D), F32),
                   jax.ShapeDtypeStruct((b, N_PAIR, 128, 128), F32)],
        scratch_shapes=[pltpu.VMEM((nbg, 1, D), F32), pltpu.VMEM((nbg, 1, D), F32),
                        pltpu.VMEM((nbg, 1, D), F32), pltpu.VMEM((nbg, 1, LORA), F32)],
        compiler_params=_cp(("parallel", "arbitrary")),
        name="rwkv",
    )(proj3, proj3, proj3, proj3, proj3, prev3, prev3, prev3, prev3, s0bd, *prm)


def _outproj_kernel(ya_ref, yb_ref, x_ref, g1_ref, w_ref, lg_ref, lb_ref, o_ref):
    bb, tt, _ = x_ref.shape
    u = (ya_ref[...] + yb_ref[...]).reshape(bb * tt, D).astype(BF16)
    y = _dot(u, w_ref[...]).reshape(bb, tt, D)
    z = ALPHA * x_ref[...] + g1_ref[...] * y
    o_ref[...] = _layer_norm(z, lg_ref[...], lb_ref[...])


def _outproj(ya3, yb3, x3, mod3, w_out, ln_g, ln_b, bb, tt):
    b, tp, _ = x3.shape
    blk = pl.BlockSpec((bb, tt, D), lambda i, t: (i, t, 0))
    full = lambda shp: pl.BlockSpec(shp, lambda i, t: (0,) * len(shp))
    return pl.pallas_call(
        _outproj_kernel,
        grid=(b // bb, tp // tt),
        in_specs=[blk, blk, blk, pl.BlockSpec((bb, 1, D), lambda i, t: (i, 0, 2)),
                  full((D, D)), full((1, D)), full((1, D))],
        out_specs=blk,
        out_shape=jax.ShapeDtypeStruct((b, tp, D), F32),
        compiler_params=_cp(("parallel", "parallel")),
        name="outproj_ln",
    )(ya3, yb3, x3, mod3, w_out, ln_g, ln_b)


FF_CHUNK = 1024


def _ffn_kernel(x_ref, sh_ref, sc_ref, g2_ref, wu_ref, wd_ref, lg_ref, lb_ref, o_ref, h_scr, acc):
    bb, tt, _ = x_ref.shape
    j = pl.program_id(2)

    @pl.when(j == 0)
    def _():
        h = x_ref[...] * (1.0 + sc_ref[...]) + sh_ref[...]
        h_scr[...] = h.reshape(bb * tt, D).astype(BF16)
        acc[...] = jnp.zeros_like(acc)

    up = jnp.maximum(_dot(h_scr[...], wu_ref[...]), 0.0)
    acc[...] += _dot((up * up).astype(BF16), wd_ref[...])

    @pl.when(j == pl.num_programs(2) - 1)
    def _():
        z = ALPHA * x_ref[...] + g2_ref[...] * acc[...].reshape(bb, tt, D)
        o_ref[...] = _layer_norm(z, lg_ref[...], lb_ref[...])


def _ffn(x3, mod3, w_up, w_down, ln_g, ln_b, bb, tt):
    b, tp, _ = x3.shape
    blk = pl.BlockSpec((bb, tt, D), lambda i, t, j: (i, t, 0))
    mblk = lambda col: pl.BlockSpec((bb, 1, D), lambda i, t, j, col=col: (i, 0, col))
    full = lambda shp: pl.BlockSpec(shp, lambda i, t, j: (0,) * len(shp))
    return pl.pallas_call(
        _ffn_kernel,
        grid=(b // bb, tp // tt, D_FF // FF_CHUNK),
        in_specs=[blk, mblk(3), mblk(4), mblk(5),
                  pl.BlockSpec((D, FF_CHUNK), lambda i, t, j: (0, j)),
                  pl.BlockSpec((FF_CHUNK, D), lambda i, t, j: (j, 0)),
                  full((1, D)), full((1, D))],
        out_specs=blk,
        out_shape=jax.ShapeDtypeStruct((b, tp, D), F32),
        scratch_shapes=[pltpu.VMEM((bb * tt, D), BF16), pltpu.VMEM((bb * tt, D), F32)],
        compiler_params=_cp(("parallel", "parallel", "arbitrary")),
        name="ffn_ln",
    )(x3, mod3, mod3, mod3, w_up, w_down, ln_g, ln_b)


def _to_pair_blockdiag(s):
    b = s.shape[0]
    s = s.reshape(b, N_PAIR, 2, HB, HB)
    z = jnp.zeros((b, N_PAIR, HB, HB), s.dtype)
    top = jnp.concatenate([s[:, :, 0], z], axis=-1)
    bot = jnp.concatenate([z, s[:, :, 1]], axis=-1)
    return jnp.concatenate([top, bot], axis=-2)


def _from_pair_blockdiag(sbd):
    b = sbd.shape[0]
    return jnp.stack([sbd[:, :, :HB, :HB], sbd[:, :, HB:, HB:]], axis=2).reshape(b, H_B, HB, HB)


def _relayout_params(p):
    w = p['w_in']
    w_in_r = jnp.concatenate(
        [w[:, :3 * D], w[:, 3 * D + 8:6 * D + 8], w[:, 6 * D + 8 + LORA:8 * D + 8 + LORA],
         w[:, 6 * D + 8:6 * D + 8 + LORA], w[:, 3 * D:3 * D + 8],
         jnp.zeros((D, NP - COL_IF - 8), F32)], axis=1).astype(BF16)
    mu = p['rwkv_mu']
    z64 = jnp.zeros((64, D), F32)
    z128 = jnp.zeros((128, D), F32)
    row = lambda a: a.reshape(1, -1)
    rw = (row(mu[0:D]), row(mu[D:2 * D]), row(mu[2 * D:3 * D]), row(mu[3 * D:3 * D + LORA]),
          row(p['rwkv_w0']), row(p['rwkv_a0']), row(p['rwkv_k_k']), row(p['rwkv_k_a']),
          row(p['rwkv_r_k']), row(p['rwkv_lnx_w']), row(p['rwkv_lnx_b']),
          jnp.concatenate([p['rwkv_w2'], z64, z128], axis=0).astype(BF16),
          jnp.concatenate([z64, p['rwkv_a2'], z128], axis=0).astype(BF16),
          jnp.concatenate([z128, p['rwkv_g2']], axis=0).astype(BF16))
    gbias = jnp.concatenate([p['mlstm_i_bias'], p['mlstm_f_bias'], jnp.zeros((120,), F32)]).reshape(1, 128)
    return dict(w_in_r=w_in_r, rw=rw, gbias=gbias,
                conv_w=p['conv_w'], conv_b=row(p['conv_b']), norm_w=row(p['mlstm_norm_w']),
                w_out=p['w_out'].astype(BF16), w_up=p['w_up'].astype(BF16), w_down=p['w_down'].astype(BF16),
                ln1_g=row(p['ln1_g']), ln1_b=row(p['ln1_b']), ln2_g=row(p['ln2_g']), ln2_b=row(p['ln2_b']))


def _tail(ya3, yb3, x3, mod3, q, bb, tt):
    x1 = _outproj(ya3, yb3, x3, mod3, q['w_out'], q['ln1_g'], q['ln1_b'], bb, tt)
    return _ffn(x1, mod3, q['w_up'], q['w_down'], q['ln2_g'], q['ln2_b'], bb, tt)


def _prompt_layer(x, mod, q, seq_tile, mlstm_chunk):
    b, t, _ = x.shape
    mod3 = mod.reshape(b, 1, N_COND)
    proj3 = _inproj(x, mod3, q['w_in_r'], 1, seq_tile)
    ya3, c1, n1, m1 = _mlstm_seq(proj3, q['conv_w'], q['conv_b'], q['gbias'], q['norm_w'], mlstm_chunk)
    prev3 = jnp.zeros((b, 1, NP), F32)
    s0 = jnp.zeros((b, N_PAIR, 128, 128), F32)
    yb3, s1 = _rwkv(proj3, prev3, s0, q['rw'], 1, RW_L, RW_L)
    y = _tail(ya3, yb3, x, mod3, q, 1, seq_tile)
    shift = _modulate_rows(x[:, t - 1, :], mod)
    conv = proj3[:, t - (CONV_W - 1):, :2 * D]
    return y, (c1, n1[:, :H_A, :], m1[:, 0, :H_A], conv, _from_pair_blockdiag(s1), shift)


def _sample_layer(x, mod, st, q, bb):
    c0, n0, m0, conv0, s0, shift0 = st
    b, t, _ = x.shape
    mod3 = mod.reshape(b, 1, N_COND)
    xp = jnp.pad(x, ((0, 0), (0, 8 - t), (0, 0)))
    proj3 = _inproj(xp, mod3, q['w_in_r'], bb, 8)
    prev = _inproj(shift0.reshape(1, b, D), jnp.zeros((1, 1, N_COND), F32), q['w_in_r'], 1, b)
    prev3 = prev.reshape(b, 1, NP)
    conv0p = jnp.pad(conv0, ((0, 0), (8 - (CONV_W - 1), 0), (0, 0)))
    n0p = jnp.pad(n0, ((0, 0), (0, 8 - H_A), (0, 0)))
    m0p = jnp.pad(m0, ((0, 0), (0, 128 - H_A))).reshape(b, 1, 128)
    ya3, c1, n1, m1 = _mlstm_step(proj3, conv0p, c0, n0p, m0p, q['conv_w'], q['conv_b'], q['gbias'],
                                  q['norm_w'], t)
    yb3, s1 = _rwkv(proj3, prev3, _to_pair_blockdiag(s0), q['rw'], RW_L // 8, 8, t)
    y = _tail(ya3, yb3, xp, mod3, q, bb, 8)
    shift = _modulate_rows(x[:, t - 1, :], mod)
    conv = jnp.concatenate([conv0, proj3[:, :t, :2 * D]], axis=1)[:, t:, :]
    return y[:, :t, :], (c1, n1[:, :H_A, :], m1[:, 0, :H_A], conv, _from_pair_blockdiag(s1), shift)


def kernel(x_prompt, x_sample, c_prompt, c_sample, state_mlstm_C, state_mlstm_n, state_mlstm_m, state_mlstm_conv, state_rwkv_S, state_rwkv_shift, w_cond, b_cond, w_in, mlstm_i_bias, mlstm_f_bias, conv_w, conv_b, mlstm_norm_w, rwkv_mu, rwkv_w0, rwkv_w2, rwkv_a0, rwkv_a2, rwkv_g2, rwkv_k_k, rwkv_k_a, rwkv_r_k, rwkv_lnx_w, rwkv_lnx_b, w_out, ln1_g, ln1_b, w_up, w_down, ln2_g, ln2_b):
    depth = w_in.shape[0]
    bp = x_prompt.shape[0]
    yp, ys = x_prompt, x_sample
    new_p = [[] for _ in range(6)]
    new_s = [[] for _ in range(6)]
    for l in range(depth):
        p = {'w_in': w_in[l], 'mlstm_i_bias': mlstm_i_bias[l], 'mlstm_f_bias': mlstm_f_bias[l],
             'conv_w': conv_w[l], 'conv_b': conv_b[l], 'mlstm_norm_w': mlstm_norm_w[l],
             'rwkv_mu': rwkv_mu[l], 'rwkv_w0': rwkv_w0[l], 'rwkv_w2': rwkv_w2[l], 'rwkv_a0': rwkv_a0[l],
             'rwkv_a2': rwkv_a2[l], 'rwkv_g2': rwkv_g2[l], 'rwkv_k_k': rwkv_k_k[l], 'rwkv_k_a': rwkv_k_a[l],
             'rwkv_r_k': rwkv_r_k[l].reshape(-1), 'rwkv_lnx_w': rwkv_lnx_w[l], 'rwkv_lnx_b': rwkv_lnx_b[l],
             'w_out': w_out[l], 'ln1_g': ln1_g[l], 'ln1_b': ln1_b[l], 'w_up': w_up[l], 'w_down': w_down[l],
             'ln2_g': ln2_g[l], 'ln2_b': ln2_b[l]}
        q = _relayout_params(p)
        mod = _cond(jnp.concatenate([c_prompt, c_sample], axis=0), w_cond[l], b_cond[l])
        yp, st_p = _prompt_layer(yp, mod[:bp], q, min(1024, yp.shape[1]), min(256, yp.shape[1]))
        st_in = (state_mlstm_C[l], state_mlstm_n[l], state_mlstm_m[l], state_mlstm_conv[l],
                 state_rwkv_S[l], state_rwkv_shift[l])
        ys, st_s = _sample_layer(ys, mod[bp:], st_in, q, min(64, ys.shape[0]))
        for lst, t in zip(new_p, st_p):
            lst.append(t)
        for lst, t in zip(new_s, st_s):
            lst.append(t)
    outs_p = [jnp.stack(t) for t in new_p]
    outs_s = [jnp.stack(t) for t in new_s]
    return (yp, ys, *outs_p, *outs_s)
```

```python
import functools

import jax
import jax.numpy as jnp
from jax import lax
from jax.experimental import pallas as pl
from jax.experimental.pallas import tpu as pltpu

F32 = jnp.float32
BF16 = jnp.bfloat16
HIGHEST = lax.Precision.HIGHEST

D = 1024
H_A = 4
DK = 256
CONV_W = 4
H_B = 16
HB = 64
N_PAIR = H_B // 2
D_FF = 4096
N_COND = 6 * D
ALPHA = 2.0 ** 0.25
LN_EPS = 1e-5
MLSTM_EPS = 1e-6
RWKV_EPS = 64e-5

COL_LORA = 8 * D
COL_IF = COL_LORA + 256
NP = 8704
LORA = 256
TN_IN = 512

RW_L = 64
NEG = -1e30
VMEM_LIMIT = 56 * 1024 * 1024


def _cp(sem):
    return pltpu.CompilerParams(dimension_semantics=sem, vmem_limit_bytes=VMEM_LIMIT)


def _dot(a, b, prec=None):
    return jnp.dot(a, b, preferred_element_type=F32, precision=prec)


def _dot_nt(a, b, prec=None):
    return lax.dot_general(a, b, (((1,), (1,)), ((), ())), preferred_element_type=F32, precision=prec)


def _dot_tn(a, b, prec=None):
    return lax.dot_general(a, b, (((0,), (0,)), ((), ())), preferred_element_type=F32, precision=prec)


def _log_sigmoid(x):
    return jnp.minimum(x, 0.0) - jnp.log1p(jnp.exp(-jnp.abs(x)))


def _silu(x):
    return x * jax.nn.sigmoid(x)


def _layer_norm(z, g, b):
    mu = jnp.mean(z, axis=-1, keepdims=True)
    zc = z - mu
    var = jnp.mean(zc * zc, axis=-1, keepdims=True)
    return zc * lax.rsqrt(var + LN_EPS) * g + b


def _cond_kernel(c_ref, w_ref, b_ref, o_ref):
    s = _silu(c_ref[...]).astype(BF16)
    o_ref[...] = _dot(s, w_ref[...].astype(BF16)) + b_ref[...]


def _cond(c, w_cond, b_cond):
    n = c.shape[0]
    tn = 512
    return pl.pallas_call(
        _cond_kernel,
        grid=(N_COND // tn,),
        in_specs=[pl.BlockSpec((n, D), lambda j: (0, 0)),
                  pl.BlockSpec((D, tn), lambda j: (0, j)),
                  pl.BlockSpec((1, tn), lambda j: (0, j))],
        out_specs=pl.BlockSpec((n, tn), lambda j: (0, j)),
        out_shape=jax.ShapeDtypeStruct((n, N_COND), F32),
        compiler_params=_cp(("arbitrary",)),
        name="cond",
    )(c, w_cond, b_cond.reshape(1, N_COND))


def _inproj_kernel(x_ref, sh_ref, sc_ref, w_ref, o_ref, h_scr):
    bb, tt, _ = x_ref.shape

    @pl.when(pl.program_id(2) == 0)
    def _():
        h = x_ref[...] * (1.0 + sc_ref[...]) + sh_ref[...]
        h_scr[...] = h.reshape(bb * tt, D).astype(BF16)

    o_ref[...] = _dot(h_scr[...], w_ref[...]).reshape(bb, tt, TN_IN)


def _inproj(x3, mod3, w_in_r, bb, tt):
    b, tp, _ = x3.shape
    return pl.pallas_call(
        _inproj_kernel,
        grid=(b // bb, tp // tt, NP // TN_IN),
        in_specs=[pl.BlockSpec((bb, tt, D), lambda i, t, j: (i, t, 0)),
                  pl.BlockSpec((bb, 1, D), lambda i, t, j: (i, 0, 0)),
                  pl.BlockSpec((bb, 1, D), lambda i, t, j: (i, 0, 1)),
                  pl.BlockSpec((D, TN_IN), lambda i, t, j: (0, j))],
        out_specs=pl.BlockSpec((bb, tt, TN_IN), lambda i, t, j: (i, t, j)),
        out_shape=jax.ShapeDtypeStruct((b, tp, NP), F32),
        scratch_shapes=[pltpu.VMEM((bb * tt, D), BF16)],
        compiler_params=_cp(("parallel", "parallel", "arbitrary")),
        name="inproj",
    )(x3, mod3, mod3, w_in_r)


def _modulate_kernel(x_ref, sh_ref, sc_ref, o_ref):
    o_ref[...] = x_ref[...] * (1.0 + sc_ref[...]) + sh_ref[...]


def _modulate_rows(x2, mod2):
    n = x2.shape[0]
    return pl.pallas_call(
        _modulate_kernel,
        grid=(1,),
        in_specs=[pl.BlockSpec((n, D), lambda i: (0, 0)),
                  pl.BlockSpec((n, D), lambda i: (0, 0)),
                  pl.BlockSpec((n, D), lambda i: (0, 1))],
        out_specs=pl.BlockSpec((n, D), lambda i: (0, 0)),
        out_shape=jax.ShapeDtypeStruct((n, D), F32),
        name="modulate_last",
    )(x2, mod2, mod2)


def _conv4(pad_ref, n_rows, cw, cb):
    acc = cb + pad_ref[8:8 + n_rows, :] * cw[3:4, :]
    acc = acc + pad_ref[7:7 + n_rows, :] * cw[2:3, :]
    acc = acc + pad_ref[6:6 + n_rows, :] * cw[1:2, :]
    acc = acc + pad_ref[5:5 + n_rows, :] * cw[0:1, :]
    return acc


def _head_norm_rows(h, eps):
    mu = jnp.mean(h, axis=-1, keepdims=True)
    hc = h - mu
    var = jnp.mean(hc * hc, axis=-1, keepdims=True)
    return hc * lax.rsqrt(var + eps)


def _mlstm_seq_kernel(qp_ref, kp_ref, v_ref, ga_ref, if_ref, cw_ref, cb_ref, gb_ref, nw_ref,
                      ya_ref, c_ref, n_ref, m_ref, padq, padk):
    L = qp_ref.shape[1]

    @pl.when(pl.program_id(1) == 0)
    def _():
        c_ref[...] = jnp.zeros_like(c_ref)
        n_ref[...] = jnp.zeros_like(n_ref)
        m_ref[...] = jnp.zeros_like(m_ref)
        padq[0:8, :] = jnp.zeros((8, D), F32)
        padk[0:8, :] = jnp.zeros((8, D), F32)

    padq[8:8 + L, :] = qp_ref[0]
    padk[8:8 + L, :] = kp_ref[0]
    cw = cw_ref[...]
    cb = cb_ref[...]
    q = _silu(_conv4(padq, L, cw[:, 0:D], cb[:, 0:D]))
    k = _silu(_conv4(padk, L, cw[:, D:2 * D], cb[:, D:2 * D])) * (DK ** -0.5)
    padq[0:8, :] = padq[L:L + 8, :]
    padk[0:8, :] = padk[L:L + 8, :]

    gpre = if_ref[0] + gb_ref[...]
    lsg = _log_sigmoid(gpre)
    row = lax.broadcasted_iota(jnp.int32, (L, L), 0)
    col = lax.broadcasted_iota(jnp.int32, (L, L), 1)
    causal = col <= row
    tril = jnp.where(causal, 1.0, 0.0).astype(F32)
    bcum = _dot(tril, lsg, HIGHEST)
    gpre_t = gpre.T
    bcum_t = bcum.T

    v_all = v_ref[0]
    ga = jax.nn.sigmoid(ga_ref[0])
    nw = nw_ref[...]
    m_all = m_ref[0]
    for h in range(H_A):
        sl = slice(h * DK, (h + 1) * DK)
        ig_col = gpre[:, h:h + 1]
        b_col = bcum[:, H_A + h:H_A + h + 1]
        ig_row = gpre_t[h:h + 1, :]
        b_row = bcum_t[H_A + h:H_A + h + 1, :]
        m_prev = m_all[:, h:h + 1]
        g_col = b_col + m_prev
        dlog = jnp.where(causal, b_col - b_row + ig_row, NEG)
        m_t = jnp.maximum(g_col, jnp.max(dlog, axis=1, keepdims=True))
        w_inter = jnp.exp(g_col - m_t)
        p = jnp.exp(dlog - m_t)
        qh = q[:, sl]
        kh = k[:, sl]
        vh = v_all[:, sl]
        qb = qh.astype(BF16)
        kb = kh.astype(BF16)
        s = _dot_nt(qb, kb) * p
        ch = c_ref[0, h]
        nh = n_ref[0, h:h + 1, :]
        num = w_inter * _dot_nt(qb, ch.astype(BF16)) + _dot(s.astype(BF16), vh.astype(BF16))
        den = w_inter * jnp.sum(qh * nh, axis=1, keepdims=True) + jnp.sum(s, axis=1, keepdims=True)
        hh = num / jnp.maximum(jnp.abs(den), jnp.exp(-m_t))
        b_last = b_col[L - 1:L, :]
        wlog = b_last - b_col + ig_col
        m_new = jnp.maximum(b_last + m_prev, jnp.max(wlog, axis=0, keepdims=True))
        decay = jnp.exp(b_last + m_prev - m_new)
        wts = jnp.exp(wlog - m_new)
        c_ref[0, h] = decay * ch + _dot_tn((wts * vh).astype(BF16), kb)
        n_ref[0, h:h + 1, :] = decay * nh + jnp.sum(wts * kh, axis=0, keepdims=True)
        m_ref[0, :, h:h + 1] = m_new
        ya_ref[0, :, sl] = ga[:, sl] * _head_norm_rows(hh, MLSTM_EPS) * nw[:, sl]


def _mlstm_seq(proj3, conv_w, conv_b, gbias, norm_w, L):
    b, tp, _ = proj3.shape
    blk = lambda j: pl.BlockSpec((1, L, D), lambda i, c, j=j: (i, c, j))
    full = lambda shp: pl.BlockSpec(shp, lambda i, c: (0,) * len(shp))
    return pl.pallas_call(
        _mlstm_seq_kernel,
        grid=(b, tp // L),
        in_specs=[blk(0), blk(1), blk(2), blk(6),
                  pl.BlockSpec((1, L, 128), lambda i, c: (i, c, COL_IF // 128)),
                  full((CONV_W, 2 * D)), full((1, 2 * D)), full((1, 128)), full((1, D))],
        out_specs=[pl.BlockSpec((1, L, D), lambda i, c: (i, c, 0)),
                   pl.BlockSpec((1, H_A, DK, DK), lambda i, c: (i, 0, 0, 0)),
                   pl.BlockSpec((1, 8, DK), lambda i, c: (i, 0, 0)),
                   pl.BlockSpec((1, 1, 128), lambda i, c: (i, 0, 0))],
        out_shape=[jax.ShapeDtypeStruct((b, tp, D), F32),
                   jax.ShapeDtypeStruct((b, H_A, DK, DK), F32),
                   jax.ShapeDtypeStruct((b, 8, DK), F32),
                   jax.ShapeDtypeStruct((b, 1, 128), F32)],
        scratch_shapes=[pltpu.VMEM((L + 8, D), F32), pltpu.VMEM((L + 8, D), F32)],
        compiler_params=_cp(("parallel", "arbitrary")),
        name="mlstm_seq",
    )(proj3, proj3, proj3, proj3, proj3, conv_w, conv_b, gbias, norm_w)


def _mlstm_step_kernel(tv, nb, qp_ref, kp_ref, v_ref, ga_ref, if_ref, conv0_ref, c0_ref, n0_ref, m0_ref,
                       cw_ref, cb_ref, gb_ref, nw_ref,
                       ya_ref, c_ref, n_ref, m_ref, padq, padk, gpad, lpad, kpad, vpad, wvpad):
    @pl.when(pl.program_id(0) == 0)
    def _():
        for r in (gpad, lpad, kpad, vpad, wvpad):
            r[...] = jnp.zeros_like(r)

    cw = cw_ref[...]
    cb = cb_ref[...]
    nw = nw_ref[...]
    gb = gb_ref[...]
    trow = lax.broadcasted_iota(jnp.int32, (8, 128), 0)
    scol = lax.broadcasted_iota(jnp.int32, (8, 128), 1)
    mask = (scol <= trow) & (scol < tv)
    rvalid = lax.broadcasted_iota(jnp.int32, (8, 1), 0) < tv
    r128 = lax.broadcasted_iota(jnp.int32, (128, 128), 0)
    c128 = lax.broadcasted_iota(jnp.int32, (128, 128), 1)
    tril = jnp.where(c128 <= r128, 1.0, 0.0).astype(F32)
    n_ref[...] = jnp.zeros_like(n_ref)
    m_ref[...] = jnp.zeros_like(m_ref)

    batches = range(nb)
    q_l, gpre_l, bcol_l, gt_l, bt_l, ga_l = [], [], [], [], [], []
    for bi in batches:
        padq[bi, 0:8, :] = conv0_ref[bi, :, 0:D]
        padk[bi, 0:8, :] = conv0_ref[bi, :, D:2 * D]
        padq[bi, 8:16, :] = qp_ref[bi]
        padk[bi, 8:16, :] = kp_ref[bi]
        q_l.append(_silu(_conv4(padq.at[bi], 8, cw[:, 0:D], cb[:, 0:D])))
        kpad[bi, 0:8, :] = _silu(_conv4(padk.at[bi], 8, cw[:, D:2 * D], cb[:, D:2 * D])) * (DK ** -0.5)
        vpad[bi, 0:8, :] = v_ref[bi]
        gpre = if_ref[bi] + gb
        gpad[bi, 0:8, :] = gpre
        lpad[bi, 0:8, :] = _log_sigmoid(gpre)
        gpre_l.append(gpre)
        ga_l.append(jax.nn.sigmoid(ga_ref[bi]))
    for bi in batches:
        bpad = _dot(tril, lpad[bi], HIGHEST)
        bcol_l.append(bpad[0:8, :])
        bt_l.append(bpad.T)
        gt_l.append(gpad[bi].T)

    probs = [(bi, h) for bi in batches for h in range(H_A)]
    sl_of = lambda h: slice(h * DK, (h + 1) * DK)
    st = {}
    for (bi, h) in probs:
        ig_col = gpre_l[bi][:, h:h + 1]
        b_col = bcol_l[bi][:, H_A + h:H_A + h + 1]
        ig_row = gt_l[bi][h:h + 1, :]
        b_row = bt_l[bi][H_A + h:H_A + h + 1, :]
        m_prev = m0_ref[bi][:, h:h + 1]
        g_col = b_col + m_prev
        dlog = jnp.where(mask, b_col - b_row + ig_row, NEG)
        m_t = jnp.maximum(g_col, jnp.max(dlog, axis=1, keepdims=True))
        b_last = b_col[tv - 1:tv, :]
        wlog = jnp.where(rvalid, b_last - b_col + ig_col, NEG)
        m_new = jnp.maximum(b_last + m_prev, jnp.max(wlog, axis=0, keepdims=True))
        wts = jnp.exp(wlog - m_new)
        wvpad[bi, 0:8, sl_of(h)] = wts * vpad[bi, 0:8, sl_of(h)]
        st[bi, h] = dict(m_t=m_t, w_inter=jnp.exp(g_col - m_t), pm=jnp.exp(dlog - m_t), m_new=m_new,
                         decay=jnp.exp(b_last + m_prev - m_new), wts=wts)
    kb = {(bi, h): kpad[bi, :, sl_of(h)].astype(BF16) for (bi, h) in probs}
    qb = {(bi, h): q_l[bi][:, sl_of(h)].astype(BF16) for (bi, h) in probs}
    s_l = {k_: _dot_nt(qb[k_], kb[k_]) * st[k_]['pm'] for k_ in probs}
    qc_l = {(bi, h): _dot_nt(qb[bi, h], c0_ref[bi, h].astype(BF16)) for (bi, h) in probs}
    sv_l = {(bi, h): _dot(s_l[bi, h].astype(BF16), vpad[bi, :, sl_of(h)].astype(BF16)) for (bi, h) in probs}
    upd_l = {(bi, h): _dot(wvpad[bi, :, sl_of(h)].T.astype(BF16), kb[bi, h]) for (bi, h) in probs}
    for (bi, h) in probs:
        sl = sl_of(h)
        d = st[bi, h]
        nh = n0_ref[bi, h:h + 1, :]
        qh = q_l[bi][:, sl]
        num = d['w_inter'] * qc_l[bi, h] + sv_l[bi, h]
        den = d['w_inter'] * jnp.sum(qh * nh, axis=1, keepdims=True) + jnp.sum(s_l[bi, h], axis=1, keepdims=True)
        hh = num / jnp.maximum(jnp.abs(den), jnp.exp(-d['m_t']))
        ya_ref[bi, :, sl] = ga_l[bi][:, sl] * _head_norm_rows(hh, MLSTM_EPS) * nw[:, sl]
        c_ref[bi, h] = d['decay'] * c0_ref[bi, h] + upd_l[bi, h]
        n_ref[bi, h:h + 1, :] = d['decay'] * nh + jnp.sum(d['wts'] * kpad[bi, 0:8, sl], axis=0, keepdims=True)
        m_ref[bi, :, h:h + 1] = d['m_new']


def _mlstm_step(proj3, conv0p, c0, n0p, m0p, conv_w, conv_b, gbias, norm_w, tv, nb):
    b = proj3.shape[0]
    blk = lambda j: pl.BlockSpec((nb, 8, D), lambda i, j=j: (i, 0, j))
    full = lambda shp: pl.BlockSpec(shp, lambda i: (0,) * len(shp))
    state_specs = [pl.BlockSpec((nb, H_A, DK, DK), lambda i: (i, 0, 0, 0)),
                   pl.BlockSpec((nb, 8, DK), lambda i: (i, 0, 0)),
                   pl.BlockSpec((nb, 1, 128), lambda i: (i, 0, 0))]
    return pl.pallas_call(
        functools.partial(_mlstm_step_kernel, tv, nb),
        grid=(b // nb,),
        in_specs=[blk(0), blk(1), blk(2), blk(6),
                  pl.BlockSpec((nb, 8, 128), lambda i: (i, 0, COL_IF // 128)),
                  pl.BlockSpec((nb, 8, 2 * D), lambda i: (i, 0, 0))] + state_specs +
                 [full((CONV_W, 2 * D)), full((1, 2 * D)), full((1, 128)), full((1, D))],
        out_specs=[pl.BlockSpec((nb, 8, D), lambda i: (i, 0, 0))] + state_specs,
        out_shape=[jax.ShapeDtypeStruct((b, 8, D), F32),
                   jax.ShapeDtypeStruct((b, H_A, DK, DK), F32),
                   jax.ShapeDtypeStruct((b, 8, DK), F32),
                   jax.ShapeDtypeStruct((b, 1, 128), F32)],
        scratch_shapes=[pltpu.VMEM((nb, 16, D), F32), pltpu.VMEM((nb, 16, D), F32),
                        pltpu.VMEM((nb, 128, 128), F32), pltpu.VMEM((nb, 128, 128), F32),
                        pltpu.VMEM((nb, 128, D), F32), pltpu.VMEM((nb, 128, D), F32),
                        pltpu.VMEM((nb, 128, D), F32)],
        compiler_params=_cp(("arbitrary",)),
        name="mlstm_step",
    )(proj3, proj3, proj3, proj3, proj3, conv0p, c0, n0p, m0p, conv_w, conv_b, gbias, norm_w)


def _bd(x, lo):
    return jnp.concatenate([jnp.where(lo, x, 0.0), jnp.where(lo, 0.0, x)], axis=0)


def _pair_sum(x, lo):
    s_lo = jnp.sum(jnp.where(lo, x, 0.0), axis=1, keepdims=True)
    s_hi = jnp.sum(jnp.where(lo, 0.0, x), axis=1, keepdims=True)
    return jnp.where(lo, s_lo, s_hi)


def _rwkv_kernel(nbg, lb, tv,
                 r_ref, k_ref, v_ref, l_ref, gb_ref, pr_ref, pk_ref, pv_ref, pl_ref, s0_ref,
                 mur_ref, muk_ref, muv_ref, mul_ref, w0_ref, a0_ref, kk_ref, ka_ref, rk_ref,
                 lw_ref, lb_ref, w2_ref, a2_ref, g2_ref,
                 y_ref, s_ref, cr, ck, cv, cl):
    L = nbg * lb

    @pl.when(pl.program_id(1) == 0)
    def _():
        s_ref[...] = s0_ref[...]
        cr[...] = pr_ref[...]
        ck[...] = pk_ref[...]
        cv[...] = pv_ref[...]
        cl[...] = pl_ref[...]

    def shift_mix(x_ref, carry, mu_ref):
        x3 = x_ref[...]
        width = x3.shape[-1]
        tpos = lax.broadcasted_iota(jnp.int32, x3.shape, 1)
        prev = jnp.where(tpos == 0, carry[...], pltpu.roll(x3, 1, 1))
        carry[...] = x3[:, lb - 1:lb, :]
        return (x3 + (prev - x3) * mu_ref[...]).reshape(L, width)

    xr = shift_mix(r_ref, cr, mur_ref)
    xk = shift_mix(k_ref, ck, muk_ref)
    xv = shift_mix(v_ref, cv, muv_ref)
    xl = shift_mix(l_ref, cl, mul_ref)

    lane_l = lax.broadcasted_iota(jnp.int32, (L, LORA), 1)
    act = jnp.where(lane_l < 64, jnp.tanh(xl), jnp.where(lane_l < 128, xl, jax.nn.sigmoid(xl))).astype(BF16)
    z = w0_ref[...] + _dot(act, w2_ref[...])
    w_log = -(jnp.maximum(-z, 0.0) + jnp.log1p(jnp.exp(-jnp.abs(z)))) - 0.5
    lw = -jnp.exp(w_log)
    a = jax.nn.sigmoid(a0_ref[...] + _dot(act, a2_ref[...]))
    g = _dot(act, g2_ref[...])
    kk = xk * kk_ref[...]
    kmod = xk * (1.0 + (a - 1.0) * ka_ref[...])
    gate_b = jax.nn.sigmoid(gb_ref[...].reshape(L, D))

    t_idx = lax.broadcasted_iota(jnp.int32, (L, 1), 0)
    if tv < lb:
        valid = (t_idx % lb) < tv
        lw = jnp.where(valid, lw, 0.0)
        kk = jnp.where(valid, kk, 0.0)
        kmod = jnp.where(valid, kmod, 0.0)
        xv = jnp.where(valid, xv, 0.0)

    row = lax.broadcasted_iota(jnp.int32, (L, L), 0)
    col = lax.broadcasted_iota(jnp.int32, (L, L), 1)
    tril = jnp.where((col <= row) & (col // lb == row // lb), 1.0, 0.0).astype(F32)
    cum = _dot(tril, lw, HIGHEST)

    lane = lax.broadcasted_iota(jnp.int32, (L, 128), 1)
    lo = lane < HB
    src = lane % HB
    trow = lax.broadcasted_iota(jnp.int32, (L, 128), 0)
    same = (src // lb) == (trow // lb)
    strict = same & (src < trow)
    incl = same & (src <= trow)
    r128 = lax.broadcasted_iota(jnp.int32, (128, 128), 0)
    c128 = lax.broadcasted_iota(jnp.int32, (128, 128), 1)
    blockdiag = (r128 < HB) == (c128 < HB)
    eye_pair = jnp.where(src == trow, 1.0, 0.0).astype(F32)

    pairs = range(N_PAIR)
    sls = [slice(p * 128, (p + 1) * 128) for p in pairs]
    groups = range(nbg)

    at_l, rt_l, bt_l, kt_l, win_l = [], [], [], [], []
    for p in pairs:
        sl = sls[p]
        kkp = kk[:, sl]
        nrm = jnp.sqrt(_pair_sum(kkp * kkp, lo))
        kap = kkp / jnp.maximum(nrm, 1e-12)
        cum_p = cum[:, sl]
        w_in = jnp.exp(cum_p)
        w_inv = jnp.exp(-cum_p)
        at_l.append(-kap * jnp.exp(cum_p - lw[:, sl]))
        rt_l.append(xr[:, sl] * w_in)
        bt_l.append(kap * a[:, sl] * w_inv)
        kt_l.append(kmod[:, sl] * w_inv)
        win_l.append(w_in)
    vp_l = [xv[:, sl] for sl in sls]
    bdv_l = [_bd(vp_l[p], lo).astype(BF16) for p in pairs]

    gm_l = [_dot_nt(jnp.concatenate([at_l[p], rt_l[p]], axis=0).astype(BF16),
                    jnp.concatenate([_bd(bt_l[p], lo), _bd(kt_l[p], lo)], axis=0).astype(BF16))
            for p in pairs]
    n_l = [jnp.where(strict, gm_l[p][0:L, 0:128], 0.0) for p in pairs]
    aak_l = [jnp.where(strict, gm_l[p][0:L, 128:256], 0.0).astype(BF16) for p in pairs]
    ark_l = [jnp.concatenate([jnp.where(incl, gm_l[p][L:2 * L, 0:128], 0.0),
                              jnp.where(incl, gm_l[p][L:2 * L, 128:256], 0.0)], axis=1).astype(BF16)
             for p in pairs]

    xs_l = [[_dot_nt(jnp.concatenate([at_l[p][gi * lb:(gi + 1) * lb], rt_l[p][gi * lb:(gi + 1) * lb]],
                                     axis=0).astype(BF16), s_ref[gi, p].astype(BF16))
             for gi in groups] for p in pairs]
    if nbg == 1:
        as_l = [xs_l[p][0][0:lb] for p in pairs]
        rs_l = [xs_l[p][0][lb:2 * lb] for p in pairs]
    else:
        as_l = [jnp.concatenate([xs_l[p][gi][0:lb] for gi in groups], axis=0) for p in pairs]
        rs_l = [jnp.concatenate([xs_l[p][gi][lb:2 * lb] for gi in groups], axis=0) for p in pairs]

    y0_l = [as_l[p] + _dot(aak_l[p], bdv_l[p]) for p in pairs]

    dm_l = [eye_pair for _ in pairs]
    s_blk = 1
    while 2 * s_blk <= lb:
        lvl = ((trow // (2 * s_blk)) == (src // (2 * s_blk))) & ((trow % (2 * s_blk)) >= s_blk) \
            & ((src % (2 * s_blk)) < s_blk)
        if s_blk == 1:
            dm_l = [dm_l[p] + jnp.where(lvl, n_l[p], 0.0) for p in pairs]
        else:
            t1_l = [_dot(jnp.where(lvl, n_l[p], 0.0).astype(BF16), _bd(dm_l[p], lo).astype(BF16))
                    for p in pairs]
            dm_l = [dm_l[p] + _dot(dm_l[p].astype(BF16), _bd(t1_l[p], lo).astype(BF16)) for p in pairs]
        s_blk *= 2
    u_l = [_dot(dm_l[p].astype(BF16), _bd(y0_l[p], lo).astype(BF16)) for p in pairs]

    o_l = [rs_l[p] + _dot(ark_l[p], jnp.concatenate([_bd(u_l[p], lo).astype(BF16), bdv_l[p]], axis=0))
           for p in pairs]

    w3_l = [win_l[p].reshape(nbg, lb, 128)[:, lb - 1:lb, :] for p in pairs]
    rhs_l = []
    for p in pairs:
        w_last = jnp.broadcast_to(w3_l[p], (nbg, lb, 128)).reshape(L, 128)
        rhs_l.append(jnp.concatenate([bt_l[p] * w_last, kt_l[p] * w_last], axis=0).astype(BF16))
    uv_l = [jnp.concatenate([u_l[p], vp_l[p]], axis=0) for p in pairs]
    if nbg == 1:
        upd_l = [_dot_tn(uv_l[p].astype(BF16), rhs_l[p]) for p in pairs]
        for p in pairs:
            s_ref[0, p] = s_ref[0, p] * w3_l[p][0] + jnp.where(blockdiag, upd_l[p], 0.0)
    else:
        cgrp = (c128 % L) // lb
        uvt_l = [uv_l[p].T for p in pairs]
        for p in pairs:
            for gi in groups:
                upd = _dot(jnp.where(cgrp == gi, uvt_l[p], 0.0).astype(BF16), rhs_l[p])
                s_ref[gi, p] = s_ref[gi, p] * w3_l[p][gi] + jnp.where(blockdiag, upd, 0.0)

    for p in pairs:
        sl = sls[p]
        o = o_l[p]
        mu = _pair_sum(o, lo) * (1.0 / HB)
        oc = o - mu
        var = _pair_sum(oc * oc, lo) * (1.0 / HB)
        on = oc * lax.rsqrt(var + RWKV_EPS) * lw_ref[:, sl] + lb_ref[:, sl]
        bonus = _pair_sum(xr[:, sl] * kmod[:, sl] * rk_ref[:, sl], lo) * vp_l[p]
        yb = (on + bonus) * g[:, sl]
        y_ref[:, :, sl] = (gate_b[:, sl] * yb).reshape(nbg, lb, 128)


def _rwkv(proj3, prev3, s0bd, prm, nbg, lb, tv):
    b, tp, _ = proj3.shape
    blk = lambda j: pl.BlockSpec((nbg, lb, D), lambda i, c, j=j: (i, c, j))
    pblk = lambda j: pl.BlockSpec((nbg, 1, D), lambda i, c, j=j: (i, 0, j))
    full = lambda a: pl.BlockSpec(a.shape, lambda i, c: (0,) * a.ndim)
    sblk = pl.BlockSpec((nbg, N_PAIR, 128, 128), lambda i, c: (i, 0, 0, 0))
    return pl.pallas_call(
        functools.partial(_rwkv_kernel, nbg, lb, tv),
        grid=(b // nbg, tp // lb),
        in_specs=[blk(3), blk(4), blk(5),
                  pl.BlockSpec((nbg, lb, LORA), lambda i, c: (i, c, COL_LORA // LORA)),
                  blk(7),
                  pblk(3), pblk(4), pblk(5),
                  pl.BlockSpec((nbg, 1, LORA), lambda i, c: (i, 0, COL_LORA // LORA)),
                  sblk] + [full(a) for a in prm],
        out_specs=[pl.BlockSpec((nbg, lb, D), lambda i, c: (i, c, 0)), sblk],
        out_shape=[jax.ShapeDtypeStruct((b, tp, D), F32),
                   jax.ShapeDtypeStruct((b, N_PAIR, 128, 128), F32)],
        scratch_shapes=[pltpu.VMEM((nbg, 1, D), F32), pltpu.VMEM((nbg, 1, D), F32),
                        pltpu.VMEM((nbg, 1, D), F32), pltpu.VMEM((nbg, 1, LORA), F32)],
        compiler_params=_cp(("parallel", "arbitrary")),
        name="rwkv",
    )(proj3, proj3, proj3, proj3, proj3, prev3, prev3, prev3, prev3, s0bd, *prm)


def _outproj_kernel(ya_ref, yb_ref, x_ref, g1_ref, w_ref, lg_ref, lb_ref, o_ref):
    bb, tt, _ = x_ref.shape
    u = (ya_ref[...] + yb_ref[...]).reshape(bb * tt, D).astype(BF16)
    y = _dot(u, w_ref[...]).reshape(bb, tt, D)
    z = ALPHA * x_ref[...] + g1_ref[...] * y
    o_ref[...] = _layer_norm(z, lg_ref[...], lb_ref[...])


def _outproj(ya3, yb3, x3, mod3, w_out, ln_g, ln_b, bb, tt):
    b, tp, _ = x3.shape
    blk = pl.BlockSpec((bb, tt, D), lambda i, t: (i, t, 0))
    full = lambda shp: pl.BlockSpec(shp, lambda i, t: (0,) * len(shp))
    return pl.pallas_call(
        _outproj_kernel,
        grid=(b // bb, tp // tt),
        in_specs=[blk, blk, blk, pl.BlockSpec((bb, 1, D), lambda i, t: (i, 0, 2)),
                  full((D, D)), full((1, D)), full((1, D))],
        out_specs=blk,
        out_shape=jax.ShapeDtypeStruct((b, tp, D), F32),
        compiler_params=_cp(("parallel", "parallel")),
        name="outproj_ln",
    )(ya3, yb3, x3, mod3, w_out, ln_g, ln_b)


FF_CHUNK = 1024


def _ffn_kernel(x_ref, sh_ref, sc_ref, g2_ref, wu_ref, wd_ref, lg_ref, lb_ref, o_ref, h_scr, acc):
    bb, tt, _ = x_ref.shape
    j = pl.program_id(2)

    @pl.when(j == 0)
    def _():
        h = x_ref[...] * (1.0 + sc_ref[...]) + sh_ref[...]
        h_scr[...] = h.reshape(bb * tt, D).astype(BF16)
        acc[...] = jnp.zeros_like(acc)

    up = jnp.maximum(_dot(h_scr[...], wu_ref[...]), 0.0)
    acc[...] += _dot((up * up).astype(BF16), wd_ref[...])

    @pl.when(j == pl.num_programs(2) - 1)
    def _():
        z = ALPHA * x_ref[...] + g2_ref[...] * acc[...].reshape(bb, tt, D)
        o_ref[...] = _layer_norm(z, lg_ref[...], lb_ref[...])


def _ffn(x3, mod3, w_up, w_down, ln_g, ln_b, bb, tt):
    b, tp, _ = x3.shape
    blk = pl.BlockSpec((bb, tt, D), lambda i, t, j: (i, t, 0))
    mblk = lambda col: pl.BlockSpec((bb, 1, D), lambda i, t, j, col=col: (i, 0, col))
    full = lambda shp: pl.BlockSpec(shp, lambda i, t, j: (0,) * len(shp))
    return pl.pallas_call(
        _ffn_kernel,
        grid=(b // bb, tp // tt, D_FF // FF_CHUNK),
        in_specs=[blk, mblk(3), mblk(4), mblk(5),
                  pl.BlockSpec((D, FF_CHUNK), lambda i, t, j: (0, j)),
                  pl.BlockSpec((FF_CHUNK, D), lambda i, t, j: (j, 0)),
                  full((1, D)), full((1, D))],
        out_specs=blk,
        out_shape=jax.ShapeDtypeStruct((b, tp, D), F32),
        scratch_shapes=[pltpu.VMEM((bb * tt, D), BF16), pltpu.VMEM((bb * tt, D), F32)],
        compiler_params=_cp(("parallel", "parallel", "arbitrary")),
        name="ffn_ln",
    )(x3, mod3, mod3, mod3, w_up, w_down, ln_g, ln_b)


def _to_pair_blockdiag(s):
    b = s.shape[0]
    s = s.reshape(b, N_PAIR, 2, HB, HB)
    z = jnp.zeros((b, N_PAIR, HB, HB), s.dtype)
    top = jnp.concatenate([s[:, :, 0], z], axis=-1)
    bot = jnp.concatenate([z, s[:, :, 1]], axis=-1)
    return jnp.concatenate([top, bot], axis=-2)


def _from_pair_blockdiag(sbd):
    b = sbd.shape[0]
    return jnp.stack([sbd[:, :, :HB, :HB], sbd[:, :, HB:, HB:]], axis=2).reshape(b, H_B, HB, HB)


def _relayout_params(p):
    w = p['w_in']
    w_in_r = jnp.concatenate(
        [w[:, :3 * D], w[:, 3 * D + 8:6 * D + 8], w[:, 6 * D + 8 + LORA:8 * D + 8 + LORA],
         w[:, 6 * D + 8:6 * D + 8 + LORA], w[:, 3 * D:3 * D + 8],
         jnp.zeros((D, NP - COL_IF - 8), F32)], axis=1).astype(BF16)
    mu = p['rwkv_mu']
    z64 = jnp.zeros((64, D), F32)
    z128 = jnp.zeros((128, D), F32)
    row = lambda a: a.reshape(1, -1)
    rw = (row(mu[0:D]), row(mu[D:2 * D]), row(mu[2 * D:3 * D]), row(mu[3 * D:3 * D + LORA]),
          row(p['rwkv_w0']), row(p['rwkv_a0']), row(p['rwkv_k_k']), row(p['rwkv_k_a']),
          row(p['rwkv_r_k']), row(p['rwkv_lnx_w']), row(p['rwkv_lnx_b']),
          jnp.concatenate([p['rwkv_w2'], z64, z128], axis=0).astype(BF16),
          jnp.concatenate([z64, p['rwkv_a2'], z128], axis=0).astype(BF16),
          jnp.concatenate([z128, p['rwkv_g2']], axis=0).astype(BF16))
    gbias = jnp.concatenate([p['mlstm_i_bias'], p['mlstm_f_bias'], jnp.zeros((120,), F32)]).reshape(1, 128)
    return dict(w_in_r=w_in_r, rw=rw, gbias=gbias,
                conv_w=p['conv_w'], conv_b=row(p['conv_b']), norm_w=row(p['mlstm_norm_w']),
                w_out=p['w_out'].astype(BF16), w_up=p['w_up'].astype(BF16), w_down=p['w_down'].astype(BF16),
                ln1_g=row(p['ln1_g']), ln1_b=row(p['ln1_b']), ln2_g=row(p['ln2_g']), ln2_b=row(p['ln2_b']))


def _tail(ya3, yb3, x3, mod3, q, bb, tt):
    x1 = _outproj(ya3, yb3, x3, mod3, q['w_out'], q['ln1_g'], q['ln1_b'], bb, tt)
    return _ffn(x1, mod3, q['w_up'], q['w_down'], q['ln2_g'], q['ln2_b'], bb, tt)


def _prompt_layer(x, mod, q, seq_tile, mlstm_chunk):
    b, t, _ = x.shape
    mod3 = mod.reshape(b, 1, N_COND)
    proj3 = _inproj(x, mod3, q['w_in_r'], 1, seq_tile)
    ya3, c1, n1, m1 = _mlstm_seq(proj3, q['conv_w'], q['conv_b'], q['gbias'], q['norm_w'], mlstm_chunk)
    prev3 = jnp.zeros((b, 1, NP), F32)
    s0 = jnp.zeros((b, N_PAIR, 128, 128), F32)
    yb3, s1 = _rwkv(proj3, prev3, s0, q['rw'], 1, RW_L, RW_L)
    y = _tail(ya3, yb3, x, mod3, q, 1, seq_tile)
    shift = _modulate_rows(x[:, t - 1, :], mod)
    conv = proj3[:, t - (CONV_W - 1):, :2 * D]
    return y, (c1, n1[:, :H_A, :], m1[:, 0, :H_A], conv, _from_pair_blockdiag(s1), shift)


def _sample_layer(x, mod, st, q, bb):
    c0, n0, m0, conv0, s0, shift0 = st
    b, t, _ = x.shape
    mod3 = mod.reshape(b, 1, N_COND)
    xp = jnp.pad(x, ((0, 0), (0, 8 - t), (0, 0)))
    proj3 = _inproj(xp, mod3, q['w_in_r'], bb, 8)
    prev = _inproj(shift0.reshape(1, b, D), jnp.zeros((1, 1, N_COND), F32), q['w_in_r'], 1, b)
    prev3 = prev.reshape(b, 1, NP)
    conv0p = jnp.pad(conv0, ((0, 0), (8 - (CONV_W - 1), 0), (0, 0)))
    n0p = jnp.pad(n0, ((0, 0), (0, 8 - H_A), (0, 0)))
    m0p = jnp.pad(m0, ((0, 0), (0, 128 - H_A))).reshape(b, 1, 128)
    ya3, c1, n1, m1 = _mlstm_step(proj3, conv0p, c0, n0p, m0p, q['conv_w'], q['conv_b'], q['gbias'],
                                  q['norm_w'], t, min(4, b))
    yb3, s1 = _rwkv(proj3, prev3, _to_pair_blockdiag(s0), q['rw'], RW_L // 8, 8, t)
    y = _tail(ya3, yb3, xp, mod3, q, bb, 8)
    shift = _modulate_rows(x[:, t - 1, :], mod)
    conv = jnp.concatenate([conv0, proj3[:, :t, :2 * D]], axis=1)[:, t:, :]
    return y[:, :t, :], (c1, n1[:, :H_A, :], m1[:, 0, :H_A], conv, _from_pair_blockdiag(s1), shift)


def kernel(x_prompt, x_sample, c_prompt, c_sample, state_mlstm_C, state_mlstm_n, state_mlstm_m, state_mlstm_conv, state_rwkv_S, state_rwkv_shift, w_cond, b_cond, w_in, mlstm_i_bias, mlstm_f_bias, conv_w, conv_b, mlstm_norm_w, rwkv_mu, rwkv_w0, rwkv_w2, rwkv_a0, rwkv_a2, rwkv_g2, rwkv_k_k, rwkv_k_a, rwkv_r_k, rwkv_lnx_w, rwkv_lnx_b, w_out, ln1_g, ln1_b, w_up, w_down, ln2_g, ln2_b):
    depth = w_in.shape[0]
    bp = x_prompt.shape[0]
    yp, ys = x_prompt, x_sample
    new_p = [[] for _ in range(6)]
    new_s = [[] for _ in range(6)]
    for l in range(depth):
        p = {'w_in': w_in[l], 'mlstm_i_bias': mlstm_i_bias[l], 'mlstm_f_bias': mlstm_f_bias[l],
             'conv_w': conv_w[l], 'conv_b': conv_b[l], 'mlstm_norm_w': mlstm_norm_w[l],
             'rwkv_mu': rwkv_mu[l], 'rwkv_w0': rwkv_w0[l], 'rwkv_w2': rwkv_w2[l], 'rwkv_a0': rwkv_a0[l],
             'rwkv_a2': rwkv_a2[l], 'rwkv_g2': rwkv_g2[l], 'rwkv_k_k': rwkv_k_k[l], 'rwkv_k_a': rwkv_k_a[l],
             'rwkv_r_k': rwkv_r_k[l].reshape(-1), 'rwkv_lnx_w': rwkv_lnx_w[l], 'rwkv_lnx_b': rwkv_lnx_b[l],
             'w_out': w_out[l], 'ln1_g': ln1_g[l], 'ln1_b': ln1_b[l], 'w_up': w_up[l], 'w_down': w_down[l],
             'ln2_g': ln2_g[l], 'ln2_b': ln2_b[l]}
        q = _relayout_params(p)
        mod = _cond(jnp.concatenate([c_prompt, c_sample], axis=0), w_cond[l], b_cond[l])
        yp, st_p = _prompt_layer(yp, mod[:bp], q, min(1024, yp.shape[1]), min(256, yp.shape[1]))
        st_in = (state_mlstm_C[l], state_mlstm_n[l], state_mlstm_m[l], state_mlstm_conv[l],
                 state_rwkv_S[l], state_rwkv_shift[l])
        ys, st_s = _sample_layer(ys, mod[bp:], st_in, q, min(64, ys.shape[0]))
        for lst, t in zip(new_p, st_p):
            lst.append(t)
        for lst, t in zip(new_s, st_s):
            lst.append(t)
    outs_p = [jnp.stack(t) for t in new_p]
    outs_s = [jnp.stack(t) for t in new_s]
    return (yp, ys, *outs_p, *outs_s)
```

```python
import functools

import jax
import jax.numpy as jnp
from jax import lax
from jax.experimental import pallas as pl
from jax.experimental.pallas import tpu as pltpu

F32 = jnp.float32
BF16 = jnp.bfloat16
HIGHEST = lax.Precision.HIGHEST

D = 1024
H_A = 4
DK = 256
CONV_W = 4
H_B = 16
HB = 64
N_PAIR = H_B // 2
D_FF = 4096
N_COND = 6 * D
ALPHA = 2.0 ** 0.25
LN_EPS = 1e-5
MLSTM_EPS = 1e-6
RWKV_EPS = 64e-5

N_MAIN = 8 * D
LORA = 256
TAIL_IF = LORA
N_TAIL = 512
TN_IN = 512
N_MAIN_TILES = N_MAIN // TN_IN

RW_L = 64
FF_CHUNK = 1024
NEG = -1e30
VMEM_LIMIT = 56 * 1024 * 1024


def _cp(sem):
    return pltpu.CompilerParams(dimension_semantics=sem, vmem_limit_bytes=VMEM_LIMIT)


def _dot(a, b, prec=None):
    return jnp.dot(a, b, preferred_element_type=F32, precision=prec)


def _dot_nt(a, b, prec=None):
    return lax.dot_general(a, b, (((1,), (1,)), ((), ())), preferred_element_type=F32, precision=prec)


def _dot_tn(a, b, prec=None):
    return lax.dot_general(a, b, (((0,), (0,)), ((), ())), preferred_element_type=F32, precision=prec)


def _log_sigmoid(x):
    return jnp.minimum(x, 0.0) - jnp.log1p(jnp.exp(-jnp.abs(x)))


def _silu(x):
    return x * jax.nn.sigmoid(x)


def _layer_norm(z, g, b):
    mu = jnp.mean(z, axis=-1, keepdims=True)
    zc = z - mu
    var = jnp.mean(zc * zc, axis=-1, keepdims=True)
    return zc * lax.rsqrt(var + LN_EPS) * g + b


def _cond_kernel(c_ref, w_ref, b_ref, o_ref):
    s = _silu(c_ref[...]).astype(BF16)
    o_ref[...] = _dot(s, w_ref[...].astype(BF16)) + b_ref[...]


def _cond(c, w_cond, b_cond):
    n = c.shape[0]
    tn = 512
    return pl.pallas_call(
        _cond_kernel,
        grid=(N_COND // tn,),
        in_specs=[pl.BlockSpec((n, D), lambda j: (0, 0)),
                  pl.BlockSpec((D, tn), lambda j: (0, j)),
                  pl.BlockSpec((1, tn), lambda j: (0, j))],
        out_specs=pl.BlockSpec((n, tn), lambda j: (0, j)),
        out_shape=jax.ShapeDtypeStruct((n, N_COND), F32),
        compiler_params=_cp(("arbitrary",)),
        name="cond",
    )(c, w_cond, b_cond.reshape(1, N_COND))


def _inproj_kernel(x_ref, sh_ref, sc_ref, w_ref, main_ref, tail_ref, h_scr):
    bb, tt, _ = x_ref.shape
    j = pl.program_id(2)

    @pl.when(j == 0)
    def _():
        h = x_ref[...] * (1.0 + sc_ref[...]) + sh_ref[...]
        h_scr[...] = h.reshape(bb * tt, D).astype(BF16)

    acc = _dot(h_scr[...], w_ref[...]).reshape(bb, tt, TN_IN)

    @pl.when(j < N_MAIN_TILES)
    def _():
        main_ref[...] = acc.astype(main_ref.dtype)

    @pl.when(j == N_MAIN_TILES)
    def _():
        tail_ref[...] = acc


def _inproj(x3, mod3, w_in_r, bb, tt, main_dtype):
    b, tp, _ = x3.shape
    return pl.pallas_call(
        _inproj_kernel,
        grid=(b // bb, tp // tt, N_MAIN_TILES + 1),
        in_specs=[pl.BlockSpec((bb, tt, D), lambda i, t, j: (i, t, 0)),
                  pl.BlockSpec((bb, 1, D), lambda i, t, j: (i, 0, 0)),
                  pl.BlockSpec((bb, 1, D), lambda i, t, j: (i, 0, 1)),
                  pl.BlockSpec((D, TN_IN), lambda i, t, j: (0, j))],
        out_specs=[pl.BlockSpec((bb, tt, TN_IN), lambda i, t, j: (i, t, jnp.minimum(j, N_MAIN_TILES - 1))),
                   pl.BlockSpec((bb, tt, N_TAIL), lambda i, t, j: (i, t, 0))],
        out_shape=[jax.ShapeDtypeStruct((b, tp, N_MAIN), main_dtype),
                   jax.ShapeDtypeStruct((b, tp, N_TAIL), F32)],
        scratch_shapes=[pltpu.VMEM((bb * tt, D), BF16)],
        compiler_params=_cp(("parallel", "parallel", "arbitrary")),
        name="inproj",
    )(x3, mod3, mod3, w_in_r)


def _modulate_kernel(x_ref, sh_ref, sc_ref, o_ref):
    o_ref[...] = x_ref[...] * (1.0 + sc_ref[...]) + sh_ref[...]


def _modulate_rows(x2, mod2):
    n = x2.shape[0]
    return pl.pallas_call(
        _modulate_kernel,
        grid=(1,),
        in_specs=[pl.BlockSpec((n, D), lambda i: (0, 0)),
                  pl.BlockSpec((n, D), lambda i: (0, 0)),
                  pl.BlockSpec((n, D), lambda i: (0, 1))],
        out_specs=pl.BlockSpec((n, D), lambda i: (0, 0)),
        out_shape=jax.ShapeDtypeStruct((n, D), F32),
        name="modulate_last",
    )(x2, mod2, mod2)


def _conv4(pad_ref, n_rows, cw, cb):
    acc = cb + pad_ref[8:8 + n_rows, :] * cw[3:4, :]
    acc = acc + pad_ref[7:7 + n_rows, :] * cw[2:3, :]
    acc = acc + pad_ref[6:6 + n_rows, :] * cw[1:2, :]
    acc = acc + pad_ref[5:5 + n_rows, :] * cw[0:1, :]
    return acc


def _head_norm_rows(h, eps):
    mu = jnp.mean(h, axis=-1, keepdims=True)
    hc = h - mu
    var = jnp.mean(hc * hc, axis=-1, keepdims=True)
    return hc * lax.rsqrt(var + eps)


def _mlstm_seq_kernel(qp_ref, kp_ref, v_ref, ga_ref, if_ref, cw_ref, cb_ref, gb_ref, nw_ref,
                      ya_ref, c_ref, n_ref, m_ref, padq, padk):
    L = qp_ref.shape[1]

    @pl.when(pl.program_id(1) == 0)
    def _():
        c_ref[...] = jnp.zeros_like(c_ref)
        n_ref[...] = jnp.zeros_like(n_ref)
        m_ref[...] = jnp.zeros_like(m_ref)
        padq[0:8, :] = jnp.zeros((8, D), F32)
        padk[0:8, :] = jnp.zeros((8, D), F32)

    padq[8:8 + L, :] = qp_ref[0].astype(F32)
    padk[8:8 + L, :] = kp_ref[0].astype(F32)
    cw = cw_ref[...]
    cb = cb_ref[...]
    q = _silu(_conv4(padq, L, cw[:, 0:D], cb[:, 0:D]))
    k = _silu(_conv4(padk, L, cw[:, D:2 * D], cb[:, D:2 * D])) * (DK ** -0.5)
    padq[0:8, :] = padq[L:L + 8, :]
    padk[0:8, :] = padk[L:L + 8, :]

    gpre = if_ref[0] + gb_ref[...]
    lsg = _log_sigmoid(gpre)
    row = lax.broadcasted_iota(jnp.int32, (L, L), 0)
    col = lax.broadcasted_iota(jnp.int32, (L, L), 1)
    causal = col <= row
    tril = jnp.where(causal, 1.0, 0.0).astype(F32)
    bcum = _dot(tril, lsg, HIGHEST)
    gpre_t = gpre.T
    bcum_t = bcum.T

    v_all = v_ref[0]
    ga = jax.nn.sigmoid(ga_ref[0].astype(F32))
    nw = nw_ref[...]
    m_all = m_ref[0]
    for h in range(H_A):
        sl = slice(h * DK, (h + 1) * DK)
        ig_col = gpre[:, h:h + 1]
        b_col = bcum[:, H_A + h:H_A + h + 1]
        ig_row = gpre_t[h:h + 1, :]
        b_row = bcum_t[H_A + h:H_A + h + 1, :]
        m_prev = m_all[:, h:h + 1]
        g_col = b_col + m_prev
        dlog = jnp.where(causal, b_col - b_row + ig_row, NEG)
        m_t = jnp.maximum(g_col, jnp.max(dlog, axis=1, keepdims=True))
        w_inter = jnp.exp(g_col - m_t)
        p = jnp.exp(dlog - m_t)
        qh = q[:, sl]
        kh = k[:, sl]
        vh = v_all[:, sl]
        qb = qh.astype(BF16)
        kb = kh.astype(BF16)
        s = _dot_nt(qb, kb) * p
        ch = c_ref[0, h]
        nh = n_ref[0, h:h + 1, :]
        num = w_inter * _dot_nt(qb, ch.astype(BF16)) + _dot(s.astype(BF16), vh.astype(BF16))
        den = w_inter * jnp.sum(qh * nh, axis=1, keepdims=True) + jnp.sum(s, axis=1, keepdims=True)
        hh = num / jnp.maximum(jnp.abs(den), jnp.exp(-m_t))
        b_last = b_col[L - 1:L, :]
        wlog = b_last - b_col + ig_col
        m_new = jnp.maximum(b_last + m_prev, jnp.max(wlog, axis=0, keepdims=True))
        decay = jnp.exp(b_last + m_prev - m_new)
        wts = jnp.exp(wlog - m_new)
        c_ref[0, h] = decay * ch + _dot_tn((wts * vh.astype(F32)).astype(BF16), kb)
        n_ref[0, h:h + 1, :] = decay * nh + jnp.sum(wts * kh, axis=0, keepdims=True)
        m_ref[0, :, h:h + 1] = m_new
        ya_ref[0, :, sl] = (ga[:, sl] * _head_norm_rows(hh, MLSTM_EPS) * nw[:, sl]).astype(ya_ref.dtype)


def _mlstm_seq(main3, tail3, conv_w, conv_b, gbias, norm_w, L):
    b, tp, _ = main3.shape
    blk = lambda j: pl.BlockSpec((1, L, D), lambda i, c, j=j: (i, c, j))
    full = lambda shp: pl.BlockSpec(shp, lambda i, c: (0,) * len(shp))
    return pl.pallas_call(
        _mlstm_seq_kernel,
        grid=(b, tp // L),
        in_specs=[blk(0), blk(1), blk(2), blk(6),
                  pl.BlockSpec((1, L, 128), lambda i, c: (i, c, TAIL_IF // 128)),
                  full((CONV_W, 2 * D)), full((1, 2 * D)), full((1, 128)), full((1, D))],
        out_specs=[pl.BlockSpec((1, L, D), lambda i, c: (i, c, 0)),
                   pl.BlockSpec((1, H_A, DK, DK), lambda i, c: (i, 0, 0, 0)),
                   pl.BlockSpec((1, 8, DK), lambda i, c: (i, 0, 0)),
                   pl.BlockSpec((1, 1, 128), lambda i, c: (i, 0, 0))],
        out_shape=[jax.ShapeDtypeStruct((b, tp, D), main3.dtype),
                   jax.ShapeDtypeStruct((b, H_A, DK, DK), F32),
                   jax.ShapeDtypeStruct((b, 8, DK), F32),
                   jax.ShapeDtypeStruct((b, 1, 128), F32)],
        scratch_shapes=[pltpu.VMEM((L + 8, D), F32), pltpu.VMEM((L + 8, D), F32)],
        compiler_params=_cp(("parallel", "arbitrary")),
        name="mlstm_seq",
    )(main3, main3, main3, main3, tail3, conv_w, conv_b, gbias, norm_w)


def _mlstm_step_kernel(tv, nb, qp_ref, kp_ref, v_ref, ga_ref, if_ref, conv0_ref, c0_ref, n0_ref, m0_ref,
                       cw_ref, cb_ref, gb_ref, nw_ref,
                       ya_ref, c_ref, n_ref, m_ref, padq, padk, gpad, lpad, kpad, vpad, wvpad):
    @pl.when(pl.program_id(0) == 0)
    def _():
        for r in (gpad, lpad, kpad, vpad, wvpad):
            r[...] = jnp.zeros_like(r)

    cw = cw_ref[...]
    cb = cb_ref[...]
    nw = nw_ref[...]
    gb = gb_ref[...]
    trow = lax.broadcasted_iota(jnp.int32, (8, 128), 0)
    scol = lax.broadcasted_iota(jnp.int32, (8, 128), 1)
    mask = (scol <= trow) & (scol < tv)
    rvalid = lax.broadcasted_iota(jnp.int32, (8, 1), 0) < tv
    r128 = lax.broadcasted_iota(jnp.int32, (128, 128), 0)
    c128 = lax.broadcasted_iota(jnp.int32, (128, 128), 1)
    tril = jnp.where(c128 <= r128, 1.0, 0.0).astype(F32)
    n_ref[...] = jnp.zeros_like(n_ref)
    m_ref[...] = jnp.zeros_like(m_ref)

    batches = range(nb)
    q_l, gpre_l, bcol_l, gt_l, bt_l, ga_l = [], [], [], [], [], []
    for bi in batches:
        padq[bi, 0:8, :] = conv0_ref[bi, :, 0:D]
        padk[bi, 0:8, :] = conv0_ref[bi, :, D:2 * D]
        padq[bi, 8:16, :] = qp_ref[bi]
        padk[bi, 8:16, :] = kp_ref[bi]
        q_l.append(_silu(_conv4(padq.at[bi], 8, cw[:, 0:D], cb[:, 0:D])))
        kpad[bi, 0:8, :] = _silu(_conv4(padk.at[bi], 8, cw[:, D:2 * D], cb[:, D:2 * D])) * (DK ** -0.5)
        vpad[bi, 0:8, :] = v_ref[bi]
        gpre = if_ref[bi] + gb
        gpad[bi, 0:8, :] = gpre
        lpad[bi, 0:8, :] = _log_sigmoid(gpre)
        gpre_l.append(gpre)
        ga_l.append(jax.nn.sigmoid(ga_ref[bi]))
    for bi in batches:
        bpad = _dot(tril, lpad[bi], HIGHEST)
        bcol_l.append(bpad[0:8, :])
        bt_l.append(bpad.T)
        gt_l.append(gpad[bi].T)

    probs = [(bi, h) for bi in batches for h in range(H_A)]
    sl_of = lambda h: slice(h * DK, (h + 1) * DK)
    st = {}
    for (bi, h) in probs:
        ig_col = gpre_l[bi][:, h:h + 1]
        b_col = bcol_l[bi][:, H_A + h:H_A + h + 1]
        ig_row = gt_l[bi][h:h + 1, :]
        b_row = bt_l[bi][H_A + h:H_A + h + 1, :]
        m_prev = m0_ref[bi][:, h:h + 1]
        g_col = b_col + m_prev
        dlog = jnp.where(mask, b_col - b_row + ig_row, NEG)
        m_t = jnp.maximum(g_col, jnp.max(dlog, axis=1, keepdims=True))
        b_last = b_col[tv - 1:tv, :]
        wlog = jnp.where(rvalid, b_last - b_col + ig_col, NEG)
        m_new = jnp.maximum(b_last + m_prev, jnp.max(wlog, axis=0, keepdims=True))
        wts = jnp.exp(wlog - m_new)
        wvpad[bi, 0:8, sl_of(h)] = wts * vpad[bi, 0:8, sl_of(h)]
        st[bi, h] = dict(m_t=m_t, w_inter=jnp.exp(g_col - m_t), pm=jnp.exp(dlog - m_t), m_new=m_new,
                         decay=jnp.exp(b_last + m_prev - m_new), wts=wts)
    kb = {(bi, h): kpad[bi, :, sl_of(h)].astype(BF16) for (bi, h) in probs}
    qb = {(bi, h): q_l[bi][:, sl_of(h)].astype(BF16) for (bi, h) in probs}
    s_l = {k_: _dot_nt(qb[k_], kb[k_]) * st[k_]['pm'] for k_ in probs}
    qc_l = {(bi, h): _dot_nt(qb[bi, h], c0_ref[bi, h].astype(BF16)) for (bi, h) in probs}
    sv_l = {(bi, h): _dot(s_l[bi, h].astype(BF16), vpad[bi, :, sl_of(h)].astype(BF16)) for (bi, h) in probs}
    upd_l = {(bi, h): _dot(wvpad[bi, :, sl_of(h)].T.astype(BF16), kb[bi, h]) for (bi, h) in probs}
    for (bi, h) in probs:
        sl = sl_of(h)
        d = st[bi, h]
        nh = n0_ref[bi, h:h + 1, :]
        qh = q_l[bi][:, sl]
        num = d['w_inter'] * qc_l[bi, h] + sv_l[bi, h]
        den = d['w_inter'] * jnp.sum(qh * nh, axis=1, keepdims=True) + jnp.sum(s_l[bi, h], axis=1, keepdims=True)
        hh = num / jnp.maximum(jnp.abs(den), jnp.exp(-d['m_t']))
        ya_ref[bi, :, sl] = ga_l[bi][:, sl] * _head_norm_rows(hh, MLSTM_EPS) * nw[:, sl]
        c_ref[bi, h] = d['decay'] * c0_ref[bi, h] + upd_l[bi, h]
        n_ref[bi, h:h + 1, :] = d['decay'] * nh + jnp.sum(d['wts'] * kpad[bi, 0:8, sl], axis=0, keepdims=True)
        m_ref[bi, :, h:h + 1] = d['m_new']


def _mlstm_step(main3, tail3, conv0p, c0, n0p, m0p, conv_w, conv_b, gbias, norm_w, tv, nb):
    b = main3.shape[0]
    blk = lambda j: pl.BlockSpec((nb, 8, D), lambda i, j=j: (i, 0, j))
    full = lambda shp: pl.BlockSpec(shp, lambda i: (0,) * len(shp))
    state_specs = [pl.BlockSpec((nb, H_A, DK, DK), lambda i: (i, 0, 0, 0)),
                   pl.BlockSpec((nb, 8, DK), lambda i: (i, 0, 0)),
                   pl.BlockSpec((nb, 1, 128), lambda i: (i, 0, 0))]
    return pl.pallas_call(
        functools.partial(_mlstm_step_kernel, tv, nb),
        grid=(b // nb,),
        in_specs=[blk(0), blk(1), blk(2), blk(6),
                  pl.BlockSpec((nb, 8, 128), lambda i: (i, 0, TAIL_IF // 128)),
                  pl.BlockSpec((nb, 8, 2 * D), lambda i: (i, 0, 0))] + state_specs +
                 [full((CONV_W, 2 * D)), full((1, 2 * D)), full((1, 128)), full((1, D))],
        out_specs=[pl.BlockSpec((nb, 8, D), lambda i: (i, 0, 0))] + state_specs,
        out_shape=[jax.ShapeDtypeStruct((b, 8, D), F32),
                   jax.ShapeDtypeStruct((b, H_A, DK, DK), F32),
                   jax.ShapeDtypeStruct((b, 8, DK), F32),
                   jax.ShapeDtypeStruct((b, 1, 128), F32)],
        scratch_shapes=[pltpu.VMEM((nb, 16, D), F32), pltpu.VMEM((nb, 16, D), F32),
                        pltpu.VMEM((nb, 128, 128), F32), pltpu.VMEM((nb, 128, 128), F32),
                        pltpu.VMEM((nb, 128, D), F32), pltpu.VMEM((nb, 128, D), F32),
                        pltpu.VMEM((nb, 128, D), F32)],
        compiler_params=_cp(("arbitrary",)),
        name="mlstm_step",
    )(main3, main3, main3, main3, tail3, conv0p, c0, n0p, m0p, conv_w, conv_b, gbias, norm_w)


def _bd(x, lo):
    return jnp.concatenate([jnp.where(lo, x, 0.0), jnp.where(lo, 0.0, x)], axis=0)


def _pair_sum(x, lo):
    s_lo = jnp.sum(jnp.where(lo, x, 0.0), axis=1, keepdims=True)
    s_hi = jnp.sum(jnp.where(lo, 0.0, x), axis=1, keepdims=True)
    return jnp.where(lo, s_lo, s_hi)


def _rwkv_kernel(nbg, lb, tv, has_state, *refs):
    (r_ref, k_ref, v_ref, gb_ref, l_ref, ya_ref), refs = refs[:6], refs[6:]
    if has_state:
        (pr_ref, pk_ref, pv_ref, pl_ref, s0_ref), refs = refs[:5], refs[5:]
    (mur_ref, muk_ref, muv_ref, mul_ref, w0_ref, a0_ref, kk_ref, ka_ref, rk_ref,
     lw_ref, lb_ref, w2_ref, a2_ref, g2_ref,
     u_ref, s_ref, sbd, cr, ck, cv, cl) = refs
    L = nbg * lb
    z64 = jnp.zeros((HB, HB), F32)

    @pl.when(pl.program_id(1) == 0)
    def _():
        if has_state:
            for gi in range(nbg):
                for p in range(N_PAIR):
                    top = jnp.concatenate([s0_ref[gi, 2 * p], z64], axis=1)
                    bot = jnp.concatenate([z64, s0_ref[gi, 2 * p + 1]], axis=1)
                    sbd[gi, p] = jnp.concatenate([top, bot], axis=0)
            cr[...] = pr_ref[...].astype(F32)
            ck[...] = pk_ref[...].astype(F32)
            cv[...] = pv_ref[...].astype(F32)
            cl[...] = pl_ref[...]
        else:
            sbd[...] = jnp.zeros_like(sbd)
            for c_ in (cr, ck, cv, cl):
                c_[...] = jnp.zeros_like(c_)

    def shift_mix(x_ref, carry, mu_ref):
        x3 = x_ref[...].astype(F32)
        width = x3.shape[-1]
        tpos = lax.broadcasted_iota(jnp.int32, x3.shape, 1)
        prev = jnp.where(tpos == 0, carry[...], pltpu.roll(x3, 1, 1))
        carry[...] = x3[:, lb - 1:lb, :]
        return (x3 + (prev - x3) * mu_ref[...]).reshape(L, width)

    xr = shift_mix(r_ref, cr, mur_ref)
    xk = shift_mix(k_ref, ck, muk_ref)
    xv = shift_mix(v_ref, cv, muv_ref)
    xl = shift_mix(l_ref, cl, mul_ref)

    lane_l = lax.broadcasted_iota(jnp.int32, (L, LORA), 1)
    act = jnp.where(lane_l < 64, jnp.tanh(xl), jnp.where(lane_l < 128, xl, jax.nn.sigmoid(xl))).astype(BF16)
    z = w0_ref[...] + _dot(act, w2_ref[...])
    w_log = -(jnp.maximum(-z, 0.0) + jnp.log1p(jnp.exp(-jnp.abs(z)))) - 0.5
    lw = -jnp.exp(w_log)
    a = jax.nn.sigmoid(a0_ref[...] + _dot(act, a2_ref[...]))
    g = _dot(act, g2_ref[...])
    kk = xk * kk_ref[...]
    kmod = xk * (1.0 + (a - 1.0) * ka_ref[...])
    gate_b = jax.nn.sigmoid(gb_ref[...].astype(F32).reshape(L, D))
    y_a = ya_ref[...].astype(F32).reshape(L, D)

    t_idx = lax.broadcasted_iota(jnp.int32, (L, 1), 0)
    if tv < lb:
        valid = (t_idx % lb) < tv
        lw = jnp.where(valid, lw, 0.0)
        kk = jnp.where(valid, kk, 0.0)
        kmod = jnp.where(valid, kmod, 0.0)
        xv = jnp.where(valid, xv, 0.0)

    row = lax.broadcasted_iota(jnp.int32, (L, L), 0)
    col = lax.broadcasted_iota(jnp.int32, (L, L), 1)
    tril = jnp.where((col <= row) & (col // lb == row // lb), 1.0, 0.0).astype(F32)
    cum = _dot(tril, lw, HIGHEST)

    lane = lax.broadcasted_iota(jnp.int32, (L, 128), 1)
    lo = lane < HB
    src = lane % HB
    trow = lax.broadcasted_iota(jnp.int32, (L, 128), 0)
    same = (src // lb) == (trow // lb)
    strict = same & (src < trow)
    incl = same & (src <= trow)
    r128 = lax.broadcasted_iota(jnp.int32, (128, 128), 0)
    c128 = lax.broadcasted_iota(jnp.int32, (128, 128), 1)
    blockdiag = (r128 < HB) == (c128 < HB)
    eye_pair = jnp.where(src == trow, 1.0, 0.0).astype(F32)

    pairs = range(N_PAIR)
    sls = [slice(p * 128, (p + 1) * 128) for p in pairs]
    groups = range(nbg)

    at_l, rt_l, bt_l, kt_l, win_l = [], [], [], [], []
    for p in pairs:
        sl = sls[p]
        kkp = kk[:, sl]
        nrm = jnp.sqrt(_pair_sum(kkp * kkp, lo))
        kap = kkp / jnp.maximum(nrm, 1e-12)
        cum_p = cum[:, sl]
        w_in = jnp.exp(cum_p)
        w_inv = jnp.exp(-cum_p)
        at_l.append(-kap * jnp.exp(cum_p - lw[:, sl]))
        rt_l.append(xr[:, sl] * w_in)
        bt_l.append(kap * a[:, sl] * w_inv)
        kt_l.append(kmod[:, sl] * w_inv)
        win_l.append(w_in)
    vp_l = [xv[:, sl] for sl in sls]
    bdv_l = [_bd(vp_l[p], lo).astype(BF16) for p in pairs]

    gm_l = [_dot_nt(jnp.concatenate([at_l[p], rt_l[p]], axis=0).astype(BF16),
                    jnp.concatenate([_bd(bt_l[p], lo), _bd(kt_l[p], lo)], axis=0).astype(BF16))
            for p in pairs]
    n_l = [jnp.where(strict, gm_l[p][0:L, 0:128], 0.0) for p in pairs]
    aak_l = [jnp.where(strict, gm_l[p][0:L, 128:256], 0.0).astype(BF16) for p in pairs]
    ark_l = [jnp.concatenate([jnp.where(incl, gm_l[p][L:2 * L, 0:128], 0.0),
                              jnp.where(incl, gm_l[p][L:2 * L, 128:256], 0.0)], axis=1).astype(BF16)
             for p in pairs]

    xs_l = [[_dot_nt(jnp.concatenate([at_l[p][gi * lb:(gi + 1) * lb], rt_l[p][gi * lb:(gi + 1) * lb]],
                                     axis=0).astype(BF16), sbd[gi, p].astype(BF16))
             for gi in groups] for p in pairs]
    if nbg == 1:
        as_l = [xs_l[p][0][0:lb] for p in pairs]
        rs_l = [xs_l[p][0][lb:2 * lb] for p in pairs]
    else:
        as_l = [jnp.concatenate([xs_l[p][gi][0:lb] for gi in groups], axis=0) for p in pairs]
        rs_l = [jnp.concatenate([xs_l[p][gi][lb:2 * lb] for gi in groups], axis=0) for p in pairs]

    y0_l = [as_l[p] + _dot(aak_l[p], bdv_l[p]) for p in pairs]

    dm_l = [eye_pair for _ in pairs]
    s_blk = 1
    while 2 * s_blk <= lb:
        lvl = ((trow // (2 * s_blk)) == (src // (2 * s_blk))) & ((trow % (2 * s_blk)) >= s_blk) \
            & ((src % (2 * s_blk)) < s_blk)
        if s_blk == 1:
            dm_l = [dm_l[p] + jnp.where(lvl, n_l[p], 0.0) for p in pairs]
        else:
            t1_l = [_dot(jnp.where(lvl, n_l[p], 0.0).astype(BF16), _bd(dm_l[p], lo).astype(BF16))
                    for p in pairs]
            dm_l = [dm_l[p] + _dot(dm_l[p].astype(BF16), _bd(t1_l[p], lo).astype(BF16)) for p in pairs]
        s_blk *= 2
    u_l = [_dot(dm_l[p].astype(BF16), _bd(y0_l[p], lo).astype(BF16)) for p in pairs]

    o_l = [rs_l[p] + _dot(ark_l[p], jnp.concatenate([_bd(u_l[p], lo).astype(BF16), bdv_l[p]], axis=0))
           for p in pairs]

    w3_l = [win_l[p].reshape(nbg, lb, 128)[:, lb - 1:lb, :] for p in pairs]
    rhs_l = []
    for p in pairs:
        w_last = jnp.broadcast_to(w3_l[p], (nbg, lb, 128)).reshape(L, 128)
        rhs_l.append(jnp.concatenate([bt_l[p] * w_last, kt_l[p] * w_last], axis=0).astype(BF16))
    uv_l = [jnp.concatenate([u_l[p], vp_l[p]], axis=0) for p in pairs]
    if nbg == 1:
        upd_l = [_dot_tn(uv_l[p].astype(BF16), rhs_l[p]) for p in pairs]
        for p in pairs:
            sbd[0, p] = sbd[0, p] * w3_l[p][0] + jnp.where(blockdiag, upd_l[p], 0.0)
    else:
        cgrp = (c128 % L) // lb
        uvt_l = [uv_l[p].T for p in pairs]
        for p in pairs:
            for gi in groups:
                upd = _dot(jnp.where(cgrp == gi, uvt_l[p], 0.0).astype(BF16), rhs_l[p])
                sbd[gi, p] = sbd[gi, p] * w3_l[p][gi] + jnp.where(blockdiag, upd, 0.0)

    out_l = []
    for p in pairs:
        sl = sls[p]
        o = o_l[p]
        mu = _pair_sum(o, lo) * (1.0 / HB)
        oc = o - mu
        var = _pair_sum(oc * oc, lo) * (1.0 / HB)
        on = oc * lax.rsqrt(var + RWKV_EPS) * lw_ref[:, sl] + lb_ref[:, sl]
        bonus = _pair_sum(xr[:, sl] * kmod[:, sl] * rk_ref[:, sl], lo) * vp_l[p]
        yb = (on + bonus) * g[:, sl]
        out_l.append(y_a[:, sl] + gate_b[:, sl] * yb)
    u_ref[...] = jnp.concatenate(out_l, axis=1).reshape(nbg, lb, D).astype(u_ref.dtype)

    @pl.when(pl.program_id(1) == pl.num_programs(1) - 1)
    def _():
        for gi in range(nbg):
            for p in range(N_PAIR):
                s_ref[gi, 2 * p] = sbd[gi, p, 0:HB, 0:HB]
                s_ref[gi, 2 * p + 1] = sbd[gi, p, HB:2 * HB, HB:2 * HB]


def _rwkv(main3, tail3, ya3, prev, s0, prm, nbg, lb, tv):
    b, tp, _ = main3.shape
    has_state = s0 is not None
    blk = lambda j: pl.BlockSpec((nbg, lb, D), lambda i, c, j=j: (i, c, j))
    pblk = lambda j: pl.BlockSpec((nbg, 1, D), lambda i, c, j=j: (i, 0, j))
    full = lambda a: pl.BlockSpec(a.shape, lambda i, c: (0,) * a.ndim)
    sblk = pl.BlockSpec((nbg, H_B, HB, HB), lambda i, c: (i, 0, 0, 0))
    in_specs = [blk(3), blk(4), blk(5), blk(7),
                pl.BlockSpec((nbg, lb, LORA), lambda i, c: (i, c, 0)),
                pl.BlockSpec((nbg, lb, D), lambda i, c: (i, c, 0))]
    args = [main3, main3, main3, main3, tail3, ya3]
    if has_state:
        in_specs += [pblk(3), pblk(4), pblk(5), pl.BlockSpec((nbg, 1, LORA), lambda i, c: (i, 0, 0)), sblk]
        args += [prev[0], prev[0], prev[0], prev[1], s0]
    in_specs += [full(a) for a in prm]
    args += list(prm)
    return pl.pallas_call(
        functools.partial(_rwkv_kernel, nbg, lb, tv, has_state),
        grid=(b // nbg, tp // lb),
        in_specs=in_specs,
        out_specs=[pl.BlockSpec((nbg, lb, D), lambda i, c: (i, c, 0)), sblk],
        out_shape=[jax.ShapeDtypeStruct((b, tp, D), main3.dtype),
                   jax.ShapeDtypeStruct((b, H_B, HB, HB), F32)],
        scratch_shapes=[pltpu.VMEM((nbg, N_PAIR, 128, 128), F32),
                        pltpu.VMEM((nbg, 1, D), F32), pltpu.VMEM((nbg, 1, D), F32),
                        pltpu.VMEM((nbg, 1, D), F32), pltpu.VMEM((nbg, 1, LORA), F32)],
        compiler_params=_cp(("parallel", "arbitrary")),
        name="rwkv",
    )(*args)


def _tail_kernel(u_ref, x_ref, g1_ref, sh_ref, sc_ref, g2_ref, wo_ref, wu_ref, wd_ref,
                 l1g_ref, l1b_ref, l2g_ref, l2b_ref, o_ref, x1_scr, h_scr, acc):
    bb, tt, _ = x_ref.shape
    j = pl.program_id(2)

    @pl.when(j == 0)
    def _():
        u = u_ref[...].reshape(bb * tt, D).astype(BF16)
        y = _dot(u, wo_ref[...]).reshape(bb, tt, D)
        x1 = _layer_norm(ALPHA * x_ref[...] + g1_ref[...] * y, l1g_ref[...], l1b_ref[...])
        x1_scr[...] = x1
        h_scr[...] = (x1 * (1.0 + sc_ref[...]) + sh_ref[...]).reshape(bb * tt, D).astype(BF16)
        acc[...] = jnp.zeros_like(acc)

    up = jnp.maximum(_dot(h_scr[...], wu_ref[...]), 0.0)
    acc[...] += _dot((up * up).astype(BF16), wd_ref[...])

    @pl.when(j == pl.num_programs(2) - 1)
    def _():
        z = ALPHA * x1_scr[...] + g2_ref[...] * acc[...].reshape(bb, tt, D)
        o_ref[...] = _layer_norm(z, l2g_ref[...], l2b_ref[...])


def _tail(u3, x3, mod3, q, bb, tt):
    b, tp, _ = x3.shape
    blk = pl.BlockSpec((bb, tt, D), lambda i, t, j: (i, t, 0))
    mblk = lambda col: pl.BlockSpec((bb, 1, D), lambda i, t, j, col=col: (i, 0, col))
    full = lambda shp: pl.BlockSpec(shp, lambda i, t, j: (0,) * len(shp))
    return pl.pallas_call(
        _tail_kernel,
        grid=(b // bb, tp // tt, D_FF // FF_CHUNK),
        in_specs=[blk, blk, mblk(2), mblk(3), mblk(4), mblk(5),
                  full((D, D)),
                  pl.BlockSpec((D, FF_CHUNK), lambda i, t, j: (0, j)),
                  pl.BlockSpec((FF_CHUNK, D), lambda i, t, j: (j, 0)),
                  full((1, D)), full((1, D)), full((1, D)), full((1, D))],
        out_specs=blk,
        out_shape=jax.ShapeDtypeStruct((b, tp, D), F32),
        scratch_shapes=[pltpu.VMEM((bb, tt, D), F32), pltpu.VMEM((bb * tt, D), BF16),
                        pltpu.VMEM((bb * tt, D), F32)],
        compiler_params=_cp(("parallel", "parallel", "arbitrary")),
        name="outproj_ffn",
    )(u3, x3, mod3, mod3, mod3, mod3, q['w_out'], q['w_up'], q['w_down'],
      q['ln1_g'], q['ln1_b'], q['ln2_g'], q['ln2_b'])


def _relayout_params(p):
    w = p['w_in']
    w_in_r = jnp.concatenate(
        [w[:, :3 * D], w[:, 3 * D + 8:6 * D + 8], w[:, 6 * D + 8 + LORA:8 * D + 8 + LORA],
         w[:, 6 * D + 8:6 * D + 8 + LORA], w[:, 3 * D:3 * D + 8],
         jnp.zeros((D, N_TAIL - LORA - 8), F32)], axis=1).astype(BF16)
    mu = p['rwkv_mu']
    z64 = jnp.zeros((64, D), F32)
    z128 = jnp.zeros((128, D), F32)
    row = lambda a: a.reshape(1, -1)
    rw = (row(mu[0:D]), row(mu[D:2 * D]), row(mu[2 * D:3 * D]), row(mu[3 * D:3 * D + LORA]),
          row(p['rwkv_w0']), row(p['rwkv_a0']), row(p['rwkv_k_k']), row(p['rwkv_k_a']),
          row(p['rwkv_r_k']), row(p['rwkv_lnx_w']), row(p['rwkv_lnx_b']),
          jnp.concatenate([p['rwkv_w2'], z64, z128], axis=0).astype(BF16),
          jnp.concatenate([z64, p['rwkv_a2'], z128], axis=0).astype(BF16),
          jnp.concatenate([z128, p['rwkv_g2']], axis=0).astype(BF16))
    gbias = jnp.concatenate([p['mlstm_i_bias'], p['mlstm_f_bias'], jnp.zeros((120,), F32)]).reshape(1, 128)
    return dict(w_in_r=w_in_r, rw=rw, gbias=gbias,
                conv_w=p['conv_w'], conv_b=row(p['conv_b']), norm_w=row(p['mlstm_norm_w']),
                w_out=p['w_out'].astype(BF16), w_up=p['w_up'].astype(BF16), w_down=p['w_down'].astype(BF16),
                ln1_g=row(p['ln1_g']), ln1_b=row(p['ln1_b']), ln2_g=row(p['ln2_g']), ln2_b=row(p['ln2_b']))


def _prompt_layer(x, mod, q, seq_tile, mlstm_chunk):
    b, t, _ = x.shape
    mod3 = mod.reshape(b, 1, N_COND)
    main3, tail3 = _inproj(x, mod3, q['w_in_r'], 1, seq_tile, BF16)
    ya3, c1, n1, m1 = _mlstm_seq(main3, tail3, q['conv_w'], q['conv_b'], q['gbias'], q['norm_w'], mlstm_chunk)
    u3, s1 = _rwkv(main3, tail3, ya3, None, None, q['rw'], 1, RW_L, RW_L)
    y = _tail(u3, x, mod3, q, 1, seq_tile)
    shift = _modulate_rows(x[:, t - 1, :], mod)
    conv = main3[:, t - (CONV_W - 1):, :2 * D].astype(F32)
    return y, (c1, n1[:, :H_A, :], m1[:, 0, :H_A], conv, s1, shift)


def _sample_layer(x, mod, st, q, bb):
    c0, n0, m0, conv0, s0, shift0 = st
    b, t, _ = x.shape
    mod3 = mod.reshape(b, 1, N_COND)
    xp = jnp.pad(x, ((0, 0), (0, 8 - t), (0, 0)))
    main3, tail3 = _inproj(xp, mod3, q['w_in_r'], bb, 8, F32)
    pm, pt = _inproj(shift0.reshape(1, b, D), jnp.zeros((1, 1, N_COND), F32), q['w_in_r'], 1, b, F32)
    prev = (pm.reshape(b, 1, N_MAIN), pt.reshape(b, 1, N_TAIL))
    conv0p = jnp.pad(conv0, ((0, 0), (8 - (CONV_W - 1), 0), (0, 0)))
    n0p = jnp.pad(n0, ((0, 0), (0, 8 - H_A), (0, 0)))
    m0p = jnp.pad(m0, ((0, 0), (0, 128 - H_A))).reshape(b, 1, 128)
    ya3, c1, n1, m1 = _mlstm_step(main3, tail3, conv0p, c0, n0p, m0p, q['conv_w'], q['conv_b'], q['gbias'],
                                  q['norm_w'], t, min(4, b))
    u3, s1 = _rwkv(main3, tail3, ya3, prev, s0, q['rw'], RW_L // 8, 8, t)
    y = _tail(u3, xp, mod3, q, bb, 8)
    shift = _modulate_rows(x[:, t - 1, :], mod)
    conv = jnp.concatenate([conv0, main3[:, :t, :2 * D]], axis=1)[:, t:, :]
    return y[:, :t, :], (c1, n1[:, :H_A, :], m1[:, 0, :H_A], conv, s1, shift)


def kernel(x_prompt, x_sample, c_prompt, c_sample, state_mlstm_C, state_mlstm_n, state_mlstm_m, state_mlstm_conv, state_rwkv_S, state_rwkv_shift, w_cond, b_cond, w_in, mlstm_i_bias, mlstm_f_bias, conv_w, conv_b, mlstm_norm_w, rwkv_mu, rwkv_w0, rwkv_w2, rwkv_a0, rwkv_a2, rwkv_g2, rwkv_k_k, rwkv_k_a, rwkv_r_k, rwkv_lnx_w, rwkv_lnx_b, w_out, ln1_g, ln1_b, w_up, w_down, ln2_g, ln2_b):
    depth = w_in.shape[0]
    bp = x_prompt.shape[0]
    yp, ys = x_prompt, x_sample
    new_p = [[] for _ in range(6)]
    new_s = [[] for _ in range(6)]
    for l in range(depth):
        p = {'w_in': w_in[l], 'mlstm_i_bias': mlstm_i_bias[l], 'mlstm_f_bias': mlstm_f_bias[l],
             'conv_w': conv_w[l], 'conv_b': conv_b[l], 'mlstm_norm_w': mlstm_norm_w[l],
             'rwkv_mu': rwkv_mu[l], 'rwkv_w0': rwkv_w0[l], 'rwkv_w2': rwkv_w2[l], 'rwkv_a0': rwkv_a0[l],
             'rwkv_a2': rwkv_a2[l], 'rwkv_g2': rwkv_g2[l], 'rwkv_k_k': rwkv_k_k[l], 'rwkv_k_a': rwkv_k_a[l],
             'rwkv_r_k': rwkv_r_k[l].reshape(-1), 'rwkv_lnx_w': rwkv_lnx_w[l], 'rwkv_lnx_b': rwkv_lnx_b[l],
             'w_out': w_out[l], 'ln1_g': ln1_g[l], 'ln1_b': ln1_b[l], 'w_up': w_up[l], 'w_down': w_down[l],
             'ln2_g': ln2_g[l], 'ln2_b': ln2_b[l]}
        q = _relayout_params(p)
        mod = _cond(jnp.concatenate([c_prompt, c_sample], axis=0), w_cond[l], b_cond[l])
        yp, st_p = _prompt_layer(yp, mod[:bp], q, min(1024, yp.shape[1]), min(256, yp.shape[1]))
        st_in = (state_mlstm_C[l], state_mlstm_n[l], state_mlstm_m[l], state_mlstm_conv[l],
                 state_rwkv_S[l], state_rwkv_shift[l])
        ys, st_s = _sample_layer(ys, mod[bp:], st_in, q, min(64, ys.shape[0]))
        for lst, t in zip(new_p, st_p):
            lst.append(t)
        for lst, t in zip(new_s, st_s):
            lst.append(t)
    outs_p = [jnp.stack(t) for t in new_p]
    outs_s = [jnp.stack(t) for t in new_s]
    return (yp, ys, *outs_p, *outs_s)
```

```python
import functools

import jax
import jax.numpy as jnp
from jax import lax
from jax.experimental import pallas as pl
from jax.experimental.pallas import tpu as pltpu

F32 = jnp.float32
BF16 = jnp.bfloat16
HIGHEST = lax.Precision.HIGHEST

D = 1024
H_A = 4
DK = 256
CONV_W = 4
H_B = 16
HB = 64
N_PAIR = H_B // 2
D_FF = 4096
N_COND = 6 * D
ALPHA = 2.0 ** 0.25
LN_EPS = 1e-5
MLSTM_EPS = 1e-6
RWKV_EPS = 64e-5

N_MAIN = 8 * D
LORA = 256
TAIL_IF = LORA
N_TAIL = 512
TN_MAIN = 2048
N_MAIN_TILES = N_MAIN // TN_MAIN

RW_L = 64
FF_CHUNK = 1024
NEG = -1e30
VMEM_LIMIT = 56 * 1024 * 1024


def _cp(sem):
    return pltpu.CompilerParams(dimension_semantics=sem, vmem_limit_bytes=VMEM_LIMIT)


def _dot(a, b, prec=None):
    return jnp.dot(a, b, preferred_element_type=F32, precision=prec)


def _dot_nt(a, b, prec=None):
    return lax.dot_general(a, b, (((1,), (1,)), ((), ())), preferred_element_type=F32, precision=prec)


def _dot_tn(a, b, prec=None):
    return lax.dot_general(a, b, (((0,), (0,)), ((), ())), preferred_element_type=F32, precision=prec)


def _log_sigmoid(x):
    return jnp.minimum(x, 0.0) - jnp.log1p(jnp.exp(-jnp.abs(x)))


def _silu(x):
    return x * jax.nn.sigmoid(x)


def _layer_norm(z, g, b):
    mu = jnp.mean(z, axis=-1, keepdims=True)
    zc = z - mu
    var = jnp.mean(zc * zc, axis=-1, keepdims=True)
    return zc * lax.rsqrt(var + LN_EPS) * g + b


def _cond_kernel(c_ref, w_ref, b_ref, o_ref):
    s = _silu(c_ref[...]).astype(BF16)
    o_ref[...] = _dot(s, w_ref[...].astype(BF16)) + b_ref[...]


def _cond(c, w_cond, b_cond):
    n = c.shape[0]
    tn = 512
    return pl.pallas_call(
        _cond_kernel,
        grid=(N_COND // tn,),
        in_specs=[pl.BlockSpec((n, D), lambda j: (0, 0)),
                  pl.BlockSpec((D, tn), lambda j: (0, j)),
                  pl.BlockSpec((1, tn), lambda j: (0, j))],
        out_specs=pl.BlockSpec((n, tn), lambda j: (0, j)),
        out_shape=jax.ShapeDtypeStruct((n, N_COND), F32),
        compiler_params=_cp(("arbitrary",)),
        name="cond",
    )(c, w_cond, b_cond.reshape(1, N_COND))


def _inproj_kernel(x_ref, sh_ref, sc_ref, wm_ref, wt_ref, main_ref, tail_ref, h_scr):
    bb, tt, _ = x_ref.shape
    j = pl.program_id(2)

    @pl.when(j == 0)
    def _():
        h = x_ref[...] * (1.0 + sc_ref[...]) + sh_ref[...]
        h_scr[...] = h.reshape(bb * tt, D).astype(BF16)

    @pl.when(j < N_MAIN_TILES)
    def _():
        main_ref[...] = _dot(h_scr[...], wm_ref[...]).reshape(bb, tt, TN_MAIN).astype(main_ref.dtype)

    @pl.when(j == N_MAIN_TILES)
    def _():
        tail_ref[...] = _dot(h_scr[...], wt_ref[...]).reshape(bb, tt, N_TAIL)


def _inproj(x3, mod3, w_main, w_tail, bb, tt, main_dtype):
    b, tp, _ = x3.shape
    last_main = N_MAIN_TILES - 1
    return pl.pallas_call(
        _inproj_kernel,
        grid=(b // bb, tp // tt, N_MAIN_TILES + 1),
        in_specs=[pl.BlockSpec((bb, tt, D), lambda i, t, j: (i, t, 0)),
                  pl.BlockSpec((bb, 1, D), lambda i, t, j: (i, 0, 0)),
                  pl.BlockSpec((bb, 1, D), lambda i, t, j: (i, 0, 1)),
                  pl.BlockSpec((D, TN_MAIN), lambda i, t, j: (0, jnp.minimum(j, last_main))),
                  pl.BlockSpec((D, N_TAIL), lambda i, t, j: (0, 0))],
        out_specs=[pl.BlockSpec((bb, tt, TN_MAIN), lambda i, t, j: (i, t, jnp.minimum(j, last_main))),
                   pl.BlockSpec((bb, tt, N_TAIL), lambda i, t, j: (i, t, 0))],
        out_shape=[jax.ShapeDtypeStruct((b, tp, N_MAIN), main_dtype),
                   jax.ShapeDtypeStruct((b, tp, N_TAIL), F32)],
        scratch_shapes=[pltpu.VMEM((bb * tt, D), BF16)],
        compiler_params=_cp(("parallel", "parallel", "arbitrary")),
        name="inproj",
    )(x3, mod3, mod3, w_main, w_tail)


def _modulate_kernel(x_ref, sh_ref, sc_ref, o_ref):
    o_ref[...] = x_ref[...] * (1.0 + sc_ref[...]) + sh_ref[...]


def _modulate_rows(x2, mod2):
    n = x2.shape[0]
    return pl.pallas_call(
        _modulate_kernel,
        grid=(1,),
        in_specs=[pl.BlockSpec((n, D), lambda i: (0, 0)),
                  pl.BlockSpec((n, D), lambda i: (0, 0)),
                  pl.BlockSpec((n, D), lambda i: (0, 1))],
        out_specs=pl.BlockSpec((n, D), lambda i: (0, 0)),
        out_shape=jax.ShapeDtypeStruct((n, D), F32),
        name="modulate_last",
    )(x2, mod2, mod2)


def _conv4(pad_ref, n_rows, cw, cb):
    acc = cb + pad_ref[8:8 + n_rows, :] * cw[3:4, :]
    acc = acc + pad_ref[7:7 + n_rows, :] * cw[2:3, :]
    acc = acc + pad_ref[6:6 + n_rows, :] * cw[1:2, :]
    acc = acc + pad_ref[5:5 + n_rows, :] * cw[0:1, :]
    return acc


def _head_norm_rows(h, eps):
    mu = jnp.mean(h, axis=-1, keepdims=True)
    hc = h - mu
    var = jnp.mean(hc * hc, axis=-1, keepdims=True)
    return hc * lax.rsqrt(var + eps)


def _mlstm_seq_kernel(qp_ref, kp_ref, v_ref, ga_ref, if_ref, cw_ref, cb_ref, gb_ref, nw_ref,
                      ya_ref, c_ref, n_ref, m_ref, padq, padk):
    L = qp_ref.shape[1]

    @pl.when(pl.program_id(1) == 0)
    def _():
        c_ref[...] = jnp.zeros_like(c_ref)
        n_ref[...] = jnp.zeros_like(n_ref)
        m_ref[...] = jnp.zeros_like(m_ref)
        padq[0:8, :] = jnp.zeros((8, D), F32)
        padk[0:8, :] = jnp.zeros((8, D), F32)

    padq[8:8 + L, :] = qp_ref[0].astype(F32)
    padk[8:8 + L, :] = kp_ref[0].astype(F32)
    cw = cw_ref[...]
    cb = cb_ref[...]
    q = _silu(_conv4(padq, L, cw[:, 0:D], cb[:, 0:D]))
    k = _silu(_conv4(padk, L, cw[:, D:2 * D], cb[:, D:2 * D])) * (DK ** -0.5)
    padq[0:8, :] = padq[L:L + 8, :]
    padk[0:8, :] = padk[L:L + 8, :]

    gpre = if_ref[0] + gb_ref[...]
    lsg = _log_sigmoid(gpre)
    row = lax.broadcasted_iota(jnp.int32, (L, L), 0)
    col = lax.broadcasted_iota(jnp.int32, (L, L), 1)
    causal = col <= row
    tril = jnp.where(causal, 1.0, 0.0).astype(F32)
    bcum = _dot(tril, lsg, HIGHEST)
    gpre_t = gpre.T
    bcum_t = bcum.T

    v_all = v_ref[0]
    ga = jax.nn.sigmoid(ga_ref[0].astype(F32))
    nw = nw_ref[...]
    m_all = m_ref[0]
    for h in range(H_A):
        sl = slice(h * DK, (h + 1) * DK)
        ig_col = gpre[:, h:h + 1]
        b_col = bcum[:, H_A + h:H_A + h + 1]
        ig_row = gpre_t[h:h + 1, :]
        b_row = bcum_t[H_A + h:H_A + h + 1, :]
        m_prev = m_all[:, h:h + 1]
        g_col = b_col + m_prev
        dlog = jnp.where(causal, b_col - b_row + ig_row, NEG)
        m_t = jnp.maximum(g_col, jnp.max(dlog, axis=1, keepdims=True))
        w_inter = jnp.exp(g_col - m_t)
        p = jnp.exp(dlog - m_t)
        qh = q[:, sl]
        kh = k[:, sl]
        vh = v_all[:, sl]
        qb = qh.astype(BF16)
        kb = kh.astype(BF16)
        s = _dot_nt(qb, kb) * p
        ch = c_ref[0, h]
        nh = n_ref[0, h:h + 1, :]
        num = w_inter * _dot_nt(qb, ch.astype(BF16)) + _dot(s.astype(BF16), vh.astype(BF16))
        den = w_inter * jnp.sum(qh * nh, axis=1, keepdims=True) + jnp.sum(s, axis=1, keepdims=True)
        hh = num / jnp.maximum(jnp.abs(den), jnp.exp(-m_t))
        b_last = b_col[L - 1:L, :]
        wlog = b_last - b_col + ig_col
        m_new = jnp.maximum(b_last + m_prev, jnp.max(wlog, axis=0, keepdims=True))
        decay = jnp.exp(b_last + m_prev - m_new)
        wts = jnp.exp(wlog - m_new)
        c_ref[0, h] = decay * ch + _dot_tn((wts * vh.astype(F32)).astype(BF16), kb)
        n_ref[0, h:h + 1, :] = decay * nh + jnp.sum(wts * kh, axis=0, keepdims=True)
        m_ref[0, :, h:h + 1] = m_new
        ya_ref[0, :, sl] = (ga[:, sl] * _head_norm_rows(hh, MLSTM_EPS) * nw[:, sl]).astype(ya_ref.dtype)


def _mlstm_seq(main3, tail3, conv_w, conv_b, gbias, norm_w, L):
    b, tp, _ = main3.shape
    blk = lambda j: pl.BlockSpec((1, L, D), lambda i, c, j=j: (i, c, j))
    full = lambda shp: pl.BlockSpec(shp, lambda i, c: (0,) * len(shp))
    return pl.pallas_call(
        _mlstm_seq_kernel,
        grid=(b, tp // L),
        in_specs=[blk(0), blk(1), blk(2), blk(6),
                  pl.BlockSpec((1, L, 128), lambda i, c: (i, c, TAIL_IF // 128)),
                  full((CONV_W, 2 * D)), full((1, 2 * D)), full((1, 128)), full((1, D))],
        out_specs=[pl.BlockSpec((1, L, D), lambda i, c: (i, c, 0)),
                   pl.BlockSpec((1, H_A, DK, DK), lambda i, c: (i, 0, 0, 0)),
                   pl.BlockSpec((1, 8, DK), lambda i, c: (i, 0, 0)),
                   pl.BlockSpec((1, 1, 128), lambda i, c: (i, 0, 0))],
        out_shape=[jax.ShapeDtypeStruct((b, tp, D), main3.dtype),
                   jax.ShapeDtypeStruct((b, H_A, DK, DK), F32),
                   jax.ShapeDtypeStruct((b, 8, DK), F32),
                   jax.ShapeDtypeStruct((b, 1, 128), F32)],
        scratch_shapes=[pltpu.VMEM((L + 8, D), F32), pltpu.VMEM((L + 8, D), F32)],
        compiler_params=_cp(("parallel", "arbitrary")),
        name="mlstm_seq",
    )(main3, main3, main3, main3, tail3, conv_w, conv_b, gbias, norm_w)


def _mlstm_step_kernel(tv, nb, qp_ref, kp_ref, v_ref, ga_ref, if_ref, conv0_ref, c0_ref, n0_ref, m0_ref,
                       cw_ref, cb_ref, gb_ref, nw_ref,
                       ya_ref, c_ref, n_ref, m_ref, padq, padk, gpad, lpad, kpad, vpad, wvpad):
    @pl.when(pl.program_id(0) == 0)
    def _():
        for r in (gpad, lpad, kpad, vpad, wvpad):
            r[...] = jnp.zeros_like(r)

    cw = cw_ref[...]
    cb = cb_ref[...]
    nw = nw_ref[...]
    gb = gb_ref[...]
    trow = lax.broadcasted_iota(jnp.int32, (8, 128), 0)
    scol = lax.broadcasted_iota(jnp.int32, (8, 128), 1)
    mask = (scol <= trow) & (scol < tv)
    rvalid = lax.broadcasted_iota(jnp.int32, (8, 1), 0) < tv
    r128 = lax.broadcasted_iota(jnp.int32, (128, 128), 0)
    c128 = lax.broadcasted_iota(jnp.int32, (128, 128), 1)
    tril = jnp.where(c128 <= r128, 1.0, 0.0).astype(F32)
    n_ref[...] = jnp.zeros_like(n_ref)
    m_ref[...] = jnp.zeros_like(m_ref)

    batches = range(nb)
    q_l, gpre_l, bcol_l, gt_l, bt_l, ga_l = [], [], [], [], [], []
    for bi in batches:
        padq[bi, 0:8, :] = conv0_ref[bi, :, 0:D]
        padk[bi, 0:8, :] = conv0_ref[bi, :, D:2 * D]
        padq[bi, 8:16, :] = qp_ref[bi]
        padk[bi, 8:16, :] = kp_ref[bi]
        q_l.append(_silu(_conv4(padq.at[bi], 8, cw[:, 0:D], cb[:, 0:D])))
        kpad[bi, 0:8, :] = _silu(_conv4(padk.at[bi], 8, cw[:, D:2 * D], cb[:, D:2 * D])) * (DK ** -0.5)
        vpad[bi, 0:8, :] = v_ref[bi]
        gpre = if_ref[bi] + gb
        gpad[bi, 0:8, :] = gpre
        lpad[bi, 0:8, :] = _log_sigmoid(gpre)
        gpre_l.append(gpre)
        ga_l.append(jax.nn.sigmoid(ga_ref[bi]))
    for bi in batches:
        bpad = _dot(tril, lpad[bi], HIGHEST)
        bcol_l.append(bpad[0:8, :])
        bt_l.append(bpad.T)
        gt_l.append(gpad[bi].T)

    probs = [(bi, h) for bi in batches for h in range(H_A)]
    sl_of = lambda h: slice(h * DK, (h + 1) * DK)
    st = {}
    for (bi, h) in probs:
        ig_col = gpre_l[bi][:, h:h + 1]
        b_col = bcol_l[bi][:, H_A + h:H_A + h + 1]
        ig_row = gt_l[bi][h:h + 1, :]
        b_row = bt_l[bi][H_A + h:H_A + h + 1, :]
        m_prev = m0_ref[bi][:, h:h + 1]
        g_col = b_col + m_prev
        dlog = jnp.where(mask, b_col - b_row + ig_row, NEG)
        m_t = jnp.maximum(g_col, jnp.max(dlog, axis=1, keepdims=True))
        b_last = b_col[tv - 1:tv, :]
        wlog = jnp.where(rvalid, b_last - b_col + ig_col, NEG)
        m_new = jnp.maximum(b_last + m_prev, jnp.max(wlog, axis=0, keepdims=True))
        wts = jnp.exp(wlog - m_new)
        wvpad[bi, 0:8, sl_of(h)] = wts * vpad[bi, 0:8, sl_of(h)]
        st[bi, h] = dict(m_t=m_t, w_inter=jnp.exp(g_col - m_t), pm=jnp.exp(dlog - m_t), m_new=m_new,
                         decay=jnp.exp(b_last + m_prev - m_new), wts=wts)
    kb = {(bi, h): kpad[bi, :, sl_of(h)].astype(BF16) for (bi, h) in probs}
    qb = {(bi, h): q_l[bi][:, sl_of(h)].astype(BF16) for (bi, h) in probs}
    s_l = {k_: _dot_nt(qb[k_], kb[k_]) * st[k_]['pm'] for k_ in probs}
    qc_l = {(bi, h): _dot_nt(qb[bi, h], c0_ref[bi, h].astype(BF16)) for (bi, h) in probs}
    sv_l = {(bi, h): _dot(s_l[bi, h].astype(BF16), vpad[bi, :, sl_of(h)].astype(BF16)) for (bi, h) in probs}
    upd_l = {(bi, h): _dot(wvpad[bi, :, sl_of(h)].T.astype(BF16), kb[bi, h]) for (bi, h) in probs}
    for (bi, h) in probs:
        sl = sl_of(h)
        d = st[bi, h]
        nh = n0_ref[bi, h:h + 1, :]
        qh = q_l[bi][:, sl]
        num = d['w_inter'] * qc_l[bi, h] + sv_l[bi, h]
        den = d['w_inter'] * jnp.sum(qh * nh, axis=1, keepdims=True) + jnp.sum(s_l[bi, h], axis=1, keepdims=True)
        hh = num / jnp.maximum(jnp.abs(den), jnp.exp(-d['m_t']))
        ya_ref[bi, :, sl] = ga_l[bi][:, sl] * _head_norm_rows(hh, MLSTM_EPS) * nw[:, sl]
        c_ref[bi, h] = d['decay'] * c0_ref[bi, h] + upd_l[bi, h]
        n_ref[bi, h:h + 1, :] = d['decay'] * nh + jnp.sum(d['wts'] * kpad[bi, 0:8, sl], axis=0, keepdims=True)
        m_ref[bi, :, h:h + 1] = d['m_new']


def _mlstm_step(main3, tail3, conv0p, c0, n0p, m0p, conv_w, conv_b, gbias, norm_w, tv, nb):
    b = main3.shape[0]
    blk = lambda j: pl.BlockSpec((nb, 8, D), lambda i, j=j: (i, 0, j))
    full = lambda shp: pl.BlockSpec(shp, lambda i: (0,) * len(shp))
    state_specs = [pl.BlockSpec((nb, H_A, DK, DK), lambda i: (i, 0, 0, 0)),
                   pl.BlockSpec((nb, 8, DK), lambda i: (i, 0, 0)),
                   pl.BlockSpec((nb, 1, 128), lambda i: (i, 0, 0))]
    return pl.pallas_call(
        functools.partial(_mlstm_step_kernel, tv, nb),
        grid=(b // nb,),
        in_specs=[blk(0), blk(1), blk(2), blk(6),
                  pl.BlockSpec((nb, 8, 128), lambda i: (i, 0, TAIL_IF // 128)),
                  pl.BlockSpec((nb, 8, 2 * D), lambda i: (i, 0, 0))] + state_specs +
                 [full((CONV_W, 2 * D)), full((1, 2 * D)), full((1, 128)), full((1, D))],
        out_specs=[pl.BlockSpec((nb, 8, D), lambda i: (i, 0, 0))] + state_specs,
        out_shape=[jax.ShapeDtypeStruct((b, 8, D), F32),
                   jax.ShapeDtypeStruct((b, H_A, DK, DK), F32),
                   jax.ShapeDtypeStruct((b, 8, DK), F32),
                   jax.ShapeDtypeStruct((b, 1, 128), F32)],
        scratch_shapes=[pltpu.VMEM((nb, 16, D), F32), pltpu.VMEM((nb, 16, D), F32),
                        pltpu.VMEM((nb, 128, 128), F32), pltpu.VMEM((nb, 128, 128), F32),
                        pltpu.VMEM((nb, 128, D), F32), pltpu.VMEM((nb, 128, D), F32),
                        pltpu.VMEM((nb, 128, D), F32)],
        compiler_params=_cp(("arbitrary",)),
        name="mlstm_step",
    )(main3, main3, main3, main3, tail3, conv0p, c0, n0p, m0p, conv_w, conv_b, gbias, norm_w)


def _bd(x, lo):
    return jnp.concatenate([jnp.where(lo, x, 0.0), jnp.where(lo, 0.0, x)], axis=0)


def _pair_sum(x, lo):
    s_lo = jnp.sum(jnp.where(lo, x, 0.0), axis=1, keepdims=True)
    s_hi = jnp.sum(jnp.where(lo, 0.0, x), axis=1, keepdims=True)
    return jnp.where(lo, s_lo, s_hi)


def _rwkv_kernel(nsub, nbg, lb, tv, has_state, *refs):
    (r_ref, k_ref, v_ref, gb_ref, l_ref, ya_ref), refs = refs[:6], refs[6:]
    if has_state:
        (pr_ref, pk_ref, pv_ref, pl_ref, s0_ref), refs = refs[:5], refs[5:]
    (mur_ref, muk_ref, muv_ref, mul_ref, w0_ref, a0_ref, kk_ref, ka_ref, rk_ref,
     lw_ref, lb_ref, w2_ref, a2_ref, g2_ref,
     u_ref, s_ref, sbd, cr, ck, cv, cl) = refs
    L = nbg * lb
    nseq = nsub * nbg
    LT = nsub * L
    z64 = jnp.zeros((HB, HB), F32)

    @pl.when(pl.program_id(1) == 0)
    def _():
        if has_state:
            for gi in range(nseq):
                for p in range(N_PAIR):
                    top = jnp.concatenate([s0_ref[gi, 2 * p], z64], axis=1)
                    bot = jnp.concatenate([z64, s0_ref[gi, 2 * p + 1]], axis=1)
                    sbd[gi, p] = jnp.concatenate([top, bot], axis=0)
            cr[...] = pr_ref[...].astype(F32)
            ck[...] = pk_ref[...].astype(F32)
            cv[...] = pv_ref[...].astype(F32)
            cl[...] = pl_ref[...]
        else:
            sbd[...] = jnp.zeros_like(sbd)
            for c_ in (cr, ck, cv, cl):
                c_[...] = jnp.zeros_like(c_)

    def shift_mix(x_ref, carry, mu_ref):
        x3 = x_ref[...].astype(F32)
        width = x3.shape[-1]
        tpos = lax.broadcasted_iota(jnp.int32, x3.shape, 1)
        prev = jnp.where(tpos == 0, carry[...], pltpu.roll(x3, 1, 1))
        carry[...] = x3[:, lb - 1:lb, :]
        return (x3 + (prev - x3) * mu_ref[...]).reshape(LT, width)

    xr = shift_mix(r_ref, cr, mur_ref)
    xk = shift_mix(k_ref, ck, muk_ref)
    xv = shift_mix(v_ref, cv, muv_ref)
    xl = shift_mix(l_ref, cl, mul_ref)

    lane_l = lax.broadcasted_iota(jnp.int32, (LT, LORA), 1)
    act = jnp.where(lane_l < 64, jnp.tanh(xl), jnp.where(lane_l < 128, xl, jax.nn.sigmoid(xl))).astype(BF16)
    z = w0_ref[...] + _dot(act, w2_ref[...])
    w_log = -(jnp.maximum(-z, 0.0) + jnp.log1p(jnp.exp(-jnp.abs(z)))) - 0.5
    lw = -jnp.exp(w_log)
    a = jax.nn.sigmoid(a0_ref[...] + _dot(act, a2_ref[...]))
    g = _dot(act, g2_ref[...])
    kk = xk * kk_ref[...]
    kmod = xk * (1.0 + (a - 1.0) * ka_ref[...])
    gate_b = jax.nn.sigmoid(gb_ref[...].astype(F32).reshape(LT, D))
    y_a = ya_ref[...].astype(F32).reshape(LT, D)

    t_idx = lax.broadcasted_iota(jnp.int32, (LT, 1), 0)
    if tv < lb:
        valid = (t_idx % lb) < tv
        lw = jnp.where(valid, lw, 0.0)
        kk = jnp.where(valid, kk, 0.0)
        kmod = jnp.where(valid, kmod, 0.0)
        xv = jnp.where(valid, xv, 0.0)

    row = lax.broadcasted_iota(jnp.int32, (L, L), 0)
    col = lax.broadcasted_iota(jnp.int32, (L, L), 1)
    tril = jnp.where((col <= row) & (col // lb == row // lb), 1.0, 0.0).astype(F32)
    subs = range(nsub)
    rows = [slice(s * L, (s + 1) * L) for s in subs]
    cum_s = [_dot(tril, lw[rows[s]], HIGHEST) for s in subs]

    lane = lax.broadcasted_iota(jnp.int32, (L, 128), 1)
    lo = lane < HB
    src = lane % HB
    trow = lax.broadcasted_iota(jnp.int32, (L, 128), 0)
    same = (src // lb) == (trow // lb)
    strict = same & (src < trow)
    incl = same & (src <= trow)
    r128 = lax.broadcasted_iota(jnp.int32, (128, 128), 0)
    c128 = lax.broadcasted_iota(jnp.int32, (128, 128), 1)
    blockdiag = (r128 < HB) == (c128 < HB)
    eye_pair = jnp.where(src == trow, 1.0, 0.0).astype(F32)

    sls = [slice(p * 128, (p + 1) * 128) for p in range(N_PAIR)]
    items = [(s, p) for s in subs for p in range(N_PAIR)]
    idx = range(len(items))
    groups = range(nbg)

    at_l, rt_l, bt_l, kt_l, win_l, vp_l = [], [], [], [], [], []
    for (s, p) in items:
        rs, sl = rows[s], sls[p]
        kkp = kk[rs, sl]
        nrm = jnp.sqrt(_pair_sum(kkp * kkp, lo))
        kap = kkp / jnp.maximum(nrm, 1e-12)
        cum_p = cum_s[s][:, sl]
        w_in = jnp.exp(cum_p)
        w_inv = jnp.exp(-cum_p)
        at_l.append(-kap * jnp.exp(cum_p - lw[rs, sl]))
        rt_l.append(xr[rs, sl] * w_in)
        bt_l.append(kap * a[rs, sl] * w_inv)
        kt_l.append(kmod[rs, sl] * w_inv)
        win_l.append(w_in)
        vp_l.append(xv[rs, sl])
    bdv_l = [_bd(vp_l[i], lo).astype(BF16) for i in idx]

    gm_l = [_dot_nt(jnp.concatenate([at_l[i], rt_l[i]], axis=0).astype(BF16),
                    jnp.concatenate([_bd(bt_l[i], lo), _bd(kt_l[i], lo)], axis=0).astype(BF16))
            for i in idx]
    n_l = [jnp.where(strict, gm_l[i][0:L, 0:128], 0.0) for i in idx]
    aak_l = [jnp.where(strict, gm_l[i][0:L, 128:256], 0.0).astype(BF16) for i in idx]
    ark_l = [jnp.concatenate([jnp.where(incl, gm_l[i][L:2 * L, 0:128], 0.0),
                              jnp.where(incl, gm_l[i][L:2 * L, 128:256], 0.0)], axis=1).astype(BF16)
             for i in idx]

    xs_l = [[_dot_nt(jnp.concatenate([at_l[i][gi * lb:(gi + 1) * lb], rt_l[i][gi * lb:(gi + 1) * lb]],
                                     axis=0).astype(BF16), sbd[items[i][0] * nbg + gi, items[i][1]].astype(BF16))
             for gi in groups] for i in idx]
    if nbg == 1:
        as_l = [xs_l[i][0][0:lb] for i in idx]
        rs_l = [xs_l[i][0][lb:2 * lb] for i in idx]
    else:
        as_l = [jnp.concatenate([xs_l[i][gi][0:lb] for gi in groups], axis=0) for i in idx]
        rs_l = [jnp.concatenate([xs_l[i][gi][lb:2 * lb] for gi in groups], axis=0) for i in idx]

    y0_l = [as_l[i] + _dot(aak_l[i], bdv_l[i]) for i in idx]

    dm_l = [eye_pair for _ in idx]
    s_blk = 1
    while 2 * s_blk <= lb:
        lvl = ((trow // (2 * s_blk)) == (src // (2 * s_blk))) & ((trow % (2 * s_blk)) >= s_blk) \
            & ((src % (2 * s_blk)) < s_blk)
        if s_blk == 1:
            dm_l = [dm_l[i] + jnp.where(lvl, n_l[i], 0.0) for i in idx]
        else:
            t1_l = [_dot(jnp.where(lvl, n_l[i], 0.0).astype(BF16), _bd(dm_l[i], lo).astype(BF16))
                    for i in idx]
            dm_l = [dm_l[i] + _dot(dm_l[i].astype(BF16), _bd(t1_l[i], lo).astype(BF16)) for i in idx]
        s_blk *= 2
    u_l = [_dot(dm_l[i].astype(BF16), _bd(y0_l[i], lo).astype(BF16)) for i in idx]

    o_l = [rs_l[i] + _dot(ark_l[i], jnp.concatenate([_bd(u_l[i], lo).astype(BF16), bdv_l[i]], axis=0))
           for i in idx]

    w3_l = [win_l[i].reshape(nbg, lb, 128)[:, lb - 1:lb, :] for i in idx]
    rhs_l = []
    for i in idx:
        w_last = jnp.broadcast_to(w3_l[i], (nbg, lb, 128)).reshape(L, 128)
        rhs_l.append(jnp.concatenate([bt_l[i] * w_last, kt_l[i] * w_last], axis=0).astype(BF16))
    uv_l = [jnp.concatenate([u_l[i], vp_l[i]], axis=0) for i in idx]
    if nbg == 1:
        upd_l = [_dot_tn(uv_l[i].astype(BF16), rhs_l[i]) for i in idx]
        for i, (s, p) in enumerate(items):
            sbd[s, p] = sbd[s, p] * w3_l[i][0] + jnp.where(blockdiag, upd_l[i], 0.0)
    else:
        cgrp = (c128 % L) // lb
        uvt_l = [uv_l[i].T for i in idx]
        for i, (s, p) in enumerate(items):
            for gi in groups:
                upd = _dot(jnp.where(cgrp == gi, uvt_l[i], 0.0).astype(BF16), rhs_l[i])
                q_ = s * nbg + gi
                sbd[q_, p] = sbd[q_, p] * w3_l[i][gi] + jnp.where(blockdiag, upd, 0.0)

    out_l = []
    for i, (s, p) in enumerate(items):
        rs, sl = rows[s], sls[p]
        o = o_l[i]
        mu = _pair_sum(o, lo) * (1.0 / HB)
        oc = o - mu
        var = _pair_sum(oc * oc, lo) * (1.0 / HB)
        on = oc * lax.rsqrt(var + RWKV_EPS) * lw_ref[:, sl] + lb_ref[:, sl]
        bonus = _pair_sum(xr[rs, sl] * kmod[rs, sl] * rk_ref[:, sl], lo) * vp_l[i]
        yb = (on + bonus) * g[rs, sl]
        out_l.append(y_a[rs, sl] + gate_b[rs, sl] * yb)
    u_rows = [jnp.concatenate(out_l[s * N_PAIR:(s + 1) * N_PAIR], axis=1) for s in subs]
    u_all = u_rows[0] if nsub == 1 else jnp.concatenate(u_rows, axis=0)
    u_ref[...] = u_all.reshape(nseq, lb, D).astype(u_ref.dtype)

    @pl.when(pl.program_id(1) == pl.num_programs(1) - 1)
    def _():
        for gi in range(nseq):
            for p in range(N_PAIR):
                s_ref[gi, 2 * p] = sbd[gi, p, 0:HB, 0:HB]
                s_ref[gi, 2 * p + 1] = sbd[gi, p, HB:2 * HB, HB:2 * HB]


def _rwkv(main3, tail3, ya3, prev, s0, prm, nsub, nbg, lb, tv):
    b, tp, _ = main3.shape
    has_state = s0 is not None
    nq = nsub * nbg
    blk = lambda j: pl.BlockSpec((nq, lb, D), lambda i, c, j=j: (i, c, j))
    pblk = lambda j: pl.BlockSpec((nq, 1, D), lambda i, c, j=j: (i, 0, j))
    full = lambda a: pl.BlockSpec(a.shape, lambda i, c: (0,) * a.ndim)
    sblk = pl.BlockSpec((nq, H_B, HB, HB), lambda i, c: (i, 0, 0, 0))
    in_specs = [blk(3), blk(4), blk(5), blk(7),
                pl.BlockSpec((nq, lb, LORA), lambda i, c: (i, c, 0)),
                pl.BlockSpec((nq, lb, D), lambda i, c: (i, c, 0))]
    args = [main3, main3, main3, main3, tail3, ya3]
    if has_state:
        in_specs += [pblk(3), pblk(4), pblk(5), pl.BlockSpec((nq, 1, LORA), lambda i, c: (i, 0, 0)), sblk]
        args += [prev[0], prev[0], prev[0], prev[1], s0]
    in_specs += [full(a) for a in prm]
    args += list(prm)
    return pl.pallas_call(
        functools.partial(_rwkv_kernel, nsub, nbg, lb, tv, has_state),
        grid=(b // nq, tp // lb),
        in_specs=in_specs,
        out_specs=[pl.BlockSpec((nq, lb, D), lambda i, c: (i, c, 0)), sblk],
        out_shape=[jax.ShapeDtypeStruct((b, tp, D), main3.dtype),
                   jax.ShapeDtypeStruct((b, H_B, HB, HB), F32)],
        scratch_shapes=[pltpu.VMEM((nq, N_PAIR, 128, 128), F32),
                        pltpu.VMEM((nq, 1, D), F32), pltpu.VMEM((nq, 1, D), F32),
                        pltpu.VMEM((nq, 1, D), F32), pltpu.VMEM((nq, 1, LORA), F32)],
        compiler_params=_cp(("parallel", "arbitrary")),
        name="rwkv",
    )(*args)


def _tail_kernel(u_ref, x_ref, g1_ref, sh_ref, sc_ref, g2_ref, wo_ref, wu_ref, wd_ref,
                 l1g_ref, l1b_ref, l2g_ref, l2b_ref, o_ref, x1_scr, h_scr, acc):
    bb, tt, _ = x_ref.shape
    j = pl.program_id(2)

    @pl.when(j == 0)
    def _():
        u = u_ref[...].reshape(bb * tt, D).astype(BF16)
        y = _dot(u, wo_ref[...]).reshape(bb, tt, D)
        x1 = _layer_norm(ALPHA * x_ref[...] + g1_ref[...] * y, l1g_ref[...], l1b_ref[...])
        x1_scr[...] = x1
        h_scr[...] = (x1 * (1.0 + sc_ref[...]) + sh_ref[...]).reshape(bb * tt, D).astype(BF16)
        acc[...] = jnp.zeros_like(acc)

    up = jnp.maximum(_dot(h_scr[...], wu_ref[...]), 0.0)
    acc[...] += _dot((up * up).astype(BF16), wd_ref[...])

    @pl.when(j == pl.num_programs(2) - 1)
    def _():
        z = ALPHA * x1_scr[...] + g2_ref[...] * acc[...].reshape(bb, tt, D)
        o_ref[...] = _layer_norm(z, l2g_ref[...], l2b_ref[...])


def _tail(u3, x3, mod3, q, bb, tt):
    b, tp, _ = x3.shape
    blk = pl.BlockSpec((bb, tt, D), lambda i, t, j: (i, t, 0))
    mblk = lambda col: pl.BlockSpec((bb, 1, D), lambda i, t, j, col=col: (i, 0, col))
    full = lambda shp: pl.BlockSpec(shp, lambda i, t, j: (0,) * len(shp))
    return pl.pallas_call(
        _tail_kernel,
        grid=(b // bb, tp // tt, D_FF // FF_CHUNK),
        in_specs=[blk, blk, mblk(2), mblk(3), mblk(4), mblk(5),
                  full((D, D)),
                  pl.BlockSpec((D, FF_CHUNK), lambda i, t, j: (0, j)),
                  pl.BlockSpec((FF_CHUNK, D), lambda i, t, j: (j, 0)),
                  full((1, D)), full((1, D)), full((1, D)), full((1, D))],
        out_specs=blk,
        out_shape=jax.ShapeDtypeStruct((b, tp, D), F32),
        scratch_shapes=[pltpu.VMEM((bb, tt, D), F32), pltpu.VMEM((bb * tt, D), BF16),
                        pltpu.VMEM((bb * tt, D), F32)],
        compiler_params=_cp(("parallel", "parallel", "arbitrary")),
        name="outproj_ffn",
    )(u3, x3, mod3, mod3, mod3, mod3, q['w_out'], q['w_up'], q['w_down'],
      q['ln1_g'], q['ln1_b'], q['ln2_g'], q['ln2_b'])


def _relayout_params(p):
    w = p['w_in']
    w_main = jnp.concatenate(
        [w[:, :3 * D], w[:, 3 * D + 8:6 * D + 8], w[:, 6 * D + 8 + LORA:8 * D + 8 + LORA]], axis=1).astype(BF16)
    w_tail = jnp.concatenate(
        [w[:, 6 * D + 8:6 * D + 8 + LORA], w[:, 3 * D:3 * D + 8],
         jnp.zeros((D, N_TAIL - LORA - 8), F32)], axis=1).astype(BF16)
    mu = p['rwkv_mu']
    z64 = jnp.zeros((64, D), F32)
    z128 = jnp.zeros((128, D), F32)
    row = lambda a: a.reshape(1, -1)
    rw = (row(mu[0:D]), row(mu[D:2 * D]), row(mu[2 * D:3 * D]), row(mu[3 * D:3 * D + LORA]),
          row(p['rwkv_w0']), row(p['rwkv_a0']), row(p['rwkv_k_k']), row(p['rwkv_k_a']),
          row(p['rwkv_r_k']), row(p['rwkv_lnx_w']), row(p['rwkv_lnx_b']),
          jnp.concatenate([p['rwkv_w2'], z64, z128], axis=0).astype(BF16),
          jnp.concatenate([z64, p['rwkv_a2'], z128], axis=0).astype(BF16),
          jnp.concatenate([z128, p['rwkv_g2']], axis=0).astype(BF16))
    gbias = jnp.concatenate([p['mlstm_i_bias'], p['mlstm_f_bias'], jnp.zeros((120,), F32)]).reshape(1, 128)
    return dict(w_main=w_main, w_tail=w_tail, rw=rw, gbias=gbias,
                conv_w=p['conv_w'], conv_b=row(p['conv_b']), norm_w=row(p['mlstm_norm_w']),
                w_out=p['w_out'].astype(BF16), w_up=p['w_up'].astype(BF16), w_down=p['w_down'].astype(BF16),
                ln1_g=row(p['ln1_g']), ln1_b=row(p['ln1_b']), ln2_g=row(p['ln2_g']), ln2_b=row(p['ln2_b']))


def _prompt_layer(x, mod, q, seq_tile, mlstm_chunk):
    b, t, _ = x.shape
    mod3 = mod.reshape(b, 1, N_COND)
    main3, tail3 = _inproj(x, mod3, q['w_main'], q['w_tail'], 1, seq_tile, BF16)
    ya3, c1, n1, m1 = _mlstm_seq(main3, tail3, q['conv_w'], q['conv_b'], q['gbias'], q['norm_w'], mlstm_chunk)
    u3, s1 = _rwkv(main3, tail3, ya3, None, None, q['rw'], min(2, b), 1, RW_L, RW_L)
    y = _tail(u3, x, mod3, q, 1, seq_tile)
    shift = _modulate_rows(x[:, t - 1, :], mod)
    conv = main3[:, t - (CONV_W - 1):, :2 * D].astype(F32)
    return y, (c1, n1[:, :H_A, :], m1[:, 0, :H_A], conv, s1, shift)


def _sample_layer(x, mod, st, q, bb):
    c0, n0, m0, conv0, s0, shift0 = st
    b, t, _ = x.shape
    mod3 = mod.reshape(b, 1, N_COND)
    xp = jnp.pad(x, ((0, 0), (0, 8 - t), (0, 0)))
    main3, tail3 = _inproj(xp, mod3, q['w_main'], q['w_tail'], bb, 8, F32)
    pm, pt = _inproj(shift0.reshape(1, b, D), jnp.zeros((1, 1, N_COND), F32), q['w_main'], q['w_tail'], 1, b, F32)
    prev = (pm.reshape(b, 1, N_MAIN), pt.reshape(b, 1, N_TAIL))
    conv0p = jnp.pad(conv0, ((0, 0), (8 - (CONV_W - 1), 0), (0, 0)))
    n0p = jnp.pad(n0, ((0, 0), (0, 8 - H_A), (0, 0)))
    m0p = jnp.pad(m0, ((0, 0), (0, 128 - H_A))).reshape(b, 1, 128)
    ya3, c1, n1, m1 = _mlstm_step(main3, tail3, conv0p, c0, n0p, m0p, q['conv_w'], q['conv_b'], q['gbias'],
                                  q['norm_w'], t, min(4, b))
    u3, s1 = _rwkv(main3, tail3, ya3, prev, s0, q['rw'], 1, RW_L // 8, 8, t)
    y = _tail(u3, xp, mod3, q, bb, 8)
    shift = _modulate_rows(x[:, t - 1, :], mod)
    conv = jnp.concatenate([conv0, main3[:, :t, :2 * D]], axis=1)[:, t:, :]
    return y[:, :t, :], (c1, n1[:, :H_A, :], m1[:, 0, :H_A], conv, s1, shift)


def kernel(x_prompt, x_sample, c_prompt, c_sample, state_mlstm_C, state_mlstm_n, state_mlstm_m, state_mlstm_conv, state_rwkv_S, state_rwkv_shift, w_cond, b_cond, w_in, mlstm_i_bias, mlstm_f_bias, conv_w, conv_b, mlstm_norm_w, rwkv_mu, rwkv_w0, rwkv_w2, rwkv_a0, rwkv_a2, rwkv_g2, rwkv_k_k, rwkv_k_a, rwkv_r_k, rwkv_lnx_w, rwkv_lnx_b, w_out, ln1_g, ln1_b, w_up, w_down, ln2_g, ln2_b):
    depth = w_in.shape[0]
    bp = x_prompt.shape[0]
    yp, ys = x_prompt, x_sample
    new_p = [[] for _ in range(6)]
    new_s = [[] for _ in range(6)]
    for l in range(depth):
        p = {'w_in': w_in[l], 'mlstm_i_bias': mlstm_i_bias[l], 'mlstm_f_bias': mlstm_f_bias[l],
             'conv_w': conv_w[l], 'conv_b': conv_b[l], 'mlstm_norm_w': mlstm_norm_w[l],
             'rwkv_mu': rwkv_mu[l], 'rwkv_w0': rwkv_w0[l], 'rwkv_w2': rwkv_w2[l], 'rwkv_a0': rwkv_a0[l],
             'rwkv_a2': rwkv_a2[l], 'rwkv_g2': rwkv_g2[l], 'rwkv_k_k': rwkv_k_k[l], 'rwkv_k_a': rwkv_k_a[l],
             'rwkv_r_k': rwkv_r_k[l].reshape(-1), 'rwkv_lnx_w': rwkv_lnx_w[l], 'rwkv_lnx_b': rwkv_lnx_b[l],
             'w_out': w_out[l], 'ln1_g': ln1_g[l], 'ln1_b': ln1_b[l], 'w_up': w_up[l], 'w_down': w_down[l],
             'ln2_g': ln2_g[l], 'ln2_b': ln2_b[l]}
        q = _relayout_params(p)
        mod = _cond(jnp.concatenate([c_prompt, c_sample], axis=0), w_cond[l], b_cond[l])
        yp, st_p = _prompt_layer(yp, mod[:bp], q, min(1024, yp.shape[1]), min(256, yp.shape[1]))
        st_in = (state_mlstm_C[l], state_mlstm_n[l], state_mlstm_m[l], state_mlstm_conv[l],
                 state_rwkv_S[l], state_rwkv_shift[l])
        ys, st_s = _sample_layer(ys, mod[bp:], st_in, q, min(64, ys.shape[0]))
        for lst, t in zip(new_p, st_p):
            lst.append(t)
        for lst, t in zip(new_s, st_s):
            lst.append(t)
    outs_p = [jnp.stack(t) for t in new_p]
    outs_s = [jnp.stack(t) for t in new_s]
    return (yp, ys, *outs_p, *outs_s)
```

```python
import functools

import jax
import jax.numpy as jnp
from jax import lax
from jax.experimental import pallas as pl
from jax.experimental.pallas import tpu as pltpu

F32 = jnp.float32
BF16 = jnp.bfloat16
HIGHEST = lax.Precision.HIGHEST

D = 1024
H_A = 4
DK = 256
CONV_W = 4
H_B = 16
HB = 64
N_PAIR = H_B // 2
D_FF = 4096
N_COND = 6 * D
ALPHA = 2.0 ** 0.25
LN_EPS = 1e-5
MLSTM_EPS = 1e-6
RWKV_EPS = 64e-5

N_MAIN = 8 * D
RWKV_SECTIONS = (3, 4, 5, 7)
LORA = 256
TAIL_IF = LORA
N_TAIL = 512
TN_MAIN = 2048
N_MAIN_TILES = N_MAIN // TN_MAIN

RW_L = 64
FF_CHUNK = 1024
NEG = -1e30
VMEM_LIMIT = 56 * 1024 * 1024


def _cp(sem):
    return pltpu.CompilerParams(dimension_semantics=sem, vmem_limit_bytes=VMEM_LIMIT)


def _dot(a, b, prec=None):
    return jnp.dot(a, b, preferred_element_type=F32, precision=prec)


def _dot_nt(a, b, prec=None):
    return lax.dot_general(a, b, (((1,), (1,)), ((), ())), preferred_element_type=F32, precision=prec)


def _dot_tn(a, b, prec=None):
    return lax.dot_general(a, b, (((0,), (0,)), ((), ())), preferred_element_type=F32, precision=prec)


def _log_sigmoid(x):
    return jnp.minimum(x, 0.0) - jnp.log1p(jnp.exp(-jnp.abs(x)))


def _silu(x):
    return x * jax.nn.sigmoid(x)


def _layer_norm(z, g, b):
    mu = jnp.mean(z, axis=-1, keepdims=True)
    zc = z - mu
    var = jnp.mean(zc * zc, axis=-1, keepdims=True)
    return zc * lax.rsqrt(var + LN_EPS) * g + b


def _cond_kernel(c_ref, w_ref, b_ref, o_ref):
    s = _silu(c_ref[...]).astype(BF16)
    o_ref[...] = _dot(s, w_ref[...].astype(BF16)) + b_ref[...]


def _cond(c, w_cond, b_cond):
    n = c.shape[0]
    tn = 512
    return pl.pallas_call(
        _cond_kernel,
        grid=(N_COND // tn,),
        in_specs=[pl.BlockSpec((n, D), lambda j: (0, 0)),
                  pl.BlockSpec((D, tn), lambda j: (0, j)),
                  pl.BlockSpec((1, tn), lambda j: (0, j))],
        out_specs=pl.BlockSpec((n, tn), lambda j: (0, j)),
        out_shape=jax.ShapeDtypeStruct((n, N_COND), F32),
        compiler_params=_cp(("arbitrary",)),
        name="cond",
    )(c, w_cond, b_cond.reshape(1, N_COND))


def _inproj_kernel(x_ref, sh_ref, sc_ref, wm_ref, wt_ref, main_ref, tail_ref, h_scr):
    bb, tt, _ = x_ref.shape
    j = pl.program_id(2)

    @pl.when(j == 0)
    def _():
        h = x_ref[...] * (1.0 + sc_ref[...]) + sh_ref[...]
        h_scr[...] = h.reshape(bb * tt, D).astype(BF16)

    @pl.when(j < N_MAIN_TILES)
    def _():
        main_ref[...] = _dot(h_scr[...], wm_ref[...]).reshape(bb, tt, TN_MAIN).astype(main_ref.dtype)

    @pl.when(j == N_MAIN_TILES)
    def _():
        tail_ref[...] = _dot(h_scr[...], wt_ref[...]).reshape(bb, tt, N_TAIL)


def _inproj(x3, mod3, w_main, w_tail, bb, tt, main_dtype):
    b, tp, _ = x3.shape
    last_main = N_MAIN_TILES - 1
    return pl.pallas_call(
        _inproj_kernel,
        grid=(b // bb, tp // tt, N_MAIN_TILES + 1),
        in_specs=[pl.BlockSpec((bb, tt, D), lambda i, t, j: (i, t, 0)),
                  pl.BlockSpec((bb, 1, D), lambda i, t, j: (i, 0, 0)),
                  pl.BlockSpec((bb, 1, D), lambda i, t, j: (i, 0, 1)),
                  pl.BlockSpec((D, TN_MAIN), lambda i, t, j: (0, jnp.minimum(j, last_main))),
                  pl.BlockSpec((D, N_TAIL), lambda i, t, j: (0, 0))],
        out_specs=[pl.BlockSpec((bb, tt, TN_MAIN), lambda i, t, j: (i, t, jnp.minimum(j, last_main))),
                   pl.BlockSpec((bb, tt, N_TAIL), lambda i, t, j: (i, t, 0))],
        out_shape=[jax.ShapeDtypeStruct((b, tp, N_MAIN), main_dtype),
                   jax.ShapeDtypeStruct((b, tp, N_TAIL), F32)],
        scratch_shapes=[pltpu.VMEM((bb * tt, D), BF16)],
        compiler_params=_cp(("parallel", "parallel", "arbitrary")),
        name="inproj",
    )(x3, mod3, mod3, w_main, w_tail)


def _modulate_kernel(x_ref, sh_ref, sc_ref, o_ref):
    o_ref[...] = x_ref[...] * (1.0 + sc_ref[...]) + sh_ref[...]


def _modulate_rows(x2, mod2):
    n = x2.shape[0]
    return pl.pallas_call(
        _modulate_kernel,
        grid=(1,),
        in_specs=[pl.BlockSpec((n, D), lambda i: (0, 0)),
                  pl.BlockSpec((n, D), lambda i: (0, 0)),
                  pl.BlockSpec((n, D), lambda i: (0, 1))],
        out_specs=pl.BlockSpec((n, D), lambda i: (0, 0)),
        out_shape=jax.ShapeDtypeStruct((n, D), F32),
        name="modulate_last",
    )(x2, mod2, mod2)


def _conv4(pad_ref, n_rows, cw, cb):
    acc = cb + pad_ref[8:8 + n_rows, :] * cw[3:4, :]
    acc = acc + pad_ref[7:7 + n_rows, :] * cw[2:3, :]
    acc = acc + pad_ref[6:6 + n_rows, :] * cw[1:2, :]
    acc = acc + pad_ref[5:5 + n_rows, :] * cw[0:1, :]
    return acc


def _head_norm_rows(h, eps):
    mu = jnp.mean(h, axis=-1, keepdims=True)
    hc = h - mu
    var = jnp.mean(hc * hc, axis=-1, keepdims=True)
    return hc * lax.rsqrt(var + eps)


def _mlstm_seq_kernel(nb, qp_ref, kp_ref, v_ref, ga_ref, if_ref, cw_ref, cb_ref, gb_ref, nw_ref,
                      ya_ref, c_ref, n_ref, m_ref, haloq, halok):
    L = qp_ref.shape[1]
    assert qp_ref.dtype == BF16 and kp_ref.dtype == BF16

    @pl.when(pl.program_id(1) == 0)
    def _():
        c_ref[...] = jnp.zeros_like(c_ref)
        n_ref[...] = jnp.zeros_like(n_ref)
        m_ref[...] = jnp.zeros_like(m_ref)
        haloq[...] = jnp.zeros_like(haloq)
        halok[...] = jnp.zeros_like(halok)

    cw = cw_ref[...]
    cb = cb_ref[...]
    nw = nw_ref[...]
    row = lax.broadcasted_iota(jnp.int32, (L, L), 0)
    col = lax.broadcasted_iota(jnp.int32, (L, L), 1)
    causal = col <= row
    tril = jnp.where(causal, 1.0, 0.0).astype(F32)
    srow = lax.broadcasted_iota(jnp.int32, (3 * L, L), 0)
    scol = lax.broadcasted_iota(jnp.int32, (3 * L, L), 1)
    shift_mat = jnp.where(scol + srow // L + 1 == srow % L, 1.0, 0.0).astype(BF16)
    r8 = lax.broadcasted_iota(jnp.int32, (8, 1), 0)

    def conv(x_bf, halo_ref, cw_, cb_):
        sh = _dot(shift_mat, x_bf)
        x = x_bf.astype(F32)
        acc = cb_ + x * cw_[3:4, :] + sh[0:L] * cw_[2:3, :] + sh[L:2 * L] * cw_[1:2, :] \
            + sh[2 * L:3 * L] * cw_[0:1, :]
        halo = halo_ref[...]
        fix = jnp.zeros((8, DK), F32)
        for j in range(1, CONV_W):
            fix = fix + jnp.where(r8 < j, pltpu.roll(halo, j, 0), 0.0) * cw_[CONV_W - 1 - j:CONV_W - j, :]
        halo_ref[...] = x[L - 8:L, :]
        return jnp.concatenate([acc[0:8] + fix, acc[8:L]], axis=0)

    items = [(bi, h) for bi in range(nb) for h in range(H_A)]
    sl_of = lambda h: slice(h * DK, (h + 1) * DK)
    q_it, k_it = {}, {}
    for (bi, h) in items:
        sl, ksl = sl_of(h), slice(D + h * DK, D + (h + 1) * DK)
        q_it[bi, h] = _silu(conv(qp_ref[bi, :, sl], haloq.at[bi, :, sl], cw[:, sl], cb[:, sl]))
        k_it[bi, h] = _silu(conv(kp_ref[bi, :, sl], halok.at[bi, :, sl], cw[:, ksl], cb[:, ksl])) * (DK ** -0.5)

    gpre_l, bcum_l, gpre_t_l, bcum_t_l = [], [], [], []
    for bi in range(nb):
        gpre = if_ref[bi] + gb_ref[...]
        bcum = _dot(tril, _log_sigmoid(gpre), HIGHEST)
        gpre_l.append(gpre)
        bcum_l.append(bcum)
        gpre_t_l.append(gpre.T)
        bcum_t_l.append(bcum.T)

    st = {}
    for (bi, h) in items:
        ig_col = gpre_l[bi][:, h:h + 1]
        b_col = bcum_l[bi][:, H_A + h:H_A + h + 1]
        ig_row = gpre_t_l[bi][h:h + 1, :]
        b_row = bcum_t_l[bi][H_A + h:H_A + h + 1, :]
        m_prev = m_ref[bi][:, h:h + 1]
        g_col = b_col + m_prev
        dlog = jnp.where(causal, b_col - b_row + ig_row, NEG)
        m_t = jnp.maximum(g_col, jnp.max(dlog, axis=1, keepdims=True))
        b_last = b_col[L - 1:L, :]
        wlog = b_last - b_col + ig_col
        m_new = jnp.maximum(b_last + m_prev, jnp.max(wlog, axis=0, keepdims=True))
        st[bi, h] = dict(m_t=m_t, w_inter=jnp.exp(g_col - m_t), p=jnp.exp(dlog - m_t), m_new=m_new,
                         decay=jnp.exp(b_last + m_prev - m_new), wts=jnp.exp(wlog - m_new))
    qb = {it: q_it[it].astype(BF16) for it in items}
    kb = {it: k_it[it].astype(BF16) for it in items}
    s_l = {it: _dot_nt(qb[it], kb[it]) * st[it]['p'] for it in items}
    qc_l = {(bi, h): _dot_nt(qb[bi, h], c_ref[bi, h].astype(BF16)) for (bi, h) in items}
    sv_l = {(bi, h): _dot(s_l[bi, h].astype(BF16), v_ref[bi, :, sl_of(h)].astype(BF16)) for (bi, h) in items}
    upd_l = {(bi, h): _dot_tn((st[bi, h]['wts'] * v_ref[bi, :, sl_of(h)].astype(F32)).astype(BF16), kb[bi, h])
             for (bi, h) in items}
    for (bi, h) in items:
        sl = sl_of(h)
        d = st[bi, h]
        qh = q_it[bi, h]
        kh = k_it[bi, h]
        nh = n_ref[bi, h:h + 1, :]
        num = d['w_inter'] * qc_l[bi, h] + sv_l[bi, h]
        den = d['w_inter'] * jnp.sum(qh * nh, axis=1, keepdims=True) + jnp.sum(s_l[bi, h], axis=1, keepdims=True)
        hh = num / jnp.maximum(jnp.abs(den), jnp.exp(-d['m_t']))
        ga = jax.nn.sigmoid(ga_ref[bi, :, sl].astype(F32))
        ya_ref[bi, :, sl] = (ga * _head_norm_rows(hh, MLSTM_EPS) * nw[:, sl]).astype(ya_ref.dtype)
        c_ref[bi, h] = d['decay'] * c_ref[bi, h] + upd_l[bi, h]
        n_ref[bi, h:h + 1, :] = d['decay'] * nh + jnp.sum(d['wts'] * kh, axis=0, keepdims=True)
        m_ref[bi, :, h:h + 1] = d['m_new']


def _mlstm_seq(main3, tail3, conv_w, conv_b, gbias, norm_w, L, nb):
    b, tp, _ = main3.shape
    blk = lambda j: pl.BlockSpec((nb, L, D), lambda i, c, j=j: (i, c, j))
    full = lambda shp: pl.BlockSpec(shp, lambda i, c: (0,) * len(shp))
    return pl.pallas_call(
        functools.partial(_mlstm_seq_kernel, nb),
        grid=(b // nb, tp // L),
        in_specs=[blk(0), blk(1), blk(2), blk(6),
                  pl.BlockSpec((nb, L, 128), lambda i, c: (i, c, TAIL_IF // 128)),
                  full((CONV_W, 2 * D)), full((1, 2 * D)), full((1, 128)), full((1, D))],
        out_specs=[pl.BlockSpec((nb, L, D), lambda i, c: (i, c, 0)),
                   pl.BlockSpec((nb, H_A, DK, DK), lambda i, c: (i, 0, 0, 0)),
                   pl.BlockSpec((nb, 8, DK), lambda i, c: (i, 0, 0)),
                   pl.BlockSpec((nb, 1, 128), lambda i, c: (i, 0, 0))],
        out_shape=[jax.ShapeDtypeStruct((b, tp, D), main3.dtype),
                   jax.ShapeDtypeStruct((b, H_A, DK, DK), F32),
                   jax.ShapeDtypeStruct((b, 8, DK), F32),
                   jax.ShapeDtypeStruct((b, 1, 128), F32)],
        scratch_shapes=[pltpu.VMEM((nb, 8, D), F32), pltpu.VMEM((nb, 8, D), F32)],
        compiler_params=_cp(("parallel", "arbitrary")),
        name="mlstm_seq",
    )(main3, main3, main3, main3, tail3, conv_w, conv_b, gbias, norm_w)


def _mlstm_step_kernel(tv, nb, qp_ref, kp_ref, v_ref, ga_ref, if_ref, conv0_ref, c0_ref, n0_ref, m0_ref,
                       cw_ref, cb_ref, gb_ref, nw_ref,
                       ya_ref, c_ref, n_ref, m_ref, padq, padk, gpad, lpad, kpad, vpad, wvpad):
    @pl.when(pl.program_id(0) == 0)
    def _():
        for r in (gpad, lpad, kpad, vpad, wvpad):
            r[...] = jnp.zeros_like(r)

    cw = cw_ref[...]
    cb = cb_ref[...]
    nw = nw_ref[...]
    gb = gb_ref[...]
    trow = lax.broadcasted_iota(jnp.int32, (8, 128), 0)
    scol = lax.broadcasted_iota(jnp.int32, (8, 128), 1)
    mask = (scol <= trow) & (scol < tv)
    rvalid = lax.broadcasted_iota(jnp.int32, (8, 1), 0) < tv
    r128 = lax.broadcasted_iota(jnp.int32, (128, 128), 0)
    c128 = lax.broadcasted_iota(jnp.int32, (128, 128), 1)
    tril = jnp.where(c128 <= r128, 1.0, 0.0).astype(F32)
    n_ref[...] = jnp.zeros_like(n_ref)
    m_ref[...] = jnp.zeros_like(m_ref)

    batches = range(nb)
    q_l, gpre_l, bcol_l, gt_l, bt_l, ga_l = [], [], [], [], [], []
    for bi in batches:
        padq[bi, 0:8, :] = conv0_ref[bi, :, 0:D]
        padk[bi, 0:8, :] = conv0_ref[bi, :, D:2 * D]
        padq[bi, 8:16, :] = qp_ref[bi]
        padk[bi, 8:16, :] = kp_ref[bi]
        q_l.append(_silu(_conv4(padq.at[bi], 8, cw[:, 0:D], cb[:, 0:D])))
        kpad[bi, 0:8, :] = _silu(_conv4(padk.at[bi], 8, cw[:, D:2 * D], cb[:, D:2 * D])) * (DK ** -0.5)
        vpad[bi, 0:8, :] = v_ref[bi]
        gpre = if_ref[bi] + gb
        gpad[bi, 0:8, :] = gpre
        lpad[bi, 0:8, :] = _log_sigmoid(gpre)
        gpre_l.append(gpre)
        ga_l.append(jax.nn.sigmoid(ga_ref[bi]))
    for bi in batches:
        bpad = _dot(tril, lpad[bi], HIGHEST)
        bcol_l.append(bpad[0:8, :])
        bt_l.append(bpad.T)
        gt_l.append(gpad[bi].T)

    probs = [(bi, h) for bi in batches for h in range(H_A)]
    sl_of = lambda h: slice(h * DK, (h + 1) * DK)
    st = {}
    for (bi, h) in probs:
        ig_col = gpre_l[bi][:, h:h + 1]
        b_col = bcol_l[bi][:, H_A + h:H_A + h + 1]
        ig_row = gt_l[bi][h:h + 1, :]
        b_row = bt_l[bi][H_A + h:H_A + h + 1, :]
        m_prev = m0_ref[bi][:, h:h + 1]
        g_col = b_col + m_prev
        dlog = jnp.where(mask, b_col - b_row + ig_row, NEG)
        m_t = jnp.maximum(g_col, jnp.max(dlog, axis=1, keepdims=True))
        b_last = b_col[tv - 1:tv, :]
        wlog = jnp.where(rvalid, b_last - b_col + ig_col, NEG)
        m_new = jnp.maximum(b_last + m_prev, jnp.max(wlog, axis=0, keepdims=True))
        wts = jnp.exp(wlog - m_new)
        wvpad[bi, 0:8, sl_of(h)] = wts * vpad[bi, 0:8, sl_of(h)]
        st[bi, h] = dict(m_t=m_t, w_inter=jnp.exp(g_col - m_t), pm=jnp.exp(dlog - m_t), m_new=m_new,
                         decay=jnp.exp(b_last + m_prev - m_new), wts=wts)
    kb = {(bi, h): kpad[bi, :, sl_of(h)].astype(BF16) for (bi, h) in probs}
    qb = {(bi, h): q_l[bi][:, sl_of(h)].astype(BF16) for (bi, h) in probs}
    s_l = {k_: _dot_nt(qb[k_], kb[k_]) * st[k_]['pm'] for k_ in probs}
    qc_l = {(bi, h): _dot_nt(qb[bi, h], c0_ref[bi, h].astype(BF16)) for (bi, h) in probs}
    sv_l = {(bi, h): _dot(s_l[bi, h].astype(BF16), vpad[bi, :, sl_of(h)].astype(BF16)) for (bi, h) in probs}
    upd_l = {(bi, h): _dot(wvpad[bi, :, sl_of(h)].T.astype(BF16), kb[bi, h]) for (bi, h) in probs}
    for (bi, h) in probs:
        sl = sl_of(h)
        d = st[bi, h]
        nh = n0_ref[bi, h:h + 1, :]
        qh = q_l[bi][:, sl]
        num = d['w_inter'] * qc_l[bi, h] + sv_l[bi, h]
        den = d['w_inter'] * jnp.sum(qh * nh, axis=1, keepdims=True) + jnp.sum(s_l[bi, h], axis=1, keepdims=True)
        hh = num / jnp.maximum(jnp.abs(den), jnp.exp(-d['m_t']))
        ya_ref[bi, :, sl] = ga_l[bi][:, sl] * _head_norm_rows(hh, MLSTM_EPS) * nw[:, sl]
        c_ref[bi, h] = d['decay'] * c0_ref[bi, h] + upd_l[bi, h]
        n_ref[bi, h:h + 1, :] = d['decay'] * nh + jnp.sum(d['wts'] * kpad[bi, 0:8, sl], axis=0, keepdims=True)
        m_ref[bi, :, h:h + 1] = d['m_new']


def _mlstm_step(main3, tail3, conv0p, c0, n0p, m0p, conv_w, conv_b, gbias, norm_w, tv, nb):
    b = main3.shape[0]
    blk = lambda j: pl.BlockSpec((nb, 8, D), lambda i, j=j: (i, 0, j))
    full = lambda shp: pl.BlockSpec(shp, lambda i: (0,) * len(shp))
    state_specs = [pl.BlockSpec((nb, H_A, DK, DK), lambda i: (i, 0, 0, 0)),
                   pl.BlockSpec((nb, 8, DK), lambda i: (i, 0, 0)),
                   pl.BlockSpec((nb, 1, 128), lambda i: (i, 0, 0))]
    return pl.pallas_call(
        functools.partial(_mlstm_step_kernel, tv, nb),
        grid=(b // nb,),
        in_specs=[blk(0), blk(1), blk(2), blk(6),
                  pl.BlockSpec((nb, 8, 128), lambda i: (i, 0, TAIL_IF // 128)),
                  pl.BlockSpec((nb, 8, 2 * D), lambda i: (i, 0, 0))] + state_specs +
                 [full((CONV_W, 2 * D)), full((1, 2 * D)), full((1, 128)), full((1, D))],
        out_specs=[pl.BlockSpec((nb, 8, D), lambda i: (i, 0, 0))] + state_specs,
        out_shape=[jax.ShapeDtypeStruct((b, 8, D), F32),
                   jax.ShapeDtypeStruct((b, H_A, DK, DK), F32),
                   jax.ShapeDtypeStruct((b, 8, DK), F32),
                   jax.ShapeDtypeStruct((b, 1, 128), F32)],
        scratch_shapes=[pltpu.VMEM((nb, 16, D), F32), pltpu.VMEM((nb, 16, D), F32),
                        pltpu.VMEM((nb, 128, 128), F32), pltpu.VMEM((nb, 128, 128), F32),
                        pltpu.VMEM((nb, 128, D), F32), pltpu.VMEM((nb, 128, D), F32),
                        pltpu.VMEM((nb, 128, D), F32)],
        compiler_params=_cp(("arbitrary",)),
        name="mlstm_step",
    )(main3, main3, main3, main3, tail3, conv0p, c0, n0p, m0p, conv_w, conv_b, gbias, norm_w)


def _bd(x, lo):
    return jnp.concatenate([jnp.where(lo, x, 0.0), jnp.where(lo, 0.0, x)], axis=0)


def _pair_sum(x, lo):
    s_lo = jnp.sum(jnp.where(lo, x, 0.0), axis=1, keepdims=True)
    s_hi = jnp.sum(jnp.where(lo, 0.0, x), axis=1, keepdims=True)
    return jnp.where(lo, s_lo, s_hi)


def _rwkv_kernel(nsub, nbg, lb, tv, has_state, *refs):
    (r_ref, k_ref, v_ref, gb_ref, l_ref, ya_ref), refs = refs[:6], refs[6:]
    if has_state:
        (pr_ref, pk_ref, pv_ref, pl_ref, s0_ref), refs = refs[:5], refs[5:]
    (mur_ref, muk_ref, muv_ref, mul_ref, w0_ref, a0_ref, kk_ref, ka_ref, rk_ref,
     lw_ref, lb_ref, w2_ref, a2_ref, g2_ref,
     u_ref, s_ref, sbd, cr, ck, cv, cl) = refs
    L = nbg * lb
    nseq = nsub * nbg
    LT = nsub * L
    z64 = jnp.zeros((HB, HB), F32)

    @pl.when(pl.program_id(1) == 0)
    def _():
        if has_state:
            for gi in range(nseq):
                for p in range(N_PAIR):
                    top = jnp.concatenate([s0_ref[gi, 2 * p], z64], axis=1)
                    bot = jnp.concatenate([z64, s0_ref[gi, 2 * p + 1]], axis=1)
                    sbd[gi, p] = jnp.concatenate([top, bot], axis=0)
            cr[...] = pr_ref[...].astype(F32)
            ck[...] = pk_ref[...].astype(F32)
            cv[...] = pv_ref[...].astype(F32)
            cl[...] = pl_ref[...]
        else:
            sbd[...] = jnp.zeros_like(sbd)
            for c_ in (cr, ck, cv, cl):
                c_[...] = jnp.zeros_like(c_)

    def shift_mix(x_ref, carry, mu_ref):
        x3 = x_ref[...].astype(F32)
        width = x3.shape[-1]
        tpos = lax.broadcasted_iota(jnp.int32, x3.shape, 1)
        prev = jnp.where(tpos == 0, carry[...], pltpu.roll(x3, 1, 1))
        carry[...] = x3[:, lb - 1:lb, :]
        return (x3 + (prev - x3) * mu_ref[...]).reshape(LT, width)

    xr = shift_mix(r_ref, cr, mur_ref)
    xk = shift_mix(k_ref, ck, muk_ref)
    xv = shift_mix(v_ref, cv, muv_ref)
    xl = shift_mix(l_ref, cl, mul_ref)

    lane_l = lax.broadcasted_iota(jnp.int32, (LT, LORA), 1)
    act = jnp.where(lane_l < 64, jnp.tanh(xl), jnp.where(lane_l < 128, xl, jax.nn.sigmoid(xl))).astype(BF16)
    z = w0_ref[...] + _dot(act, w2_ref[...])
    w_log = -(jnp.maximum(-z, 0.0) + jnp.log(1.0 + jnp.exp(-jnp.abs(z)))) - 0.5
    lw = -jnp.exp(w_log)
    a = jax.nn.sigmoid(a0_ref[...] + _dot(act, a2_ref[...]))
    g = _dot(act, g2_ref[...])
    kk = xk * kk_ref[...]
    kmod = xk * (1.0 + (a - 1.0) * ka_ref[...])
    gate_b = jax.nn.sigmoid(gb_ref[...].astype(F32).reshape(LT, D))
    y_a = ya_ref[...].astype(F32).reshape(LT, D)

    t_idx = lax.broadcasted_iota(jnp.int32, (LT, 1), 0)
    if tv < lb:
        valid = (t_idx % lb) < tv
        lw = jnp.where(valid, lw, 0.0)
        kk = jnp.where(valid, kk, 0.0)
        kmod = jnp.where(valid, kmod, 0.0)
        xv = jnp.where(valid, xv, 0.0)

    row = lax.broadcasted_iota(jnp.int32, (L, L), 0)
    col = lax.broadcasted_iota(jnp.int32, (L, L), 1)
    tril = jnp.where((col <= row) & (col // lb == row // lb), 1.0, 0.0).astype(F32)
    subs = range(nsub)
    rows = [slice(s * L, (s + 1) * L) for s in subs]
    cum_s = [_dot(tril, lw[rows[s]], HIGHEST) for s in subs]

    lane = lax.broadcasted_iota(jnp.int32, (L, 128), 1)
    lo = lane < HB
    src = lane % HB
    trow = lax.broadcasted_iota(jnp.int32, (L, 128), 0)
    same = (src // lb) == (trow // lb)
    strict = same & (src < trow)
    incl = same & (src <= trow)
    r128 = lax.broadcasted_iota(jnp.int32, (128, 128), 0)
    c128 = lax.broadcasted_iota(jnp.int32, (128, 128), 1)
    blockdiag = (r128 < HB) == (c128 < HB)
    eye_pair = jnp.where(src == trow, 1.0, 0.0).astype(F32)

    sls = [slice(p * 128, (p + 1) * 128) for p in range(N_PAIR)]
    items = [(s, p) for s in subs for p in range(N_PAIR)]
    idx = range(len(items))
    groups = range(nbg)

    at_l, rt_l, bt_l, kt_l, win_l, vp_l = [], [], [], [], [], []
    for (s, p) in items:
        rs, sl = rows[s], sls[p]
        kkp = kk[rs, sl]
        kap = kkp * lax.rsqrt(jnp.maximum(_pair_sum(kkp * kkp, lo), 1e-24))
        cum_p = cum_s[s][:, sl]
        w_in = jnp.exp(cum_p)
        w_inv = jnp.exp(-cum_p)
        at_l.append(-kap * jnp.exp(cum_p - lw[rs, sl]))
        rt_l.append(xr[rs, sl] * w_in)
        bt_l.append(kap * a[rs, sl] * w_inv)
        kt_l.append(kmod[rs, sl] * w_inv)
        win_l.append(w_in)
        vp_l.append(xv[rs, sl])
    bdv_l = [_bd(vp_l[i], lo).astype(BF16) for i in idx]

    gm_l = [_dot_nt(jnp.concatenate([at_l[i], rt_l[i]], axis=0).astype(BF16),
                    jnp.concatenate([_bd(bt_l[i], lo), _bd(kt_l[i], lo)], axis=0).astype(BF16))
            for i in idx]
    n_l = [jnp.where(strict, gm_l[i][0:L, 0:128], 0.0) for i in idx]
    aak_l = [jnp.where(strict, gm_l[i][0:L, 128:256], 0.0).astype(BF16) for i in idx]
    ark_l = [jnp.concatenate([jnp.where(incl, gm_l[i][L:2 * L, 0:128], 0.0),
                              jnp.where(incl, gm_l[i][L:2 * L, 128:256], 0.0)], axis=1).astype(BF16)
             for i in idx]

    xs_l = [[_dot_nt(jnp.concatenate([at_l[i][gi * lb:(gi + 1) * lb], rt_l[i][gi * lb:(gi + 1) * lb]],
                                     axis=0).astype(BF16), sbd[items[i][0] * nbg + gi, items[i][1]].astype(BF16))
             for gi in groups] for i in idx]
    if nbg == 1:
        as_l = [xs_l[i][0][0:lb] for i in idx]
        rs_l = [xs_l[i][0][lb:2 * lb] for i in idx]
    else:
        as_l = [jnp.concatenate([xs_l[i][gi][0:lb] for gi in groups], axis=0) for i in idx]
        rs_l = [jnp.concatenate([xs_l[i][gi][lb:2 * lb] for gi in groups], axis=0) for i in idx]

    y0_l = [as_l[i] + _dot(aak_l[i], bdv_l[i]) for i in idx]

    dm_l = [eye_pair for _ in idx]
    s_blk = 1
    while 2 * s_blk <= lb:
        lvl = ((trow // (2 * s_blk)) == (src // (2 * s_blk))) & ((trow % (2 * s_blk)) >= s_blk) \
            & ((src % (2 * s_blk)) < s_blk)
        if s_blk == 1:
            dm_l = [dm_l[i] + jnp.where(lvl, n_l[i], 0.0) for i in idx]
        else:
            t1_l = [_dot(jnp.where(lvl, n_l[i], 0.0).astype(BF16), _bd(dm_l[i], lo).astype(BF16))
                    for i in idx]
            dm_l = [dm_l[i] + _dot(dm_l[i].astype(BF16), _bd(t1_l[i], lo).astype(BF16)) for i in idx]
        s_blk *= 2
    u_l = [_dot(dm_l[i].astype(BF16), _bd(y0_l[i], lo).astype(BF16)) for i in idx]

    o_l = [rs_l[i] + _dot(ark_l[i], jnp.concatenate([_bd(u_l[i], lo).astype(BF16), bdv_l[i]], axis=0))
           for i in idx]

    w3_l = [win_l[i].reshape(nbg, lb, 128)[:, lb - 1:lb, :] for i in idx]
    rhs_l = []
    for i in idx:
        w_last = jnp.broadcast_to(w3_l[i], (nbg, lb, 128)).reshape(L, 128)
        rhs_l.append(jnp.concatenate([bt_l[i] * w_last, kt_l[i] * w_last], axis=0).astype(BF16))
    uv_l = [jnp.concatenate([u_l[i], vp_l[i]], axis=0) for i in idx]
    if nbg == 1:
        upd_l = [_dot_tn(uv_l[i].astype(BF16), rhs_l[i]) for i in idx]
        for i, (s, p) in enumerate(items):
            sbd[s, p] = sbd[s, p] * w3_l[i][0] + jnp.where(blockdiag, upd_l[i], 0.0)
    else:
        cgrp = (c128 % L) // lb
        uvt_l = [uv_l[i].T for i in idx]
        for i, (s, p) in enumerate(items):
            for gi in groups:
                upd = _dot(jnp.where(cgrp == gi, uvt_l[i], 0.0).astype(BF16), rhs_l[i])
                q_ = s * nbg + gi
                sbd[q_, p] = sbd[q_, p] * w3_l[i][gi] + jnp.where(blockdiag, upd, 0.0)

    out_l = []
    for i, (s, p) in enumerate(items):
        rs, sl = rows[s], sls[p]
        o = o_l[i]
        mu = _pair_sum(o, lo) * (1.0 / HB)
        oc = o - mu
        var = _pair_sum(oc * oc, lo) * (1.0 / HB)
        on = oc * lax.rsqrt(var + RWKV_EPS) * lw_ref[:, sl] + lb_ref[:, sl]
        bonus = _pair_sum(xr[rs, sl] * kmod[rs, sl] * rk_ref[:, sl], lo) * vp_l[i]
        yb = (on + bonus) * g[rs, sl]
        out_l.append(y_a[rs, sl] + gate_b[rs, sl] * yb)
    u_rows = [jnp.concatenate(out_l[s * N_PAIR:(s + 1) * N_PAIR], axis=1) for s in subs]
    u_all = u_rows[0] if nsub == 1 else jnp.concatenate(u_rows, axis=0)
    u_ref[...] = u_all.reshape(nseq, lb, D).astype(u_ref.dtype)

    @pl.when(pl.program_id(1) == pl.num_programs(1) - 1)
    def _():
        for gi in range(nseq):
            for p in range(N_PAIR):
                s_ref[gi, 2 * p] = sbd[gi, p, 0:HB, 0:HB]
                s_ref[gi, 2 * p + 1] = sbd[gi, p, HB:2 * HB, HB:2 * HB]


def _rwkv(main3, cols, tail3, ya3, prev, s0, prm, nsub, nbg, lb, tv):
    b, tp, _ = main3.shape
    has_state = s0 is not None
    nq = nsub * nbg
    blk = lambda j: pl.BlockSpec((nq, lb, D), lambda i, c, j=j: (i, c, j))
    pblk = lambda j: pl.BlockSpec((nq, 1, D), lambda i, c, j=j: (i, 0, j))
    full = lambda a: pl.BlockSpec(a.shape, lambda i, c: (0,) * a.ndim)
    sblk = pl.BlockSpec((nq, H_B, HB, HB), lambda i, c: (i, 0, 0, 0))
    c_r, c_k, c_v, c_gb = cols
    in_specs = [blk(c_r), blk(c_k), blk(c_v), blk(c_gb),
                pl.BlockSpec((nq, lb, LORA), lambda i, c: (i, c, 0)),
                pl.BlockSpec((nq, lb, D), lambda i, c: (i, c, 0))]
    args = [main3, main3, main3, main3, tail3, ya3]
    if has_state:
        in_specs += [pblk(c_r), pblk(c_k), pblk(c_v), pl.BlockSpec((nq, 1, LORA), lambda i, c: (i, 0, 0)), sblk]
        args += [prev[0], prev[0], prev[0], prev[1], s0]
    in_specs += [full(a) for a in prm]
    args += list(prm)
    return pl.pallas_call(
        functools.partial(_rwkv_kernel, nsub, nbg, lb, tv, has_state),
        grid=(b // nq, tp // lb),
        in_specs=in_specs,
        out_specs=[pl.BlockSpec((nq, lb, D), lambda i, c: (i, c, 0)), sblk],
        out_shape=[jax.ShapeDtypeStruct((b, tp, D), main3.dtype),
                   jax.ShapeDtypeStruct((b, H_B, HB, HB), F32)],
        scratch_shapes=[pltpu.VMEM((nq, N_PAIR, 128, 128), F32),
                        pltpu.VMEM((nq, 1, D), F32), pltpu.VMEM((nq, 1, D), F32),
                        pltpu.VMEM((nq, 1, D), F32), pltpu.VMEM((nq, 1, LORA), F32)],
        compiler_params=_cp(("parallel", "arbitrary")),
        name="rwkv",
    )(*args)


def _tail_kernel(u_ref, x_ref, g1_ref, sh_ref, sc_ref, g2_ref, wo_ref, wu_ref, wd_ref,
                 l1g_ref, l1b_ref, l2g_ref, l2b_ref, o_ref, x1_scr, h_scr, acc):
    bb, tt, _ = x_ref.shape
    j = pl.program_id(2)

    @pl.when(j == 0)
    def _():
        u = u_ref[...].reshape(bb * tt, D).astype(BF16)
        y = _dot(u, wo_ref[...]).reshape(bb, tt, D)
        x1 = _layer_norm(ALPHA * x_ref[...] + g1_ref[...] * y, l1g_ref[...], l1b_ref[...])
        x1_scr[...] = x1
        h_scr[...] = (x1 * (1.0 + sc_ref[...]) + sh_ref[...]).reshape(bb * tt, D).astype(BF16)
        acc[...] = jnp.zeros_like(acc)

    up = jnp.maximum(_dot(h_scr[...], wu_ref[...]), 0.0)
    acc[...] += _dot((up * up).astype(BF16), wd_ref[...])

    @pl.when(j == pl.num_programs(2) - 1)
    def _():
        z = ALPHA * x1_scr[...] + g2_ref[...] * acc[...].reshape(bb, tt, D)
        o_ref[...] = _layer_norm(z, l2g_ref[...], l2b_ref[...])


def _tail(u3, x3, mod3, q, bb, tt):
    b, tp, _ = x3.shape
    blk = pl.BlockSpec((bb, tt, D), lambda i, t, j: (i, t, 0))
    mblk = lambda col: pl.BlockSpec((bb, 1, D), lambda i, t, j, col=col: (i, 0, col))
    full = lambda shp: pl.BlockSpec(shp, lambda i, t, j: (0,) * len(shp))
    return pl.pallas_call(
        _tail_kernel,
        grid=(b // bb, tp // tt, D_FF // FF_CHUNK),
        in_specs=[blk, blk, mblk(2), mblk(3), mblk(4), mblk(5),
                  full((D, D)),
                  pl.BlockSpec((D, FF_CHUNK), lambda i, t, j: (0, j)),
                  pl.BlockSpec((FF_CHUNK, D), lambda i, t, j: (j, 0)),
                  full((1, D)), full((1, D)), full((1, D)), full((1, D))],
        out_specs=blk,
        out_shape=jax.ShapeDtypeStruct((b, tp, D), F32),
        scratch_shapes=[pltpu.VMEM((bb, tt, D), F32), pltpu.VMEM((bb * tt, D), BF16),
                        pltpu.VMEM((bb * tt, D), F32)],
        compiler_params=_cp(("parallel", "parallel", "arbitrary")),
        name="outproj_ffn",
    )(u3, x3, mod3, mod3, mod3, mod3, q['w_out'], q['w_up'], q['w_down'],
      q['ln1_g'], q['ln1_b'], q['ln2_g'], q['ln2_b'])


def _relayout_params(p):
    w = p['w_in']
    w_main = jnp.concatenate(
        [w[:, :3 * D], w[:, 3 * D + 8:6 * D + 8], w[:, 6 * D + 8 + LORA:8 * D + 8 + LORA]], axis=1).astype(BF16)
    w_tail = jnp.concatenate(
        [w[:, 6 * D + 8:6 * D + 8 + LORA], w[:, 3 * D:3 * D + 8],
         jnp.zeros((D, N_TAIL - LORA - 8), F32)], axis=1).astype(BF16)
    mu = p['rwkv_mu']
    z64 = jnp.zeros((64, D), F32)
    z128 = jnp.zeros((128, D), F32)
    row = lambda a: a.reshape(1, -1)
    rw = (row(mu[0:D]), row(mu[D:2 * D]), row(mu[2 * D:3 * D]), row(mu[3 * D:3 * D + LORA]),
          row(p['rwkv_w0']), row(p['rwkv_a0']), row(p['rwkv_k_k']), row(p['rwkv_k_a']),
          row(p['rwkv_r_k']), row(p['rwkv_lnx_w']), row(p['rwkv_lnx_b']),
          jnp.concatenate([p['rwkv_w2'], z64, z128], axis=0).astype(BF16),
          jnp.concatenate([z64, p['rwkv_a2'], z128], axis=0).astype(BF16),
          jnp.concatenate([z128, p['rwkv_g2']], axis=0).astype(BF16))
    gbias = jnp.concatenate([p['mlstm_i_bias'], p['mlstm_f_bias'], jnp.zeros((120,), F32)]).reshape(1, 128)
    return dict(w_main=w_main, w_tail=w_tail, rw=rw, gbias=gbias,
                conv_w=p['conv_w'], conv_b=row(p['conv_b']), norm_w=row(p['mlstm_norm_w']),
                w_out=p['w_out'].astype(BF16), w_up=p['w_up'].astype(BF16), w_down=p['w_down'].astype(BF16),
                ln1_g=row(p['ln1_g']), ln1_b=row(p['ln1_b']), ln2_g=row(p['ln2_g']), ln2_b=row(p['ln2_b']))


def _prompt_layer(x, mod, q, seq_tile, mlstm_chunk):
    b, t, _ = x.shape
    mod3 = mod.reshape(b, 1, N_COND)
    main3, tail3 = _inproj(x, mod3, q['w_main'], q['w_tail'], 1, seq_tile, BF16)
    ya3, c1, n1, m1 = _mlstm_seq(main3, tail3, q['conv_w'], q['conv_b'], q['gbias'], q['norm_w'],
                                 mlstm_chunk, min(2, b))
    u3, s1 = _rwkv(main3, RWKV_SECTIONS, tail3, ya3, None, None, q['rw'], min(2, b), 1, RW_L, RW_L)
    y = _tail(u3, x, mod3, q, 1, seq_tile)
    shift = _modulate_rows(x[:, t - 1, :], mod)
    conv = main3[:, t - (CONV_W - 1):, :2 * D].astype(F32)
    return y, (c1, n1[:, :H_A, :], m1[:, 0, :H_A], conv, s1, shift)


def _sample_layer(x, mod, st, q, bb):
    c0, n0, m0, conv0, s0, shift0 = st
    b, t, _ = x.shape
    mod3 = mod.reshape(b, 1, N_COND)
    xp = jnp.pad(x, ((0, 0), (0, 8 - t), (0, 0)))
    main3, tail3 = _inproj(xp, mod3, q['w_main'], q['w_tail'], bb, 8, F32)
    pm, pt = _inproj(shift0.reshape(1, b, D), jnp.zeros((1, 1, N_COND), F32), q['w_main'], q['w_tail'], 1, b, F32)
    prev = (pm.reshape(b, 1, N_MAIN), pt.reshape(b, 1, N_TAIL))
    conv0p = jnp.pad(conv0, ((0, 0), (8 - (CONV_W - 1), 0), (0, 0)))
    n0p = jnp.pad(n0, ((0, 0), (0, 8 - H_A), (0, 0)))
    m0p = jnp.pad(m0, ((0, 0), (0, 128 - H_A))).reshape(b, 1, 128)
    ya3, c1, n1, m1 = _mlstm_step(main3, tail3, conv0p, c0, n0p, m0p, q['conv_w'], q['conv_b'], q['gbias'],
                                  q['norm_w'], t, min(4, b))
    u3, s1 = _rwkv(main3, RWKV_SECTIONS, tail3, ya3, prev, s0, q['rw'], 1, RW_L // 8, 8, t)
    y = _tail(u3, xp, mod3, q, bb, 8)
    shift = _modulate_rows(x[:, t - 1, :], mod)
    conv = jnp.concatenate([conv0, main3[:, :t, :2 * D]], axis=1)[:, t:, :]
    return y[:, :t, :], (c1, n1[:, :H_A, :], m1[:, 0, :H_A], conv, s1, shift)


def kernel(x_prompt, x_sample, c_prompt, c_sample, state_mlstm_C, state_mlstm_n, state_mlstm_m, state_mlstm_conv, state_rwkv_S, state_rwkv_shift, w_cond, b_cond, w_in, mlstm_i_bias, mlstm_f_bias, conv_w, conv_b, mlstm_norm_w, rwkv_mu, rwkv_w0, rwkv_w2, rwkv_a0, rwkv_a2, rwkv_g2, rwkv_k_k, rwkv_k_a, rwkv_r_k, rwkv_lnx_w, rwkv_lnx_b, w_out, ln1_g, ln1_b, w_up, w_down, ln2_g, ln2_b):
    depth = w_in.shape[0]
    bp = x_prompt.shape[0]
    yp, ys = x_prompt, x_sample
    new_p = [[] for _ in range(6)]
    new_s = [[] for _ in range(6)]
    for l in range(depth):
        p = {'w_in': w_in[l], 'mlstm_i_bias': mlstm_i_bias[l], 'mlstm_f_bias': mlstm_f_bias[l],
             'conv_w': conv_w[l], 'conv_b': conv_b[l], 'mlstm_norm_w': mlstm_norm_w[l],
             'rwkv_mu': rwkv_mu[l], 'rwkv_w0': rwkv_w0[l], 'rwkv_w2': rwkv_w2[l], 'rwkv_a0': rwkv_a0[l],
             'rwkv_a2': rwkv_a2[l], 'rwkv_g2': rwkv_g2[l], 'rwkv_k_k': rwkv_k_k[l], 'rwkv_k_a': rwkv_k_a[l],
             'rwkv_r_k': rwkv_r_k[l].reshape(-1), 'rwkv_lnx_w': rwkv_lnx_w[l], 'rwkv_lnx_b': rwkv_lnx_b[l],
             'w_out': w_out[l], 'ln1_g': ln1_g[l], 'ln1_b': ln1_b[l], 'w_up': w_up[l], 'w_down': w_down[l],
             'ln2_g': ln2_g[l], 'ln2_b': ln2_b[l]}
        q = _relayout_params(p)
        mod = _cond(jnp.concatenate([c_prompt, c_sample], axis=0), w_cond[l], b_cond[l])
        yp, st_p = _prompt_layer(yp, mod[:bp], q, min(1024, yp.shape[1]), min(256, yp.shape[1]))
        st_in = (state_mlstm_C[l], state_mlstm_n[l], state_mlstm_m[l], state_mlstm_conv[l],
                 state_rwkv_S[l], state_rwkv_shift[l])
        ys, st_s = _sample_layer(ys, mod[bp:], st_in, q, min(64, ys.shape[0]))
        for lst, t in zip(new_p, st_p):
            lst.append(t)
        for lst, t in zip(new_s, st_s):
            lst.append(t)
    outs_p = [jnp.stack(t) for t in new_p]
    outs_s = [jnp.stack(t) for t in new_s]
    return (yp, ys, *outs_p, *outs_s)
```

```python
import functools

import jax
import jax.numpy as jnp
from jax import lax
from jax.experimental import pallas as pl
from jax.experimental.pallas import tpu as pltpu

F32 = jnp.float32
BF16 = jnp.bfloat16
HIGHEST = lax.Precision.HIGHEST

D = 1024
H_A = 4
DK = 256
CONV_W = 4
H_B = 16
HB = 64
N_PAIR = H_B // 2
D_FF = 4096
N_COND = 6 * D
ALPHA = 2.0 ** 0.25
LN_EPS = 1e-5
MLSTM_EPS = 1e-6
RWKV_EPS = 64e-5

N_MAIN = 8 * D
RWKV_SECTIONS = (3, 4, 5, 7)
LORA = 256
TAIL_IF = LORA
N_TAIL = 512
TN_MAIN = 2048
N_MAIN_TILES = N_MAIN // TN_MAIN

RW_L = 64
FF_CHUNK = 1024
NEG = -1e30
VMEM_LIMIT = 56 * 1024 * 1024


def _cp(sem):
    return pltpu.CompilerParams(dimension_semantics=sem, vmem_limit_bytes=VMEM_LIMIT)


def _dot(a, b, prec=None):
    return jnp.dot(a, b, preferred_element_type=F32, precision=prec)


def _dot_nt(a, b, prec=None):
    return lax.dot_general(a, b, (((1,), (1,)), ((), ())), preferred_element_type=F32, precision=prec)


def _dot_tn(a, b, prec=None):
    return lax.dot_general(a, b, (((0,), (0,)), ((), ())), preferred_element_type=F32, precision=prec)


def _log_sigmoid(x):
    return jnp.minimum(x, 0.0) - jnp.log1p(jnp.exp(-jnp.abs(x)))


def _sigmoid(x):
    return 0.5 * jnp.tanh(0.5 * x) + 0.5


def _silu(x):
    return x * _sigmoid(x)


def _layer_norm(z, g, b):
    mu = jnp.mean(z, axis=-1, keepdims=True)
    zc = z - mu
    var = jnp.mean(zc * zc, axis=-1, keepdims=True)
    return zc * lax.rsqrt(var + LN_EPS) * g + b


def _cond_kernel(c_ref, w_ref, b_ref, o_ref):
    s = _silu(c_ref[...]).astype(BF16)
    o_ref[...] = _dot(s, w_ref[...].astype(BF16)) + b_ref[...]


def _cond(c, w_cond, b_cond):
    n = c.shape[0]
    tn = 512
    return pl.pallas_call(
        _cond_kernel,
        grid=(N_COND // tn,),
        in_specs=[pl.BlockSpec((n, D), lambda j: (0, 0)),
                  pl.BlockSpec((D, tn), lambda j: (0, j)),
                  pl.BlockSpec((1, tn), lambda j: (0, j))],
        out_specs=pl.BlockSpec((n, tn), lambda j: (0, j)),
        out_shape=jax.ShapeDtypeStruct((n, N_COND), F32),
        compiler_params=_cp(("arbitrary",)),
        name="cond",
    )(c, w_cond, b_cond.reshape(1, N_COND))


def _inproj_kernel(x_ref, sh_ref, sc_ref, wm_ref, wt_ref, main_ref, tail_ref, h_scr):
    bb, tt, _ = x_ref.shape
    j = pl.program_id(2)

    @pl.when(j == 0)
    def _():
        h = x_ref[...] * (1.0 + sc_ref[...]) + sh_ref[...]
        h_scr[...] = h.reshape(bb * tt, D).astype(BF16)

    main_ref[...] = _dot(h_scr[...], wm_ref[...]).reshape(bb, tt, TN_MAIN).astype(main_ref.dtype)

    @pl.when(j == N_MAIN_TILES - 1)
    def _():
        tail_ref[...] = _dot(h_scr[...], wt_ref[...]).reshape(bb, tt, N_TAIL)


def _inproj(x3, mod3, w_main, w_tail, bb, tt, main_dtype):
    b, tp, _ = x3.shape
    return pl.pallas_call(
        _inproj_kernel,
        grid=(b // bb, tp // tt, N_MAIN_TILES),
        in_specs=[pl.BlockSpec((bb, tt, D), lambda i, t, j: (i, t, 0)),
                  pl.BlockSpec((bb, 1, D), lambda i, t, j: (i, 0, 0)),
                  pl.BlockSpec((bb, 1, D), lambda i, t, j: (i, 0, 1)),
                  pl.BlockSpec((D, TN_MAIN), lambda i, t, j: (0, j)),
                  pl.BlockSpec((D, N_TAIL), lambda i, t, j: (0, 0))],
        out_specs=[pl.BlockSpec((bb, tt, TN_MAIN), lambda i, t, j: (i, t, j)),
                   pl.BlockSpec((bb, tt, N_TAIL), lambda i, t, j: (i, t, 0))],
        out_shape=[jax.ShapeDtypeStruct((b, tp, N_MAIN), main_dtype),
                   jax.ShapeDtypeStruct((b, tp, N_TAIL), F32)],
        scratch_shapes=[pltpu.VMEM((bb * tt, D), BF16)],
        compiler_params=_cp(("parallel", "parallel", "arbitrary")),
        name="inproj",
    )(x3, mod3, mod3, w_main, w_tail)


def _modulate_kernel(x_ref, sh_ref, sc_ref, o_ref):
    o_ref[...] = x_ref[...] * (1.0 + sc_ref[...]) + sh_ref[...]


def _modulate_rows(x2, mod2):
    n = x2.shape[0]
    return pl.pallas_call(
        _modulate_kernel,
        grid=(1,),
        in_specs=[pl.BlockSpec((n, D), lambda i: (0, 0)),
                  pl.BlockSpec((n, D), lambda i: (0, 0)),
                  pl.BlockSpec((n, D), lambda i: (0, 1))],
        out_specs=pl.BlockSpec((n, D), lambda i: (0, 0)),
        out_shape=jax.ShapeDtypeStruct((n, D), F32),
        name="modulate_last",
    )(x2, mod2, mod2)


def _conv4(pad_ref, n_rows, cw, cb):
    acc = cb + pad_ref[8:8 + n_rows, :] * cw[3:4, :]
    acc = acc + pad_ref[7:7 + n_rows, :] * cw[2:3, :]
    acc = acc + pad_ref[6:6 + n_rows, :] * cw[1:2, :]
    acc = acc + pad_ref[5:5 + n_rows, :] * cw[0:1, :]
    return acc


def _head_norm_rows(h, eps):
    mu = jnp.mean(h, axis=-1, keepdims=True)
    hc = h - mu
    var = jnp.mean(hc * hc, axis=-1, keepdims=True)
    return hc * lax.rsqrt(var + eps)


def _mlstm_seq_kernel(nb, qp_ref, kp_ref, v_ref, ga_ref, if_ref, cw_ref, cb_ref, gb_ref, nw_ref,
                      ya_ref, c_ref, n_ref, m_ref, haloq, halok):
    L = qp_ref.shape[1]
    assert qp_ref.dtype == BF16 and kp_ref.dtype == BF16

    @pl.when(pl.program_id(1) == 0)
    def _():
        c_ref[...] = jnp.zeros_like(c_ref)
        n_ref[...] = jnp.zeros_like(n_ref)
        m_ref[...] = jnp.zeros_like(m_ref)
        haloq[...] = jnp.zeros_like(haloq)
        halok[...] = jnp.zeros_like(halok)

    cw = cw_ref[...]
    cb = cb_ref[...]
    nw = nw_ref[...]
    row = lax.broadcasted_iota(jnp.int32, (L, L), 0)
    col = lax.broadcasted_iota(jnp.int32, (L, L), 1)
    causal = col <= row
    tril = jnp.where(causal, 1.0, 0.0).astype(F32)
    srow = lax.broadcasted_iota(jnp.int32, (3 * L, L), 0)
    scol = lax.broadcasted_iota(jnp.int32, (3 * L, L), 1)
    shift_mat = jnp.where(scol + srow // L + 1 == srow % L, 1.0, 0.0).astype(BF16)
    r8 = lax.broadcasted_iota(jnp.int32, (8, 1), 0)

    def conv(x_bf, halo_ref, cw_, cb_):
        sh = _dot(shift_mat, x_bf)
        x = x_bf.astype(F32)
        acc = cb_ + x * cw_[3:4, :] + sh[0:L] * cw_[2:3, :] + sh[L:2 * L] * cw_[1:2, :] \
            + sh[2 * L:3 * L] * cw_[0:1, :]
        halo = halo_ref[...]
        fix = jnp.zeros((8, DK), F32)
        for j in range(1, CONV_W):
            fix = fix + jnp.where(r8 < j, pltpu.roll(halo, j, 0), 0.0) * cw_[CONV_W - 1 - j:CONV_W - j, :]
        halo_ref[...] = x[L - 8:L, :]
        return jnp.concatenate([acc[0:8] + fix, acc[8:L]], axis=0)

    items = [(bi, h) for bi in range(nb) for h in range(H_A)]
    sl_of = lambda h: slice(h * DK, (h + 1) * DK)
    q_it, k_it = {}, {}
    for (bi, h) in items:
        sl, ksl = sl_of(h), slice(D + h * DK, D + (h + 1) * DK)
        q_it[bi, h] = _silu(conv(qp_ref[bi, :, sl], haloq.at[bi, :, sl], cw[:, sl], cb[:, sl]))
        k_it[bi, h] = _silu(conv(kp_ref[bi, :, sl], halok.at[bi, :, sl], cw[:, ksl], cb[:, ksl])) * (DK ** -0.5)

    gpre_l, bcum_l, gpre_t_l, bcum_t_l = [], [], [], []
    for bi in range(nb):
        gpre = if_ref[bi] + gb_ref[...]
        bcum = _dot(tril, _log_sigmoid(gpre), HIGHEST)
        gpre_l.append(gpre)
        bcum_l.append(bcum)
        gpre_t_l.append(gpre.T)
        bcum_t_l.append(bcum.T)

    st = {}
    for (bi, h) in items:
        ig_col = gpre_l[bi][:, h:h + 1]
        b_col = bcum_l[bi][:, H_A + h:H_A + h + 1]
        ig_row = gpre_t_l[bi][h:h + 1, :]
        b_row = bcum_t_l[bi][H_A + h:H_A + h + 1, :]
        m_prev = m_ref[bi][:, h:h + 1]
        g_col = b_col + m_prev
        dlog = jnp.where(causal, b_col - b_row + ig_row, NEG)
        m_t = jnp.maximum(g_col, jnp.max(dlog, axis=1, keepdims=True))
        b_last = b_col[L - 1:L, :]
        wlog = b_last - b_col + ig_col
        m_new = jnp.maximum(b_last + m_prev, jnp.max(wlog, axis=0, keepdims=True))
        st[bi, h] = dict(m_t=m_t, w_inter=jnp.exp(g_col - m_t), p=jnp.exp(dlog - m_t), m_new=m_new,
                         decay=jnp.exp(b_last + m_prev - m_new), wts=jnp.exp(wlog - m_new))
    qb = {it: q_it[it].astype(BF16) for it in items}
    kb = {it: k_it[it].astype(BF16) for it in items}
    s_l = {it: _dot_nt(qb[it], kb[it]) * st[it]['p'] for it in items}
    qc_l = {(bi, h): _dot_nt(qb[bi, h], c_ref[bi, h].astype(BF16)) for (bi, h) in items}
    sv_l = {(bi, h): _dot(s_l[bi, h].astype(BF16), v_ref[bi, :, sl_of(h)].astype(BF16)) for (bi, h) in items}
    upd_l = {(bi, h): _dot_tn((st[bi, h]['wts'] * v_ref[bi, :, sl_of(h)].astype(F32)).astype(BF16), kb[bi, h])
             for (bi, h) in items}
    for (bi, h) in items:
        sl = sl_of(h)
        d = st[bi, h]
        qh = q_it[bi, h]
        kh = k_it[bi, h]
        nh = n_ref[bi, h:h + 1, :]
        num = d['w_inter'] * qc_l[bi, h] + sv_l[bi, h]
        den = d['w_inter'] * jnp.sum(qh * nh, axis=1, keepdims=True) + jnp.sum(s_l[bi, h], axis=1, keepdims=True)
        hh = num / jnp.maximum(jnp.abs(den), jnp.exp(-d['m_t']))
        ga = _sigmoid(ga_ref[bi, :, sl].astype(F32))
        ya_ref[bi, :, sl] = (ga * _head_norm_rows(hh, MLSTM_EPS) * nw[:, sl]).astype(ya_ref.dtype)
        c_ref[bi, h] = d['decay'] * c_ref[bi, h] + upd_l[bi, h]
        n_ref[bi, h:h + 1, :] = d['decay'] * nh + jnp.sum(d['wts'] * kh, axis=0, keepdims=True)
        m_ref[bi, :, h:h + 1] = d['m_new']


def _mlstm_seq(main3, tail3, conv_w, conv_b, gbias, norm_w, L, nb):
    b, tp, _ = main3.shape
    blk = lambda j: pl.BlockSpec((nb, L, D), lambda i, c, j=j: (i, c, j))
    full = lambda shp: pl.BlockSpec(shp, lambda i, c: (0,) * len(shp))
    return pl.pallas_call(
        functools.partial(_mlstm_seq_kernel, nb),
        grid=(b // nb, tp // L),
        in_specs=[blk(0), blk(1), blk(2), blk(6),
                  pl.BlockSpec((nb, L, 128), lambda i, c: (i, c, TAIL_IF // 128)),
                  full((CONV_W, 2 * D)), full((1, 2 * D)), full((1, 128)), full((1, D))],
        out_specs=[pl.BlockSpec((nb, L, D), lambda i, c: (i, c, 0)),
                   pl.BlockSpec((nb, H_A, DK, DK), lambda i, c: (i, 0, 0, 0)),
                   pl.BlockSpec((nb, 8, DK), lambda i, c: (i, 0, 0)),
                   pl.BlockSpec((nb, 1, 128), lambda i, c: (i, 0, 0))],
        out_shape=[jax.ShapeDtypeStruct((b, tp, D), main3.dtype),
                   jax.ShapeDtypeStruct((b, H_A, DK, DK), F32),
                   jax.ShapeDtypeStruct((b, 8, DK), F32),
                   jax.ShapeDtypeStruct((b, 1, 128), F32)],
        scratch_shapes=[pltpu.VMEM((nb, 8, D), F32), pltpu.VMEM((nb, 8, D), F32)],
        compiler_params=_cp(("parallel", "arbitrary")),
        name="mlstm_seq",
    )(main3, main3, main3, main3, tail3, conv_w, conv_b, gbias, norm_w)


def _mlstm_step_kernel(tv, nb, qp_ref, kp_ref, v_ref, ga_ref, if_ref, conv0_ref, c0_ref, n0_ref, m0_ref,
                       cw_ref, cb_ref, gb_ref, nw_ref,
                       ya_ref, c_ref, n_ref, m_ref, padq, padk, gpad, lpad, kpad, vpad, wvpad):
    @pl.when(pl.program_id(0) == 0)
    def _():
        for r in (gpad, lpad, kpad, vpad, wvpad):
            r[...] = jnp.zeros_like(r)

    cw = cw_ref[...]
    cb = cb_ref[...]
    nw = nw_ref[...]
    gb = gb_ref[...]
    trow = lax.broadcasted_iota(jnp.int32, (8, 128), 0)
    scol = lax.broadcasted_iota(jnp.int32, (8, 128), 1)
    mask = (scol <= trow) & (scol < tv)
    rvalid = lax.broadcasted_iota(jnp.int32, (8, 1), 0) < tv
    r128 = lax.broadcasted_iota(jnp.int32, (128, 128), 0)
    c128 = lax.broadcasted_iota(jnp.int32, (128, 128), 1)
    tril = jnp.where(c128 <= r128, 1.0, 0.0).astype(F32)
    n_ref[...] = jnp.zeros_like(n_ref)
    m_ref[...] = jnp.zeros_like(m_ref)

    batches = range(nb)
    q_l, gpre_l, bcol_l, gt_l, bt_l, ga_l = [], [], [], [], [], []
    for bi in batches:
        padq[bi, 0:8, :] = conv0_ref[bi, :, 0:D]
        padk[bi, 0:8, :] = conv0_ref[bi, :, D:2 * D]
        padq[bi, 8:16, :] = qp_ref[bi]
        padk[bi, 8:16, :] = kp_ref[bi]
        q_l.append(_silu(_conv4(padq.at[bi], 8, cw[:, 0:D], cb[:, 0:D])))
        kpad[bi, 0:8, :] = _silu(_conv4(padk.at[bi], 8, cw[:, D:2 * D], cb[:, D:2 * D])) * (DK ** -0.5)
        vpad[bi, 0:8, :] = v_ref[bi]
        gpre = if_ref[bi] + gb
        gpad[bi, 0:8, :] = gpre
        lpad[bi, 0:8, :] = _log_sigmoid(gpre)
        gpre_l.append(gpre)
        ga_l.append(_sigmoid(ga_ref[bi]))
    for bi in batches:
        bpad = _dot(tril, lpad[bi], HIGHEST)
        bcol_l.append(bpad[0:8, :])
        bt_l.append(bpad.T)
        gt_l.append(gpad[bi].T)

    probs = [(bi, h) for bi in batches for h in range(H_A)]
    sl_of = lambda h: slice(h * DK, (h + 1) * DK)
    st = {}
    for (bi, h) in probs:
        ig_col = gpre_l[bi][:, h:h + 1]
        b_col = bcol_l[bi][:, H_A + h:H_A + h + 1]
        ig_row = gt_l[bi][h:h + 1, :]
        b_row = bt_l[bi][H_A + h:H_A + h + 1, :]
        m_prev = m0_ref[bi][:, h:h + 1]
        g_col = b_col + m_prev
        dlog = jnp.where(mask, b_col - b_row + ig_row, NEG)
        m_t = jnp.maximum(g_col, jnp.max(dlog, axis=1, keepdims=True))
        b_last = b_col[tv - 1:tv, :]
        wlog = jnp.where(rvalid, b_last - b_col + ig_col, NEG)
        m_new = jnp.maximum(b_last + m_prev, jnp.max(wlog, axis=0, keepdims=True))
        wts = jnp.exp(wlog - m_new)
        wvpad[bi, 0:8, sl_of(h)] = wts * vpad[bi, 0:8, sl_of(h)]
        st[bi, h] = dict(m_t=m_t, w_inter=jnp.exp(g_col - m_t), pm=jnp.exp(dlog - m_t), m_new=m_new,
                         decay=jnp.exp(b_last + m_prev - m_new), wts=wts)
    kb = {(bi, h): kpad[bi, :, sl_of(h)].astype(BF16) for (bi, h) in probs}
    qb = {(bi, h): q_l[bi][:, sl_of(h)].astype(BF16) for (bi, h) in probs}
    s_l = {k_: _dot_nt(qb[k_], kb[k_]) * st[k_]['pm'] for k_ in probs}
    qc_l = {(bi, h): _dot_nt(qb[bi, h], c0_ref[bi, h].astype(BF16)) for (bi, h) in probs}
    sv_l = {(bi, h): _dot(s_l[bi, h].astype(BF16), vpad[bi, :, sl_of(h)].astype(BF16)) for (bi, h) in probs}
    upd_l = {(bi, h): _dot(wvpad[bi, :, sl_of(h)].T.astype(BF16), kb[bi, h]) for (bi, h) in probs}
    for (bi, h) in probs:
        sl = sl_of(h)
        d = st[bi, h]
        nh = n0_ref[bi, h:h + 1, :]
        qh = q_l[bi][:, sl]
        num = d['w_inter'] * qc_l[bi, h] + sv_l[bi, h]
        den = d['w_inter'] * jnp.sum(qh * nh, axis=1, keepdims=True) + jnp.sum(s_l[bi, h], axis=1, keepdims=True)
        hh = num / jnp.maximum(jnp.abs(den), jnp.exp(-d['m_t']))
        ya_ref[bi, :, sl] = ga_l[bi][:, sl] * _head_norm_rows(hh, MLSTM_EPS) * nw[:, sl]
        c_ref[bi, h] = d['decay'] * c0_ref[bi, h] + upd_l[bi, h]
        n_ref[bi, h:h + 1, :] = d['decay'] * nh + jnp.sum(d['wts'] * kpad[bi, 0:8, sl], axis=0, keepdims=True)
        m_ref[bi, :, h:h + 1] = d['m_new']


def _mlstm_step(main3, tail3, conv0p, c0, n0p, m0p, conv_w, conv_b, gbias, norm_w, tv, nb):
    b = main3.shape[0]
    blk = lambda j: pl.BlockSpec((nb, 8, D), lambda i, j=j: (i, 0, j))
    full = lambda shp: pl.BlockSpec(shp, lambda i: (0,) * len(shp))
    state_specs = [pl.BlockSpec((nb, H_A, DK, DK), lambda i: (i, 0, 0, 0)),
                   pl.BlockSpec((nb, 8, DK), lambda i: (i, 0, 0)),
                   pl.BlockSpec((nb, 1, 128), lambda i: (i, 0, 0))]
    return pl.pallas_call(
        functools.partial(_mlstm_step_kernel, tv, nb),
        grid=(b // nb,),
        in_specs=[blk(0), blk(1), blk(2), blk(6),
                  pl.BlockSpec((nb, 8, 128), lambda i: (i, 0, TAIL_IF // 128)),
                  pl.BlockSpec((nb, 8, 2 * D), lambda i: (i, 0, 0))] + state_specs +
                 [full((CONV_W, 2 * D)), full((1, 2 * D)), full((1, 128)), full((1, D))],
        out_specs=[pl.BlockSpec((nb, 8, D), lambda i: (i, 0, 0))] + state_specs,
        out_shape=[jax.ShapeDtypeStruct((b, 8, D), F32),
                   jax.ShapeDtypeStruct((b, H_A, DK, DK), F32),
                   jax.ShapeDtypeStruct((b, 8, DK), F32),
                   jax.ShapeDtypeStruct((b, 1, 128), F32)],
        scratch_shapes=[pltpu.VMEM((nb, 16, D), F32), pltpu.VMEM((nb, 16, D), F32),
                        pltpu.VMEM((nb, 128, 128), F32), pltpu.VMEM((nb, 128, 128), F32),
                        pltpu.VMEM((nb, 128, D), F32), pltpu.VMEM((nb, 128, D), F32),
                        pltpu.VMEM((nb, 128, D), F32)],
        compiler_params=_cp(("arbitrary",)),
        name="mlstm_step",
    )(main3, main3, main3, main3, tail3, conv0p, c0, n0p, m0p, conv_w, conv_b, gbias, norm_w)


def _bd(x, lo):
    return jnp.concatenate([jnp.where(lo, x, 0.0), jnp.where(lo, 0.0, x)], axis=0)


def _pair_sum(x, lo):
    s_lo = jnp.sum(jnp.where(lo, x, 0.0), axis=1, keepdims=True)
    s_hi = jnp.sum(jnp.where(lo, 0.0, x), axis=1, keepdims=True)
    return jnp.where(lo, s_lo, s_hi)


def _rwkv_kernel(nsub, nbg, lb, tv, has_state, *refs):
    (r_ref, k_ref, v_ref, gb_ref, l_ref, ya_ref), refs = refs[:6], refs[6:]
    if has_state:
        (pr_ref, pk_ref, pv_ref, pl_ref, s0_ref), refs = refs[:5], refs[5:]
    (mur_ref, muk_ref, muv_ref, mul_ref, w0_ref, a0_ref, kk_ref, ka_ref, rk_ref,
     lw_ref, lb_ref, w2_ref, a2_ref, g2_ref,
     u_ref, s_ref, sbd, cr, ck, cv, cl) = refs
    L = nbg * lb
    nseq = nsub * nbg
    LT = nsub * L
    z64 = jnp.zeros((HB, HB), F32)

    @pl.when(pl.program_id(1) == 0)
    def _():
        if has_state:
            for gi in range(nseq):
                for p in range(N_PAIR):
                    top = jnp.concatenate([s0_ref[gi, 2 * p], z64], axis=1)
                    bot = jnp.concatenate([z64, s0_ref[gi, 2 * p + 1]], axis=1)
                    sbd[gi, p] = jnp.concatenate([top, bot], axis=0)
            cr[...] = pr_ref[...].astype(F32)
            ck[...] = pk_ref[...].astype(F32)
            cv[...] = pv_ref[...].astype(F32)
            cl[...] = pl_ref[...]
        else:
            sbd[...] = jnp.zeros_like(sbd)
            for c_ in (cr, ck, cv, cl):
                c_[...] = jnp.zeros_like(c_)

    def shift_mix(x_ref, carry, mu_ref):
        x3 = x_ref[...].astype(F32)
        width = x3.shape[-1]
        tpos = lax.broadcasted_iota(jnp.int32, x3.shape, 1)
        prev = jnp.where(tpos == 0, carry[...], pltpu.roll(x3, 1, 1))
        carry[...] = x3[:, lb - 1:lb, :]
        return (x3 + (prev - x3) * mu_ref[...]).reshape(LT, width)

    xr = shift_mix(r_ref, cr, mur_ref)
    xk = shift_mix(k_ref, ck, muk_ref)
    xv = shift_mix(v_ref, cv, muv_ref)
    xl = shift_mix(l_ref, cl, mul_ref)

    lane_l = lax.broadcasted_iota(jnp.int32, (LT, LORA), 1)
    act = jnp.where(lane_l < 64, jnp.tanh(xl), jnp.where(lane_l < 128, xl, _sigmoid(xl))).astype(BF16)
    z = w0_ref[...] + _dot(act, w2_ref[...])
    w_log = -(jnp.maximum(-z, 0.0) + jnp.log(1.0 + jnp.exp(-jnp.abs(z)))) - 0.5
    lw = -jnp.exp(w_log)
    a = _sigmoid(a0_ref[...] + _dot(act, a2_ref[...]))
    g = _dot(act, g2_ref[...])
    kk = xk * kk_ref[...]
    kmod = xk * (1.0 + (a - 1.0) * ka_ref[...])
    gate_b = _sigmoid(gb_ref[...].astype(F32).reshape(LT, D))
    y_a = ya_ref[...].astype(F32).reshape(LT, D)

    t_idx = lax.broadcasted_iota(jnp.int32, (LT, 1), 0)
    if tv < lb:
        valid = (t_idx % lb) < tv
        lw = jnp.where(valid, lw, 0.0)
        kk = jnp.where(valid, kk, 0.0)
        kmod = jnp.where(valid, kmod, 0.0)
        xv = jnp.where(valid, xv, 0.0)

    row = lax.broadcasted_iota(jnp.int32, (L, L), 0)
    col = lax.broadcasted_iota(jnp.int32, (L, L), 1)
    tril = jnp.where((col <= row) & (col // lb == row // lb), 1.0, 0.0).astype(F32)
    subs = range(nsub)
    rows = [slice(s * L, (s + 1) * L) for s in subs]
    cum_s = [_dot(tril, lw[rows[s]], HIGHEST) for s in subs]

    lane = lax.broadcasted_iota(jnp.int32, (L, 128), 1)
    lo = lane < HB
    src = lane % HB
    trow = lax.broadcasted_iota(jnp.int32, (L, 128), 0)
    same = (src // lb) == (trow // lb)
    strict = same & (src < trow)
    incl = same & (src <= trow)
    r128 = lax.broadcasted_iota(jnp.int32, (128, 128), 0)
    c128 = lax.broadcasted_iota(jnp.int32, (128, 128), 1)
    blockdiag = (r128 < HB) == (c128 < HB)
    eye_pair = jnp.where(src == trow, 1.0, 0.0).astype(F32)

    sls = [slice(p * 128, (p + 1) * 128) for p in range(N_PAIR)]
    items = [(s, p) for s in subs for p in range(N_PAIR)]
    idx = range(len(items))
    groups = range(nbg)

    at_l, rt_l, bt_l, kt_l, win_l, vp_l = [], [], [], [], [], []
    for (s, p) in items:
        rs, sl = rows[s], sls[p]
        kkp = kk[rs, sl]
        kap = kkp * lax.rsqrt(jnp.maximum(_pair_sum(kkp * kkp, lo), 1e-24))
        cum_p = cum_s[s][:, sl]
        w_in = jnp.exp(cum_p)
        w_inv = jnp.exp(-cum_p)
        at_l.append(-kap * jnp.exp(cum_p - lw[rs, sl]))
        rt_l.append(xr[rs, sl] * w_in)
        bt_l.append(kap * a[rs, sl] * w_inv)
        kt_l.append(kmod[rs, sl] * w_inv)
        win_l.append(w_in)
        vp_l.append(xv[rs, sl])
    bdv_l = [_bd(vp_l[i], lo).astype(BF16) for i in idx]

    gm_l = [_dot_nt(jnp.concatenate([at_l[i], rt_l[i]], axis=0).astype(BF16),
                    jnp.concatenate([_bd(bt_l[i], lo), _bd(kt_l[i], lo)], axis=0).astype(BF16))
            for i in idx]
    n_l = [jnp.where(strict, gm_l[i][0:L, 0:128], 0.0) for i in idx]
    aak_l = [jnp.where(strict, gm_l[i][0:L, 128:256], 0.0).astype(BF16) for i in idx]
    ark_l = [jnp.concatenate([jnp.where(incl, gm_l[i][L:2 * L, 0:128], 0.0),
                              jnp.where(incl, gm_l[i][L:2 * L, 128:256], 0.0)], axis=1).astype(BF16)
             for i in idx]

    xs_l = [[_dot_nt(jnp.concatenate([at_l[i][gi * lb:(gi + 1) * lb], rt_l[i][gi * lb:(gi + 1) * lb]],
                                     axis=0).astype(BF16), sbd[items[i][0] * nbg + gi, items[i][1]].astype(BF16))
             for gi in groups] for i in idx]
    if nbg == 1:
        as_l = [xs_l[i][0][0:lb] for i in idx]
        rs_l = [xs_l[i][0][lb:2 * lb] for i in idx]
    else:
        as_l = [jnp.concatenate([xs_l[i][gi][0:lb] for gi in groups], axis=0) for i in idx]
        rs_l = [jnp.concatenate([xs_l[i][gi][lb:2 * lb] for gi in groups], axis=0) for i in idx]

    y0_l = [as_l[i] + _dot(aak_l[i], bdv_l[i]) for i in idx]

    dm_l = [eye_pair for _ in idx]
    s_blk = 1
    while 2 * s_blk <= lb:
        lvl = ((trow // (2 * s_blk)) == (src // (2 * s_blk))) & ((trow % (2 * s_blk)) >= s_blk) \
            & ((src % (2 * s_blk)) < s_blk)
        if s_blk == 1:
            dm_l = [dm_l[i] + jnp.where(lvl, n_l[i], 0.0) for i in idx]
        else:
            t1_l = [_dot(jnp.where(lvl, n_l[i], 0.0).astype(BF16), _bd(dm_l[i], lo).astype(BF16))
                    for i in idx]
            dm_l = [dm_l[i] + _dot(dm_l[i].astype(BF16), _bd(t1_l[i], lo).astype(BF16)) for i in idx]
        s_blk *= 2
    u_l = [_dot(dm_l[i].astype(BF16), _bd(y0_l[i], lo).astype(BF16)) for i in idx]

    o_l = [rs_l[i] + _dot(ark_l[i], jnp.concatenate([_bd(u_l[i], lo).astype(BF16), bdv_l[i]], axis=0))
           for i in idx]

    w3_l = [win_l[i].reshape(nbg, lb, 128)[:, lb - 1:lb, :] for i in idx]
    rhs_l = []
    for i in idx:
        w_last = jnp.broadcast_to(w3_l[i], (nbg, lb, 128)).reshape(L, 128)
        rhs_l.append(jnp.concatenate([bt_l[i] * w_last, kt_l[i] * w_last], axis=0).astype(BF16))
    uv_l = [jnp.concatenate([u_l[i], vp_l[i]], axis=0) for i in idx]
    if nbg == 1:
        upd_l = [_dot_tn(uv_l[i].astype(BF16), rhs_l[i]) for i in idx]
        for i, (s, p) in enumerate(items):
            sbd[s, p] = sbd[s, p] * w3_l[i][0] + jnp.where(blockdiag, upd_l[i], 0.0)
    else:
        cgrp = (c128 % L) // lb
        uvt_l = [uv_l[i].T for i in idx]
        for i, (s, p) in enumerate(items):
            for gi in groups:
                upd = _dot(jnp.where(cgrp == gi, uvt_l[i], 0.0).astype(BF16), rhs_l[i])
                q_ = s * nbg + gi
                sbd[q_, p] = sbd[q_, p] * w3_l[i][gi] + jnp.where(blockdiag, upd, 0.0)

    out_l = []
    for i, (s, p) in enumerate(items):
        rs, sl = rows[s], sls[p]
        o = o_l[i]
        mu = _pair_sum(o, lo) * (1.0 / HB)
        oc = o - mu
        var = _pair_sum(oc * oc, lo) * (1.0 / HB)
        on = oc * lax.rsqrt(var + RWKV_EPS) * lw_ref[:, sl] + lb_ref[:, sl]
        bonus = _pair_sum(xr[rs, sl] * kmod[rs, sl] * rk_ref[:, sl], lo) * vp_l[i]
        yb = (on + bonus) * g[rs, sl]
        out_l.append(y_a[rs, sl] + gate_b[rs, sl] * yb)
    u_rows = [jnp.concatenate(out_l[s * N_PAIR:(s + 1) * N_PAIR], axis=1) for s in subs]
    u_all = u_rows[0] if nsub == 1 else jnp.concatenate(u_rows, axis=0)
    u_ref[...] = u_all.reshape(nseq, lb, D).astype(u_ref.dtype)

    @pl.when(pl.program_id(1) == pl.num_programs(1) - 1)
    def _():
        for gi in range(nseq):
            for p in range(N_PAIR):
                s_ref[gi, 2 * p] = sbd[gi, p, 0:HB, 0:HB]
                s_ref[gi, 2 * p + 1] = sbd[gi, p, HB:2 * HB, HB:2 * HB]


def _rwkv(main3, cols, tail3, ya3, prev, s0, prm, nsub, nbg, lb, tv):
    b, tp, _ = main3.shape
    has_state = s0 is not None
    nq = nsub * nbg
    blk = lambda j: pl.BlockSpec((nq, lb, D), lambda i, c, j=j: (i, c, j))
    pblk = lambda j: pl.BlockSpec((nq, 1, D), lambda i, c, j=j: (i, 0, j))
    full = lambda a: pl.BlockSpec(a.shape, lambda i, c: (0,) * a.ndim)
    sblk = pl.BlockSpec((nq, H_B, HB, HB), lambda i, c: (i, 0, 0, 0))
    c_r, c_k, c_v, c_gb = cols
    in_specs = [blk(c_r), blk(c_k), blk(c_v), blk(c_gb),
                pl.BlockSpec((nq, lb, LORA), lambda i, c: (i, c, 0)),
                pl.BlockSpec((nq, lb, D), lambda i, c: (i, c, 0))]
    args = [main3, main3, main3, main3, tail3, ya3]
    if has_state:
        in_specs += [pblk(c_r), pblk(c_k), pblk(c_v), pl.BlockSpec((nq, 1, LORA), lambda i, c: (i, 0, 0)), sblk]
        args += [prev[0], prev[0], prev[0], prev[1], s0]
    in_specs += [full(a) for a in prm]
    args += list(prm)
    return pl.pallas_call(
        functools.partial(_rwkv_kernel, nsub, nbg, lb, tv, has_state),
        grid=(b // nq, tp // lb),
        in_specs=in_specs,
        out_specs=[pl.BlockSpec((nq, lb, D), lambda i, c: (i, c, 0)), sblk],
        out_shape=[jax.ShapeDtypeStruct((b, tp, D), main3.dtype),
                   jax.ShapeDtypeStruct((b, H_B, HB, HB), F32)],
        scratch_shapes=[pltpu.VMEM((nq, N_PAIR, 128, 128), F32),
                        pltpu.VMEM((nq, 1, D), F32), pltpu.VMEM((nq, 1, D), F32),
                        pltpu.VMEM((nq, 1, D), F32), pltpu.VMEM((nq, 1, LORA), F32)],
        compiler_params=_cp(("parallel", "arbitrary")),
        name="rwkv",
    )(*args)


def _tail_kernel(u_ref, x_ref, g1_ref, sh_ref, sc_ref, g2_ref, wo_ref, wu_ref, wd_ref,
                 l1g_ref, l1b_ref, l2g_ref, l2b_ref, o_ref, x1_scr, h_scr, acc):
    bb, tt, _ = x_ref.shape
    j = pl.program_id(2)

    @pl.when(j == 0)
    def _():
        u = u_ref[...].reshape(bb * tt, D).astype(BF16)
        y = _dot(u, wo_ref[...]).reshape(bb, tt, D)
        x1 = _layer_norm(ALPHA * x_ref[...] + g1_ref[...] * y, l1g_ref[...], l1b_ref[...])
        x1_scr[...] = x1
        h_scr[...] = (x1 * (1.0 + sc_ref[...]) + sh_ref[...]).reshape(bb * tt, D).astype(BF16)
        acc[...] = jnp.zeros_like(acc)

    up = jnp.maximum(_dot(h_scr[...], wu_ref[...]), 0.0)
    acc[...] += _dot((up * up).astype(BF16), wd_ref[...])

    @pl.when(j == pl.num_programs(2) - 1)
    def _():
        z = ALPHA * x1_scr[...] + g2_ref[...] * acc[...].reshape(bb, tt, D)
        o_ref[...] = _layer_norm(z, l2g_ref[...], l2b_ref[...])


def _tail(u3, x3, mod3, q, bb, tt):
    b, tp, _ = x3.shape
    blk = pl.BlockSpec((bb, tt, D), lambda i, t, j: (i, t, 0))
    mblk = lambda col: pl.BlockSpec((bb, 1, D), lambda i, t, j, col=col: (i, 0, col))
    full = lambda shp: pl.BlockSpec(shp, lambda i, t, j: (0,) * len(shp))
    return pl.pallas_call(
        _tail_kernel,
        grid=(b // bb, tp // tt, D_FF // FF_CHUNK),
        in_specs=[blk, blk, mblk(2), mblk(3), mblk(4), mblk(5),
                  full((D, D)),
                  pl.BlockSpec((D, FF_CHUNK), lambda i, t, j: (0, j)),
                  pl.BlockSpec((FF_CHUNK, D), lambda i, t, j: (j, 0)),
                  full((1, D)), full((1, D)), full((1, D)), full((1, D))],
        out_specs=blk,
        out_shape=jax.ShapeDtypeStruct((b, tp, D), F32),
        scratch_shapes=[pltpu.VMEM((bb, tt, D), F32), pltpu.VMEM((bb * tt, D), BF16),
                        pltpu.VMEM((bb * tt, D), F32)],
        compiler_params=_cp(("parallel", "parallel", "arbitrary")),
        name="outproj_ffn",
    )(u3, x3, mod3, mod3, mod3, mod3, q['w_out'], q['w_up'], q['w_down'],
      q['ln1_g'], q['ln1_b'], q['ln2_g'], q['ln2_b'])


def _relayout_params(p):
    w = p['w_in']
    w_main = jnp.concatenate(
        [w[:, :3 * D], w[:, 3 * D + 8:6 * D + 8], w[:, 6 * D + 8 + LORA:8 * D + 8 + LORA]], axis=1).astype(BF16)
    w_tail = jnp.concatenate(
        [w[:, 6 * D + 8:6 * D + 8 + LORA], w[:, 3 * D:3 * D + 8],
         jnp.zeros((D, N_TAIL - LORA - 8), F32)], axis=1).astype(BF16)
    mu = p['rwkv_mu']
    z64 = jnp.zeros((64, D), F32)
    z128 = jnp.zeros((128, D), F32)
    row = lambda a: a.reshape(1, -1)
    rw = (row(mu[0:D]), row(mu[D:2 * D]), row(mu[2 * D:3 * D]), row(mu[3 * D:3 * D + LORA]),
          row(p['rwkv_w0']), row(p['rwkv_a0']), row(p['rwkv_k_k']), row(p['rwkv_k_a']),
          row(p['rwkv_r_k']), row(p['rwkv_lnx_w']), row(p['rwkv_lnx_b']),
          jnp.concatenate([p['rwkv_w2'], z64, z128], axis=0).astype(BF16),
          jnp.concatenate([z64, p['rwkv_a2'], z128], axis=0).astype(BF16),
          jnp.concatenate([z128, p['rwkv_g2']], axis=0).astype(BF16))
    gbias = jnp.concatenate([p['mlstm_i_bias'], p['mlstm_f_bias'], jnp.zeros((120,), F32)]).reshape(1, 128)
    return dict(w_main=w_main, w_tail=w_tail, rw=rw, gbias=gbias,
                conv_w=p['conv_w'], conv_b=row(p['conv_b']), norm_w=row(p['mlstm_norm_w']),
                w_out=p['w_out'].astype(BF16), w_up=p['w_up'].astype(BF16), w_down=p['w_down'].astype(BF16),
                ln1_g=row(p['ln1_g']), ln1_b=row(p['ln1_b']), ln2_g=row(p['ln2_g']), ln2_b=row(p['ln2_b']))


def _prompt_layer(x, mod, q, seq_tile, mlstm_chunk):
    b, t, _ = x.shape
    mod3 = mod.reshape(b, 1, N_COND)
    main3, tail3 = _inproj(x, mod3, q['w_main'], q['w_tail'], 1, seq_tile, BF16)
    ya3, c1, n1, m1 = _mlstm_seq(main3, tail3, q['conv_w'], q['conv_b'], q['gbias'], q['norm_w'],
                                 mlstm_chunk, min(2, b))
    u3, s1 = _rwkv(main3, RWKV_SECTIONS, tail3, ya3, None, None, q['rw'], min(2, b), 1, RW_L, RW_L)
    y = _tail(u3, x, mod3, q, 1, seq_tile)
    shift = _modulate_rows(x[:, t - 1, :], mod)
    conv = main3[:, t - (CONV_W - 1):, :2 * D].astype(F32)
    return y, (c1, n1[:, :H_A, :], m1[:, 0, :H_A], conv, s1, shift)


def _sample_layer(x, mod, st, q, bb):
    c0, n0, m0, conv0, s0, shift0 = st
    b, t, _ = x.shape
    mod3 = mod.reshape(b, 1, N_COND)
    xp = jnp.pad(x, ((0, 0), (0, 8 - t), (0, 0)))
    main3, tail3 = _inproj(xp, mod3, q['w_main'], q['w_tail'], bb, 8, F32)
    pm, pt = _inproj(shift0.reshape(1, b, D), jnp.zeros((1, 1, N_COND), F32), q['w_main'], q['w_tail'], 1, b, F32)
    prev = (pm.reshape(b, 1, N_MAIN), pt.reshape(b, 1, N_TAIL))
    conv0p = jnp.pad(conv0, ((0, 0), (8 - (CONV_W - 1), 0), (0, 0)))
    n0p = jnp.pad(n0, ((0, 0), (0, 8 - H_A), (0, 0)))
    m0p = jnp.pad(m0, ((0, 0), (0, 128 - H_A))).reshape(b, 1, 128)
    ya3, c1, n1, m1 = _mlstm_step(main3, tail3, conv0p, c0, n0p, m0p, q['conv_w'], q['conv_b'], q['gbias'],
                                  q['norm_w'], t, min(4, b))
    u3, s1 = _rwkv(main3, RWKV_SECTIONS, tail3, ya3, prev, s0, q['rw'], 1, RW_L // 8, 8, t)
    y = _tail(u3, xp, mod3, q, bb, 8)
    shift = _modulate_rows(x[:, t - 1, :], mod)
    conv = jnp.concatenate([conv0, main3[:, :t, :2 * D]], axis=1)[:, t:, :]
    return y[:, :t, :], (c1, n1[:, :H_A, :], m1[:, 0, :H_A], conv, s1, shift)


def kernel(x_prompt, x_sample, c_prompt, c_sample, state_mlstm_C, state_mlstm_n, state_mlstm_m, state_mlstm_conv, state_rwkv_S, state_rwkv_shift, w_cond, b_cond, w_in, mlstm_i_bias, mlstm_f_bias, conv_w, conv_b, mlstm_norm_w, rwkv_mu, rwkv_w0, rwkv_w2, rwkv_a0, rwkv_a2, rwkv_g2, rwkv_k_k, rwkv_k_a, rwkv_r_k, rwkv_lnx_w, rwkv_lnx_b, w_out, ln1_g, ln1_b, w_up, w_down, ln2_g, ln2_b):
    depth = w_in.shape[0]
    bp = x_prompt.shape[0]
    yp, ys = x_prompt, x_sample
    new_p = [[] for _ in range(6)]
    new_s = [[] for _ in range(6)]
    for l in range(depth):
        p = {'w_in': w_in[l], 'mlstm_i_bias': mlstm_i_bias[l], 'mlstm_f_bias': mlstm_f_bias[l],
             'conv_w': conv_w[l], 'conv_b': conv_b[l], 'mlstm_norm_w': mlstm_norm_w[l],
             'rwkv_mu': rwkv_mu[l], 'rwkv_w0': rwkv_w0[l], 'rwkv_w2': rwkv_w2[l], 'rwkv_a0': rwkv_a0[l],
             'rwkv_a2': rwkv_a2[l], 'rwkv_g2': rwkv_g2[l], 'rwkv_k_k': rwkv_k_k[l], 'rwkv_k_a': rwkv_k_a[l],
             'rwkv_r_k': rwkv_r_k[l].reshape(-1), 'rwkv_lnx_w': rwkv_lnx_w[l], 'rwkv_lnx_b': rwkv_lnx_b[l],
             'w_out': w_out[l], 'ln1_g': ln1_g[l], 'ln1_b': ln1_b[l], 'w_up': w_up[l], 'w_down': w_down[l],
             'ln2_g': ln2_g[l], 'ln2_b': ln2_b[l]}
        q = _relayout_params(p)
        mod = _cond(jnp.concatenate([c_prompt, c_sample], axis=0), w_cond[l], b_cond[l])
        yp, st_p = _prompt_layer(yp, mod[:bp], q, min(1024, yp.shape[1]), min(256, yp.shape[1]))
        st_in = (state_mlstm_C[l], state_mlstm_n[l], state_mlstm_m[l], state_mlstm_conv[l],
                 state_rwkv_S[l], state_rwkv_shift[l])
        ys, st_s = _sample_layer(ys, mod[bp:], st_in, q, min(64, ys.shape[0]))
        for lst, t in zip(new_p, st_p):
            lst.append(t)
        for lst, t in zip(new_s, st_s):
            lst.append(t)
    outs_p = [jnp.stack(t) for t in new_p]
    outs_s = [jnp.stack(t) for t in new_s]
    return (yp, ys, *outs_p, *outs_s)
```

```python
import functools

import jax
import jax.numpy as jnp
from jax import lax
from jax.experimental import pallas as pl
from jax.experimental.pallas import tpu as pltpu

F32 = jnp.float32
BF16 = jnp.bfloat16
HIGHEST = lax.Precision.HIGHEST

D = 1024
H_A = 4
DK = 256
CONV_W = 4
H_B = 16
HB = 64
N_PAIR = H_B // 2
D_FF = 4096
N_COND = 6 * D
ALPHA = 2.0 ** 0.25
LN_EPS = 1e-5
MLSTM_EPS = 1e-6
RWKV_EPS = 64e-5

N_MAIN = 8 * D
RWKV_SECTIONS = (3, 4, 5, 7)
LORA = 256
TAIL_IF = LORA
N_TAIL = 512
TN_MAIN = 2048
N_MAIN_TILES = N_MAIN // TN_MAIN

RW_L = 64
FF_CHUNK = 1024
NEG = -1e30
VMEM_LIMIT = 56 * 1024 * 1024


def _cp(sem):
    return pltpu.CompilerParams(dimension_semantics=sem, vmem_limit_bytes=VMEM_LIMIT)


def _dot(a, b, prec=None):
    return jnp.dot(a, b, preferred_element_type=F32, precision=prec)


def _dot_nt(a, b, prec=None):
    return lax.dot_general(a, b, (((1,), (1,)), ((), ())), preferred_element_type=F32, precision=prec)


def _dot_tn(a, b, prec=None):
    return lax.dot_general(a, b, (((0,), (0,)), ((), ())), preferred_element_type=F32, precision=prec)


def _log_sigmoid(x):
    return jnp.minimum(x, 0.0) - jnp.log1p(jnp.exp(-jnp.abs(x)))


def _sigmoid(x):
    return 0.5 * jnp.tanh(0.5 * x) + 0.5


def _silu(x):
    return x * _sigmoid(x)


def _layer_norm(z, g, b):
    mu = jnp.mean(z, axis=-1, keepdims=True)
    zc = z - mu
    var = jnp.mean(zc * zc, axis=-1, keepdims=True)
    return zc * lax.rsqrt(var + LN_EPS) * g + b


def _cond_kernel(c_ref, w_ref, b_ref, o_ref):
    s = _silu(c_ref[...]).astype(BF16)
    o_ref[...] = _dot(s, w_ref[...].astype(BF16)) + b_ref[...]


def _cond(c, w_cond, b_cond):
    n = c.shape[0]
    tn = 1536
    return pl.pallas_call(
        _cond_kernel,
        grid=(N_COND // tn,),
        in_specs=[pl.BlockSpec((n, D), lambda j: (0, 0)),
                  pl.BlockSpec((D, tn), lambda j: (0, j)),
                  pl.BlockSpec((1, tn), lambda j: (0, j))],
        out_specs=pl.BlockSpec((n, tn), lambda j: (0, j)),
        out_shape=jax.ShapeDtypeStruct((n, N_COND), F32),
        compiler_params=_cp(("arbitrary",)),
        name="cond",
    )(c, w_cond, b_cond.reshape(1, N_COND))


def _inproj_kernel(x_ref, sh_ref, sc_ref, wm_ref, wt_ref, main_ref, tail_ref, h_scr):
    bb, tt, _ = x_ref.shape
    j = pl.program_id(2)

    @pl.when(j == 0)
    def _():
        h = x_ref[...] * (1.0 + sc_ref[...]) + sh_ref[...]
        h_scr[...] = h.reshape(bb * tt, D).astype(BF16)

    main_ref[...] = _dot(h_scr[...], wm_ref[...]).reshape(bb, tt, TN_MAIN).astype(main_ref.dtype)

    @pl.when(j == N_MAIN_TILES - 1)
    def _():
        tail_ref[...] = _dot(h_scr[...], wt_ref[...]).reshape(bb, tt, N_TAIL)


def _inproj(x3, mod3, w_main, w_tail, bb, tt, main_dtype):
    b, tp, _ = x3.shape
    return pl.pallas_call(
        _inproj_kernel,
        grid=(b // bb, tp // tt, N_MAIN_TILES),
        in_specs=[pl.BlockSpec((bb, tt, D), lambda i, t, j: (i, t, 0)),
                  pl.BlockSpec((bb, 1, D), lambda i, t, j: (i, 0, 0)),
                  pl.BlockSpec((bb, 1, D), lambda i, t, j: (i, 0, 1)),
                  pl.BlockSpec((D, TN_MAIN), lambda i, t, j: (0, j)),
                  pl.BlockSpec((D, N_TAIL), lambda i, t, j: (0, 0))],
        out_specs=[pl.BlockSpec((bb, tt, TN_MAIN), lambda i, t, j: (i, t, j)),
                   pl.BlockSpec((bb, tt, N_TAIL), lambda i, t, j: (i, t, 0))],
        out_shape=[jax.ShapeDtypeStruct((b, tp, N_MAIN), main_dtype),
                   jax.ShapeDtypeStruct((b, tp, N_TAIL), F32)],
        scratch_shapes=[pltpu.VMEM((bb * tt, D), BF16)],
        compiler_params=_cp(("parallel", "parallel", "arbitrary")),
        name="inproj",
    )(x3, mod3, mod3, w_main, w_tail)


def _modulate_kernel(x_ref, sh_ref, sc_ref, o_ref):
    o_ref[...] = x_ref[...] * (1.0 + sc_ref[...]) + sh_ref[...]


def _modulate_rows(x2, mod2):
    n = x2.shape[0]
    return pl.pallas_call(
        _modulate_kernel,
        grid=(1,),
        in_specs=[pl.BlockSpec((n, D), lambda i: (0, 0)),
                  pl.BlockSpec((n, D), lambda i: (0, 0)),
                  pl.BlockSpec((n, D), lambda i: (0, 1))],
        out_specs=pl.BlockSpec((n, D), lambda i: (0, 0)),
        out_shape=jax.ShapeDtypeStruct((n, D), F32),
        name="modulate_last",
    )(x2, mod2, mod2)


def _conv4(pad_ref, n_rows, cw, cb):
    acc = cb + pad_ref[8:8 + n_rows, :] * cw[3:4, :]
    acc = acc + pad_ref[7:7 + n_rows, :] * cw[2:3, :]
    acc = acc + pad_ref[6:6 + n_rows, :] * cw[1:2, :]
    acc = acc + pad_ref[5:5 + n_rows, :] * cw[0:1, :]
    return acc


def _head_norm_rows(h, eps):
    mu = jnp.mean(h, axis=-1, keepdims=True)
    hc = h - mu
    var = jnp.mean(hc * hc, axis=-1, keepdims=True)
    return hc * lax.rsqrt(var + eps)


def _mlstm_seq_kernel(nb, qp_ref, kp_ref, v_ref, ga_ref, if_ref, cw_ref, cb_ref, gb_ref, nw_ref,
                      ya_ref, c_ref, n_ref, m_ref, haloq, halok):
    L = qp_ref.shape[1]
    assert qp_ref.dtype == BF16 and kp_ref.dtype == BF16

    @pl.when(pl.program_id(1) == 0)
    def _():
        c_ref[...] = jnp.zeros_like(c_ref)
        n_ref[...] = jnp.zeros_like(n_ref)
        m_ref[...] = jnp.zeros_like(m_ref)
        haloq[...] = jnp.zeros_like(haloq)
        halok[...] = jnp.zeros_like(halok)

    cw = cw_ref[...]
    cb = cb_ref[...]
    nw = nw_ref[...]
    row = lax.broadcasted_iota(jnp.int32, (L, L), 0)
    col = lax.broadcasted_iota(jnp.int32, (L, L), 1)
    causal = col <= row
    tril = jnp.where(causal, 1.0, 0.0).astype(F32)
    srow = lax.broadcasted_iota(jnp.int32, (3 * L, L), 0)
    scol = lax.broadcasted_iota(jnp.int32, (3 * L, L), 1)
    shift_mat = jnp.where(scol + srow // L + 1 == srow % L, 1.0, 0.0).astype(BF16)
    r8 = lax.broadcasted_iota(jnp.int32, (8, 1), 0)

    def conv(x_bf, halo_ref, cw_, cb_):
        sh = _dot(shift_mat, x_bf)
        x = x_bf.astype(F32)
        acc = cb_ + x * cw_[3:4, :] + sh[0:L] * cw_[2:3, :] + sh[L:2 * L] * cw_[1:2, :] \
            + sh[2 * L:3 * L] * cw_[0:1, :]
        halo = halo_ref[...]
        fix = jnp.zeros((8, DK), F32)
        for j in range(1, CONV_W):
            fix = fix + jnp.where(r8 < j, pltpu.roll(halo, j, 0), 0.0) * cw_[CONV_W - 1 - j:CONV_W - j, :]
        halo_ref[...] = x[L - 8:L, :]
        return jnp.concatenate([acc[0:8] + fix, acc[8:L]], axis=0)

    items = [(bi, h) for bi in range(nb) for h in range(H_A)]
    sl_of = lambda h: slice(h * DK, (h + 1) * DK)
    q_it, k_it = {}, {}
    for (bi, h) in items:
        sl, ksl = sl_of(h), slice(D + h * DK, D + (h + 1) * DK)
        q_it[bi, h] = _silu(conv(qp_ref[bi, :, sl], haloq.at[bi, :, sl], cw[:, sl], cb[:, sl]))
        k_it[bi, h] = _silu(conv(kp_ref[bi, :, sl], halok.at[bi, :, sl], cw[:, ksl], cb[:, ksl])) * (DK ** -0.5)

    gpre_l, bcum_l, gpre_t_l, bcum_t_l = [], [], [], []
    for bi in range(nb):
        gpre = if_ref[bi] + gb_ref[...]
        bcum = _dot(tril, _log_sigmoid(gpre), HIGHEST)
        gpre_l.append(gpre)
        bcum_l.append(bcum)
        gpre_t_l.append(gpre.T)
        bcum_t_l.append(bcum.T)

    st = {}
    for (bi, h) in items:
        ig_col = gpre_l[bi][:, h:h + 1]
        b_col = bcum_l[bi][:, H_A + h:H_A + h + 1]
        ig_row = gpre_t_l[bi][h:h + 1, :]
        b_row = bcum_t_l[bi][H_A + h:H_A + h + 1, :]
        m_prev = m_ref[bi][:, h:h + 1]
        g_col = b_col + m_prev
        dlog = jnp.where(causal, b_col - b_row + ig_row, NEG)
        m_t = jnp.maximum(g_col, jnp.max(dlog, axis=1, keepdims=True))
        b_last = b_col[L - 1:L, :]
        wlog = b_last - b_col + ig_col
        m_new = jnp.maximum(b_last + m_prev, jnp.max(wlog, axis=0, keepdims=True))
        st[bi, h] = dict(m_t=m_t, w_inter=jnp.exp(g_col - m_t), p=jnp.exp(dlog - m_t), m_new=m_new,
                         decay=jnp.exp(b_last + m_prev - m_new), wts=jnp.exp(wlog - m_new))
    qb = {it: q_it[it].astype(BF16) for it in items}
    kb = {it: k_it[it].astype(BF16) for it in items}
    s_l = {it: _dot_nt(qb[it], kb[it]) * st[it]['p'] for it in items}
    qc_l = {(bi, h): _dot_nt(qb[bi, h], c_ref[bi, h].astype(BF16)) for (bi, h) in items}
    sv_l = {(bi, h): _dot(s_l[bi, h].astype(BF16), v_ref[bi, :, sl_of(h)].astype(BF16)) for (bi, h) in items}
    upd_l = {(bi, h): _dot_tn((st[bi, h]['wts'] * v_ref[bi, :, sl_of(h)].astype(F32)).astype(BF16), kb[bi, h])
             for (bi, h) in items}
    for (bi, h) in items:
        sl = sl_of(h)
        d = st[bi, h]
        qh = q_it[bi, h]
        kh = k_it[bi, h]
        nh = n_ref[bi, h:h + 1, :]
        num = d['w_inter'] * qc_l[bi, h] + sv_l[bi, h]
        den = d['w_inter'] * jnp.sum(qh * nh, axis=1, keepdims=True) + jnp.sum(s_l[bi, h], axis=1, keepdims=True)
        hh = num / jnp.maximum(jnp.abs(den), jnp.exp(-d['m_t']))
        ga = _sigmoid(ga_ref[bi, :, sl].astype(F32))
        ya_ref[bi, :, sl] = (ga * _head_norm_rows(hh, MLSTM_EPS) * nw[:, sl]).astype(ya_ref.dtype)
        c_ref[bi, h] = d['decay'] * c_ref[bi, h] + upd_l[bi, h]
        n_ref[bi, h:h + 1, :] = d['decay'] * nh + jnp.sum(d['wts'] * kh, axis=0, keepdims=True)
        m_ref[bi, :, h:h + 1] = d['m_new']


def _mlstm_seq(main3, tail3, conv_w, conv_b, gbias, norm_w, L, nb):
    b, tp, _ = main3.shape
    blk = lambda j: pl.BlockSpec((nb, L, D), lambda i, c, j=j: (i, c, j))
    full = lambda shp: pl.BlockSpec(shp, lambda i, c: (0,) * len(shp))
    return pl.pallas_call(
        functools.partial(_mlstm_seq_kernel, nb),
        grid=(b // nb, tp // L),
        in_specs=[blk(0), blk(1), blk(2), blk(6),
                  pl.BlockSpec((nb, L, 128), lambda i, c: (i, c, TAIL_IF // 128)),
                  full((CONV_W, 2 * D)), full((1, 2 * D)), full((1, 128)), full((1, D))],
        out_specs=[pl.BlockSpec((nb, L, D), lambda i, c: (i, c, 0)),
                   pl.BlockSpec((nb, H_A, DK, DK), lambda i, c: (i, 0, 0, 0)),
                   pl.BlockSpec((nb, 8, DK), lambda i, c: (i, 0, 0)),
                   pl.BlockSpec((nb, 1, 128), lambda i, c: (i, 0, 0))],
        out_shape=[jax.ShapeDtypeStruct((b, tp, D), main3.dtype),
                   jax.ShapeDtypeStruct((b, H_A, DK, DK), F32),
                   jax.ShapeDtypeStruct((b, 8, DK), F32),
                   jax.ShapeDtypeStruct((b, 1, 128), F32)],
        scratch_shapes=[pltpu.VMEM((nb, 8, D), F32), pltpu.VMEM((nb, 8, D), F32)],
        compiler_params=_cp(("parallel", "arbitrary")),
        name="mlstm_seq",
    )(main3, main3, main3, main3, tail3, conv_w, conv_b, gbias, norm_w)


def _mlstm_step_kernel(tv, nb, qp_ref, kp_ref, v_ref, ga_ref, if_ref, conv0_ref, c0_ref, n0_ref, m0_ref,
                       cw_ref, cb_ref, gb_ref, nw_ref,
                       ya_ref, c_ref, n_ref, m_ref, padq, padk, gpad, lpad, kpad, vpad, wvpad):
    @pl.when(pl.program_id(0) == 0)
    def _():
        for r in (gpad, lpad, kpad, vpad, wvpad):
            r[...] = jnp.zeros_like(r)

    cw = cw_ref[...]
    cb = cb_ref[...]
    nw = nw_ref[...]
    gb = gb_ref[...]
    trow = lax.broadcasted_iota(jnp.int32, (8, 128), 0)
    scol = lax.broadcasted_iota(jnp.int32, (8, 128), 1)
    mask = (scol <= trow) & (scol < tv)
    rvalid = lax.broadcasted_iota(jnp.int32, (8, 1), 0) < tv
    r128 = lax.broadcasted_iota(jnp.int32, (128, 128), 0)
    c128 = lax.broadcasted_iota(jnp.int32, (128, 128), 1)
    tril = jnp.where(c128 <= r128, 1.0, 0.0).astype(F32)
    n_ref[...] = jnp.zeros_like(n_ref)
    m_ref[...] = jnp.zeros_like(m_ref)

    batches = range(nb)
    q_l, gpre_l, bcol_l, gt_l, bt_l, ga_l = [], [], [], [], [], []
    for bi in batches:
        padq[bi, 0:8, :] = conv0_ref[bi, :, 0:D]
        padk[bi, 0:8, :] = conv0_ref[bi, :, D:2 * D]
        padq[bi, 8:16, :] = qp_ref[bi]
        padk[bi, 8:16, :] = kp_ref[bi]
        q_l.append(_silu(_conv4(padq.at[bi], 8, cw[:, 0:D], cb[:, 0:D])))
        kpad[bi, 0:8, :] = _silu(_conv4(padk.at[bi], 8, cw[:, D:2 * D], cb[:, D:2 * D])) * (DK ** -0.5)
        vpad[bi, 0:8, :] = v_ref[bi]
        gpre = if_ref[bi] + gb
        gpad[bi, 0:8, :] = gpre
        lpad[bi, 0:8, :] = _log_sigmoid(gpre)
        gpre_l.append(gpre)
        ga_l.append(_sigmoid(ga_ref[bi]))
    for bi in batches:
        bpad = _dot(tril, lpad[bi], HIGHEST)
        bcol_l.append(bpad[0:8, :])
        bt_l.append(bpad.T)
        gt_l.append(gpad[bi].T)

    probs = [(bi, h) for bi in batches for h in range(H_A)]
    sl_of = lambda h: slice(h * DK, (h + 1) * DK)
    st = {}
    for (bi, h) in probs:
        ig_col = gpre_l[bi][:, h:h + 1]
        b_col = bcol_l[bi][:, H_A + h:H_A + h + 1]
        ig_row = gt_l[bi][h:h + 1, :]
        b_row = bt_l[bi][H_A + h:H_A + h + 1, :]
        m_prev = m0_ref[bi][:, h:h + 1]
        g_col = b_col + m_prev
        dlog = jnp.where(mask, b_col - b_row + ig_row, NEG)
        m_t = jnp.maximum(g_col, jnp.max(dlog, axis=1, keepdims=True))
        b_last = b_col[tv - 1:tv, :]
        wlog = jnp.where(rvalid, b_last - b_col + ig_col, NEG)
        m_new = jnp.maximum(b_last + m_prev, jnp.max(wlog, axis=0, keepdims=True))
        wts = jnp.exp(wlog - m_new)
        wvpad[bi, 0:8, sl_of(h)] = wts * vpad[bi, 0:8, sl_of(h)]
        st[bi, h] = dict(m_t=m_t, w_inter=jnp.exp(g_col - m_t), pm=jnp.exp(dlog - m_t), m_new=m_new,
                         decay=jnp.exp(b_last + m_prev - m_new), wts=wts)
    kb = {(bi, h): kpad[bi, :, sl_of(h)].astype(BF16) for (bi, h) in probs}
    qb = {(bi, h): q_l[bi][:, sl_of(h)].astype(BF16) for (bi, h) in probs}
    s_l = {k_: _dot_nt(qb[k_], kb[k_]) * st[k_]['pm'] for k_ in probs}
    qc_l = {(bi, h): _dot_nt(qb[bi, h], c0_ref[bi, h].astype(BF16)) for (bi, h) in probs}
    sv_l = {(bi, h): _dot(s_l[bi, h].astype(BF16), vpad[bi, :, sl_of(h)].astype(BF16)) for (bi, h) in probs}
    upd_l = {(bi, h): _dot(wvpad[bi, :, sl_of(h)].T.astype(BF16), kb[bi, h]) for (bi, h) in probs}
    for (bi, h) in probs:
        sl = sl_of(h)
        d = st[bi, h]
        nh = n0_ref[bi, h:h + 1, :]
        qh = q_l[bi][:, sl]
        num = d['w_inter'] * qc_l[bi, h] + sv_l[bi, h]
        den = d['w_inter'] * jnp.sum(qh * nh, axis=1, keepdims=True) + jnp.sum(s_l[bi, h], axis=1, keepdims=True)
        hh = num / jnp.maximum(jnp.abs(den), jnp.exp(-d['m_t']))
        ya_ref[bi, :, sl] = ga_l[bi][:, sl] * _head_norm_rows(hh, MLSTM_EPS) * nw[:, sl]
        c_ref[bi, h] = d['decay'] * c0_ref[bi, h] + upd_l[bi, h]
        n_ref[bi, h:h + 1, :] = d['decay'] * nh + jnp.sum(d['wts'] * kpad[bi, 0:8, sl], axis=0, keepdims=True)
        m_ref[bi, :, h:h + 1] = d['m_new']


def _mlstm_step(main3, tail3, conv0p, c0, n0p, m0p, conv_w, conv_b, gbias, norm_w, tv, nb):
    b = main3.shape[0]
    blk = lambda j: pl.BlockSpec((nb, 8, D), lambda i, j=j: (i, 0, j))
    full = lambda shp: pl.BlockSpec(shp, lambda i: (0,) * len(shp))
    state_specs = [pl.BlockSpec((nb, H_A, DK, DK), lambda i: (i, 0, 0, 0)),
                   pl.BlockSpec((nb, 8, DK), lambda i: (i, 0, 0)),
                   pl.BlockSpec((nb, 1, 128), lambda i: (i, 0, 0))]
    return pl.pallas_call(
        functools.partial(_mlstm_step_kernel, tv, nb),
        grid=(b // nb,),
        in_specs=[blk(0), blk(1), blk(2), blk(6),
                  pl.BlockSpec((nb, 8, 128), lambda i: (i, 0, TAIL_IF // 128)),
                  pl.BlockSpec((nb, 8, 2 * D), lambda i: (i, 0, 0))] + state_specs +
                 [full((CONV_W, 2 * D)), full((1, 2 * D)), full((1, 128)), full((1, D))],
        out_specs=[pl.BlockSpec((nb, 8, D), lambda i: (i, 0, 0))] + state_specs,
        out_shape=[jax.ShapeDtypeStruct((b, 8, D), F32),
                   jax.ShapeDtypeStruct((b, H_A, DK, DK), F32),
                   jax.ShapeDtypeStruct((b, 8, DK), F32),
                   jax.ShapeDtypeStruct((b, 1, 128), F32)],
        scratch_shapes=[pltpu.VMEM((nb, 16, D), F32), pltpu.VMEM((nb, 16, D), F32),
                        pltpu.VMEM((nb, 128, 128), F32), pltpu.VMEM((nb, 128, 128), F32),
                        pltpu.VMEM((nb, 128, D), F32), pltpu.VMEM((nb, 128, D), F32),
                        pltpu.VMEM((nb, 128, D), F32)],
        compiler_params=_cp(("arbitrary",)),
        name="mlstm_step",
    )(main3, main3, main3, main3, tail3, conv0p, c0, n0p, m0p, conv_w, conv_b, gbias, norm_w)


def _bd(x, lo):
    return jnp.concatenate([jnp.where(lo, x, 0.0), jnp.where(lo, 0.0, x)], axis=0)


def _pair_sum(x, lo):
    s_lo = jnp.sum(jnp.where(lo, x, 0.0), axis=1, keepdims=True)
    s_hi = jnp.sum(jnp.where(lo, 0.0, x), axis=1, keepdims=True)
    return jnp.where(lo, s_lo, s_hi)


def _rwkv_kernel(nsub, nbg, lb, tv, has_state, *refs):
    (r_ref, k_ref, v_ref, gb_ref, l_ref, ya_ref), refs = refs[:6], refs[6:]
    if has_state:
        (pr_ref, pk_ref, pv_ref, pl_ref, s0_ref), refs = refs[:5], refs[5:]
    (mur_ref, muk_ref, muv_ref, mul_ref, w0_ref, a0_ref, kk_ref, ka_ref, rk_ref,
     lw_ref, lb_ref, w2_ref, a2_ref, g2_ref,
     u_ref, s_ref, sbd, cr, ck, cv, cl) = refs
    L = nbg * lb
    nseq = nsub * nbg
    LT = nsub * L
    z64 = jnp.zeros((HB, HB), F32)

    @pl.when(pl.program_id(1) == 0)
    def _():
        if has_state:
            for gi in range(nseq):
                for p in range(N_PAIR):
                    top = jnp.concatenate([s0_ref[gi, 2 * p], z64], axis=1)
                    bot = jnp.concatenate([z64, s0_ref[gi, 2 * p + 1]], axis=1)
                    sbd[gi, p] = jnp.concatenate([top, bot], axis=0)
            cr[...] = pr_ref[...].astype(F32)
            ck[...] = pk_ref[...].astype(F32)
            cv[...] = pv_ref[...].astype(F32)
            cl[...] = pl_ref[...]
        else:
            sbd[...] = jnp.zeros_like(sbd)
            for c_ in (cr, ck, cv, cl):
                c_[...] = jnp.zeros_like(c_)

    def shift_mix(x_ref, carry, mu_ref):
        x3 = x_ref[...].astype(F32)
        width = x3.shape[-1]
        tpos = lax.broadcasted_iota(jnp.int32, x3.shape, 1)
        prev = jnp.where(tpos == 0, carry[...], pltpu.roll(x3, 1, 1))
        carry[...] = x3[:, lb - 1:lb, :]
        return (x3 + (prev - x3) * mu_ref[...]).reshape(LT, width)

    xr = shift_mix(r_ref, cr, mur_ref)
    xk = shift_mix(k_ref, ck, muk_ref)
    xv = shift_mix(v_ref, cv, muv_ref)
    xl = shift_mix(l_ref, cl, mul_ref)

    lane_l = lax.broadcasted_iota(jnp.int32, (LT, LORA), 1)
    act = jnp.where(lane_l < 64, jnp.tanh(xl), jnp.where(lane_l < 128, xl, _sigmoid(xl))).astype(BF16)
    z = w0_ref[...] + _dot(act, w2_ref[...])
    w_log = -(jnp.maximum(-z, 0.0) + jnp.log(1.0 + jnp.exp(-jnp.abs(z)))) - 0.5
    lw = -jnp.exp(w_log)
    a = _sigmoid(a0_ref[...] + _dot(act, a2_ref[...]))
    g = _dot(act, g2_ref[...])
    kk = xk * kk_ref[...]
    kmod = xk * (1.0 + (a - 1.0) * ka_ref[...])
    gate_b = _sigmoid(gb_ref[...].astype(F32).reshape(LT, D))
    y_a = ya_ref[...].astype(F32).reshape(LT, D)

    t_idx = lax.broadcasted_iota(jnp.int32, (LT, 1), 0)
    if tv < lb:
        valid = (t_idx % lb) < tv
        lw = jnp.where(valid, lw, 0.0)
        kk = jnp.where(valid, kk, 0.0)
        kmod = jnp.where(valid, kmod, 0.0)
        xv = jnp.where(valid, xv, 0.0)

    row = lax.broadcasted_iota(jnp.int32, (L, L), 0)
    col = lax.broadcasted_iota(jnp.int32, (L, L), 1)
    tril = jnp.where((col <= row) & (col // lb == row // lb), 1.0, 0.0).astype(F32)
    subs = range(nsub)
    rows = [slice(s * L, (s + 1) * L) for s in subs]
    cum_s = [_dot(tril, lw[rows[s]], HIGHEST) for s in subs]

    lane = lax.broadcasted_iota(jnp.int32, (L, 128), 1)
    lo = lane < HB
    src = lane % HB
    trow = lax.broadcasted_iota(jnp.int32, (L, 128), 0)
    same = (src // lb) == (trow // lb)
    strict = same & (src < trow)
    incl = same & (src <= trow)
    r128 = lax.broadcasted_iota(jnp.int32, (128, 128), 0)
    c128 = lax.broadcasted_iota(jnp.int32, (128, 128), 1)
    blockdiag = (r128 < HB) == (c128 < HB)
    eye_pair = jnp.where(src == trow, 1.0, 0.0).astype(F32)

    sls = [slice(p * 128, (p + 1) * 128) for p in range(N_PAIR)]
    items = [(s, p) for s in subs for p in range(N_PAIR)]
    idx = range(len(items))
    groups = range(nbg)

    at_l, rt_l, bt_l, kt_l, win_l, vp_l = [], [], [], [], [], []
    for (s, p) in items:
        rs, sl = rows[s], sls[p]
        kkp = kk[rs, sl]
        kap = kkp * lax.rsqrt(jnp.maximum(_pair_sum(kkp * kkp, lo), 1e-24))
        cum_p = cum_s[s][:, sl]
        w_in = jnp.exp(cum_p)
        w_inv = jnp.exp(-cum_p)
        at_l.append(-kap * jnp.exp(cum_p - lw[rs, sl]))
        rt_l.append(xr[rs, sl] * w_in)
        bt_l.append(kap * a[rs, sl] * w_inv)
        kt_l.append(kmod[rs, sl] * w_inv)
        win_l.append(w_in)
        vp_l.append(xv[rs, sl])
    bdv_l = [_bd(vp_l[i], lo).astype(BF16) for i in idx]

    gm_l = [_dot_nt(jnp.concatenate([at_l[i], rt_l[i]], axis=0).astype(BF16),
                    jnp.concatenate([_bd(bt_l[i], lo), _bd(kt_l[i], lo)], axis=0).astype(BF16))
            for i in idx]
    n_l = [jnp.where(strict, gm_l[i][0:L, 0:128], 0.0) for i in idx]
    aak_l = [jnp.where(strict, gm_l[i][0:L, 128:256], 0.0).astype(BF16) for i in idx]
    ark_l = [jnp.concatenate([jnp.where(incl, gm_l[i][L:2 * L, 0:128], 0.0),
                              jnp.where(incl, gm_l[i][L:2 * L, 128:256], 0.0)], axis=1).astype(BF16)
             for i in idx]

    xs_l = [[_dot_nt(jnp.concatenate([at_l[i][gi * lb:(gi + 1) * lb], rt_l[i][gi * lb:(gi + 1) * lb]],
                                     axis=0).astype(BF16), sbd[items[i][0] * nbg + gi, items[i][1]].astype(BF16))
             for gi in groups] for i in idx]
    if nbg == 1:
        as_l = [xs_l[i][0][0:lb] for i in idx]
        rs_l = [xs_l[i][0][lb:2 * lb] for i in idx]
    else:
        as_l = [jnp.concatenate([xs_l[i][gi][0:lb] for gi in groups], axis=0) for i in idx]
        rs_l = [jnp.concatenate([xs_l[i][gi][lb:2 * lb] for gi in groups], axis=0) for i in idx]

    y0_l = [as_l[i] + _dot(aak_l[i], bdv_l[i]) for i in idx]

    dm_l = [eye_pair for _ in idx]
    s_blk = 1
    while 2 * s_blk <= lb:
        lvl = ((trow // (2 * s_blk)) == (src // (2 * s_blk))) & ((trow % (2 * s_blk)) >= s_blk) \
            & ((src % (2 * s_blk)) < s_blk)
        if s_blk == 1:
            dm_l = [dm_l[i] + jnp.where(lvl, n_l[i], 0.0) for i in idx]
        else:
            t1_l = [_dot(jnp.where(lvl, n_l[i], 0.0).astype(BF16), _bd(dm_l[i], lo).astype(BF16))
                    for i in idx]
            dm_l = [dm_l[i] + _dot(dm_l[i].astype(BF16), _bd(t1_l[i], lo).astype(BF16)) for i in idx]
        s_blk *= 2
    u_l = [_dot(dm_l[i].astype(BF16), _bd(y0_l[i], lo).astype(BF16)) for i in idx]

    o_l = [rs_l[i] + _dot(ark_l[i], jnp.concatenate([_bd(u_l[i], lo).astype(BF16), bdv_l[i]], axis=0))
           for i in idx]

    w3_l = [win_l[i].reshape(nbg, lb, 128)[:, lb - 1:lb, :] for i in idx]
    rhs_l = []
    for i in idx:
        w_last = jnp.broadcast_to(w3_l[i], (nbg, lb, 128)).reshape(L, 128)
        rhs_l.append(jnp.concatenate([bt_l[i] * w_last, kt_l[i] * w_last], axis=0).astype(BF16))
    uv_l = [jnp.concatenate([u_l[i], vp_l[i]], axis=0) for i in idx]
    if nbg == 1:
        upd_l = [_dot_tn(uv_l[i].astype(BF16), rhs_l[i]) for i in idx]
        for i, (s, p) in enumerate(items):
            sbd[s, p] = sbd[s, p] * w3_l[i][0] + jnp.where(blockdiag, upd_l[i], 0.0)
    else:
        cgrp = (c128 % L) // lb
        uvt_l = [uv_l[i].T for i in idx]
        for i, (s, p) in enumerate(items):
            for gi in groups:
                upd = _dot(jnp.where(cgrp == gi, uvt_l[i], 0.0).astype(BF16), rhs_l[i])
                q_ = s * nbg + gi
                sbd[q_, p] = sbd[q_, p] * w3_l[i][gi] + jnp.where(blockdiag, upd, 0.0)

    out_l = []
    for i, (s, p) in enumerate(items):
        rs, sl = rows[s], sls[p]
        o = o_l[i]
        mu = _pair_sum(o, lo) * (1.0 / HB)
        oc = o - mu
        var = _pair_sum(oc * oc, lo) * (1.0 / HB)
        on = oc * lax.rsqrt(var + RWKV_EPS) * lw_ref[:, sl] + lb_ref[:, sl]
        bonus = _pair_sum(xr[rs, sl] * kmod[rs, sl] * rk_ref[:, sl], lo) * vp_l[i]
        yb = (on + bonus) * g[rs, sl]
        out_l.append(y_a[rs, sl] + gate_b[rs, sl] * yb)
    u_rows = [jnp.concatenate(out_l[s * N_PAIR:(s + 1) * N_PAIR], axis=1) for s in subs]
    u_all = u_rows[0] if nsub == 1 else jnp.concatenate(u_rows, axis=0)
    u_ref[...] = u_all.reshape(nseq, lb, D).astype(u_ref.dtype)

    @pl.when(pl.program_id(1) == pl.num_programs(1) - 1)
    def _():
        for gi in range(nseq):
            for p in range(N_PAIR):
                s_ref[gi, 2 * p] = sbd[gi, p, 0:HB, 0:HB]
                s_ref[gi, 2 * p + 1] = sbd[gi, p, HB:2 * HB, HB:2 * HB]


def _rwkv(main3, cols, tail3, ya3, prev, s0, prm, nsub, nbg, lb, tv):
    b, tp, _ = main3.shape
    has_state = s0 is not None
    nq = nsub * nbg
    blk = lambda j: pl.BlockSpec((nq, lb, D), lambda i, c, j=j: (i, c, j))
    pblk = lambda j: pl.BlockSpec((nq, 1, D), lambda i, c, j=j: (i, 0, j))
    full = lambda a: pl.BlockSpec(a.shape, lambda i, c: (0,) * a.ndim)
    sblk = pl.BlockSpec((nq, H_B, HB, HB), lambda i, c: (i, 0, 0, 0))
    c_r, c_k, c_v, c_gb = cols
    in_specs = [blk(c_r), blk(c_k), blk(c_v), blk(c_gb),
                pl.BlockSpec((nq, lb, LORA), lambda i, c: (i, c, 0)),
                pl.BlockSpec((nq, lb, D), lambda i, c: (i, c, 0))]
    args = [main3, main3, main3, main3, tail3, ya3]
    if has_state:
        in_specs += [pblk(c_r), pblk(c_k), pblk(c_v), pl.BlockSpec((nq, 1, LORA), lambda i, c: (i, 0, 0)), sblk]
        args += [prev[0], prev[0], prev[0], prev[1], s0]
    in_specs += [full(a) for a in prm]
    args += list(prm)
    return pl.pallas_call(
        functools.partial(_rwkv_kernel, nsub, nbg, lb, tv, has_state),
        grid=(b // nq, tp // lb),
        in_specs=in_specs,
        out_specs=[pl.BlockSpec((nq, lb, D), lambda i, c: (i, c, 0)), sblk],
        out_shape=[jax.ShapeDtypeStruct((b, tp, D), main3.dtype),
                   jax.ShapeDtypeStruct((b, H_B, HB, HB), F32)],
        scratch_shapes=[pltpu.VMEM((nq, N_PAIR, 128, 128), F32),
                        pltpu.VMEM((nq, 1, D), F32), pltpu.VMEM((nq, 1, D), F32),
                        pltpu.VMEM((nq, 1, D), F32), pltpu.VMEM((nq, 1, LORA), F32)],
        compiler_params=_cp(("parallel", "arbitrary")),
        name="rwkv",
    )(*args)


def _tail_kernel(u_ref, x_ref, g1_ref, sh_ref, sc_ref, g2_ref, wo_ref, wu_ref, wd_ref,
                 l1g_ref, l1b_ref, l2g_ref, l2b_ref, o_ref, x1_scr, h_scr, acc):
    bb, tt, _ = x_ref.shape
    j = pl.program_id(2)

    @pl.when(j == 0)
    def _():
        u = u_ref[...].reshape(bb * tt, D).astype(BF16)
        y = _dot(u, wo_ref[...]).reshape(bb, tt, D)
        x1 = _layer_norm(ALPHA * x_ref[...] + g1_ref[...] * y, l1g_ref[...], l1b_ref[...])
        x1_scr[...] = x1
        h_scr[...] = (x1 * (1.0 + sc_ref[...]) + sh_ref[...]).reshape(bb * tt, D).astype(BF16)
        acc[...] = jnp.zeros_like(acc)

    up = jnp.maximum(_dot(h_scr[...], wu_ref[...]), 0.0)
    acc[...] += _dot((up * up).astype(BF16), wd_ref[...])

    @pl.when(j == pl.num_programs(2) - 1)
    def _():
        z = ALPHA * x1_scr[...] + g2_ref[...] * acc[...].reshape(bb, tt, D)
        o_ref[...] = _layer_norm(z, l2g_ref[...], l2b_ref[...])


def _tail(u3, x3, mod3, q, bb, tt):
    b, tp, _ = x3.shape
    blk = pl.BlockSpec((bb, tt, D), lambda i, t, j: (i, t, 0))
    mblk = lambda col: pl.BlockSpec((bb, 1, D), lambda i, t, j, col=col: (i, 0, col))
    full = lambda shp: pl.BlockSpec(shp, lambda i, t, j: (0,) * len(shp))
    return pl.pallas_call(
        _tail_kernel,
        grid=(b // bb, tp // tt, D_FF // FF_CHUNK),
        in_specs=[blk, blk, mblk(2), mblk(3), mblk(4), mblk(5),
                  full((D, D)),
                  pl.BlockSpec((D, FF_CHUNK), lambda i, t, j: (0, j)),
                  pl.BlockSpec((FF_CHUNK, D), lambda i, t, j: (j, 0)),
                  full((1, D)), full((1, D)), full((1, D)), full((1, D))],
        out_specs=blk,
        out_shape=jax.ShapeDtypeStruct((b, tp, D), F32),
        scratch_shapes=[pltpu.VMEM((bb, tt, D), F32), pltpu.VMEM((bb * tt, D), BF16),
                        pltpu.VMEM((bb * tt, D), F32)],
        compiler_params=_cp(("parallel", "parallel", "arbitrary")),
        name="outproj_ffn",
    )(u3, x3, mod3, mod3, mod3, mod3, q['w_out'], q['w_up'], q['w_down'],
      q['ln1_g'], q['ln1_b'], q['ln2_g'], q['ln2_b'])


def _relayout_params(p):
    w = p['w_in']
    w_main = jnp.concatenate(
        [w[:, :3 * D], w[:, 3 * D + 8:6 * D + 8], w[:, 6 * D + 8 + LORA:8 * D + 8 + LORA]], axis=1).astype(BF16)
    w_tail = jnp.concatenate(
        [w[:, 6 * D + 8:6 * D + 8 + LORA], w[:, 3 * D:3 * D + 8],
         jnp.zeros((D, N_TAIL - LORA - 8), F32)], axis=1).astype(BF16)
    mu = p['rwkv_mu']
    z64 = jnp.zeros((64, D), F32)
    z128 = jnp.zeros((128, D), F32)
    row = lambda a: a.reshape(1, -1)
    rw = (row(mu[0:D]), row(mu[D:2 * D]), row(mu[2 * D:3 * D]), row(mu[3 * D:3 * D + LORA]),
          row(p['rwkv_w0']), row(p['rwkv_a0']), row(p['rwkv_k_k']), row(p['rwkv_k_a']),
          row(p['rwkv_r_k']), row(p['rwkv_lnx_w']), row(p['rwkv_lnx_b']),
          jnp.concatenate([p['rwkv_w2'], z64, z128], axis=0).astype(BF16),
          jnp.concatenate([z64, p['rwkv_a2'], z128], axis=0).astype(BF16),
          jnp.concatenate([z128, p['rwkv_g2']], axis=0).astype(BF16))
    gbias = jnp.concatenate([p['mlstm_i_bias'], p['mlstm_f_bias'], jnp.zeros((120,), F32)]).reshape(1, 128)
    return dict(w_main=w_main, w_tail=w_tail, rw=rw, gbias=gbias,
                conv_w=p['conv_w'], conv_b=row(p['conv_b']), norm_w=row(p['mlstm_norm_w']),
                w_out=p['w_out'].astype(BF16), w_up=p['w_up'].astype(BF16), w_down=p['w_down'].astype(BF16),
                ln1_g=row(p['ln1_g']), ln1_b=row(p['ln1_b']), ln2_g=row(p['ln2_g']), ln2_b=row(p['ln2_b']))


def _prompt_layer(x, mod, q, seq_tile, mlstm_chunk):
    b, t, _ = x.shape
    mod3 = mod.reshape(b, 1, N_COND)
    main3, tail3 = _inproj(x, mod3, q['w_main'], q['w_tail'], 1, seq_tile, BF16)
    ya3, c1, n1, m1 = _mlstm_seq(main3, tail3, q['conv_w'], q['conv_b'], q['gbias'], q['norm_w'],
                                 mlstm_chunk, min(2, b))
    u3, s1 = _rwkv(main3, RWKV_SECTIONS, tail3, ya3, None, None, q['rw'], min(2, b), 1, RW_L, RW_L)
    y = _tail(u3, x, mod3, q, 1, seq_tile)
    shift = _modulate_rows(x[:, t - 1, :], mod)
    conv = main3[:, t - (CONV_W - 1):, :2 * D].astype(F32)
    return y, (c1, n1[:, :H_A, :], m1[:, 0, :H_A], conv, s1, shift)


def _sample_layer(x, mod, st, q, bb):
    c0, n0, m0, conv0, s0, shift0 = st
    b, t, _ = x.shape
    mod3 = mod.reshape(b, 1, N_COND)
    xp = jnp.pad(x, ((0, 0), (0, 8 - t), (0, 0)))
    main3, tail3 = _inproj(xp, mod3, q['w_main'], q['w_tail'], bb, 8, F32)
    pm, pt = _inproj(shift0.reshape(1, b, D), jnp.zeros((1, 1, N_COND), F32), q['w_main'], q['w_tail'], 1, b, F32)
    prev = (pm.reshape(b, 1, N_MAIN), pt.reshape(b, 1, N_TAIL))
    conv0p = jnp.pad(conv0, ((0, 0), (8 - (CONV_W - 1), 0), (0, 0)))
    n0p = jnp.pad(n0, ((0, 0), (0, 8 - H_A), (0, 0)))
    m0p = jnp.pad(m0, ((0, 0), (0, 128 - H_A))).reshape(b, 1, 128)
    ya3, c1, n1, m1 = _mlstm_step(main3, tail3, conv0p, c0, n0p, m0p, q['conv_w'], q['conv_b'], q['gbias'],
                                  q['norm_w'], t, min(4, b))
    u3, s1 = _rwkv(main3, RWKV_SECTIONS, tail3, ya3, prev, s0, q['rw'], 1, RW_L // 8, 8, t)
    y = _tail(u3, xp, mod3, q, bb, 8)
    shift = _modulate_rows(x[:, t - 1, :], mod)
    conv = jnp.concatenate([conv0, main3[:, :t, :2 * D]], axis=1)[:, t:, :]
    return y[:, :t, :], (c1, n1[:, :H_A, :], m1[:, 0, :H_A], conv, s1, shift)


def kernel(x_prompt, x_sample, c_prompt, c_sample, state_mlstm_C, state_mlstm_n, state_mlstm_m, state_mlstm_conv, state_rwkv_S, state_rwkv_shift, w_cond, b_cond, w_in, mlstm_i_bias, mlstm_f_bias, conv_w, conv_b, mlstm_norm_w, rwkv_mu, rwkv_w0, rwkv_w2, rwkv_a0, rwkv_a2, rwkv_g2, rwkv_k_k, rwkv_k_a, rwkv_r_k, rwkv_lnx_w, rwkv_lnx_b, w_out, ln1_g, ln1_b, w_up, w_down, ln2_g, ln2_b):
    depth = w_in.shape[0]
    bp = x_prompt.shape[0]
    yp, ys = x_prompt, x_sample
    new_p = [[] for _ in range(6)]
    new_s = [[] for _ in range(6)]
    for l in range(depth):
        p = {'w_in': w_in[l], 'mlstm_i_bias': mlstm_i_bias[l], 'mlstm_f_bias': mlstm_f_bias[l],
             'conv_w': conv_w[l], 'conv_b': conv_b[l], 'mlstm_norm_w': mlstm_norm_w[l],
             'rwkv_mu': rwkv_mu[l], 'rwkv_w0': rwkv_w0[l], 'rwkv_w2': rwkv_w2[l], 'rwkv_a0': rwkv_a0[l],
             'rwkv_a2': rwkv_a2[l], 'rwkv_g2': rwkv_g2[l], 'rwkv_k_k': rwkv_k_k[l], 'rwkv_k_a': rwkv_k_a[l],
             'rwkv_r_k': rwkv_r_k[l].reshape(-1), 'rwkv_lnx_w': rwkv_lnx_w[l], 'rwkv_lnx_b': rwkv_lnx_b[l],
             'w_out': w_out[l], 'ln1_g': ln1_g[l], 'ln1_b': ln1_b[l], 'w_up': w_up[l], 'w_down': w_down[l],
             'ln2_g': ln2_g[l], 'ln2_b': ln2_b[l]}
        q = _relayout_params(p)
        mod = _cond(jnp.concatenate([c_prompt, c_sample], axis=0), w_cond[l], b_cond[l])
        yp, st_p = _prompt_layer(yp, mod[:bp], q, min(1024, yp.shape[1]), min(256, yp.shape[1]))
        st_in = (state_mlstm_C[l], state_mlstm_n[l], state_mlstm_m[l], state_mlstm_conv[l],
                 state_rwkv_S[l], state_rwkv_shift[l])
        ys, st_s = _sample_layer(ys, mod[bp:], st_in, q, min(128, ys.shape[0]))
        for lst, t in zip(new_p, st_p):
            lst.append(t)
        for lst, t in zip(new_s, st_s):
            lst.append(t)
    outs_p = [jnp.stack(t) for t in new_p]
    outs_s = [jnp.stack(t) for t in new_s]
    return (yp, ys, *outs_p, *outs_s)
```

```python
import functools

import jax
import jax.numpy as jnp
from jax import lax
from jax.experimental import pallas as pl
from jax.experimental.pallas import tpu as pltpu

F32 = jnp.float32
BF16 = jnp.bfloat16
HIGHEST = lax.Precision.HIGHEST

D = 1024
H_A = 4
DK = 256
CONV_W = 4
H_B = 16
HB = 64
N_PAIR = H_B // 2
D_FF = 4096
N_COND = 6 * D
ALPHA = 2.0 ** 0.25
LN_EPS = 1e-5
MLSTM_EPS = 1e-6
RWKV_EPS = 64e-5

N_MAIN = 8 * D
RWKV_SECTIONS = (3, 4, 5, 7)
LORA = 256
TAIL_IF = LORA
N_TAIL = 512
TN_MAIN = 2048
N_MAIN_TILES = N_MAIN // TN_MAIN

RW_L = 64
FF_CHUNK = 1024
NEG = -1e30
VMEM_LIMIT = 56 * 1024 * 1024


def _cp(sem):
    return pltpu.CompilerParams(dimension_semantics=sem, vmem_limit_bytes=VMEM_LIMIT)


def _dot(a, b, prec=None):
    return jnp.dot(a, b, preferred_element_type=F32, precision=prec)


def _dot_nt(a, b, prec=None):
    return lax.dot_general(a, b, (((1,), (1,)), ((), ())), preferred_element_type=F32, precision=prec)


def _dot_tn(a, b, prec=None):
    return lax.dot_general(a, b, (((0,), (0,)), ((), ())), preferred_element_type=F32, precision=prec)


def _cumsum_rows(tril01, x):
    hi = x.astype(BF16)
    r1 = x - hi.astype(F32)
    mid = r1.astype(BF16)
    lo = (r1 - mid.astype(F32)).astype(BF16)
    t = tril01.astype(BF16)
    return _dot(t, hi) + _dot(t, mid) + _dot(t, lo)


def _log_sigmoid(x):
    return jnp.minimum(x, 0.0) - jnp.log1p(jnp.exp(-jnp.abs(x)))


def _sigmoid(x):
    return 0.5 * jnp.tanh(0.5 * x) + 0.5


def _silu(x):
    return x * _sigmoid(x)


def _layer_norm(z, g, b):
    mu = jnp.mean(z, axis=-1, keepdims=True)
    zc = z - mu
    var = jnp.mean(zc * zc, axis=-1, keepdims=True)
    return zc * lax.rsqrt(var + LN_EPS) * g + b


def _cond_kernel(c_ref, w_ref, b_ref, o_ref):
    s = _silu(c_ref[...]).astype(BF16)
    o_ref[...] = _dot(s, w_ref[...].astype(BF16)) + b_ref[...]


def _cond(c, w_cond, b_cond):
    n = c.shape[0]
    tn = 1536
    return pl.pallas_call(
        _cond_kernel,
        grid=(N_COND // tn,),
        in_specs=[pl.BlockSpec((n, D), lambda j: (0, 0)),
                  pl.BlockSpec((D, tn), lambda j: (0, j)),
                  pl.BlockSpec((1, tn), lambda j: (0, j))],
        out_specs=pl.BlockSpec((n, tn), lambda j: (0, j)),
        out_shape=jax.ShapeDtypeStruct((n, N_COND), F32),
        compiler_params=_cp(("arbitrary",)),
        name="cond",
    )(c, w_cond, b_cond.reshape(1, N_COND))


def _inproj_kernel(x_ref, sh_ref, sc_ref, wm_ref, wt_ref, main_ref, tail_ref, h_scr):
    bb, tt, _ = x_ref.shape
    j = pl.program_id(2)

    @pl.when(j == 0)
    def _():
        h = x_ref[...] * (1.0 + sc_ref[...]) + sh_ref[...]
        h_scr[...] = h.reshape(bb * tt, D).astype(BF16)

    main_ref[...] = _dot(h_scr[...], wm_ref[...]).reshape(bb, tt, TN_MAIN).astype(main_ref.dtype)

    @pl.when(j == N_MAIN_TILES - 1)
    def _():
        tail_ref[...] = _dot(h_scr[...], wt_ref[...]).reshape(bb, tt, N_TAIL)


def _inproj(x3, mod3, w_main, w_tail, bb, tt, main_dtype):
    b, tp, _ = x3.shape
    return pl.pallas_call(
        _inproj_kernel,
        grid=(b // bb, tp // tt, N_MAIN_TILES),
        in_specs=[pl.BlockSpec((bb, tt, D), lambda i, t, j: (i, t, 0)),
                  pl.BlockSpec((bb, 1, D), lambda i, t, j: (i, 0, 0)),
                  pl.BlockSpec((bb, 1, D), lambda i, t, j: (i, 0, 1)),
                  pl.BlockSpec((D, TN_MAIN), lambda i, t, j: (0, j)),
                  pl.BlockSpec((D, N_TAIL), lambda i, t, j: (0, 0))],
        out_specs=[pl.BlockSpec((bb, tt, TN_MAIN), lambda i, t, j: (i, t, j)),
                   pl.BlockSpec((bb, tt, N_TAIL), lambda i, t, j: (i, t, 0))],
        out_shape=[jax.ShapeDtypeStruct((b, tp, N_MAIN), main_dtype),
                   jax.ShapeDtypeStruct((b, tp, N_TAIL), F32)],
        scratch_shapes=[pltpu.VMEM((bb * tt, D), BF16)],
        compiler_params=_cp(("parallel", "parallel", "arbitrary")),
        name="inproj",
    )(x3, mod3, mod3, w_main, w_tail)


def _modulate_kernel(x_ref, sh_ref, sc_ref, o_ref):
    o_ref[...] = x_ref[...] * (1.0 + sc_ref[...]) + sh_ref[...]


def _modulate_rows(x2, mod2):
    n = x2.shape[0]
    return pl.pallas_call(
        _modulate_kernel,
        grid=(1,),
        in_specs=[pl.BlockSpec((n, D), lambda i: (0, 0)),
                  pl.BlockSpec((n, D), lambda i: (0, 0)),
                  pl.BlockSpec((n, D), lambda i: (0, 1))],
        out_specs=pl.BlockSpec((n, D), lambda i: (0, 0)),
        out_shape=jax.ShapeDtypeStruct((n, D), F32),
        name="modulate_last",
    )(x2, mod2, mod2)


def _conv4(pad_ref, n_rows, cw, cb):
    acc = cb + pad_ref[8:8 + n_rows, :] * cw[3:4, :]
    acc = acc + pad_ref[7:7 + n_rows, :] * cw[2:3, :]
    acc = acc + pad_ref[6:6 + n_rows, :] * cw[1:2, :]
    acc = acc + pad_ref[5:5 + n_rows, :] * cw[0:1, :]
    return acc


def _head_norm_rows(h, eps):
    mu = jnp.mean(h, axis=-1, keepdims=True)
    hc = h - mu
    var = jnp.mean(hc * hc, axis=-1, keepdims=True)
    return hc * lax.rsqrt(var + eps)


def _mlstm_seq_kernel(nb, qp_ref, kp_ref, v_ref, ga_ref, if_ref, cw_ref, cb_ref, gb_ref, nw_ref,
                      ya_ref, c_ref, n_ref, m_ref, haloq, halok):
    L = qp_ref.shape[1]
    assert qp_ref.dtype == BF16 and kp_ref.dtype == BF16

    @pl.when(pl.program_id(1) == 0)
    def _():
        c_ref[...] = jnp.zeros_like(c_ref)
        n_ref[...] = jnp.zeros_like(n_ref)
        m_ref[...] = jnp.zeros_like(m_ref)
        haloq[...] = jnp.zeros_like(haloq)
        halok[...] = jnp.zeros_like(halok)

    cw = cw_ref[...]
    cb = cb_ref[...]
    nw = nw_ref[...]
    row = lax.broadcasted_iota(jnp.int32, (L, L), 0)
    col = lax.broadcasted_iota(jnp.int32, (L, L), 1)
    causal = col <= row
    tril = jnp.where(causal, 1.0, 0.0).astype(F32)
    srow = lax.broadcasted_iota(jnp.int32, (3 * L, L), 0)
    scol = lax.broadcasted_iota(jnp.int32, (3 * L, L), 1)
    shift_mat = jnp.where(scol + srow // L + 1 == srow % L, 1.0, 0.0).astype(BF16)
    r8 = lax.broadcasted_iota(jnp.int32, (8, 1), 0)

    def conv(x_bf, halo_ref, cw_, cb_):
        sh = _dot(shift_mat, x_bf)
        x = x_bf.astype(F32)
        acc = cb_ + x * cw_[3:4, :] + sh[0:L] * cw_[2:3, :] + sh[L:2 * L] * cw_[1:2, :] \
            + sh[2 * L:3 * L] * cw_[0:1, :]
        halo = halo_ref[...]
        fix = jnp.zeros((8, DK), F32)
        for j in range(1, CONV_W):
            fix = fix + jnp.where(r8 < j, pltpu.roll(halo, j, 0), 0.0) * cw_[CONV_W - 1 - j:CONV_W - j, :]
        halo_ref[...] = x[L - 8:L, :]
        return jnp.concatenate([acc[0:8] + fix, acc[8:L]], axis=0)

    items = [(bi, h) for bi in range(nb) for h in range(H_A)]
    sl_of = lambda h: slice(h * DK, (h + 1) * DK)
    q_it, k_it = {}, {}
    for (bi, h) in items:
        sl, ksl = sl_of(h), slice(D + h * DK, D + (h + 1) * DK)
        q_it[bi, h] = _silu(conv(qp_ref[bi, :, sl], haloq.at[bi, :, sl], cw[:, sl], cb[:, sl]))
        k_it[bi, h] = _silu(conv(kp_ref[bi, :, sl], halok.at[bi, :, sl], cw[:, ksl], cb[:, ksl])) * (DK ** -0.5)

    gpre_l, bcum_l, gpre_t_l, bcum_t_l = [], [], [], []
    for bi in range(nb):
        gpre = if_ref[bi] + gb_ref[...]
        bcum = _dot(tril, _log_sigmoid(gpre), HIGHEST)
        gpre_l.append(gpre)
        bcum_l.append(bcum)
        gpre_t_l.append(gpre.T)
        bcum_t_l.append(bcum.T)

    st = {}
    for (bi, h) in items:
        ig_col = gpre_l[bi][:, h:h + 1]
        b_col = bcum_l[bi][:, H_A + h:H_A + h + 1]
        ig_row = gpre_t_l[bi][h:h + 1, :]
        b_row = bcum_t_l[bi][H_A + h:H_A + h + 1, :]
        m_prev = m_ref[bi][:, h:h + 1]
        g_col = b_col + m_prev
        dlog = jnp.where(causal, b_col - b_row + ig_row, NEG)
        m_t = jnp.maximum(g_col, jnp.max(dlog, axis=1, keepdims=True))
        b_last = b_col[L - 1:L, :]
        wlog = b_last - b_col + ig_col
        m_new = jnp.maximum(b_last + m_prev, jnp.max(wlog, axis=0, keepdims=True))
        st[bi, h] = dict(m_t=m_t, w_inter=jnp.exp(g_col - m_t), p=jnp.exp(dlog - m_t), m_new=m_new,
                         decay=jnp.exp(b_last + m_prev - m_new), wts=jnp.exp(wlog - m_new))
    qb = {it: q_it[it].astype(BF16) for it in items}
    kb = {it: k_it[it].astype(BF16) for it in items}
    s_l = {it: _dot_nt(qb[it], kb[it]) * st[it]['p'] for it in items}
    qc_l = {(bi, h): _dot_nt(qb[bi, h], c_ref[bi, h].astype(BF16)) for (bi, h) in items}
    sv_l = {(bi, h): _dot(s_l[bi, h].astype(BF16), v_ref[bi, :, sl_of(h)].astype(BF16)) for (bi, h) in items}
    upd_l = {(bi, h): _dot_tn((st[bi, h]['wts'] * v_ref[bi, :, sl_of(h)].astype(F32)).astype(BF16), kb[bi, h])
             for (bi, h) in items}
    for (bi, h) in items:
        sl = sl_of(h)
        d = st[bi, h]
        qh = q_it[bi, h]
        kh = k_it[bi, h]
        nh = n_ref[bi, h:h + 1, :]
        num = d['w_inter'] * qc_l[bi, h] + sv_l[bi, h]
        den = d['w_inter'] * jnp.sum(qh * nh, axis=1, keepdims=True) + jnp.sum(s_l[bi, h], axis=1, keepdims=True)
        hh = num / jnp.maximum(jnp.abs(den), jnp.exp(-d['m_t']))
        ga = _sigmoid(ga_ref[bi, :, sl].astype(F32))
        ya_ref[bi, :, sl] = (ga * _head_norm_rows(hh, MLSTM_EPS) * nw[:, sl]).astype(ya_ref.dtype)
        c_ref[bi, h] = d['decay'] * c_ref[bi, h] + upd_l[bi, h]
        n_ref[bi, h:h + 1, :] = d['decay'] * nh + jnp.sum(d['wts'] * kh, axis=0, keepdims=True)
        m_ref[bi, :, h:h + 1] = d['m_new']


def _mlstm_seq(main3, tail3, conv_w, conv_b, gbias, norm_w, L, nb):
    b, tp, _ = main3.shape
    blk = lambda j: pl.BlockSpec((nb, L, D), lambda i, c, j=j: (i, c, j))
    full = lambda shp: pl.BlockSpec(shp, lambda i, c: (0,) * len(shp))
    return pl.pallas_call(
        functools.partial(_mlstm_seq_kernel, nb),
        grid=(b // nb, tp // L),
        in_specs=[blk(0), blk(1), blk(2), blk(6),
                  pl.BlockSpec((nb, L, 128), lambda i, c: (i, c, TAIL_IF // 128)),
                  full((CONV_W, 2 * D)), full((1, 2 * D)), full((1, 128)), full((1, D))],
        out_specs=[pl.BlockSpec((nb, L, D), lambda i, c: (i, c, 0)),
                   pl.BlockSpec((nb, H_A, DK, DK), lambda i, c: (i, 0, 0, 0)),
                   pl.BlockSpec((nb, 8, DK), lambda i, c: (i, 0, 0)),
                   pl.BlockSpec((nb, 1, 128), lambda i, c: (i, 0, 0))],
        out_shape=[jax.ShapeDtypeStruct((b, tp, D), main3.dtype),
                   jax.ShapeDtypeStruct((b, H_A, DK, DK), F32),
                   jax.ShapeDtypeStruct((b, 8, DK), F32),
                   jax.ShapeDtypeStruct((b, 1, 128), F32)],
        scratch_shapes=[pltpu.VMEM((nb, 8, D), F32), pltpu.VMEM((nb, 8, D), F32)],
        compiler_params=_cp(("parallel", "arbitrary")),
        name="mlstm_seq",
    )(main3, main3, main3, main3, tail3, conv_w, conv_b, gbias, norm_w)


def _mlstm_step_kernel(tv, nb, qp_ref, kp_ref, v_ref, ga_ref, if_ref, conv0_ref, c0_ref, n0_ref, m0_ref,
                       cw_ref, cb_ref, gb_ref, nw_ref,
                       ya_ref, c_ref, n_ref, m_ref, padq, padk, gpad, lpad, kpad, vpad, wvpad):
    @pl.when(pl.program_id(0) == 0)
    def _():
        for r in (gpad, lpad, kpad, vpad, wvpad):
            r[...] = jnp.zeros_like(r)

    cw = cw_ref[...]
    cb = cb_ref[...]
    nw = nw_ref[...]
    gb = gb_ref[...]
    trow = lax.broadcasted_iota(jnp.int32, (8, 128), 0)
    scol = lax.broadcasted_iota(jnp.int32, (8, 128), 1)
    mask = (scol <= trow) & (scol < tv)
    rvalid = lax.broadcasted_iota(jnp.int32, (8, 1), 0) < tv
    r128 = lax.broadcasted_iota(jnp.int32, (128, 128), 0)
    c128 = lax.broadcasted_iota(jnp.int32, (128, 128), 1)
    tril = jnp.where(c128 <= r128, 1.0, 0.0).astype(F32)
    n_ref[...] = jnp.zeros_like(n_ref)
    m_ref[...] = jnp.zeros_like(m_ref)

    batches = range(nb)
    q_l, gpre_l, bcol_l, gt_l, bt_l, ga_l = [], [], [], [], [], []
    for bi in batches:
        padq[bi, 0:8, :] = conv0_ref[bi, :, 0:D]
        padk[bi, 0:8, :] = conv0_ref[bi, :, D:2 * D]
        padq[bi, 8:16, :] = qp_ref[bi]
        padk[bi, 8:16, :] = kp_ref[bi]
        q_l.append(_silu(_conv4(padq.at[bi], 8, cw[:, 0:D], cb[:, 0:D])))
        kpad[bi, 0:8, :] = _silu(_conv4(padk.at[bi], 8, cw[:, D:2 * D], cb[:, D:2 * D])) * (DK ** -0.5)
        vpad[bi, 0:8, :] = v_ref[bi]
        gpre = if_ref[bi] + gb
        gpad[bi, 0:8, :] = gpre
        lpad[bi, 0:8, :] = _log_sigmoid(gpre)
        gpre_l.append(gpre)
        ga_l.append(_sigmoid(ga_ref[bi]))
    for bi in batches:
        bpad = _dot(tril, lpad[bi], HIGHEST)
        bcol_l.append(bpad[0:8, :])
        bt_l.append(bpad.T)
        gt_l.append(gpad[bi].T)

    probs = [(bi, h) for bi in batches for h in range(H_A)]
    sl_of = lambda h: slice(h * DK, (h + 1) * DK)
    st = {}
    for (bi, h) in probs:
        ig_col = gpre_l[bi][:, h:h + 1]
        b_col = bcol_l[bi][:, H_A + h:H_A + h + 1]
        ig_row = gt_l[bi][h:h + 1, :]
        b_row = bt_l[bi][H_A + h:H_A + h + 1, :]
        m_prev = m0_ref[bi][:, h:h + 1]
        g_col = b_col + m_prev
        dlog = jnp.where(mask, b_col - b_row + ig_row, NEG)
        m_t = jnp.maximum(g_col, jnp.max(dlog, axis=1, keepdims=True))
        b_last = b_col[tv - 1:tv, :]
        wlog = jnp.where(rvalid, b_last - b_col + ig_col, NEG)
        m_new = jnp.maximum(b_last + m_prev, jnp.max(wlog, axis=0, keepdims=True))
        wts = jnp.exp(wlog - m_new)
        wvpad[bi, 0:8, sl_of(h)] = wts * vpad[bi, 0:8, sl_of(h)]
        st[bi, h] = dict(m_t=m_t, w_inter=jnp.exp(g_col - m_t), pm=jnp.exp(dlog - m_t), m_new=m_new,
                         decay=jnp.exp(b_last + m_prev - m_new), wts=wts)
    kb = {(bi, h): kpad[bi, :, sl_of(h)].astype(BF16) for (bi, h) in probs}
    qb = {(bi, h): q_l[bi][:, sl_of(h)].astype(BF16) for (bi, h) in probs}
    s_l = {k_: _dot_nt(qb[k_], kb[k_]) * st[k_]['pm'] for k_ in probs}
    qc_l = {(bi, h): _dot_nt(qb[bi, h], c0_ref[bi, h].astype(BF16)) for (bi, h) in probs}
    sv_l = {(bi, h): _dot(s_l[bi, h].astype(BF16), vpad[bi, :, sl_of(h)].astype(BF16)) for (bi, h) in probs}
    upd_l = {(bi, h): _dot(wvpad[bi, :, sl_of(h)].T.astype(BF16), kb[bi, h]) for (bi, h) in probs}
    for (bi, h) in probs:
        sl = sl_of(h)
        d = st[bi, h]
        nh = n0_ref[bi, h:h + 1, :]
        qh = q_l[bi][:, sl]
        num = d['w_inter'] * qc_l[bi, h] + sv_l[bi, h]
        den = d['w_inter'] * jnp.sum(qh * nh, axis=1, keepdims=True) + jnp.sum(s_l[bi, h], axis=1, keepdims=True)
        hh = num / jnp.maximum(jnp.abs(den), jnp.exp(-d['m_t']))
        ya_ref[bi, :, sl] = ga_l[bi][:, sl] * _head_norm_rows(hh, MLSTM_EPS) * nw[:, sl]
        c_ref[bi, h] = d['decay'] * c0_ref[bi, h] + upd_l[bi, h]
        n_ref[bi, h:h + 1, :] = d['decay'] * nh + jnp.sum(d['wts'] * kpad[bi, 0:8, sl], axis=0, keepdims=True)
        m_ref[bi, :, h:h + 1] = d['m_new']


def _mlstm_step(main3, tail3, conv0p, c0, n0p, m0p, conv_w, conv_b, gbias, norm_w, tv, nb):
    b = main3.shape[0]
    blk = lambda j: pl.BlockSpec((nb, 8, D), lambda i, j=j: (i, 0, j))
    full = lambda shp: pl.BlockSpec(shp, lambda i: (0,) * len(shp))
    state_specs = [pl.BlockSpec((nb, H_A, DK, DK), lambda i: (i, 0, 0, 0)),
                   pl.BlockSpec((nb, 8, DK), lambda i: (i, 0, 0)),
                   pl.BlockSpec((nb, 1, 128), lambda i: (i, 0, 0))]
    return pl.pallas_call(
        functools.partial(_mlstm_step_kernel, tv, nb),
        grid=(b // nb,),
        in_specs=[blk(0), blk(1), blk(2), blk(6),
                  pl.BlockSpec((nb, 8, 128), lambda i: (i, 0, TAIL_IF // 128)),
                  pl.BlockSpec((nb, 8, 2 * D), lambda i: (i, 0, 0))] + state_specs +
                 [full((CONV_W, 2 * D)), full((1, 2 * D)), full((1, 128)), full((1, D))],
        out_specs=[pl.BlockSpec((nb, 8, D), lambda i: (i, 0, 0))] + state_specs,
        out_shape=[jax.ShapeDtypeStruct((b, 8, D), F32),
                   jax.ShapeDtypeStruct((b, H_A, DK, DK), F32),
                   jax.ShapeDtypeStruct((b, 8, DK), F32),
                   jax.ShapeDtypeStruct((b, 1, 128), F32)],
        scratch_shapes=[pltpu.VMEM((nb, 16, D), F32), pltpu.VMEM((nb, 16, D), F32),
                        pltpu.VMEM((nb, 128, 128), F32), pltpu.VMEM((nb, 128, 128), F32),
                        pltpu.VMEM((nb, 128, D), F32), pltpu.VMEM((nb, 128, D), F32),
                        pltpu.VMEM((nb, 128, D), F32)],
        compiler_params=_cp(("arbitrary",)),
        name="mlstm_step",
    )(main3, main3, main3, main3, tail3, conv0p, c0, n0p, m0p, conv_w, conv_b, gbias, norm_w)


def _bd(x, lo):
    return jnp.concatenate([jnp.where(lo, x, 0.0), jnp.where(lo, 0.0, x)], axis=0)


def _pair_sum(x, lo):
    s_lo = jnp.sum(jnp.where(lo, x, 0.0), axis=1, keepdims=True)
    s_hi = jnp.sum(jnp.where(lo, 0.0, x), axis=1, keepdims=True)
    return jnp.where(lo, s_lo, s_hi)


def _rwkv_kernel(nsub, nbg, lb, tv, has_state, *refs):
    (r_ref, k_ref, v_ref, gb_ref, l_ref, ya_ref), refs = refs[:6], refs[6:]
    if has_state:
        (pr_ref, pk_ref, pv_ref, pl_ref, s0_ref), refs = refs[:5], refs[5:]
    (mur_ref, muk_ref, muv_ref, mul_ref, w0_ref, a0_ref, kk_ref, ka_ref, rk_ref,
     lw_ref, lb_ref, w2_ref, a2_ref, g2_ref,
     u_ref, s_ref, sbd, cr, ck, cv, cl) = refs
    L = nbg * lb
    nseq = nsub * nbg
    LT = nsub * L
    z64 = jnp.zeros((HB, HB), F32)

    @pl.when(pl.program_id(1) == 0)
    def _():
        if has_state:
            for gi in range(nseq):
                for p in range(N_PAIR):
                    top = jnp.concatenate([s0_ref[gi, 2 * p], z64], axis=1)
                    bot = jnp.concatenate([z64, s0_ref[gi, 2 * p + 1]], axis=1)
                    sbd[gi, p] = jnp.concatenate([top, bot], axis=0)
            cr[...] = pr_ref[...].astype(F32)
            ck[...] = pk_ref[...].astype(F32)
            cv[...] = pv_ref[...].astype(F32)
            cl[...] = pl_ref[...]
        else:
            sbd[...] = jnp.zeros_like(sbd)
            for c_ in (cr, ck, cv, cl):
                c_[...] = jnp.zeros_like(c_)

    def shift_mix(x_ref, carry, mu_ref):
        x3 = x_ref[...].astype(F32)
        width = x3.shape[-1]
        tpos = lax.broadcasted_iota(jnp.int32, x3.shape, 1)
        prev = jnp.where(tpos == 0, carry[...], pltpu.roll(x3, 1, 1))
        carry[...] = x3[:, lb - 1:lb, :]
        return (x3 + (prev - x3) * mu_ref[...]).reshape(LT, width)

    xr = shift_mix(r_ref, cr, mur_ref)
    xk = shift_mix(k_ref, ck, muk_ref)
    xv = shift_mix(v_ref, cv, muv_ref)
    xl = shift_mix(l_ref, cl, mul_ref)

    lane_l = lax.broadcasted_iota(jnp.int32, (LT, LORA), 1)
    act = jnp.where(lane_l < 64, jnp.tanh(xl), jnp.where(lane_l < 128, xl, _sigmoid(xl))).astype(BF16)
    z = w0_ref[...] + _dot(act, w2_ref[...])
    w_log = -(jnp.maximum(-z, 0.0) + jnp.log(1.0 + jnp.exp(-jnp.abs(z)))) - 0.5
    lw = -jnp.exp(w_log)
    a = _sigmoid(a0_ref[...] + _dot(act, a2_ref[...]))
    g = _dot(act, g2_ref[...])
    kk = xk * kk_ref[...]
    kmod = xk * (1.0 + (a - 1.0) * ka_ref[...])
    t_idx = lax.broadcasted_iota(jnp.int32, (LT, 1), 0)
    if tv < lb:
        valid = (t_idx % lb) < tv
        lw = jnp.where(valid, lw, 0.0)
        kk = jnp.where(valid, kk, 0.0)
        kmod = jnp.where(valid, kmod, 0.0)
        xv = jnp.where(valid, xv, 0.0)

    row = lax.broadcasted_iota(jnp.int32, (L, L), 0)
    col = lax.broadcasted_iota(jnp.int32, (L, L), 1)
    tril = jnp.where((col <= row) & (col // lb == row // lb), 1.0, 0.0).astype(F32)
    subs = range(nsub)
    rows = [slice(s * L, (s + 1) * L) for s in subs]
    cum_s = [_cumsum_rows(tril, lw[rows[s]]) for s in subs]

    lane = lax.broadcasted_iota(jnp.int32, (L, 128), 1)
    lo = lane < HB
    src = lane % HB
    trow = lax.broadcasted_iota(jnp.int32, (L, 128), 0)
    same = (src // lb) == (trow // lb)
    strict = same & (src < trow)
    incl = same & (src <= trow)
    r128 = lax.broadcasted_iota(jnp.int32, (128, 128), 0)
    c128 = lax.broadcasted_iota(jnp.int32, (128, 128), 1)
    blockdiag = (r128 < HB) == (c128 < HB)
    eye_pair = jnp.where(src == trow, 1.0, 0.0).astype(F32)

    sls = [slice(p * 128, (p + 1) * 128) for p in range(N_PAIR)]
    items = [(s, p) for s in subs for p in range(N_PAIR)]
    idx = range(len(items))
    groups = range(nbg)

    at_l, rt_l, bt_l, kt_l, win_l, vp_l = [], [], [], [], [], []
    for (s, p) in items:
        rs, sl = rows[s], sls[p]
        kkp = kk[rs, sl]
        kap = kkp * lax.rsqrt(jnp.maximum(_pair_sum(kkp * kkp, lo), 1e-24))
        cum_p = cum_s[s][:, sl]
        w_in = jnp.exp(cum_p)
        w_inv = jnp.exp(-cum_p)
        at_l.append(-kap * jnp.exp(cum_p - lw[rs, sl]))
        rt_l.append(xr[rs, sl] * w_in)
        bt_l.append(kap * a[rs, sl] * w_inv)
        kt_l.append(kmod[rs, sl] * w_inv)
        win_l.append(w_in)
        vp_l.append(xv[rs, sl])
    bdv_l = [_bd(vp_l[i], lo).astype(BF16) for i in idx]

    gm_l = [_dot_nt(jnp.concatenate([at_l[i], rt_l[i]], axis=0).astype(BF16),
                    jnp.concatenate([_bd(bt_l[i], lo), _bd(kt_l[i], lo)], axis=0).astype(BF16))
            for i in idx]
    n_l = [jnp.where(strict, gm_l[i][0:L, 0:128], 0.0) for i in idx]
    aak_l = [jnp.where(strict, gm_l[i][0:L, 128:256], 0.0).astype(BF16) for i in idx]
    ark_l = [jnp.concatenate([jnp.where(incl, gm_l[i][L:2 * L, 0:128], 0.0),
                              jnp.where(incl, gm_l[i][L:2 * L, 128:256], 0.0)], axis=1).astype(BF16)
             for i in idx]

    xs_l = [[_dot_nt(jnp.concatenate([at_l[i][gi * lb:(gi + 1) * lb], rt_l[i][gi * lb:(gi + 1) * lb]],
                                     axis=0).astype(BF16), sbd[items[i][0] * nbg + gi, items[i][1]].astype(BF16))
             for gi in groups] for i in idx]
    if nbg == 1:
        as_l = [xs_l[i][0][0:lb] for i in idx]
        rs_l = [xs_l[i][0][lb:2 * lb] for i in idx]
    else:
        as_l = [jnp.concatenate([xs_l[i][gi][0:lb] for gi in groups], axis=0) for i in idx]
        rs_l = [jnp.concatenate([xs_l[i][gi][lb:2 * lb] for gi in groups], axis=0) for i in idx]

    y0_l = [as_l[i] + _dot(aak_l[i], bdv_l[i]) for i in idx]

    dm_l = [eye_pair for _ in idx]
    s_blk = 1
    while 2 * s_blk <= lb:
        lvl = ((trow // (2 * s_blk)) == (src // (2 * s_blk))) & ((trow % (2 * s_blk)) >= s_blk) \
            & ((src % (2 * s_blk)) < s_blk)
        if s_blk == 1:
            dm_l = [dm_l[i] + jnp.where(lvl, n_l[i], 0.0) for i in idx]
        else:
            t1_l = [_dot(jnp.where(lvl, n_l[i], 0.0).astype(BF16), _bd(dm_l[i], lo).astype(BF16))
                    for i in idx]
            dm_l = [dm_l[i] + _dot(dm_l[i].astype(BF16), _bd(t1_l[i], lo).astype(BF16)) for i in idx]
        s_blk *= 2
    u_l = [_dot(dm_l[i].astype(BF16), _bd(y0_l[i], lo).astype(BF16)) for i in idx]

    gate_b = _sigmoid(gb_ref[...].astype(F32).reshape(LT, D))
    y_a = ya_ref[...].astype(F32).reshape(LT, D)
    bonus_l = [_pair_sum(xr[rows[s], sls[p]] * kmod[rows[s], sls[p]] * rk_ref[:, sls[p]], lo) * vp_l[i]
               for i, (s, p) in enumerate(items)]

    o_l = [rs_l[i] + _dot(ark_l[i], jnp.concatenate([_bd(u_l[i], lo).astype(BF16), bdv_l[i]], axis=0))
           for i in idx]

    w3_l = [win_l[i].reshape(nbg, lb, 128)[:, lb - 1:lb, :] for i in idx]
    rhs_l = []
    for i in idx:
        w_last = jnp.broadcast_to(w3_l[i], (nbg, lb, 128)).reshape(L, 128)
        rhs_l.append(jnp.concatenate([bt_l[i] * w_last, kt_l[i] * w_last], axis=0).astype(BF16))
    uv_l = [jnp.concatenate([u_l[i], vp_l[i]], axis=0) for i in idx]
    if nbg == 1:
        upd_l = [_dot_tn(uv_l[i].astype(BF16), rhs_l[i]) for i in idx]
        for i, (s, p) in enumerate(items):
            sbd[s, p] = sbd[s, p] * w3_l[i][0] + jnp.where(blockdiag, upd_l[i], 0.0)
    else:
        cgrp = (c128 % L) // lb
        uvt_l = [uv_l[i].T for i in idx]
        for i, (s, p) in enumerate(items):
            for gi in groups:
                upd = _dot(jnp.where(cgrp == gi, uvt_l[i], 0.0).astype(BF16), rhs_l[i])
                q_ = s * nbg + gi
                sbd[q_, p] = sbd[q_, p] * w3_l[i][gi] + jnp.where(blockdiag, upd, 0.0)

    out_l = []
    for i, (s, p) in enumerate(items):
        rs, sl = rows[s], sls[p]
        o = o_l[i]
        mu = _pair_sum(o, lo) * (1.0 / HB)
        oc = o - mu
        var = _pair_sum(oc * oc, lo) * (1.0 / HB)
        on = oc * lax.rsqrt(var + RWKV_EPS) * lw_ref[:, sl] + lb_ref[:, sl]
        yb = (on + bonus_l[i]) * g[rs, sl]
        out_l.append(y_a[rs, sl] + gate_b[rs, sl] * yb)
    u_rows = [jnp.concatenate(out_l[s * N_PAIR:(s + 1) * N_PAIR], axis=1) for s in subs]
    u_all = u_rows[0] if nsub == 1 else jnp.concatenate(u_rows, axis=0)
    u_ref[...] = u_all.reshape(nseq, lb, D).astype(u_ref.dtype)

    @pl.when(pl.program_id(1) == pl.num_programs(1) - 1)
    def _():
        for gi in range(nseq):
            for p in range(N_PAIR):
                s_ref[gi, 2 * p] = sbd[gi, p, 0:HB, 0:HB]
                s_ref[gi, 2 * p + 1] = sbd[gi, p, HB:2 * HB, HB:2 * HB]


def _rwkv(main3, cols, tail3, ya3, prev, s0, prm, nsub, nbg, lb, tv):
    b, tp, _ = main3.shape
    has_state = s0 is not None
    nq = nsub * nbg
    blk = lambda j: pl.BlockSpec((nq, lb, D), lambda i, c, j=j: (i, c, j))
    pblk = lambda j: pl.BlockSpec((nq, 1, D), lambda i, c, j=j: (i, 0, j))
    full = lambda a: pl.BlockSpec(a.shape, lambda i, c: (0,) * a.ndim)
    sblk = pl.BlockSpec((nq, H_B, HB, HB), lambda i, c: (i, 0, 0, 0))
    c_r, c_k, c_v, c_gb = cols
    in_specs = [blk(c_r), blk(c_k), blk(c_v), blk(c_gb),
                pl.BlockSpec((nq, lb, LORA), lambda i, c: (i, c, 0)),
                pl.BlockSpec((nq, lb, D), lambda i, c: (i, c, 0))]
    args = [main3, main3, main3, main3, tail3, ya3]
    if has_state:
        in_specs += [pblk(c_r), pblk(c_k), pblk(c_v), pl.BlockSpec((nq, 1, LORA), lambda i, c: (i, 0, 0)), sblk]
        args += [prev[0], prev[0], prev[0], prev[1], s0]
    in_specs += [full(a) for a in prm]
    args += list(prm)
    return pl.pallas_call(
        functools.partial(_rwkv_kernel, nsub, nbg, lb, tv, has_state),
        grid=(b // nq, tp // lb),
        in_specs=in_specs,
        out_specs=[pl.BlockSpec((nq, lb, D), lambda i, c: (i, c, 0)), sblk],
        out_shape=[jax.ShapeDtypeStruct((b, tp, D), main3.dtype),
                   jax.ShapeDtypeStruct((b, H_B, HB, HB), F32)],
        scratch_shapes=[pltpu.VMEM((nq, N_PAIR, 128, 128), F32),
                        pltpu.VMEM((nq, 1, D), F32), pltpu.VMEM((nq, 1, D), F32),
                        pltpu.VMEM((nq, 1, D), F32), pltpu.VMEM((nq, 1, LORA), F32)],
        compiler_params=_cp(("parallel", "arbitrary")),
        name="rwkv",
    )(*args)


def _tail_kernel(u_ref, x_ref, g1_ref, sh_ref, sc_ref, g2_ref, wo_ref, wu_ref, wd_ref,
                 l1g_ref, l1b_ref, l2g_ref, l2b_ref, o_ref, x1_scr, h_scr, acc):
    bb, tt, _ = x_ref.shape
    j = pl.program_id(2)

    @pl.when(j == 0)
    def _():
        u = u_ref[...].reshape(bb * tt, D).astype(BF16)
        y = _dot(u, wo_ref[...]).reshape(bb, tt, D)
        x1 = _layer_norm(ALPHA * x_ref[...] + g1_ref[...] * y, l1g_ref[...], l1b_ref[...])
        x1_scr[...] = x1
        h_scr[...] = (x1 * (1.0 + sc_ref[...]) + sh_ref[...]).reshape(bb * tt, D).astype(BF16)
        acc[...] = jnp.zeros_like(acc)

    up = jnp.maximum(_dot(h_scr[...], wu_ref[...]), 0.0)
    acc[...] += _dot((up * up).astype(BF16), wd_ref[...])

    @pl.when(j == pl.num_programs(2) - 1)
    def _():
        z = ALPHA * x1_scr[...] + g2_ref[...] * acc[...].reshape(bb, tt, D)
        o_ref[...] = _layer_norm(z, l2g_ref[...], l2b_ref[...])


def _tail(u3, x3, mod3, q, bb, tt):
    b, tp, _ = x3.shape
    blk = pl.BlockSpec((bb, tt, D), lambda i, t, j: (i, t, 0))
    mblk = lambda col: pl.BlockSpec((bb, 1, D), lambda i, t, j, col=col: (i, 0, col))
    full = lambda shp: pl.BlockSpec(shp, lambda i, t, j: (0,) * len(shp))
    return pl.pallas_call(
        _tail_kernel,
        grid=(b // bb, tp // tt, D_FF // FF_CHUNK),
        in_specs=[blk, blk, mblk(2), mblk(3), mblk(4), mblk(5),
                  full((D, D)),
                  pl.BlockSpec((D, FF_CHUNK), lambda i, t, j: (0, j)),
                  pl.BlockSpec((FF_CHUNK, D), lambda i, t, j: (j, 0)),
                  full((1, D)), full((1, D)), full((1, D)), full((1, D))],
        out_specs=blk,
        out_shape=jax.ShapeDtypeStruct((b, tp, D), F32),
        scratch_shapes=[pltpu.VMEM((bb, tt, D), F32), pltpu.VMEM((bb * tt, D), BF16),
                        pltpu.VMEM((bb * tt, D), F32)],
        compiler_params=_cp(("parallel", "parallel", "arbitrary")),
        name="outproj_ffn",
    )(u3, x3, mod3, mod3, mod3, mod3, q['w_out'], q['w_up'], q['w_down'],
      q['ln1_g'], q['ln1_b'], q['ln2_g'], q['ln2_b'])


def _relayout_params(p):
    w = p['w_in']
    w_main = jnp.concatenate(
        [w[:, :3 * D], w[:, 3 * D + 8:6 * D + 8], w[:, 6 * D + 8 + LORA:8 * D + 8 + LORA]], axis=1).astype(BF16)
    w_tail = jnp.concatenate(
        [w[:, 6 * D + 8:6 * D + 8 + LORA], w[:, 3 * D:3 * D + 8],
         jnp.zeros((D, N_TAIL - LORA - 8), F32)], axis=1).astype(BF16)
    mu = p['rwkv_mu']
    z64 = jnp.zeros((64, D), F32)
    z128 = jnp.zeros((128, D), F32)
    row = lambda a: a.reshape(1, -1)
    rw = (row(mu[0:D]), row(mu[D:2 * D]), row(mu[2 * D:3 * D]), row(mu[3 * D:3 * D + LORA]),
          row(p['rwkv_w0']), row(p['rwkv_a0']), row(p['rwkv_k_k']), row(p['rwkv_k_a']),
          row(p['rwkv_r_k']), row(p['rwkv_lnx_w']), row(p['rwkv_lnx_b']),
          jnp.concatenate([p['rwkv_w2'], z64, z128], axis=0).astype(BF16),
          jnp.concatenate([z64, p['rwkv_a2'], z128], axis=0).astype(BF16),
          jnp.concatenate([z128, p['rwkv_g2']], axis=0).astype(BF16))
    gbias = jnp.concatenate([p['mlstm_i_bias'], p['mlstm_f_bias'], jnp.zeros((120,), F32)]).reshape(1, 128)
    return dict(w_main=w_main, w_tail=w_tail, rw=rw, gbias=gbias,
                conv_w=p['conv_w'], conv_b=row(p['conv_b']), norm_w=row(p['mlstm_norm_w']),
                w_out=p['w_out'].astype(BF16), w_up=p['w_up'].astype(BF16), w_down=p['w_down'].astype(BF16),
                ln1_g=row(p['ln1_g']), ln1_b=row(p['ln1_b']), ln2_g=row(p['ln2_g']), ln2_b=row(p['ln2_b']))


def _prompt_layer(x, mod, q, seq_tile, mlstm_chunk):
    b, t, _ = x.shape
    mod3 = mod.reshape(b, 1, N_COND)
    main3, tail3 = _inproj(x, mod3, q['w_main'], q['w_tail'], 1, seq_tile, BF16)
    ya3, c1, n1, m1 = _mlstm_seq(main3, tail3, q['conv_w'], q['conv_b'], q['gbias'], q['norm_w'],
                                 mlstm_chunk, min(2, b))
    u3, s1 = _rwkv(main3, RWKV_SECTIONS, tail3, ya3, None, None, q['rw'], min(2, b), 1, RW_L, RW_L)
    y = _tail(u3, x, mod3, q, 1, seq_tile)
    shift = _modulate_rows(x[:, t - 1, :], mod)
    conv = main3[:, t - (CONV_W - 1):, :2 * D].astype(F32)
    return y, (c1, n1[:, :H_A, :], m1[:, 0, :H_A], conv, s1, shift)


def _sample_layer(x, mod, st, q, bb):
    c0, n0, m0, conv0, s0, shift0 = st
    b, t, _ = x.shape
    mod3 = mod.reshape(b, 1, N_COND)
    xp = jnp.pad(x, ((0, 0), (0, 8 - t), (0, 0)))
    main3, tail3 = _inproj(xp, mod3, q['w_main'], q['w_tail'], bb, 8, F32)
    pm, pt = _inproj(shift0.reshape(1, b, D), jnp.zeros((1, 1, N_COND), F32), q['w_main'], q['w_tail'], 1, b, F32)
    prev = (pm.reshape(b, 1, N_MAIN), pt.reshape(b, 1, N_TAIL))
    conv0p = jnp.pad(conv0, ((0, 0), (8 - (CONV_W - 1), 0), (0, 0)))
    n0p = jnp.pad(n0, ((0, 0), (0, 8 - H_A), (0, 0)))
    m0p = jnp.pad(m0, ((0, 0), (0, 128 - H_A))).reshape(b, 1, 128)
    ya3, c1, n1, m1 = _mlstm_step(main3, tail3, conv0p, c0, n0p, m0p, q['conv_w'], q['conv_b'], q['gbias'],
                                  q['norm_w'], t, min(4, b))
    u3, s1 = _rwkv(main3, RWKV_SECTIONS, tail3, ya3, prev, s0, q['rw'], 1, RW_L // 8, 8, t)
    y = _tail(u3, xp, mod3, q, bb, 8)
    shift = _modulate_rows(x[:, t - 1, :], mod)
    conv = jnp.concatenate([conv0, main3[:, :t, :2 * D]], axis=1)[:, t:, :]
    return y[:, :t, :], (c1, n1[:, :H_A, :], m1[:, 0, :H_A], conv, s1, shift)


def kernel(x_prompt, x_sample, c_prompt, c_sample, state_mlstm_C, state_mlstm_n, state_mlstm_m, state_mlstm_conv, state_rwkv_S, state_rwkv_shift, w_cond, b_cond, w_in, mlstm_i_bias, mlstm_f_bias, conv_w, conv_b, mlstm_norm_w, rwkv_mu, rwkv_w0, rwkv_w2, rwkv_a0, rwkv_a2, rwkv_g2, rwkv_k_k, rwkv_k_a, rwkv_r_k, rwkv_lnx_w, rwkv_lnx_b, w_out, ln1_g, ln1_b, w_up, w_down, ln2_g, ln2_b):
    depth = w_in.shape[0]
    bp = x_prompt.shape[0]
    yp, ys = x_prompt, x_sample
    new_p = [[] for _ in range(6)]
    new_s = [[] for _ in range(6)]
    for l in range(depth):
        p = {'w_in': w_in[l], 'mlstm_i_bias': mlstm_i_bias[l], 'mlstm_f_bias': mlstm_f_bias[l],
             'conv_w': conv_w[l], 'conv_b': conv_b[l], 'mlstm_norm_w': mlstm_norm_w[l],
             'rwkv_mu': rwkv_mu[l], 'rwkv_w0': rwkv_w0[l], 'rwkv_w2': rwkv_w2[l], 'rwkv_a0': rwkv_a0[l],
             'rwkv_a2': rwkv_a2[l], 'rwkv_g2': rwkv_g2[l], 'rwkv_k_k': rwkv_k_k[l], 'rwkv_k_a': rwkv_k_a[l],
             'rwkv_r_k': rwkv_r_k[l].reshape(-1), 'rwkv_lnx_w': rwkv_lnx_w[l], 'rwkv_lnx_b': rwkv_lnx_b[l],
             'w_out': w_out[l], 'ln1_g': ln1_g[l], 'ln1_b': ln1_b[l], 'w_up': w_up[l], 'w_down': w_down[l],
             'ln2_g': ln2_g[l], 'ln2_b': ln2_b[l]}
        q = _relayout_params(p)
        mod = _cond(jnp.concatenate([c_prompt, c_sample], axis=0), w_cond[l], b_cond[l])
        yp, st_p = _prompt_layer(yp, mod[:bp], q, min(1024, yp.shape[1]), min(256, yp.shape[1]))
        st_in = (state_mlstm_C[l], state_mlstm_n[l], state_mlstm_m[l], state_mlstm_conv[l],
                 state_rwkv_S[l], state_rwkv_shift[l])
        ys, st_s = _sample_layer(ys, mod[bp:], st_in, q, min(128, ys.shape[0]))
        for lst, t in zip(new_p, st_p):
            lst.append(t)
        for lst, t in zip(new_s, st_s):
            lst.append(t)
    outs_p = [jnp.stack(t) for t in new_p]
    outs_s = [jnp.stack(t) for t in new_s]
    return (yp, ys, *outs_p, *outs_s)
```

```python
import functools

import jax
import jax.numpy as jnp
from jax import lax
from jax.experimental import pallas as pl
from jax.experimental.pallas import tpu as pltpu

F32 = jnp.float32
BF16 = jnp.bfloat16
HIGHEST = lax.Precision.HIGHEST

D = 1024
H_A = 4
DK = 256
CONV_W = 4
H_B = 16
HB = 64
N_PAIR = H_B // 2
D_FF = 4096
N_COND = 6 * D
ALPHA = 2.0 ** 0.25
LN_EPS = 1e-5
MLSTM_EPS = 1e-6
RWKV_EPS = 64e-5

N_MAIN = 8 * D
RWKV_SECTIONS = (3, 4, 5, 7)
LORA = 256
TAIL_IF = LORA
N_TAIL = 512
TN_MAIN = 2048
N_MAIN_TILES = N_MAIN // TN_MAIN

RW_L = 64
FF_CHUNK = 1024
NEG = -1e30
VMEM_LIMIT = 56 * 1024 * 1024


def _cp(sem):
    return pltpu.CompilerParams(dimension_semantics=sem, vmem_limit_bytes=VMEM_LIMIT)


def _dot(a, b, prec=None):
    return jnp.dot(a, b, preferred_element_type=F32, precision=prec)


def _dot_nt(a, b, prec=None):
    return lax.dot_general(a, b, (((1,), (1,)), ((), ())), preferred_element_type=F32, precision=prec)


def _dot_tn(a, b, prec=None):
    return lax.dot_general(a, b, (((0,), (0,)), ((), ())), preferred_element_type=F32, precision=prec)


def _cumsum_rows(tril01, x):
    hi = x.astype(BF16)
    r1 = x - hi.astype(F32)
    mid = r1.astype(BF16)
    lo = (r1 - mid.astype(F32)).astype(BF16)
    t = tril01.astype(BF16)
    return _dot(t, hi) + _dot(t, mid) + _dot(t, lo)


def _log_sigmoid(x):
    return jnp.minimum(x, 0.0) - jnp.log1p(jnp.exp(-jnp.abs(x)))


def _sigmoid(x):
    return 0.5 * jnp.tanh(0.5 * x) + 0.5


def _silu(x):
    h = 0.5 * x
    return h + h * jnp.tanh(h)


def _layer_norm(z, g, b):
    mu = jnp.mean(z, axis=-1, keepdims=True)
    zc = z - mu
    var = jnp.mean(zc * zc, axis=-1, keepdims=True)
    return zc * lax.rsqrt(var + LN_EPS) * g + b


def _cond_kernel(c_ref, w_ref, b_ref, o_ref):
    s = _silu(c_ref[...]).astype(BF16)
    o_ref[...] = _dot(s, w_ref[...].astype(BF16)) + b_ref[...]


def _cond(c, w_cond, b_cond):
    n = c.shape[0]
    tn = 1536
    return pl.pallas_call(
        _cond_kernel,
        grid=(N_COND // tn,),
        in_specs=[pl.BlockSpec((n, D), lambda j: (0, 0)),
                  pl.BlockSpec((D, tn), lambda j: (0, j)),
                  pl.BlockSpec((1, tn), lambda j: (0, j))],
        out_specs=pl.BlockSpec((n, tn), lambda j: (0, j)),
        out_shape=jax.ShapeDtypeStruct((n, N_COND), F32),
        compiler_params=_cp(("arbitrary",)),
        name="cond",
    )(c, w_cond, b_cond.reshape(1, N_COND))


def _inproj_kernel(x_ref, sh_ref, sc_ref, wm_ref, wt_ref, main_ref, tail_ref, h_scr):
    bb, tt, _ = x_ref.shape
    j = pl.program_id(2)

    @pl.when(j == 0)
    def _():
        h = x_ref[...] * (1.0 + sc_ref[...]) + sh_ref[...]
        h_scr[...] = h.reshape(bb * tt, D).astype(BF16)

    main_ref[...] = _dot(h_scr[...], wm_ref[...]).reshape(bb, tt, TN_MAIN).astype(main_ref.dtype)

    @pl.when(j == N_MAIN_TILES - 1)
    def _():
        tail_ref[...] = _dot(h_scr[...], wt_ref[...]).reshape(bb, tt, N_TAIL)


def _inproj(x3, mod3, w_main, w_tail, bb, tt, main_dtype):
    b, tp, _ = x3.shape
    return pl.pallas_call(
        _inproj_kernel,
        grid=(b // bb, tp // tt, N_MAIN_TILES),
        in_specs=[pl.BlockSpec((bb, tt, D), lambda i, t, j: (i, t, 0)),
                  pl.BlockSpec((bb, 1, D), lambda i, t, j: (i, 0, 0)),
                  pl.BlockSpec((bb, 1, D), lambda i, t, j: (i, 0, 1)),
                  pl.BlockSpec((D, TN_MAIN), lambda i, t, j: (0, j)),
                  pl.BlockSpec((D, N_TAIL), lambda i, t, j: (0, 0))],
        out_specs=[pl.BlockSpec((bb, tt, TN_MAIN), lambda i, t, j: (i, t, j)),
                   pl.BlockSpec((bb, tt, N_TAIL), lambda i, t, j: (i, t, 0))],
        out_shape=[jax.ShapeDtypeStruct((b, tp, N_MAIN), main_dtype),
                   jax.ShapeDtypeStruct((b, tp, N_TAIL), F32)],
        scratch_shapes=[pltpu.VMEM((bb * tt, D), BF16)],
        compiler_params=_cp(("parallel", "parallel", "arbitrary")),
        name="inproj",
    )(x3, mod3, mod3, w_main, w_tail)


def _modulate_kernel(x_ref, sh_ref, sc_ref, o_ref):
    o_ref[...] = x_ref[...] * (1.0 + sc_ref[...]) + sh_ref[...]


def _modulate_rows(x2, mod2):
    n = x2.shape[0]
    return pl.pallas_call(
        _modulate_kernel,
        grid=(1,),
        in_specs=[pl.BlockSpec((n, D), lambda i: (0, 0)),
                  pl.BlockSpec((n, D), lambda i: (0, 0)),
                  pl.BlockSpec((n, D), lambda i: (0, 1))],
        out_specs=pl.BlockSpec((n, D), lambda i: (0, 0)),
        out_shape=jax.ShapeDtypeStruct((n, D), F32),
        name="modulate_last",
    )(x2, mod2, mod2)


def _conv4(pad_ref, n_rows, cw, cb):
    acc = cb + pad_ref[8:8 + n_rows, :] * cw[3:4, :]
    acc = acc + pad_ref[7:7 + n_rows, :] * cw[2:3, :]
    acc = acc + pad_ref[6:6 + n_rows, :] * cw[1:2, :]
    acc = acc + pad_ref[5:5 + n_rows, :] * cw[0:1, :]
    return acc


def _head_norm_rows(h, eps):
    mu = jnp.mean(h, axis=-1, keepdims=True)
    hc = h - mu
    var = jnp.mean(hc * hc, axis=-1, keepdims=True)
    return hc * lax.rsqrt(var + eps)


def _mlstm_seq_kernel(nb, qp_ref, kp_ref, v_ref, ga_ref, if_ref, cw_ref, cb_ref, gb_ref, nw_ref,
                      ya_ref, c_ref, n_ref, m_ref, haloq, halok):
    L = qp_ref.shape[1]
    assert qp_ref.dtype == BF16 and kp_ref.dtype == BF16

    @pl.when(pl.program_id(1) == 0)
    def _():
        c_ref[...] = jnp.zeros_like(c_ref)
        n_ref[...] = jnp.zeros_like(n_ref)
        m_ref[...] = jnp.zeros_like(m_ref)
        haloq[...] = jnp.zeros_like(haloq)
        halok[...] = jnp.zeros_like(halok)

    cw = cw_ref[...]
    cb = cb_ref[...]
    nw = nw_ref[...]
    row = lax.broadcasted_iota(jnp.int32, (L, L), 0)
    col = lax.broadcasted_iota(jnp.int32, (L, L), 1)
    causal = col <= row
    tril = jnp.where(causal, 1.0, 0.0).astype(F32)
    srow = lax.broadcasted_iota(jnp.int32, (3 * L, L), 0)
    scol = lax.broadcasted_iota(jnp.int32, (3 * L, L), 1)
    shift_mat = jnp.where(scol + srow // L + 1 == srow % L, 1.0, 0.0).astype(BF16)
    r8 = lax.broadcasted_iota(jnp.int32, (8, 1), 0)

    def conv(x_bf, halo_ref, cw_, cb_):
        sh = _dot(shift_mat, x_bf)
        x = x_bf.astype(F32)
        acc = cb_ + x * cw_[3:4, :] + sh[0:L] * cw_[2:3, :] + sh[L:2 * L] * cw_[1:2, :] \
            + sh[2 * L:3 * L] * cw_[0:1, :]
        halo = halo_ref[...]
        fix = jnp.zeros((8, DK), F32)
        for j in range(1, CONV_W):
            fix = fix + jnp.where(r8 < j, pltpu.roll(halo, j, 0), 0.0) * cw_[CONV_W - 1 - j:CONV_W - j, :]
        halo_ref[...] = x[L - 8:L, :]
        return jnp.concatenate([acc[0:8] + fix, acc[8:L]], axis=0)

    items = [(bi, h) for bi in range(nb) for h in range(H_A)]
    sl_of = lambda h: slice(h * DK, (h + 1) * DK)
    q_it, k_it = {}, {}
    for (bi, h) in items:
        sl, ksl = sl_of(h), slice(D + h * DK, D + (h + 1) * DK)
        q_it[bi, h] = _silu(conv(qp_ref[bi, :, sl], haloq.at[bi, :, sl], cw[:, sl], cb[:, sl]))
        k_it[bi, h] = _silu(conv(kp_ref[bi, :, sl], halok.at[bi, :, sl], cw[:, ksl], cb[:, ksl])) * (DK ** -0.5)

    gpre_l, bcum_l, gpre_t_l, bcum_t_l = [], [], [], []
    for bi in range(nb):
        gpre = if_ref[bi] + gb_ref[...]
        bcum = _dot(tril, _log_sigmoid(gpre), HIGHEST)
        gpre_l.append(gpre)
        bcum_l.append(bcum)
        gpre_t_l.append(gpre.T)
        bcum_t_l.append(bcum.T)

    st = {}
    for (bi, h) in items:
        ig_col = gpre_l[bi][:, h:h + 1]
        b_col = bcum_l[bi][:, H_A + h:H_A + h + 1]
        ig_row = gpre_t_l[bi][h:h + 1, :]
        b_row = bcum_t_l[bi][H_A + h:H_A + h + 1, :]
        m_prev = m_ref[bi][:, h:h + 1]
        g_col = b_col + m_prev
        dlog = jnp.where(causal, b_col - b_row + ig_row, NEG)
        m_t = jnp.maximum(g_col, jnp.max(dlog, axis=1, keepdims=True))
        b_last = b_col[L - 1:L, :]
        wlog = b_last - b_col + ig_col
        m_new = jnp.maximum(b_last + m_prev, jnp.max(wlog, axis=0, keepdims=True))
        st[bi, h] = dict(m_t=m_t, w_inter=jnp.exp(g_col - m_t), p=jnp.exp(dlog - m_t), m_new=m_new,
                         decay=jnp.exp(b_last + m_prev - m_new), wts=jnp.exp(wlog - m_new))
    qb = {it: q_it[it].astype(BF16) for it in items}
    kb = {it: k_it[it].astype(BF16) for it in items}
    s_l = {it: _dot_nt(qb[it], kb[it]) * st[it]['p'] for it in items}
    qc_l = {(bi, h): _dot_nt(qb[bi, h], c_ref[bi, h].astype(BF16)) for (bi, h) in items}
    sv_l = {(bi, h): _dot(s_l[bi, h].astype(BF16), v_ref[bi, :, sl_of(h)].astype(BF16)) for (bi, h) in items}
    upd_l = {(bi, h): _dot_tn((st[bi, h]['wts'] * v_ref[bi, :, sl_of(h)].astype(F32)).astype(BF16), kb[bi, h])
             for (bi, h) in items}
    for (bi, h) in items:
        sl = sl_of(h)
        d = st[bi, h]
        qh = q_it[bi, h]
        kh = k_it[bi, h]
        nh = n_ref[bi, h:h + 1, :]
        num = d['w_inter'] * qc_l[bi, h] + sv_l[bi, h]
        den = d['w_inter'] * jnp.sum(qh * nh, axis=1, keepdims=True) + jnp.sum(s_l[bi, h], axis=1, keepdims=True)
        hh = num / jnp.maximum(jnp.abs(den), jnp.exp(-d['m_t']))
        ga = _sigmoid(ga_ref[bi, :, sl].astype(F32))
        ya_ref[bi, :, sl] = (ga * _head_norm_rows(hh, MLSTM_EPS) * nw[:, sl]).astype(ya_ref.dtype)
        c_ref[bi, h] = d['decay'] * c_ref[bi, h] + upd_l[bi, h]
        n_ref[bi, h:h + 1, :] = d['decay'] * nh + jnp.sum(d['wts'] * kh, axis=0, keepdims=True)
        m_ref[bi, :, h:h + 1] = d['m_new']


def _mlstm_seq(main3, tail3, conv_w, conv_b, gbias, norm_w, L, nb):
    b, tp, _ = main3.shape
    blk = lambda j: pl.BlockSpec((nb, L, D), lambda i, c, j=j: (i, c, j))
    full = lambda shp: pl.BlockSpec(shp, lambda i, c: (0,) * len(shp))
    return pl.pallas_call(
        functools.partial(_mlstm_seq_kernel, nb),
        grid=(b // nb, tp // L),
        in_specs=[blk(0), blk(1), blk(2), blk(6),
                  pl.BlockSpec((nb, L, 128), lambda i, c: (i, c, TAIL_IF // 128)),
                  full((CONV_W, 2 * D)), full((1, 2 * D)), full((1, 128)), full((1, D))],
        out_specs=[pl.BlockSpec((nb, L, D), lambda i, c: (i, c, 0)),
                   pl.BlockSpec((nb, H_A, DK, DK), lambda i, c: (i, 0, 0, 0)),
                   pl.BlockSpec((nb, 8, DK), lambda i, c: (i, 0, 0)),
                   pl.BlockSpec((nb, 1, 128), lambda i, c: (i, 0, 0))],
        out_shape=[jax.ShapeDtypeStruct((b, tp, D), main3.dtype),
                   jax.ShapeDtypeStruct((b, H_A, DK, DK), F32),
                   jax.ShapeDtypeStruct((b, 8, DK), F32),
                   jax.ShapeDtypeStruct((b, 1, 128), F32)],
        scratch_shapes=[pltpu.VMEM((nb, 8, D), F32), pltpu.VMEM((nb, 8, D), F32)],
        compiler_params=_cp(("parallel", "arbitrary")),
        name="mlstm_seq",
    )(main3, main3, main3, main3, tail3, conv_w, conv_b, gbias, norm_w)


def _mlstm_step_kernel(tv, nb, qp_ref, kp_ref, v_ref, ga_ref, if_ref, conv0_ref, c0_ref, n0_ref, m0_ref,
                       cw_ref, cb_ref, gb_ref, nw_ref,
                       ya_ref, c_ref, n_ref, m_ref, padq, padk, gpad, lpad, kpad, vpad, wvpad):
    @pl.when(pl.program_id(0) == 0)
    def _():
        for r in (gpad, lpad, kpad, vpad, wvpad):
            r[...] = jnp.zeros_like(r)

    cw = cw_ref[...]
    cb = cb_ref[...]
    nw = nw_ref[...]
    gb = gb_ref[...]
    trow = lax.broadcasted_iota(jnp.int32, (8, 128), 0)
    scol = lax.broadcasted_iota(jnp.int32, (8, 128), 1)
    mask = (scol <= trow) & (scol < tv)
    rvalid = lax.broadcasted_iota(jnp.int32, (8, 1), 0) < tv
    r128 = lax.broadcasted_iota(jnp.int32, (128, 128), 0)
    c128 = lax.broadcasted_iota(jnp.int32, (128, 128), 1)
    tril = jnp.where(c128 <= r128, 1.0, 0.0).astype(F32)
    n_ref[...] = jnp.zeros_like(n_ref)
    m_ref[...] = jnp.zeros_like(m_ref)

    batches = range(nb)
    q_l, gpre_l, bcol_l, gt_l, bt_l, ga_l = [], [], [], [], [], []
    for bi in batches:
        padq[bi, 0:8, :] = conv0_ref[bi, :, 0:D]
        padk[bi, 0:8, :] = conv0_ref[bi, :, D:2 * D]
        padq[bi, 8:16, :] = qp_ref[bi]
        padk[bi, 8:16, :] = kp_ref[bi]
        q_l.append(_silu(_conv4(padq.at[bi], 8, cw[:, 0:D], cb[:, 0:D])))
        kpad[bi, 0:8, :] = _silu(_conv4(padk.at[bi], 8, cw[:, D:2 * D], cb[:, D:2 * D])) * (DK ** -0.5)
        vpad[bi, 0:8, :] = v_ref[bi]
        gpre = if_ref[bi] + gb
        gpad[bi, 0:8, :] = gpre
        lpad[bi, 0:8, :] = _log_sigmoid(gpre)
        gpre_l.append(gpre)
        ga_l.append(_sigmoid(ga_ref[bi]))
    for bi in batches:
        bpad = _dot(tril, lpad[bi], HIGHEST)
        bcol_l.append(bpad[0:8, :])
        bt_l.append(bpad.T)
        gt_l.append(gpad[bi].T)

    probs = [(bi, h) for bi in batches for h in range(H_A)]
    sl_of = lambda h: slice(h * DK, (h + 1) * DK)
    st = {}
    for (bi, h) in probs:
        ig_col = gpre_l[bi][:, h:h + 1]
        b_col = bcol_l[bi][:, H_A + h:H_A + h + 1]
        ig_row = gt_l[bi][h:h + 1, :]
        b_row = bt_l[bi][H_A + h:H_A + h + 1, :]
        m_prev = m0_ref[bi][:, h:h + 1]
        g_col = b_col + m_prev
        dlog = jnp.where(mask, b_col - b_row + ig_row, NEG)
        m_t = jnp.maximum(g_col, jnp.max(dlog, axis=1, keepdims=True))
        b_last = b_col[tv - 1:tv, :]
        wlog = jnp.where(rvalid, b_last - b_col + ig_col, NEG)
        m_new = jnp.maximum(b_last + m_prev, jnp.max(wlog, axis=0, keepdims=True))
        wts = jnp.exp(wlog - m_new)
        wvpad[bi, 0:8, sl_of(h)] = wts * vpad[bi, 0:8, sl_of(h)]
        st[bi, h] = dict(m_t=m_t, w_inter=jnp.exp(g_col - m_t), pm=jnp.exp(dlog - m_t), m_new=m_new,
                         decay=jnp.exp(b_last + m_prev - m_new), wts=wts)
    kb = {(bi, h): kpad[bi, :, sl_of(h)].astype(BF16) for (bi, h) in probs}
    qb = {(bi, h): q_l[bi][:, sl_of(h)].astype(BF16) for (bi, h) in probs}
    s_l = {k_: _dot_nt(qb[k_], kb[k_]) * st[k_]['pm'] for k_ in probs}
    qc_l = {(bi, h): _dot_nt(qb[bi, h], c0_ref[bi, h].astype(BF16)) for (bi, h) in probs}
    sv_l = {(bi, h): _dot(s_l[bi, h].astype(BF16), vpad[bi, :, sl_of(h)].astype(BF16)) for (bi, h) in probs}
    upd_l = {(bi, h): _dot(wvpad[bi, :, sl_of(h)].T.astype(BF16), kb[bi, h]) for (bi, h) in probs}
    for (bi, h) in probs:
        sl = sl_of(h)
        d = st[bi, h]
        nh = n0_ref[bi, h:h + 1, :]
        qh = q_l[bi][:, sl]
        num = d['w_inter'] * qc_l[bi, h] + sv_l[bi, h]
        den = d['w_inter'] * jnp.sum(qh * nh, axis=1, keepdims=True) + jnp.sum(s_l[bi, h], axis=1, keepdims=True)
        hh = num / jnp.maximum(jnp.abs(den), jnp.exp(-d['m_t']))
        ya_ref[bi, :, sl] = ga_l[bi][:, sl] * _head_norm_rows(hh, MLSTM_EPS) * nw[:, sl]
        c_ref[bi, h] = d['decay'] * c0_ref[bi, h] + upd_l[bi, h]
        n_ref[bi, h:h + 1, :] = d['decay'] * nh + jnp.sum(d['wts'] * kpad[bi, 0:8, sl], axis=0, keepdims=True)
        m_ref[bi, :, h:h + 1] = d['m_new']


def _mlstm_step(main3, tail3, conv0p, c0, n0p, m0p, conv_w, conv_b, gbias, norm_w, tv, nb):
    b = main3.shape[0]
    blk = lambda j: pl.BlockSpec((nb, 8, D), lambda i, j=j: (i, 0, j))
    full = lambda shp: pl.BlockSpec(shp, lambda i: (0,) * len(shp))
    state_specs = [pl.BlockSpec((nb, H_A, DK, DK), lambda i: (i, 0, 0, 0)),
                   pl.BlockSpec((nb, 8, DK), lambda i: (i, 0, 0)),
                   pl.BlockSpec((nb, 1, 128), lambda i: (i, 0, 0))]
    return pl.pallas_call(
        functools.partial(_mlstm_step_kernel, tv, nb),
        grid=(b // nb,),
        in_specs=[blk(0), blk(1), blk(2), blk(6),
                  pl.BlockSpec((nb, 8, 128), lambda i: (i, 0, TAIL_IF // 128)),
                  pl.BlockSpec((nb, 8, 2 * D), lambda i: (i, 0, 0))] + state_specs +
                 [full((CONV_W, 2 * D)), full((1, 2 * D)), full((1, 128)), full((1, D))],
        out_specs=[pl.BlockSpec((nb, 8, D), lambda i: (i, 0, 0))] + state_specs,
        out_shape=[jax.ShapeDtypeStruct((b, 8, D), F32),
                   jax.ShapeDtypeStruct((b, H_A, DK, DK), F32),
                   jax.ShapeDtypeStruct((b, 8, DK), F32),
                   jax.ShapeDtypeStruct((b, 1, 128), F32)],
        scratch_shapes=[pltpu.VMEM((nb, 16, D), F32), pltpu.VMEM((nb, 16, D), F32),
                        pltpu.VMEM((nb, 128, 128), F32), pltpu.VMEM((nb, 128, 128), F32),
                        pltpu.VMEM((nb, 128, D), F32), pltpu.VMEM((nb, 128, D), F32),
                        pltpu.VMEM((nb, 128, D), F32)],
        compiler_params=_cp(("arbitrary",)),
        name="mlstm_step",
    )(main3, main3, main3, main3, tail3, conv0p, c0, n0p, m0p, conv_w, conv_b, gbias, norm_w)


def _bd(x, lo):
    return jnp.concatenate([jnp.where(lo, x, 0.0), jnp.where(lo, 0.0, x)], axis=0)


def _pair_sum(x, lo):
    s_lo = jnp.sum(jnp.where(lo, x, 0.0), axis=1, keepdims=True)
    s_hi = jnp.sum(jnp.where(lo, 0.0, x), axis=1, keepdims=True)
    return jnp.where(lo, s_lo, s_hi)


def _rwkv_kernel(nsub, nbg, lb, tv, has_state, *refs):
    (r_ref, k_ref, v_ref, gb_ref, l_ref, ya_ref), refs = refs[:6], refs[6:]
    if has_state:
        (pr_ref, pk_ref, pv_ref, pl_ref, s0_ref), refs = refs[:5], refs[5:]
    (mur_ref, muk_ref, muv_ref, mul_ref, w0_ref, a0_ref, kk_ref, ka_ref, rk_ref,
     lw_ref, lb_ref, w2_ref, a2_ref, g2_ref,
     u_ref, s_ref, sbd, cr, ck, cv, cl) = refs
    L = nbg * lb
    nseq = nsub * nbg
    LT = nsub * L
    z64 = jnp.zeros((HB, HB), F32)

    @pl.when(pl.program_id(1) == 0)
    def _():
        if has_state:
            for gi in range(nseq):
                for p in range(N_PAIR):
                    top = jnp.concatenate([s0_ref[gi, 2 * p], z64], axis=1)
                    bot = jnp.concatenate([z64, s0_ref[gi, 2 * p + 1]], axis=1)
                    sbd[gi, p] = jnp.concatenate([top, bot], axis=0)
            cr[...] = pr_ref[...].astype(F32)
            ck[...] = pk_ref[...].astype(F32)
            cv[...] = pv_ref[...].astype(F32)
            cl[...] = pl_ref[...]
        else:
            sbd[...] = jnp.zeros_like(sbd)
            for c_ in (cr, ck, cv, cl):
                c_[...] = jnp.zeros_like(c_)

    def shift_mix(x_ref, carry, mu_ref):
        x3 = x_ref[...].astype(F32)
        width = x3.shape[-1]
        tpos = lax.broadcasted_iota(jnp.int32, x3.shape, 1)
        prev = jnp.where(tpos == 0, carry[...], pltpu.roll(x3, 1, 1))
        carry[...] = x3[:, lb - 1:lb, :]
        return (x3 + (prev - x3) * mu_ref[...]).reshape(LT, width)

    xr = shift_mix(r_ref, cr, mur_ref)
    xk = shift_mix(k_ref, ck, muk_ref)
    xv = shift_mix(v_ref, cv, muv_ref)
    xl = shift_mix(l_ref, cl, mul_ref)

    lane_l = lax.broadcasted_iota(jnp.int32, (LT, LORA), 1)
    act = jnp.where(lane_l < 64, jnp.tanh(xl), jnp.where(lane_l < 128, xl, _sigmoid(xl))).astype(BF16)
    z = w0_ref[...] + _dot(act, w2_ref[...])
    w_log = -(jnp.maximum(-z, 0.0) + jnp.log(1.0 + jnp.exp(-jnp.abs(z)))) - 0.5
    lw = -jnp.exp(w_log)
    a = _sigmoid(a0_ref[...] + _dot(act, a2_ref[...]))
    g = _dot(act, g2_ref[...])
    kk = xk * kk_ref[...]
    kmod = xk * (1.0 + (a - 1.0) * ka_ref[...])
    t_idx = lax.broadcasted_iota(jnp.int32, (LT, 1), 0)
    if tv < lb:
        valid = (t_idx % lb) < tv
        lw = jnp.where(valid, lw, 0.0)
        kk = jnp.where(valid, kk, 0.0)
        kmod = jnp.where(valid, kmod, 0.0)
        xv = jnp.where(valid, xv, 0.0)

    row = lax.broadcasted_iota(jnp.int32, (L, L), 0)
    col = lax.broadcasted_iota(jnp.int32, (L, L), 1)
    tril = jnp.where((col <= row) & (col // lb == row // lb), 1.0, 0.0).astype(F32)
    subs = range(nsub)
    rows = [slice(s * L, (s + 1) * L) for s in subs]
    cum_s = [_cumsum_rows(tril, lw[rows[s]]) for s in subs]

    lane = lax.broadcasted_iota(jnp.int32, (L, 128), 1)
    lo = lane < HB
    src = lane % HB
    trow = lax.broadcasted_iota(jnp.int32, (L, 128), 0)
    same = (src // lb) == (trow // lb)
    strict = same & (src < trow)
    incl = same & (src <= trow)
    r128 = lax.broadcasted_iota(jnp.int32, (128, 128), 0)
    c128 = lax.broadcasted_iota(jnp.int32, (128, 128), 1)
    blockdiag = (r128 < HB) == (c128 < HB)
    eye_pair = jnp.where(src == trow, 1.0, 0.0).astype(F32)

    sls = [slice(p * 128, (p + 1) * 128) for p in range(N_PAIR)]
    items = [(s, p) for s in subs for p in range(N_PAIR)]
    idx = range(len(items))
    groups = range(nbg)

    at_l, rt_l, bt_l, kt_l, win_l, vp_l = [], [], [], [], [], []
    for (s, p) in items:
        rs, sl = rows[s], sls[p]
        kkp = kk[rs, sl]
        kap = kkp * lax.rsqrt(jnp.maximum(_pair_sum(kkp * kkp, lo), 1e-24))
        cum_p = cum_s[s][:, sl]
        w_in = jnp.exp(cum_p)
        w_inv = jnp.exp(-cum_p)
        at_l.append(-kap * jnp.exp(cum_p - lw[rs, sl]))
        rt_l.append(xr[rs, sl] * w_in)
        bt_l.append(kap * a[rs, sl] * w_inv)
        kt_l.append(kmod[rs, sl] * w_inv)
        win_l.append(w_in)
        vp_l.append(xv[rs, sl])
    bdv_l = [_bd(vp_l[i], lo).astype(BF16) for i in idx]

    gm_l = [_dot_nt(jnp.concatenate([at_l[i], rt_l[i]], axis=0).astype(BF16),
                    jnp.concatenate([_bd(bt_l[i], lo), _bd(kt_l[i], lo)], axis=0).astype(BF16))
            for i in idx]
    n_l = [jnp.where(strict, gm_l[i][0:L, 0:128], 0.0) for i in idx]
    aak_l = [jnp.where(strict, gm_l[i][0:L, 128:256], 0.0).astype(BF16) for i in idx]
    ark_l = [jnp.concatenate([jnp.where(incl, gm_l[i][L:2 * L, 0:128], 0.0),
                              jnp.where(incl, gm_l[i][L:2 * L, 128:256], 0.0)], axis=1).astype(BF16)
             for i in idx]

    xs_l = [[_dot_nt(jnp.concatenate([at_l[i][gi * lb:(gi + 1) * lb], rt_l[i][gi * lb:(gi + 1) * lb]],
                                     axis=0).astype(BF16), sbd[items[i][0] * nbg + gi, items[i][1]].astype(BF16))
             for gi in groups] for i in idx]
    if nbg == 1:
        as_l = [xs_l[i][0][0:lb] for i in idx]
        rs_l = [xs_l[i][0][lb:2 * lb] for i in idx]
    else:
        as_l = [jnp.concatenate([xs_l[i][gi][0:lb] for gi in groups], axis=0) for i in idx]
        rs_l = [jnp.concatenate([xs_l[i][gi][lb:2 * lb] for gi in groups], axis=0) for i in idx]

    y0_l = [as_l[i] + _dot(aak_l[i], bdv_l[i]) for i in idx]

    dm_l = [eye_pair for _ in idx]
    s_blk = 1
    while 2 * s_blk <= lb:
        lvl = ((trow // (2 * s_blk)) == (src // (2 * s_blk))) & ((trow % (2 * s_blk)) >= s_blk) \
            & ((src % (2 * s_blk)) < s_blk)
        if s_blk == 1:
            dm_l = [dm_l[i] + jnp.where(lvl, n_l[i], 0.0) for i in idx]
        else:
            t1_l = [_dot(jnp.where(lvl, n_l[i], 0.0).astype(BF16), _bd(dm_l[i], lo).astype(BF16))
                    for i in idx]
            dm_l = [dm_l[i] + _dot(dm_l[i].astype(BF16), _bd(t1_l[i], lo).astype(BF16)) for i in idx]
        s_blk *= 2
    u_l = [_dot(dm_l[i].astype(BF16), _bd(y0_l[i], lo).astype(BF16)) for i in idx]

    gate_b = _sigmoid(gb_ref[...].astype(F32).reshape(LT, D))
    y_a = ya_ref[...].astype(F32).reshape(LT, D)
    bonus_l = [_pair_sum(xr[rows[s], sls[p]] * kmod[rows[s], sls[p]] * rk_ref[:, sls[p]], lo) * vp_l[i]
               for i, (s, p) in enumerate(items)]

    o_l = [rs_l[i] + _dot(ark_l[i], jnp.concatenate([_bd(u_l[i], lo).astype(BF16), bdv_l[i]], axis=0))
           for i in idx]

    w3_l = [win_l[i].reshape(nbg, lb, 128)[:, lb - 1:lb, :] for i in idx]
    rhs_l = []
    for i in idx:
        w_last = jnp.broadcast_to(w3_l[i], (nbg, lb, 128)).reshape(L, 128)
        rhs_l.append(jnp.concatenate([bt_l[i] * w_last, kt_l[i] * w_last], axis=0).astype(BF16))
    uv_l = [jnp.concatenate([u_l[i], vp_l[i]], axis=0) for i in idx]
    if nbg == 1:
        upd_l = [_dot_tn(uv_l[i].astype(BF16), rhs_l[i]) for i in idx]
        for i, (s, p) in enumerate(items):
            sbd[s, p] = sbd[s, p] * w3_l[i][0] + jnp.where(blockdiag, upd_l[i], 0.0)
    else:
        cgrp = (c128 % L) // lb
        uvt_l = [uv_l[i].T for i in idx]
        for i, (s, p) in enumerate(items):
            for gi in groups:
                upd = _dot(jnp.where(cgrp == gi, uvt_l[i], 0.0).astype(BF16), rhs_l[i])
                q_ = s * nbg + gi
                sbd[q_, p] = sbd[q_, p] * w3_l[i][gi] + jnp.where(blockdiag, upd, 0.0)

    out_l = []
    for i, (s, p) in enumerate(items):
        rs, sl = rows[s], sls[p]
        o = o_l[i]
        mu = _pair_sum(o, lo) * (1.0 / HB)
        oc = o - mu
        var = _pair_sum(oc * oc, lo) * (1.0 / HB)
        on = oc * lax.rsqrt(var + RWKV_EPS) * lw_ref[:, sl] + lb_ref[:, sl]
        yb = (on + bonus_l[i]) * g[rs, sl]
        out_l.append(y_a[rs, sl] + gate_b[rs, sl] * yb)
    u_rows = [jnp.concatenate(out_l[s * N_PAIR:(s + 1) * N_PAIR], axis=1) for s in subs]
    u_all = u_rows[0] if nsub == 1 else jnp.concatenate(u_rows, axis=0)
    u_ref[...] = u_all.reshape(nseq, lb, D).astype(u_ref.dtype)

    @pl.when(pl.program_id(1) == pl.num_programs(1) - 1)
    def _():
        for gi in range(nseq):
            for p in range(N_PAIR):
                s_ref[gi, 2 * p] = sbd[gi, p, 0:HB, 0:HB]
                s_ref[gi, 2 * p + 1] = sbd[gi, p, HB:2 * HB, HB:2 * HB]


def _rwkv(main3, cols, tail3, ya3, prev, s0, prm, nsub, nbg, lb, tv):
    b, tp, _ = main3.shape
    has_state = s0 is not None
    nq = nsub * nbg
    blk = lambda j: pl.BlockSpec((nq, lb, D), lambda i, c, j=j: (i, c, j))
    pblk = lambda j: pl.BlockSpec((nq, 1, D), lambda i, c, j=j: (i, 0, j))
    full = lambda a: pl.BlockSpec(a.shape, lambda i, c: (0,) * a.ndim)
    sblk = pl.BlockSpec((nq, H_B, HB, HB), lambda i, c: (i, 0, 0, 0))
    c_r, c_k, c_v, c_gb = cols
    in_specs = [blk(c_r), blk(c_k), blk(c_v), blk(c_gb),
                pl.BlockSpec((nq, lb, LORA), lambda i, c: (i, c, 0)),
                pl.BlockSpec((nq, lb, D), lambda i, c: (i, c, 0))]
    args = [main3, main3, main3, main3, tail3, ya3]
    if has_state:
        in_specs += [pblk(c_r), pblk(c_k), pblk(c_v), pl.BlockSpec((nq, 1, LORA), lambda i, c: (i, 0, 0)), sblk]
        args += [prev[0], prev[0], prev[0], prev[1], s0]
    in_specs += [full(a) for a in prm]
    args += list(prm)
    return pl.pallas_call(
        functools.partial(_rwkv_kernel, nsub, nbg, lb, tv, has_state),
        grid=(b // nq, tp // lb),
        in_specs=in_specs,
        out_specs=[pl.BlockSpec((nq, lb, D), lambda i, c: (i, c, 0)), sblk],
        out_shape=[jax.ShapeDtypeStruct((b, tp, D), main3.dtype),
                   jax.ShapeDtypeStruct((b, H_B, HB, HB), F32)],
        scratch_shapes=[pltpu.VMEM((nq, N_PAIR, 128, 128), F32),
                        pltpu.VMEM((nq, 1, D), F32), pltpu.VMEM((nq, 1, D), F32),
                        pltpu.VMEM((nq, 1, D), F32), pltpu.VMEM((nq, 1, LORA), F32)],
        compiler_params=_cp(("parallel", "arbitrary")),
        name="rwkv",
    )(*args)


def _tail_kernel(u_ref, x_ref, g1_ref, sh_ref, sc_ref, g2_ref, wo_ref, wu_ref, wd_ref,
                 l1g_ref, l1b_ref, l2g_ref, l2b_ref, o_ref, x1_scr, h_scr, acc):
    bb, tt, _ = x_ref.shape
    j = pl.program_id(2)

    @pl.when(j == 0)
    def _():
        u = u_ref[...].reshape(bb * tt, D).astype(BF16)
        y = _dot(u, wo_ref[...]).reshape(bb, tt, D)
        x1 = _layer_norm(ALPHA * x_ref[...] + g1_ref[...] * y, l1g_ref[...], l1b_ref[...])
        x1_scr[...] = x1
        h_scr[...] = (x1 * (1.0 + sc_ref[...]) + sh_ref[...]).reshape(bb * tt, D).astype(BF16)
        acc[...] = jnp.zeros_like(acc)

    up = jnp.maximum(_dot(h_scr[...], wu_ref[...]), 0.0)
    acc[...] += _dot((up * up).astype(BF16), wd_ref[...])

    @pl.when(j == pl.num_programs(2) - 1)
    def _():
        z = ALPHA * x1_scr[...] + g2_ref[...] * acc[...].reshape(bb, tt, D)
        o_ref[...] = _layer_norm(z, l2g_ref[...], l2b_ref[...])


def _tail(u3, x3, mod3, q, bb, tt):
    b, tp, _ = x3.shape
    blk = pl.BlockSpec((bb, tt, D), lambda i, t, j: (i, t, 0))
    mblk = lambda col: pl.BlockSpec((bb, 1, D), lambda i, t, j, col=col: (i, 0, col))
    full = lambda shp: pl.BlockSpec(shp, lambda i, t, j: (0,) * len(shp))
    return pl.pallas_call(
        _tail_kernel,
        grid=(b // bb, tp // tt, D_FF // FF_CHUNK),
        in_specs=[blk, blk, mblk(2), mblk(3), mblk(4), mblk(5),
                  full((D, D)),
                  pl.BlockSpec((D, FF_CHUNK), lambda i, t, j: (0, j)),
                  pl.BlockSpec((FF_CHUNK, D), lambda i, t, j: (j, 0)),
                  full((1, D)), full((1, D)), full((1, D)), full((1, D))],
        out_specs=blk,
        out_shape=jax.ShapeDtypeStruct((b, tp, D), F32),
        scratch_shapes=[pltpu.VMEM((bb, tt, D), F32), pltpu.VMEM((bb * tt, D), BF16),
                        pltpu.VMEM((bb * tt, D), F32)],
        compiler_params=_cp(("parallel", "parallel", "arbitrary")),
        name="outproj_ffn",
    )(u3, x3, mod3, mod3, mod3, mod3, q['w_out'], q['w_up'], q['w_down'],
      q['ln1_g'], q['ln1_b'], q['ln2_g'], q['ln2_b'])


def _relayout_params(p):
    w = p['w_in']
    w_main = jnp.concatenate(
        [w[:, :3 * D], w[:, 3 * D + 8:6 * D + 8], w[:, 6 * D + 8 + LORA:8 * D + 8 + LORA]], axis=1).astype(BF16)
    w_tail = jnp.concatenate(
        [w[:, 6 * D + 8:6 * D + 8 + LORA], w[:, 3 * D:3 * D + 8],
         jnp.zeros((D, N_TAIL - LORA - 8), F32)], axis=1).astype(BF16)
    mu = p['rwkv_mu']
    z64 = jnp.zeros((64, D), F32)
    z128 = jnp.zeros((128, D), F32)
    row = lambda a: a.reshape(1, -1)
    rw = (row(mu[0:D]), row(mu[D:2 * D]), row(mu[2 * D:3 * D]), row(mu[3 * D:3 * D + LORA]),
          row(p['rwkv_w0']), row(p['rwkv_a0']), row(p['rwkv_k_k']), row(p['rwkv_k_a']),
          row(p['rwkv_r_k']), row(p['rwkv_lnx_w']), row(p['rwkv_lnx_b']),
          jnp.concatenate([p['rwkv_w2'], z64, z128], axis=0).astype(BF16),
          jnp.concatenate([z64, p['rwkv_a2'], z128], axis=0).astype(BF16),
          jnp.concatenate([z128, p['rwkv_g2']], axis=0).astype(BF16))
    gbias = jnp.concatenate([p['mlstm_i_bias'], p['mlstm_f_bias'], jnp.zeros((120,), F32)]).reshape(1, 128)
    return dict(w_main=w_main, w_tail=w_tail, rw=rw, gbias=gbias,
                conv_w=p['conv_w'], conv_b=row(p['conv_b']), norm_w=row(p['mlstm_norm_w']),
                w_out=p['w_out'].astype(BF16), w_up=p['w_up'].astype(BF16), w_down=p['w_down'].astype(BF16),
                ln1_g=row(p['ln1_g']), ln1_b=row(p['ln1_b']), ln2_g=row(p['ln2_g']), ln2_b=row(p['ln2_b']))


def _prompt_layer(x, mod, q, seq_tile, mlstm_chunk):
    b, t, _ = x.shape
    mod3 = mod.reshape(b, 1, N_COND)
    main3, tail3 = _inproj(x, mod3, q['w_main'], q['w_tail'], 1, seq_tile, BF16)
    ya3, c1, n1, m1 = _mlstm_seq(main3, tail3, q['conv_w'], q['conv_b'], q['gbias'], q['norm_w'],
                                 mlstm_chunk, min(2, b))
    u3, s1 = _rwkv(main3, RWKV_SECTIONS, tail3, ya3, None, None, q['rw'], min(2, b), 1, RW_L, RW_L)
    y = _tail(u3, x, mod3, q, 1, seq_tile)
    shift = _modulate_rows(x[:, t - 1, :], mod)
    conv = main3[:, t - (CONV_W - 1):, :2 * D].astype(F32)
    return y, (c1, n1[:, :H_A, :], m1[:, 0, :H_A], conv, s1, shift)


def _sample_layer(x, mod, st, q, bb):
    c0, n0, m0, conv0, s0, shift0 = st
    b, t, _ = x.shape
    mod3 = mod.reshape(b, 1, N_COND)
    xp = jnp.pad(x, ((0, 0), (0, 8 - t), (0, 0)))
    main3, tail3 = _inproj(xp, mod3, q['w_main'], q['w_tail'], bb, 8, F32)
    pm, pt = _inproj(shift0.reshape(1, b, D), jnp.zeros((1, 1, N_COND), F32), q['w_main'], q['w_tail'], 1, b, F32)
    prev = (pm.reshape(b, 1, N_MAIN), pt.reshape(b, 1, N_TAIL))
    conv0p = jnp.pad(conv0, ((0, 0), (8 - (CONV_W - 1), 0), (0, 0)))
    n0p = jnp.pad(n0, ((0, 0), (0, 8 - H_A), (0, 0)))
    m0p = jnp.pad(m0, ((0, 0), (0, 128 - H_A))).reshape(b, 1, 128)
    ya3, c1, n1, m1 = _mlstm_step(main3, tail3, conv0p, c0, n0p, m0p, q['conv_w'], q['conv_b'], q['gbias'],
                                  q['norm_w'], t, min(4, b))
    u3, s1 = _rwkv(main3, RWKV_SECTIONS, tail3, ya3, prev, s0, q['rw'], 1, RW_L // 8, 8, t)
    y = _tail(u3, xp, mod3, q, bb, 8)
    shift = _modulate_rows(x[:, t - 1, :], mod)
    conv = jnp.concatenate([conv0, main3[:, :t, :2 * D]], axis=1)[:, t:, :]
    return y[:, :t, :], (c1, n1[:, :H_A, :], m1[:, 0, :H_A], conv, s1, shift)


def kernel(x_prompt, x_sample, c_prompt, c_sample, state_mlstm_C, state_mlstm_n, state_mlstm_m, state_mlstm_conv, state_rwkv_S, state_rwkv_shift, w_cond, b_cond, w_in, mlstm_i_bias, mlstm_f_bias, conv_w, conv_b, mlstm_norm_w, rwkv_mu, rwkv_w0, rwkv_w2, rwkv_a0, rwkv_a2, rwkv_g2, rwkv_k_k, rwkv_k_a, rwkv_r_k, rwkv_lnx_w, rwkv_lnx_b, w_out, ln1_g, ln1_b, w_up, w_down, ln2_g, ln2_b):
    depth = w_in.shape[0]
    bp = x_prompt.shape[0]
    yp, ys = x_prompt, x_sample
    new_p = [[] for _ in range(6)]
    new_s = [[] for _ in range(6)]
    for l in range(depth):
        p = {'w_in': w_in[l], 'mlstm_i_bias': mlstm_i_bias[l], 'mlstm_f_bias': mlstm_f_bias[l],
             'conv_w': conv_w[l], 'conv_b': conv_b[l], 'mlstm_norm_w': mlstm_norm_w[l],
             'rwkv_mu': rwkv_mu[l], 'rwkv_w0': rwkv_w0[l], 'rwkv_w2': rwkv_w2[l], 'rwkv_a0': rwkv_a0[l],
             'rwkv_a2': rwkv_a2[l], 'rwkv_g2': rwkv_g2[l], 'rwkv_k_k': rwkv_k_k[l], 'rwkv_k_a': rwkv_k_a[l],
             'rwkv_r_k': rwkv_r_k[l].reshape(-1), 'rwkv_lnx_w': rwkv_lnx_w[l], 'rwkv_lnx_b': rwkv_lnx_b[l],
             'w_out': w_out[l], 'ln1_g': ln1_g[l], 'ln1_b': ln1_b[l], 'w_up': w_up[l], 'w_down': w_down[l],
             'ln2_g': ln2_g[l], 'ln2_b': ln2_b[l]}
        q = _relayout_params(p)
        mod = _cond(jnp.concatenate([c_prompt, c_sample], axis=0), w_cond[l], b_cond[l])
        st_in = (state_mlstm_C[l], state_mlstm_n[l], state_mlstm_m[l], state_mlstm_conv[l],
                 state_rwkv_S[l], state_rwkv_shift[l])
        ys, st_s = _sample_layer(ys, mod[bp:], st_in, q, min(128, ys.shape[0]))
        yp, st_p = _prompt_layer(yp, mod[:bp], q, min(1024, yp.shape[1]), min(256, yp.shape[1]))
        for lst, t in zip(new_p, st_p):
            lst.append(t)
        for lst, t in zip(new_s, st_s):
            lst.append(t)
    outs_p = [jnp.stack(t) for t in new_p]
    outs_s = [jnp.stack(t) for t in new_s]
    return (yp, ys, *outs_p, *outs_s)
```

```python
import functools

import jax
import jax.numpy as jnp
from jax import lax
from jax.experimental import pallas as pl
from jax.experimental.pallas import tpu as pltpu

F32 = jnp.float32
BF16 = jnp.bfloat16
HIGHEST = lax.Precision.HIGHEST

D = 1024
H_A = 4
DK = 256
CONV_W = 4
H_B = 16
HB = 64
N_PAIR = H_B // 2
D_FF = 4096
N_COND = 6 * D
ALPHA = 2.0 ** 0.25
LN_EPS = 1e-5
MLSTM_EPS = 1e-6
RWKV_EPS = 64e-5

N_MAIN = 8 * D
RWKV_SECTIONS = (3, 4, 5, 7)
LORA = 256
TAIL_IF = LORA
N_TAIL = 512
TN_MAIN = 2048
N_MAIN_TILES = N_MAIN // TN_MAIN

RW_L = 64
PREP_EVERY = 6
FF_CHUNK = 1024
NEG = -1e30
VMEM_LIMIT = 56 * 1024 * 1024


def _cp(sem):
    return pltpu.CompilerParams(dimension_semantics=sem, vmem_limit_bytes=VMEM_LIMIT)


def _dot(a, b, prec=None):
    return jnp.dot(a, b, preferred_element_type=F32, precision=prec)


def _dot_nt(a, b, prec=None):
    return lax.dot_general(a, b, (((1,), (1,)), ((), ())), preferred_element_type=F32, precision=prec)


def _dot_tn(a, b, prec=None):
    return lax.dot_general(a, b, (((0,), (0,)), ((), ())), preferred_element_type=F32, precision=prec)


def _cumsum_rows(tril01, x):
    hi = x.astype(BF16)
    r1 = x - hi.astype(F32)
    mid = r1.astype(BF16)
    lo = (r1 - mid.astype(F32)).astype(BF16)
    t = tril01.astype(BF16)
    return _dot(t, hi) + _dot(t, mid) + _dot(t, lo)


def _log_sigmoid(x):
    return jnp.minimum(x, 0.0) - jnp.log1p(jnp.exp(-jnp.abs(x)))


def _sigmoid(x):
    return 0.5 * jnp.tanh(0.5 * x) + 0.5


def _silu(x):
    h = 0.5 * x
    return h + h * jnp.tanh(h)


def _layer_norm(z, g, b):
    mu = jnp.mean(z, axis=-1, keepdims=True)
    zc = z - mu
    var = jnp.mean(zc * zc, axis=-1, keepdims=True)
    return zc * lax.rsqrt(var + LN_EPS) * g + b


def _cond_kernel(c_ref, w_ref, b_ref, o_ref):
    s = _silu(c_ref[...]).astype(BF16)
    o_ref[...] = _dot(s, w_ref[...].astype(BF16)) + b_ref[...]


def _cond(c, w_cond, b_cond):
    n = c.shape[0]
    tn = 1536
    return pl.pallas_call(
        _cond_kernel,
        grid=(N_COND // tn,),
        in_specs=[pl.BlockSpec((n, D), lambda j: (0, 0)),
                  pl.BlockSpec((D, tn), lambda j: (0, j)),
                  pl.BlockSpec((1, tn), lambda j: (0, j))],
        out_specs=pl.BlockSpec((n, tn), lambda j: (0, j)),
        out_shape=jax.ShapeDtypeStruct((n, N_COND), F32),
        compiler_params=_cp(("arbitrary",)),
        name="cond",
    )(c, w_cond, b_cond.reshape(1, N_COND))


def _inproj_kernel(x_ref, sh_ref, sc_ref, wm_ref, wt_ref, main_ref, tail_ref, h_scr):
    bb, tt, _ = x_ref.shape
    j = pl.program_id(2)

    @pl.when(j == 0)
    def _():
        h = x_ref[...] * (1.0 + sc_ref[...]) + sh_ref[...]
        h_scr[...] = h.reshape(bb * tt, D).astype(BF16)

    main_ref[...] = _dot(h_scr[...], wm_ref[...]).reshape(bb, tt, TN_MAIN).astype(main_ref.dtype)

    @pl.when(j == N_MAIN_TILES - 1)
    def _():
        tail_ref[...] = _dot(h_scr[...], wt_ref[...]).reshape(bb, tt, N_TAIL)


def _inproj(x3, mod3, w_main, w_tail, bb, tt, main_dtype):
    b, tp, _ = x3.shape
    return pl.pallas_call(
        _inproj_kernel,
        grid=(b // bb, tp // tt, N_MAIN_TILES),
        in_specs=[pl.BlockSpec((bb, tt, D), lambda i, t, j: (i, t, 0)),
                  pl.BlockSpec((bb, 1, D), lambda i, t, j: (i, 0, 0)),
                  pl.BlockSpec((bb, 1, D), lambda i, t, j: (i, 0, 1)),
                  pl.BlockSpec((D, TN_MAIN), lambda i, t, j: (0, j)),
                  pl.BlockSpec((D, N_TAIL), lambda i, t, j: (0, 0))],
        out_specs=[pl.BlockSpec((bb, tt, TN_MAIN), lambda i, t, j: (i, t, j)),
                   pl.BlockSpec((bb, tt, N_TAIL), lambda i, t, j: (i, t, 0))],
        out_shape=[jax.ShapeDtypeStruct((b, tp, N_MAIN), main_dtype),
                   jax.ShapeDtypeStruct((b, tp, N_TAIL), F32)],
        scratch_shapes=[pltpu.VMEM((bb * tt, D), BF16)],
        compiler_params=_cp(("parallel", "parallel", "arbitrary")),
        name="inproj",
    )(x3, mod3, mod3, w_main, w_tail)


def _modulate_kernel(x_ref, sh_ref, sc_ref, o_ref):
    o_ref[...] = x_ref[...] * (1.0 + sc_ref[...]) + sh_ref[...]


def _modulate_rows(x2, mod2):
    n = x2.shape[0]
    return pl.pallas_call(
        _modulate_kernel,
        grid=(1,),
        in_specs=[pl.BlockSpec((n, D), lambda i: (0, 0)),
                  pl.BlockSpec((n, D), lambda i: (0, 0)),
                  pl.BlockSpec((n, D), lambda i: (0, 1))],
        out_specs=pl.BlockSpec((n, D), lambda i: (0, 0)),
        out_shape=jax.ShapeDtypeStruct((n, D), F32),
        name="modulate_last",
    )(x2, mod2, mod2)


def _conv4(pad_ref, n_rows, cw, cb):
    acc = cb + pad_ref[8:8 + n_rows, :] * cw[3:4, :]
    acc = acc + pad_ref[7:7 + n_rows, :] * cw[2:3, :]
    acc = acc + pad_ref[6:6 + n_rows, :] * cw[1:2, :]
    acc = acc + pad_ref[5:5 + n_rows, :] * cw[0:1, :]
    return acc


def _head_norm_rows(h, eps):
    mu = jnp.mean(h, axis=-1, keepdims=True)
    hc = h - mu
    var = jnp.mean(hc * hc, axis=-1, keepdims=True)
    return hc * lax.rsqrt(var + eps)


def _mlstm_seq_kernel(nb, qp_ref, kp_ref, v_ref, ga_ref, if_ref, cw_ref, cb_ref, gb_ref, nw_ref,
                      ya_ref, c_ref, n_ref, m_ref, haloq, halok):
    L = qp_ref.shape[1]
    assert qp_ref.dtype == BF16 and kp_ref.dtype == BF16

    @pl.when(pl.program_id(1) == 0)
    def _():
        c_ref[...] = jnp.zeros_like(c_ref)
        n_ref[...] = jnp.zeros_like(n_ref)
        m_ref[...] = jnp.zeros_like(m_ref)
        haloq[...] = jnp.zeros_like(haloq)
        halok[...] = jnp.zeros_like(halok)

    cw = cw_ref[...]
    cb = cb_ref[...]
    nw = nw_ref[...]
    row = lax.broadcasted_iota(jnp.int32, (L, L), 0)
    col = lax.broadcasted_iota(jnp.int32, (L, L), 1)
    causal = col <= row
    tril = jnp.where(causal, 1.0, 0.0).astype(F32)
    srow = lax.broadcasted_iota(jnp.int32, (3 * L, L), 0)
    scol = lax.broadcasted_iota(jnp.int32, (3 * L, L), 1)
    shift_mat = jnp.where(scol + srow // L + 1 == srow % L, 1.0, 0.0).astype(BF16)
    r8 = lax.broadcasted_iota(jnp.int32, (8, 1), 0)

    def conv(x_bf, halo_ref, cw_, cb_):
        sh = _dot(shift_mat, x_bf)
        x = x_bf.astype(F32)
        acc = cb_ + x * cw_[3:4, :] + sh[0:L] * cw_[2:3, :] + sh[L:2 * L] * cw_[1:2, :] \
            + sh[2 * L:3 * L] * cw_[0:1, :]
        halo = halo_ref[...]
        fix = jnp.zeros((8, DK), F32)
        for j in range(1, CONV_W):
            fix = fix + jnp.where(r8 < j, pltpu.roll(halo, j, 0), 0.0) * cw_[CONV_W - 1 - j:CONV_W - j, :]
        halo_ref[...] = x[L - 8:L, :]
        return jnp.concatenate([acc[0:8] + fix, acc[8:L]], axis=0)

    items = [(bi, h) for bi in range(nb) for h in range(H_A)]
    sl_of = lambda h: slice(h * DK, (h + 1) * DK)
    q_it, k_it = {}, {}
    for (bi, h) in items:
        sl, ksl = sl_of(h), slice(D + h * DK, D + (h + 1) * DK)
        q_it[bi, h] = _silu(conv(qp_ref[bi, :, sl], haloq.at[bi, :, sl], cw[:, sl], cb[:, sl]))
        k_it[bi, h] = _silu(conv(kp_ref[bi, :, sl], halok.at[bi, :, sl], cw[:, ksl], cb[:, ksl])) * (DK ** -0.5)

    gpre_l, bcum_l, gpre_t_l, bcum_t_l = [], [], [], []
    for bi in range(nb):
        gpre = if_ref[bi] + gb_ref[...]
        bcum = _dot(tril, _log_sigmoid(gpre), HIGHEST)
        gpre_l.append(gpre)
        bcum_l.append(bcum)
        gpre_t_l.append(gpre.T)
        bcum_t_l.append(bcum.T)

    st = {}
    for (bi, h) in items:
        ig_col = gpre_l[bi][:, h:h + 1]
        b_col = bcum_l[bi][:, H_A + h:H_A + h + 1]
        ig_row = gpre_t_l[bi][h:h + 1, :]
        b_row = bcum_t_l[bi][H_A + h:H_A + h + 1, :]
        m_prev = m_ref[bi][:, h:h + 1]
        g_col = b_col + m_prev
        dlog = jnp.where(causal, b_col - b_row + ig_row, NEG)
        m_t = jnp.maximum(g_col, jnp.max(dlog, axis=1, keepdims=True))
        b_last = b_col[L - 1:L, :]
        wlog = b_last - b_col + ig_col
        m_new = jnp.maximum(b_last + m_prev, jnp.max(wlog, axis=0, keepdims=True))
        st[bi, h] = dict(m_t=m_t, w_inter=jnp.exp(g_col - m_t), p=jnp.exp(dlog - m_t), m_new=m_new,
                         decay=jnp.exp(b_last + m_prev - m_new), wts=jnp.exp(wlog - m_new))
    qb = {it: q_it[it].astype(BF16) for it in items}
    kb = {it: k_it[it].astype(BF16) for it in items}
    s_l = {it: _dot_nt(qb[it], kb[it]) * st[it]['p'] for it in items}
    qc_l = {(bi, h): _dot_nt(qb[bi, h], c_ref[bi, h].astype(BF16)) for (bi, h) in items}
    sv_l = {(bi, h): _dot(s_l[bi, h].astype(BF16), v_ref[bi, :, sl_of(h)].astype(BF16)) for (bi, h) in items}
    upd_l = {(bi, h): _dot_tn((st[bi, h]['wts'] * v_ref[bi, :, sl_of(h)].astype(F32)).astype(BF16), kb[bi, h])
             for (bi, h) in items}
    for (bi, h) in items:
        sl = sl_of(h)
        d = st[bi, h]
        qh = q_it[bi, h]
        kh = k_it[bi, h]
        nh = n_ref[bi, h:h + 1, :]
        num = d['w_inter'] * qc_l[bi, h] + sv_l[bi, h]
        den = d['w_inter'] * jnp.sum(qh * nh, axis=1, keepdims=True) + jnp.sum(s_l[bi, h], axis=1, keepdims=True)
        hh = num / jnp.maximum(jnp.abs(den), jnp.exp(-d['m_t']))
        ga = _sigmoid(ga_ref[bi, :, sl].astype(F32))
        ya_ref[bi, :, sl] = (ga * _head_norm_rows(hh, MLSTM_EPS) * nw[:, sl]).astype(ya_ref.dtype)
        c_ref[bi, h] = d['decay'] * c_ref[bi, h] + upd_l[bi, h]
        n_ref[bi, h:h + 1, :] = d['decay'] * nh + jnp.sum(d['wts'] * kh, axis=0, keepdims=True)
        m_ref[bi, :, h:h + 1] = d['m_new']


def _mlstm_seq(main3, tail3, conv_w, conv_b, gbias, norm_w, L, nb):
    b, tp, _ = main3.shape
    blk = lambda j: pl.BlockSpec((nb, L, D), lambda i, c, j=j: (i, c, j))
    full = lambda shp: pl.BlockSpec(shp, lambda i, c: (0,) * len(shp))
    return pl.pallas_call(
        functools.partial(_mlstm_seq_kernel, nb),
        grid=(b // nb, tp // L),
        in_specs=[blk(0), blk(1), blk(2), blk(6),
                  pl.BlockSpec((nb, L, 128), lambda i, c: (i, c, TAIL_IF // 128)),
                  full((CONV_W, 2 * D)), full((1, 2 * D)), full((1, 128)), full((1, D))],
        out_specs=[pl.BlockSpec((nb, L, D), lambda i, c: (i, c, 0)),
                   pl.BlockSpec((nb, H_A, DK, DK), lambda i, c: (i, 0, 0, 0)),
                   pl.BlockSpec((nb, 8, DK), lambda i, c: (i, 0, 0)),
                   pl.BlockSpec((nb, 1, 128), lambda i, c: (i, 0, 0))],
        out_shape=[jax.ShapeDtypeStruct((b, tp, D), main3.dtype),
                   jax.ShapeDtypeStruct((b, H_A, DK, DK), F32),
                   jax.ShapeDtypeStruct((b, 8, DK), F32),
                   jax.ShapeDtypeStruct((b, 1, 128), F32)],
        scratch_shapes=[pltpu.VMEM((nb, 8, D), F32), pltpu.VMEM((nb, 8, D), F32)],
        compiler_params=_cp(("parallel", "arbitrary")),
        name="mlstm_seq",
    )(main3, main3, main3, main3, tail3, conv_w, conv_b, gbias, norm_w)


def _mlstm_step_kernel(tv, nb, qp_ref, kp_ref, v_ref, ga_ref, if_ref, conv0_ref, c0_ref, n0_ref, m0_ref,
                       cw_ref, cb_ref, gb_ref, nw_ref,
                       ya_ref, c_ref, n_ref, m_ref, padq, padk, gpad, lpad, kpad, vpad, wvpad):
    @pl.when(pl.program_id(0) == 0)
    def _():
        for r in (gpad, lpad, kpad, vpad, wvpad):
            r[...] = jnp.zeros_like(r)

    cw = cw_ref[...]
    cb = cb_ref[...]
    nw = nw_ref[...]
    gb = gb_ref[...]
    trow = lax.broadcasted_iota(jnp.int32, (8, 128), 0)
    scol = lax.broadcasted_iota(jnp.int32, (8, 128), 1)
    mask = (scol <= trow) & (scol < tv)
    rvalid = lax.broadcasted_iota(jnp.int32, (8, 1), 0) < tv
    r128 = lax.broadcasted_iota(jnp.int32, (128, 128), 0)
    c128 = lax.broadcasted_iota(jnp.int32, (128, 128), 1)
    tril = jnp.where(c128 <= r128, 1.0, 0.0).astype(F32)
    n_ref[...] = jnp.zeros_like(n_ref)
    m_ref[...] = jnp.zeros_like(m_ref)

    batches = range(nb)
    q_l, gpre_l, bcol_l, gt_l, bt_l, ga_l = [], [], [], [], [], []
    for bi in batches:
        padq[bi, 0:8, :] = conv0_ref[bi, :, 0:D]
        padk[bi, 0:8, :] = conv0_ref[bi, :, D:2 * D]
        padq[bi, 8:16, :] = qp_ref[bi]
        padk[bi, 8:16, :] = kp_ref[bi]
        q_l.append(_silu(_conv4(padq.at[bi], 8, cw[:, 0:D], cb[:, 0:D])))
        kpad[bi, 0:8, :] = _silu(_conv4(padk.at[bi], 8, cw[:, D:2 * D], cb[:, D:2 * D])) * (DK ** -0.5)
        vpad[bi, 0:8, :] = v_ref[bi]
        gpre = if_ref[bi] + gb
        gpad[bi, 0:8, :] = gpre
        lpad[bi, 0:8, :] = _log_sigmoid(gpre)
        gpre_l.append(gpre)
        ga_l.append(_sigmoid(ga_ref[bi]))
    for bi in batches:
        bpad = _dot(tril, lpad[bi], HIGHEST)
        bcol_l.append(bpad[0:8, :])
        bt_l.append(bpad.T)
        gt_l.append(gpad[bi].T)

    probs = [(bi, h) for bi in batches for h in range(H_A)]
    sl_of = lambda h: slice(h * DK, (h + 1) * DK)
    st = {}
    for (bi, h) in probs:
        ig_col = gpre_l[bi][:, h:h + 1]
        b_col = bcol_l[bi][:, H_A + h:H_A + h + 1]
        ig_row = gt_l[bi][h:h + 1, :]
        b_row = bt_l[bi][H_A + h:H_A + h + 1, :]
        m_prev = m0_ref[bi][:, h:h + 1]
        g_col = b_col + m_prev
        dlog = jnp.where(mask, b_col - b_row + ig_row, NEG)
        m_t = jnp.maximum(g_col, jnp.max(dlog, axis=1, keepdims=True))
        b_last = b_col[tv - 1:tv, :]
        wlog = jnp.where(rvalid, b_last - b_col + ig_col, NEG)
        m_new = jnp.maximum(b_last + m_prev, jnp.max(wlog, axis=0, keepdims=True))
        wts = jnp.exp(wlog - m_new)
        wvpad[bi, 0:8, sl_of(h)] = wts * vpad[bi, 0:8, sl_of(h)]
        st[bi, h] = dict(m_t=m_t, w_inter=jnp.exp(g_col - m_t), pm=jnp.exp(dlog - m_t), m_new=m_new,
                         decay=jnp.exp(b_last + m_prev - m_new), wts=wts)
    kb = {(bi, h): kpad[bi, :, sl_of(h)].astype(BF16) for (bi, h) in probs}
    qb = {(bi, h): q_l[bi][:, sl_of(h)].astype(BF16) for (bi, h) in probs}
    s_l = {k_: _dot_nt(qb[k_], kb[k_]) * st[k_]['pm'] for k_ in probs}
    qc_l = {(bi, h): _dot_nt(qb[bi, h], c0_ref[bi, h].astype(BF16)) for (bi, h) in probs}
    sv_l = {(bi, h): _dot(s_l[bi, h].astype(BF16), vpad[bi, :, sl_of(h)].astype(BF16)) for (bi, h) in probs}
    upd_l = {(bi, h): _dot(wvpad[bi, :, sl_of(h)].T.astype(BF16), kb[bi, h]) for (bi, h) in probs}
    for (bi, h) in probs:
        sl = sl_of(h)
        d = st[bi, h]
        nh = n0_ref[bi, h:h + 1, :]
        qh = q_l[bi][:, sl]
        num = d['w_inter'] * qc_l[bi, h] + sv_l[bi, h]
        den = d['w_inter'] * jnp.sum(qh * nh, axis=1, keepdims=True) + jnp.sum(s_l[bi, h], axis=1, keepdims=True)
        hh = num / jnp.maximum(jnp.abs(den), jnp.exp(-d['m_t']))
        ya_ref[bi, :, sl] = ga_l[bi][:, sl] * _head_norm_rows(hh, MLSTM_EPS) * nw[:, sl]
        c_ref[bi, h] = d['decay'] * c0_ref[bi, h] + upd_l[bi, h]
        n_ref[bi, h:h + 1, :] = d['decay'] * nh + jnp.sum(d['wts'] * kpad[bi, 0:8, sl], axis=0, keepdims=True)
        m_ref[bi, :, h:h + 1] = d['m_new']


def _mlstm_step(main3, tail3, conv0p, c0, n0p, m0p, conv_w, conv_b, gbias, norm_w, tv, nb):
    b = main3.shape[0]
    blk = lambda j: pl.BlockSpec((nb, 8, D), lambda i, j=j: (i, 0, j))
    full = lambda shp: pl.BlockSpec(shp, lambda i: (0,) * len(shp))
    state_specs = [pl.BlockSpec((nb, H_A, DK, DK), lambda i: (i, 0, 0, 0)),
                   pl.BlockSpec((nb, 8, DK), lambda i: (i, 0, 0)),
                   pl.BlockSpec((nb, 1, 128), lambda i: (i, 0, 0))]
    return pl.pallas_call(
        functools.partial(_mlstm_step_kernel, tv, nb),
        grid=(b // nb,),
        in_specs=[blk(0), blk(1), blk(2), blk(6),
                  pl.BlockSpec((nb, 8, 128), lambda i: (i, 0, TAIL_IF // 128)),
                  pl.BlockSpec((nb, 8, 2 * D), lambda i: (i, 0, 0))] + state_specs +
                 [full((CONV_W, 2 * D)), full((1, 2 * D)), full((1, 128)), full((1, D))],
        out_specs=[pl.BlockSpec((nb, 8, D), lambda i: (i, 0, 0))] + state_specs,
        out_shape=[jax.ShapeDtypeStruct((b, 8, D), F32),
                   jax.ShapeDtypeStruct((b, H_A, DK, DK), F32),
                   jax.ShapeDtypeStruct((b, 8, DK), F32),
                   jax.ShapeDtypeStruct((b, 1, 128), F32)],
        scratch_shapes=[pltpu.VMEM((nb, 16, D), F32), pltpu.VMEM((nb, 16, D), F32),
                        pltpu.VMEM((nb, 128, 128), F32), pltpu.VMEM((nb, 128, 128), F32),
                        pltpu.VMEM((nb, 128, D), F32), pltpu.VMEM((nb, 128, D), F32),
                        pltpu.VMEM((nb, 128, D), F32)],
        compiler_params=_cp(("arbitrary",)),
        name="mlstm_step",
    )(main3, main3, main3, main3, tail3, conv0p, c0, n0p, m0p, conv_w, conv_b, gbias, norm_w)


def _bd(x, lo):
    return jnp.concatenate([jnp.where(lo, x, 0.0), jnp.where(lo, 0.0, x)], axis=0)


def _pair_sum(x, lo):
    s_lo = jnp.sum(jnp.where(lo, x, 0.0), axis=1, keepdims=True)
    s_hi = jnp.sum(jnp.where(lo, 0.0, x), axis=1, keepdims=True)
    return jnp.where(lo, s_lo, s_hi)


def _rwkv_kernel(nsub, nbg, lb, tv, has_state, *refs):
    (r_ref, k_ref, v_ref, gb_ref, l_ref, ya_ref), refs = refs[:6], refs[6:]
    if has_state:
        (pr_ref, pk_ref, pv_ref, pl_ref, s0_ref), refs = refs[:5], refs[5:]
    (mur_ref, muk_ref, muv_ref, mul_ref, w0_ref, a0_ref, kk_ref, ka_ref, rk_ref,
     lw_ref, lb_ref, w2_ref, a2_ref, g2_ref,
     u_ref, s_ref, sbd, cr, ck, cv, cl) = refs
    L = nbg * lb
    nseq = nsub * nbg
    LT = nsub * L
    z64 = jnp.zeros((HB, HB), F32)

    @pl.when(pl.program_id(1) == 0)
    def _():
        if has_state:
            for gi in range(nseq):
                for p in range(N_PAIR):
                    top = jnp.concatenate([s0_ref[gi, 2 * p], z64], axis=1)
                    bot = jnp.concatenate([z64, s0_ref[gi, 2 * p + 1]], axis=1)
                    sbd[gi, p] = jnp.concatenate([top, bot], axis=0)
            cr[...] = pr_ref[...].astype(F32)
            ck[...] = pk_ref[...].astype(F32)
            cv[...] = pv_ref[...].astype(F32)
            cl[...] = pl_ref[...]
        else:
            sbd[...] = jnp.zeros_like(sbd)
            for c_ in (cr, ck, cv, cl):
                c_[...] = jnp.zeros_like(c_)

    def shift_mix(x_ref, carry, mu_ref):
        x3 = x_ref[...].astype(F32)
        width = x3.shape[-1]
        tpos = lax.broadcasted_iota(jnp.int32, x3.shape, 1)
        prev = jnp.where(tpos == 0, carry[...], pltpu.roll(x3, 1, 1))
        carry[...] = x3[:, lb - 1:lb, :]
        return (x3 + (prev - x3) * mu_ref[...]).reshape(LT, width)

    xr = shift_mix(r_ref, cr, mur_ref)
    xk = shift_mix(k_ref, ck, muk_ref)
    xv = shift_mix(v_ref, cv, muv_ref)
    xl = shift_mix(l_ref, cl, mul_ref)

    lane_l = lax.broadcasted_iota(jnp.int32, (LT, LORA), 1)
    act = jnp.where(lane_l < 64, jnp.tanh(xl), jnp.where(lane_l < 128, xl, _sigmoid(xl))).astype(BF16)
    z = w0_ref[...] + _dot(act, w2_ref[...])
    w_log = -(jnp.maximum(-z, 0.0) + jnp.log(1.0 + jnp.exp(-jnp.abs(z)))) - 0.5
    lw = -jnp.exp(w_log)
    a = _sigmoid(a0_ref[...] + _dot(act, a2_ref[...]))
    g = _dot(act, g2_ref[...])
    kk = xk * kk_ref[...]
    kmod = xk * (1.0 + (a - 1.0) * ka_ref[...])
    t_idx = lax.broadcasted_iota(jnp.int32, (LT, 1), 0)
    if tv < lb:
        valid = (t_idx % lb) < tv
        lw = jnp.where(valid, lw, 0.0)
        kk = jnp.where(valid, kk, 0.0)
        kmod = jnp.where(valid, kmod, 0.0)
        xv = jnp.where(valid, xv, 0.0)

    row = lax.broadcasted_iota(jnp.int32, (L, L), 0)
    col = lax.broadcasted_iota(jnp.int32, (L, L), 1)
    tril = jnp.where((col <= row) & (col // lb == row // lb), 1.0, 0.0).astype(F32)
    subs = range(nsub)
    rows = [slice(s * L, (s + 1) * L) for s in subs]
    cum_s = [_cumsum_rows(tril, lw[rows[s]]) for s in subs]

    lane = lax.broadcasted_iota(jnp.int32, (L, 128), 1)
    lo = lane < HB
    src = lane % HB
    trow = lax.broadcasted_iota(jnp.int32, (L, 128), 0)
    same = (src // lb) == (trow // lb)
    strict = same & (src < trow)
    incl = same & (src <= trow)
    r128 = lax.broadcasted_iota(jnp.int32, (128, 128), 0)
    c128 = lax.broadcasted_iota(jnp.int32, (128, 128), 1)
    blockdiag = (r128 < HB) == (c128 < HB)
    eye_pair = jnp.where(src == trow, 1.0, 0.0).astype(F32)

    sls = [slice(p * 128, (p + 1) * 128) for p in range(N_PAIR)]
    items = [(s, p) for s in subs for p in range(N_PAIR)]
    idx = range(len(items))
    groups = range(nbg)

    at_l, rt_l, bt_l, kt_l, win_l, vp_l = [], [], [], [], [], []
    for (s, p) in items:
        rs, sl = rows[s], sls[p]
        kkp = kk[rs, sl]
        kap = kkp * lax.rsqrt(jnp.maximum(_pair_sum(kkp * kkp, lo), 1e-24))
        cum_p = cum_s[s][:, sl]
        w_in = jnp.exp(cum_p)
        w_inv = jnp.exp(-cum_p)
        at_l.append(-kap * jnp.exp(cum_p - lw[rs, sl]))
        rt_l.append(xr[rs, sl] * w_in)
        bt_l.append(kap * a[rs, sl] * w_inv)
        kt_l.append(kmod[rs, sl] * w_inv)
        win_l.append(w_in)
        vp_l.append(xv[rs, sl])
    bdv_l = [_bd(vp_l[i], lo).astype(BF16) for i in idx]

    gm_l = [_dot_nt(jnp.concatenate([at_l[i], rt_l[i]], axis=0).astype(BF16),
                    jnp.concatenate([_bd(bt_l[i], lo), _bd(kt_l[i], lo)], axis=0).astype(BF16))
            for i in idx]
    n_l = [jnp.where(strict, gm_l[i][0:L, 0:128], 0.0) for i in idx]
    aak_l = [jnp.where(strict, gm_l[i][0:L, 128:256], 0.0).astype(BF16) for i in idx]
    ark_l = [jnp.concatenate([jnp.where(incl, gm_l[i][L:2 * L, 0:128], 0.0),
                              jnp.where(incl, gm_l[i][L:2 * L, 128:256], 0.0)], axis=1).astype(BF16)
             for i in idx]

    xs_l = [[_dot_nt(jnp.concatenate([at_l[i][gi * lb:(gi + 1) * lb], rt_l[i][gi * lb:(gi + 1) * lb]],
                                     axis=0).astype(BF16), sbd[items[i][0] * nbg + gi, items[i][1]].astype(BF16))
             for gi in groups] for i in idx]
    if nbg == 1:
        as_l = [xs_l[i][0][0:lb] for i in idx]
        rs_l = [xs_l[i][0][lb:2 * lb] for i in idx]
    else:
        as_l = [jnp.concatenate([xs_l[i][gi][0:lb] for gi in groups], axis=0) for i in idx]
        rs_l = [jnp.concatenate([xs_l[i][gi][lb:2 * lb] for gi in groups], axis=0) for i in idx]

    y0_l = [as_l[i] + _dot(aak_l[i], bdv_l[i]) for i in idx]

    dm_l = [eye_pair for _ in idx]
    s_blk = 1
    while 2 * s_blk <= lb:
        lvl = ((trow // (2 * s_blk)) == (src // (2 * s_blk))) & ((trow % (2 * s_blk)) >= s_blk) \
            & ((src % (2 * s_blk)) < s_blk)
        if s_blk == 1:
            dm_l = [dm_l[i] + jnp.where(lvl, n_l[i], 0.0) for i in idx]
        else:
            t1_l = [_dot(jnp.where(lvl, n_l[i], 0.0).astype(BF16), _bd(dm_l[i], lo).astype(BF16))
                    for i in idx]
            dm_l = [dm_l[i] + _dot(dm_l[i].astype(BF16), _bd(t1_l[i], lo).astype(BF16)) for i in idx]
        s_blk *= 2
    u_l = [_dot(dm_l[i].astype(BF16), _bd(y0_l[i], lo).astype(BF16)) for i in idx]

    gate_b = _sigmoid(gb_ref[...].astype(F32).reshape(LT, D))
    y_a = ya_ref[...].astype(F32).reshape(LT, D)
    bonus_l = [_pair_sum(xr[rows[s], sls[p]] * kmod[rows[s], sls[p]] * rk_ref[:, sls[p]], lo) * vp_l[i]
               for i, (s, p) in enumerate(items)]

    o_l = [rs_l[i] + _dot(ark_l[i], jnp.concatenate([_bd(u_l[i], lo).astype(BF16), bdv_l[i]], axis=0))
           for i in idx]

    w3_l = [win_l[i].reshape(nbg, lb, 128)[:, lb - 1:lb, :] for i in idx]
    rhs_l = []
    for i in idx:
        w_last = jnp.broadcast_to(w3_l[i], (nbg, lb, 128)).reshape(L, 128)
        rhs_l.append(jnp.concatenate([bt_l[i] * w_last, kt_l[i] * w_last], axis=0).astype(BF16))
    uv_l = [jnp.concatenate([u_l[i], vp_l[i]], axis=0) for i in idx]
    if nbg == 1:
        upd_l = [_dot_tn(uv_l[i].astype(BF16), rhs_l[i]) for i in idx]
        for i, (s, p) in enumerate(items):
            sbd[s, p] = sbd[s, p] * w3_l[i][0] + jnp.where(blockdiag, upd_l[i], 0.0)
    else:
        cgrp = (c128 % L) // lb
        uvt_l = [uv_l[i].T for i in idx]
        for i, (s, p) in enumerate(items):
            for gi in groups:
                upd = _dot(jnp.where(cgrp == gi, uvt_l[i], 0.0).astype(BF16), rhs_l[i])
                q_ = s * nbg + gi
                sbd[q_, p] = sbd[q_, p] * w3_l[i][gi] + jnp.where(blockdiag, upd, 0.0)

    out_l = []
    for i, (s, p) in enumerate(items):
        rs, sl = rows[s], sls[p]
        o = o_l[i]
        mu = _pair_sum(o, lo) * (1.0 / HB)
        oc = o - mu
        var = _pair_sum(oc * oc, lo) * (1.0 / HB)
        on = oc * lax.rsqrt(var + RWKV_EPS) * lw_ref[:, sl] + lb_ref[:, sl]
        yb = (on + bonus_l[i]) * g[rs, sl]
        out_l.append(y_a[rs, sl] + gate_b[rs, sl] * yb)
    u_rows = [jnp.concatenate(out_l[s * N_PAIR:(s + 1) * N_PAIR], axis=1) for s in subs]
    u_all = u_rows[0] if nsub == 1 else jnp.concatenate(u_rows, axis=0)
    u_ref[...] = u_all.reshape(nseq, lb, D).astype(u_ref.dtype)

    @pl.when(pl.program_id(1) == pl.num_programs(1) - 1)
    def _():
        for gi in range(nseq):
            for p in range(N_PAIR):
                s_ref[gi, 2 * p] = sbd[gi, p, 0:HB, 0:HB]
                s_ref[gi, 2 * p + 1] = sbd[gi, p, HB:2 * HB, HB:2 * HB]


def _rwkv(main3, cols, tail3, ya3, prev, s0, prm, nsub, nbg, lb, tv):
    b, tp, _ = main3.shape
    has_state = s0 is not None
    nq = nsub * nbg
    blk = lambda j: pl.BlockSpec((nq, lb, D), lambda i, c, j=j: (i, c, j))
    pblk = lambda j: pl.BlockSpec((nq, 1, D), lambda i, c, j=j: (i, 0, j))
    full = lambda a: pl.BlockSpec(a.shape, lambda i, c: (0,) * a.ndim)
    sblk = pl.BlockSpec((nq, H_B, HB, HB), lambda i, c: (i, 0, 0, 0))
    c_r, c_k, c_v, c_gb = cols
    in_specs = [blk(c_r), blk(c_k), blk(c_v), blk(c_gb),
                pl.BlockSpec((nq, lb, LORA), lambda i, c: (i, c, 0)),
                pl.BlockSpec((nq, lb, D), lambda i, c: (i, c, 0))]
    args = [main3, main3, main3, main3, tail3, ya3]
    if has_state:
        in_specs += [pblk(c_r), pblk(c_k), pblk(c_v), pl.BlockSpec((nq, 1, LORA), lambda i, c: (i, 0, 0)), sblk]
        args += [prev[0], prev[0], prev[0], prev[1], s0]
    in_specs += [full(a) for a in prm]
    args += list(prm)
    return pl.pallas_call(
        functools.partial(_rwkv_kernel, nsub, nbg, lb, tv, has_state),
        grid=(b // nq, tp // lb),
        in_specs=in_specs,
        out_specs=[pl.BlockSpec((nq, lb, D), lambda i, c: (i, c, 0)), sblk],
        out_shape=[jax.ShapeDtypeStruct((b, tp, D), main3.dtype),
                   jax.ShapeDtypeStruct((b, H_B, HB, HB), F32)],
        scratch_shapes=[pltpu.VMEM((nq, N_PAIR, 128, 128), F32),
                        pltpu.VMEM((nq, 1, D), F32), pltpu.VMEM((nq, 1, D), F32),
                        pltpu.VMEM((nq, 1, D), F32), pltpu.VMEM((nq, 1, LORA), F32)],
        compiler_params=_cp(("parallel", "arbitrary")),
        name="rwkv",
    )(*args)


def _rwkv_seq_kernel(nsub, r_ref, k_ref, v_ref, l_ref, gb_ref, ya_ref,
                     mur_ref, muk_ref, muv_ref, mul_ref, w0_ref, a0_ref, kk_ref, ka_ref, rk_ref,
                     lw_ref, lb_ref, w2_ref, a2_ref, g2_ref,
                     u_ref, s_ref, sbd, cr, ck, cv, cl, stg_b0, stg_f0, stg_b1, stg_f1):
    L = RW_L
    LT = nsub * L
    c = pl.program_id(1)

    @pl.when(c == 0)
    def _():
        sbd[...] = jnp.zeros_like(sbd)
        for c_ in (cr, ck, cv, cl):
            c_[...] = jnp.zeros_like(c_)
        stg_b1[...] = jnp.zeros_like(stg_b1)
        stg_f1[...] = jnp.zeros_like(stg_f1)

    row = lax.broadcasted_iota(jnp.int32, (L, L), 0)
    col = lax.broadcasted_iota(jnp.int32, (L, L), 1)
    tril = jnp.where(col <= row, 1.0, 0.0).astype(F32)
    lane = lax.broadcasted_iota(jnp.int32, (L, 128), 1)
    lo = lane < HB
    src = lane % HB
    trow = lax.broadcasted_iota(jnp.int32, (L, 128), 0)
    strict = src < trow
    incl = src <= trow
    r128 = lax.broadcasted_iota(jnp.int32, (128, 128), 0)
    c128 = lax.broadcasted_iota(jnp.int32, (128, 128), 1)
    blockdiag = (r128 < HB) == (c128 < HB)
    eye_pair = jnp.where(src == trow, 1.0, 0.0).astype(F32)
    subs = range(nsub)
    rows = [slice(s * L, (s + 1) * L) for s in subs]
    sls = [slice(p * 128, (p + 1) * 128) for p in range(N_PAIR)]
    items = [(s, p) for s in subs for p in range(N_PAIR)]
    idx = range(len(items))

    def prep_gen(wb, wf):
        def shift_mix(x_ref, carry, mu_ref):
            x3 = x_ref[...].astype(F32)
            tpos = lax.broadcasted_iota(jnp.int32, x3.shape, 1)
            prev = jnp.where(tpos == 0, carry[...], pltpu.roll(x3, 1, 1))
            carry[...] = x3[:, L - 1:L, :]
            return (x3 + (prev - x3) * mu_ref[...]).reshape(LT, x3.shape[-1])

        xl = shift_mix(l_ref, cl, mul_ref)
        lane_l = lax.broadcasted_iota(jnp.int32, (LT, LORA), 1)
        act = jnp.where(lane_l < 64, jnp.tanh(xl), jnp.where(lane_l < 128, xl, _sigmoid(xl))).astype(BF16)
        yield
        z = w0_ref[...] + _dot(act, w2_ref[...])
        w_log = -(jnp.maximum(-z, 0.0) + jnp.log(1.0 + jnp.exp(-jnp.abs(z)))) - 0.5
        lw = -jnp.exp(w_log)
        yield
        cum_s = [_cumsum_rows(tril, lw[rows[s]]) for s in subs]
        yield
        a = _sigmoid(a0_ref[...] + _dot(act, a2_ref[...]))
        yield
        g = _dot(act, g2_ref[...])
        yield
        xk = shift_mix(k_ref, ck, muk_ref)
        kk = xk * kk_ref[...]
        kmod = xk * (1.0 + (a - 1.0) * ka_ref[...])
        yield
        xr = shift_mix(r_ref, cr, mur_ref)
        yield
        xv = shift_mix(v_ref, cv, muv_ref)
        yield
        for i, (s, p) in enumerate(items):
            rs, sl = rows[s], sls[p]
            kkp = kk[rs, sl]
            kap = kkp * lax.rsqrt(jnp.maximum(_pair_sum(kkp * kkp, lo), 1e-24))
            cum_p = cum_s[s][:, sl]
            w_in = jnp.exp(cum_p)
            w_inv = jnp.exp(-cum_p)
            at = -kap * jnp.exp(cum_p - lw[rs, sl])
            rt = xr[rs, sl] * w_in
            bt = kap * a[rs, sl] * w_inv
            kt = kmod[rs, sl] * w_inv
            vp = xv[rs, sl]
            yield
            wb[i, 0:2 * L] = jnp.concatenate([at, rt], axis=0).astype(BF16)
            wb[i, 2 * L:6 * L] = jnp.concatenate([_bd(bt, lo), _bd(kt, lo)], axis=0).astype(BF16)
            wb[i, 6 * L:8 * L] = _bd(vp, lo).astype(BF16)
            wf[i, 0:L] = bt
            wf[i, L:2 * L] = kt
            wf[i, 2 * L:3 * L] = vp
            wf[i, 3 * L:4 * L] = g[rs, sl]
            wf[i, 4 * L:5 * L] = _pair_sum(xr[rs, sl] * kmod[rs, sl] * rk_ref[:, sl], lo) * vp
            wf[i, 5 * L:6 * L] = w_in
            yield

    def stage_gen(rb, rf):
        def each(fn):
            out = []
            for i in idx:
                out.append(fn(i))
                yield
            return out

        xb = [rb[i, 0:2 * L] for i in idx]
        bdv = [rb[i, 6 * L:8 * L] for i in idx]
        gm = yield from each(lambda i: _dot_nt(xb[i], rb[i, 2 * L:6 * L]))
        n_l = [jnp.where(strict, gm[i][0:L, 0:128], 0.0) for i in idx]
        aak = [jnp.where(strict, gm[i][0:L, 128:256], 0.0).astype(BF16) for i in idx]
        ark = [jnp.concatenate([jnp.where(incl, gm[i][L:2 * L, 0:128], 0.0),
                                jnp.where(incl, gm[i][L:2 * L, 128:256], 0.0)], axis=1).astype(BF16)
               for i in idx]
        xs = yield from each(lambda i: _dot_nt(xb[i], sbd[items[i][0], items[i][1]].astype(BF16)))
        y0 = yield from each(lambda i: xs[i][0:L] + _dot(aak[i], bdv[i]))
        dm = [eye_pair for _ in idx]
        s_blk = 1
        while 2 * s_blk <= L:
            lvl = ((trow // (2 * s_blk)) == (src // (2 * s_blk))) & ((trow % (2 * s_blk)) >= s_blk) \
                & ((src % (2 * s_blk)) < s_blk)
            if s_blk == 1:
                dm = [dm[i] + jnp.where(lvl, n_l[i], 0.0) for i in idx]
            else:
                t1 = yield from each(lambda i, lvl=lvl, dm=dm: _dot(
                    jnp.where(lvl, n_l[i], 0.0).astype(BF16), _bd(dm[i], lo).astype(BF16)))
                dm = yield from each(lambda i, dm=dm, t1=t1: dm[i] + _dot(
                    dm[i].astype(BF16), _bd(t1[i], lo).astype(BF16)))
            s_blk *= 2
        u = yield from each(lambda i: _dot(dm[i].astype(BF16), _bd(y0[i], lo).astype(BF16)))
        o = yield from each(lambda i: xs[i][L:2 * L] + _dot(
            ark[i], jnp.concatenate([_bd(u[i], lo).astype(BF16), bdv[i]], axis=0)))

        def state_update(i):
            w_last = rf[i, 6 * L - 1:6 * L]
            rhs = jnp.concatenate([rf[i, 0:L] * w_last, rf[i, L:2 * L] * w_last],
                                  axis=0).astype(BF16)
            uv = jnp.concatenate([u[i], rf[i, 2 * L:3 * L]], axis=0).astype(BF16)
            return _dot_tn(uv, rhs)

        upd = yield from each(state_update)
        for i, (s, p) in enumerate(items):
            sbd[s, p] = sbd[s, p] * rf[i, 6 * L - 1:6 * L] + jnp.where(blockdiag, upd[i], 0.0)
        yield
        gate_b = _sigmoid(gb_ref[...].astype(F32).reshape(LT, D))
        y_a = ya_ref[...].astype(F32).reshape(LT, D)
        out_l = []
        for i, (s, p) in enumerate(items):
            rs, sl = rows[s], sls[p]
            mu = _pair_sum(o[i], lo) * (1.0 / HB)
            oc = o[i] - mu
            var = _pair_sum(oc * oc, lo) * (1.0 / HB)
            on = oc * lax.rsqrt(var + RWKV_EPS) * lw_ref[:, sl] + lb_ref[:, sl]
            yb = (on + rf[i, 4 * L:5 * L]) * rf[i, 3 * L:4 * L]
            out_l.append(y_a[rs, sl] + gate_b[rs, sl] * yb)
            yield
        u_rows = [jnp.concatenate(out_l[s * N_PAIR:(s + 1) * N_PAIR], axis=1) for s in subs]
        u_all = u_rows[0] if nsub == 1 else jnp.concatenate(u_rows, axis=0)
        u_ref[...] = u_all.reshape(nsub, L, D).astype(u_ref.dtype)
        yield

    def run(wb, wf, rb, rf):
        prep = prep_gen(wb, wf)
        for k, _ in enumerate(stage_gen(rb, rf)):
            if k % PREP_EVERY == 0:
                next(prep, None)
        for _ in prep:
            pass

    @pl.when(c % 2 == 0)
    def _():
        run(stg_b0, stg_f0, stg_b1, stg_f1)

    @pl.when(c % 2 == 1)
    def _():
        run(stg_b1, stg_f1, stg_b0, stg_f0)

    @pl.when(c == pl.num_programs(1) - 1)
    def _():
        for gi in range(nsub):
            for p in range(N_PAIR):
                s_ref[gi, 2 * p] = sbd[gi, p, 0:HB, 0:HB]
                s_ref[gi, 2 * p + 1] = sbd[gi, p, HB:2 * HB, HB:2 * HB]


def _rwkv_seq(main3, cols, tail3, ya3, prm, nsub):
    b, tp, _ = main3.shape
    nc = tp // RW_L
    nxt = lambda c: jnp.minimum(c, nc - 1)
    cur = lambda c: jnp.maximum(c - 1, 0)
    c_r, c_k, c_v, c_gb = cols
    blk = lambda j: pl.BlockSpec((nsub, RW_L, D), lambda i, c, j=j: (i, nxt(c), j))
    full = lambda a: pl.BlockSpec(a.shape, lambda i, c: (0,) * a.ndim)
    n_items = nsub * N_PAIR
    return pl.pallas_call(
        functools.partial(_rwkv_seq_kernel, nsub),
        grid=(b // nsub, nc + 1),
        in_specs=[blk(c_r), blk(c_k), blk(c_v),
                  pl.BlockSpec((nsub, RW_L, LORA), lambda i, c: (i, nxt(c), 0)),
                  pl.BlockSpec((nsub, RW_L, D), lambda i, c: (i, cur(c), c_gb)),
                  pl.BlockSpec((nsub, RW_L, D), lambda i, c: (i, cur(c), 0))] + [full(a) for a in prm],
        out_specs=[pl.BlockSpec((nsub, RW_L, D), lambda i, c: (i, cur(c), 0)),
                   pl.BlockSpec((nsub, H_B, HB, HB), lambda i, c: (i, 0, 0, 0))],
        out_shape=[jax.ShapeDtypeStruct((b, tp, D), main3.dtype),
                   jax.ShapeDtypeStruct((b, H_B, HB, HB), F32)],
        scratch_shapes=[pltpu.VMEM((nsub, N_PAIR, 128, 128), F32),
                        pltpu.VMEM((nsub, 1, D), F32), pltpu.VMEM((nsub, 1, D), F32),
                        pltpu.VMEM((nsub, 1, D), F32), pltpu.VMEM((nsub, 1, LORA), F32),
                        pltpu.VMEM((n_items, 8 * RW_L, 128), BF16), pltpu.VMEM((n_items, 6 * RW_L, 128), F32),
                        pltpu.VMEM((n_items, 8 * RW_L, 128), BF16), pltpu.VMEM((n_items, 6 * RW_L, 128), F32)],
        compiler_params=_cp(("parallel", "arbitrary")),
        name="rwkv_seq",
    )(main3, main3, main3, tail3, main3, ya3, *prm)


def _tail_kernel(u_ref, x_ref, g1_ref, sh_ref, sc_ref, g2_ref, wo_ref, wu_ref, wd_ref,
                 l1g_ref, l1b_ref, l2g_ref, l2b_ref, o_ref, x1_scr, h_scr, acc):
    bb, tt, _ = x_ref.shape
    j = pl.program_id(2)

    @pl.when(j == 0)
    def _():
        u = u_ref[...].reshape(bb * tt, D).astype(BF16)
        y = _dot(u, wo_ref[...]).reshape(bb, tt, D)
        x1 = _layer_norm(ALPHA * x_ref[...] + g1_ref[...] * y, l1g_ref[...], l1b_ref[...])
        x1_scr[...] = x1
        h_scr[...] = (x1 * (1.0 + sc_ref[...]) + sh_ref[...]).reshape(bb * tt, D).astype(BF16)
        acc[...] = jnp.zeros_like(acc)

    up = jnp.maximum(_dot(h_scr[...], wu_ref[...]), 0.0)
    acc[...] += _dot((up * up).astype(BF16), wd_ref[...])

    @pl.when(j == pl.num_programs(2) - 1)
    def _():
        z = ALPHA * x1_scr[...] + g2_ref[...] * acc[...].reshape(bb, tt, D)
        o_ref[...] = _layer_norm(z, l2g_ref[...], l2b_ref[...])


def _tail(u3, x3, mod3, q, bb, tt):
    b, tp, _ = x3.shape
    blk = pl.BlockSpec((bb, tt, D), lambda i, t, j: (i, t, 0))
    mblk = lambda col: pl.BlockSpec((bb, 1, D), lambda i, t, j, col=col: (i, 0, col))
    full = lambda shp: pl.BlockSpec(shp, lambda i, t, j: (0,) * len(shp))
    return pl.pallas_call(
        _tail_kernel,
        grid=(b // bb, tp // tt, D_FF // FF_CHUNK),
        in_specs=[blk, blk, mblk(2), mblk(3), mblk(4), mblk(5),
                  full((D, D)),
                  pl.BlockSpec((D, FF_CHUNK), lambda i, t, j: (0, j)),
                  pl.BlockSpec((FF_CHUNK, D), lambda i, t, j: (j, 0)),
                  full((1, D)), full((1, D)), full((1, D)), full((1, D))],
        out_specs=blk,
        out_shape=jax.ShapeDtypeStruct((b, tp, D), F32),
        scratch_shapes=[pltpu.VMEM((bb, tt, D), F32), pltpu.VMEM((bb * tt, D), BF16),
                        pltpu.VMEM((bb * tt, D), F32)],
        compiler_params=_cp(("parallel", "parallel", "arbitrary")),
        name="outproj_ffn",
    )(u3, x3, mod3, mod3, mod3, mod3, q['w_out'], q['w_up'], q['w_down'],
      q['ln1_g'], q['ln1_b'], q['ln2_g'], q['ln2_b'])


def _relayout_params(p):
    w = p['w_in']
    w_main = jnp.concatenate(
        [w[:, :3 * D], w[:, 3 * D + 8:6 * D + 8], w[:, 6 * D + 8 + LORA:8 * D + 8 + LORA]], axis=1).astype(BF16)
    w_tail = jnp.concatenate(
        [w[:, 6 * D + 8:6 * D + 8 + LORA], w[:, 3 * D:3 * D + 8],
         jnp.zeros((D, N_TAIL - LORA - 8), F32)], axis=1).astype(BF16)
    mu = p['rwkv_mu']
    z64 = jnp.zeros((64, D), F32)
    z128 = jnp.zeros((128, D), F32)
    row = lambda a: a.reshape(1, -1)
    rw = (row(mu[0:D]), row(mu[D:2 * D]), row(mu[2 * D:3 * D]), row(mu[3 * D:3 * D + LORA]),
          row(p['rwkv_w0']), row(p['rwkv_a0']), row(p['rwkv_k_k']), row(p['rwkv_k_a']),
          row(p['rwkv_r_k']), row(p['rwkv_lnx_w']), row(p['rwkv_lnx_b']),
          jnp.concatenate([p['rwkv_w2'], z64, z128], axis=0).astype(BF16),
          jnp.concatenate([z64, p['rwkv_a2'], z128], axis=0).astype(BF16),
          jnp.concatenate([z128, p['rwkv_g2']], axis=0).astype(BF16))
    gbias = jnp.concatenate([p['mlstm_i_bias'], p['mlstm_f_bias'], jnp.zeros((120,), F32)]).reshape(1, 128)
    return dict(w_main=w_main, w_tail=w_tail, rw=rw, gbias=gbias,
                conv_w=p['conv_w'], conv_b=row(p['conv_b']), norm_w=row(p['mlstm_norm_w']),
                w_out=p['w_out'].astype(BF16), w_up=p['w_up'].astype(BF16), w_down=p['w_down'].astype(BF16),
                ln1_g=row(p['ln1_g']), ln1_b=row(p['ln1_b']), ln2_g=row(p['ln2_g']), ln2_b=row(p['ln2_b']))


def _prompt_layer(x, mod, q, seq_tile, mlstm_chunk):
    b, t, _ = x.shape
    mod3 = mod.reshape(b, 1, N_COND)
    main3, tail3 = _inproj(x, mod3, q['w_main'], q['w_tail'], 1, seq_tile, BF16)
    ya3, c1, n1, m1 = _mlstm_seq(main3, tail3, q['conv_w'], q['conv_b'], q['gbias'], q['norm_w'],
                                 mlstm_chunk, min(2, b))
    u3, s1 = _rwkv_seq(main3, RWKV_SECTIONS, tail3, ya3, q['rw'], min(2, b))
    y = _tail(u3, x, mod3, q, 1, seq_tile)
    shift = _modulate_rows(x[:, t - 1, :], mod)
    conv = main3[:, t - (CONV_W - 1):, :2 * D].astype(F32)
    return y, (c1, n1[:, :H_A, :], m1[:, 0, :H_A], conv, s1, shift)


def _sample_layer(x, mod, st, q, bb):
    c0, n0, m0, conv0, s0, shift0 = st
    b, t, _ = x.shape
    mod3 = mod.reshape(b, 1, N_COND)
    xp = jnp.pad(x, ((0, 0), (0, 8 - t), (0, 0)))
    main3, tail3 = _inproj(xp, mod3, q['w_main'], q['w_tail'], bb, 8, F32)
    pm, pt = _inproj(shift0.reshape(1, b, D), jnp.zeros((1, 1, N_COND), F32), q['w_main'], q['w_tail'], 1, b, F32)
    prev = (pm.reshape(b, 1, N_MAIN), pt.reshape(b, 1, N_TAIL))
    conv0p = jnp.pad(conv0, ((0, 0), (8 - (CONV_W - 1), 0), (0, 0)))
    n0p = jnp.pad(n0, ((0, 0), (0, 8 - H_A), (0, 0)))
    m0p = jnp.pad(m0, ((0, 0), (0, 128 - H_A))).reshape(b, 1, 128)
    ya3, c1, n1, m1 = _mlstm_step(main3, tail3, conv0p, c0, n0p, m0p, q['conv_w'], q['conv_b'], q['gbias'],
                                  q['norm_w'], t, min(4, b))
    u3, s1 = _rwkv(main3, RWKV_SECTIONS, tail3, ya3, prev, s0, q['rw'], 1, RW_L // 8, 8, t)
    y = _tail(u3, xp, mod3, q, bb, 8)
    shift = _modulate_rows(x[:, t - 1, :], mod)
    conv = jnp.concatenate([conv0, main3[:, :t, :2 * D]], axis=1)[:, t:, :]
    return y[:, :t, :], (c1, n1[:, :H_A, :], m1[:, 0, :H_A], conv, s1, shift)


def kernel(x_prompt, x_sample, c_prompt, c_sample, state_mlstm_C, state_mlstm_n, state_mlstm_m, state_mlstm_conv, state_rwkv_S, state_rwkv_shift, w_cond, b_cond, w_in, mlstm_i_bias, mlstm_f_bias, conv_w, conv_b, mlstm_norm_w, rwkv_mu, rwkv_w0, rwkv_w2, rwkv_a0, rwkv_a2, rwkv_g2, rwkv_k_k, rwkv_k_a, rwkv_r_k, rwkv_lnx_w, rwkv_lnx_b, w_out, ln1_g, ln1_b, w_up, w_down, ln2_g, ln2_b):
    depth = w_in.shape[0]
    bp = x_prompt.shape[0]
    yp, ys = x_prompt, x_sample
    new_p = [[] for _ in range(6)]
    new_s = [[] for _ in range(6)]
    for l in range(depth):
        p = {'w_in': w_in[l], 'mlstm_i_bias': mlstm_i_bias[l], 'mlstm_f_bias': mlstm_f_bias[l],
             'conv_w': conv_w[l], 'conv_b': conv_b[l], 'mlstm_norm_w': mlstm_norm_w[l],
             'rwkv_mu': rwkv_mu[l], 'rwkv_w0': rwkv_w0[l], 'rwkv_w2': rwkv_w2[l], 'rwkv_a0': rwkv_a0[l],
             'rwkv_a2': rwkv_a2[l], 'rwkv_g2': rwkv_g2[l], 'rwkv_k_k': rwkv_k_k[l], 'rwkv_k_a': rwkv_k_a[l],
             'rwkv_r_k': rwkv_r_k[l].reshape(-1), 'rwkv_lnx_w': rwkv_lnx_w[l], 'rwkv_lnx_b': rwkv_lnx_b[l],
             'w_out': w_out[l], 'ln1_g': ln1_g[l], 'ln1_b': ln1_b[l], 'w_up': w_up[l], 'w_down': w_down[l],
             'ln2_g': ln2_g[l], 'ln2_b': ln2_b[l]}
        q = _relayout_params(p)
        mod = _cond(jnp.concatenate([c_prompt, c_sample], axis=0), w_cond[l], b_cond[l])
        st_in = (state_mlstm_C[l], state_mlstm_n[l], state_mlstm_m[l], state_mlstm_conv[l],
                 state_rwkv_S[l], state_rwkv_shift[l])
        ys, st_s = _sample_layer(ys, mod[bp:], st_in, q, min(128, ys.shape[0]))
        yp, st_p = _prompt_layer(yp, mod[:bp], q, min(1024, yp.shape[1]), min(256, yp.shape[1]))
        for lst, t in zip(new_p, st_p):
            lst.append(t)
        for lst, t in zip(new_s, st_s):
            lst.append(t)
    outs_p = [jnp.stack(t) for t in new_p]
    outs_s = [jnp.stack(t) for t in new_s]
    return (yp, ys, *outs_p, *outs_s)
```

```python
import functools

import jax
import jax.numpy as jnp
from jax import lax
from jax.experimental import pallas as pl
from jax.experimental.pallas import tpu as pltpu

F32 = jnp.float32
BF16 = jnp.bfloat16
HIGHEST = lax.Precision.HIGHEST

D = 1024
H_A = 4
DK = 256
CONV_W = 4
H_B = 16
HB = 64
N_PAIR = H_B // 2
D_FF = 4096
N_COND = 6 * D
ALPHA = 2.0 ** 0.25
LN_EPS = 1e-5
MLSTM_EPS = 1e-6
RWKV_EPS = 64e-5

N_MAIN = 8 * D
RWKV_SECTIONS = (3, 4, 5, 7)
LORA = 256
TAIL_IF = LORA
N_TAIL = 512
TN_MAIN = 2048
N_MAIN_TILES = N_MAIN // TN_MAIN

RW_L = 64
FF_CHUNK = 1024
NEG = -1e30
VMEM_LIMIT = 56 * 1024 * 1024


def _cp(sem):
    return pltpu.CompilerParams(dimension_semantics=sem, vmem_limit_bytes=VMEM_LIMIT)


def _dot(a, b, prec=None):
    return jnp.dot(a, b, preferred_element_type=F32, precision=prec)


def _dot_nt(a, b, prec=None):
    return lax.dot_general(a, b, (((1,), (1,)), ((), ())), preferred_element_type=F32, precision=prec)


def _dot_tn(a, b, prec=None):
    return lax.dot_general(a, b, (((0,), (0,)), ((), ())), preferred_element_type=F32, precision=prec)


def _cumsum_rows(tril01, x):
    hi = x.astype(BF16)
    r1 = x - hi.astype(F32)
    mid = r1.astype(BF16)
    lo = (r1 - mid.astype(F32)).astype(BF16)
    t = tril01.astype(BF16)
    return _dot(t, hi) + _dot(t, mid) + _dot(t, lo)


def _log_sigmoid(x):
    return jnp.minimum(x, 0.0) - jnp.log1p(jnp.exp(-jnp.abs(x)))


def _sigmoid(x):
    return 0.5 * jnp.tanh(0.5 * x) + 0.5


def _silu(x):
    h = 0.5 * x
    return h + h * jnp.tanh(h)


def _layer_norm(z, g, b):
    mu = jnp.mean(z, axis=-1, keepdims=True)
    zc = z - mu
    var = jnp.mean(zc * zc, axis=-1, keepdims=True)
    return zc * lax.rsqrt(var + LN_EPS) * g + b


def _cond_kernel(c_ref, w_ref, b_ref, o_ref):
    s = _silu(c_ref[...]).astype(BF16)
    o_ref[...] = _dot(s, w_ref[...].astype(BF16)) + b_ref[...]


def _cond(c, w_cond, b_cond):
    n = c.shape[0]
    tn = 1536
    return pl.pallas_call(
        _cond_kernel,
        grid=(N_COND // tn,),
        in_specs=[pl.BlockSpec((n, D), lambda j: (0, 0)),
                  pl.BlockSpec((D, tn), lambda j: (0, j)),
                  pl.BlockSpec((1, tn), lambda j: (0, j))],
        out_specs=pl.BlockSpec((n, tn), lambda j: (0, j)),
        out_shape=jax.ShapeDtypeStruct((n, N_COND), F32),
        compiler_params=_cp(("arbitrary",)),
        name="cond",
    )(c, w_cond, b_cond.reshape(1, N_COND))


def _inproj_kernel(x_ref, sh_ref, sc_ref, wm_ref, wt_ref, main_ref, tail_ref, h_scr):
    bb, tt, _ = x_ref.shape
    j = pl.program_id(2)

    @pl.when(j == 0)
    def _():
        h = x_ref[...] * (1.0 + sc_ref[...]) + sh_ref[...]
        h_scr[...] = h.reshape(bb * tt, D).astype(BF16)

    main_ref[...] = _dot(h_scr[...], wm_ref[...]).reshape(bb, tt, TN_MAIN).astype(main_ref.dtype)

    @pl.when(j == N_MAIN_TILES - 1)
    def _():
        tail_ref[...] = _dot(h_scr[...], wt_ref[...]).reshape(bb, tt, N_TAIL)


def _inproj(x3, mod3, w_main, w_tail, bb, tt, main_dtype):
    b, tp, _ = x3.shape
    return pl.pallas_call(
        _inproj_kernel,
        grid=(b // bb, tp // tt, N_MAIN_TILES),
        in_specs=[pl.BlockSpec((bb, tt, D), lambda i, t, j: (i, t, 0)),
                  pl.BlockSpec((bb, 1, D), lambda i, t, j: (i, 0, 0)),
                  pl.BlockSpec((bb, 1, D), lambda i, t, j: (i, 0, 1)),
                  pl.BlockSpec((D, TN_MAIN), lambda i, t, j: (0, j)),
                  pl.BlockSpec((D, N_TAIL), lambda i, t, j: (0, 0))],
        out_specs=[pl.BlockSpec((bb, tt, TN_MAIN), lambda i, t, j: (i, t, j)),
                   pl.BlockSpec((bb, tt, N_TAIL), lambda i, t, j: (i, t, 0))],
        out_shape=[jax.ShapeDtypeStruct((b, tp, N_MAIN), main_dtype),
                   jax.ShapeDtypeStruct((b, tp, N_TAIL), F32)],
        scratch_shapes=[pltpu.VMEM((bb * tt, D), BF16)],
        compiler_params=_cp(("parallel", "parallel", "arbitrary")),
        name="inproj",
    )(x3, mod3, mod3, w_main, w_tail)


def _modulate_kernel(x_ref, sh_ref, sc_ref, o_ref):
    o_ref[...] = x_ref[...] * (1.0 + sc_ref[...]) + sh_ref[...]


def _modulate_rows(x2, mod2):
    n = x2.shape[0]
    return pl.pallas_call(
        _modulate_kernel,
        grid=(1,),
        in_specs=[pl.BlockSpec((n, D), lambda i: (0, 0)),
                  pl.BlockSpec((n, D), lambda i: (0, 0)),
                  pl.BlockSpec((n, D), lambda i: (0, 1))],
        out_specs=pl.BlockSpec((n, D), lambda i: (0, 0)),
        out_shape=jax.ShapeDtypeStruct((n, D), F32),
        name="modulate_last",
    )(x2, mod2, mod2)


def _conv4(pad_ref, n_rows, cw, cb):
    acc = cb + pad_ref[8:8 + n_rows, :] * cw[3:4, :]
    acc = acc + pad_ref[7:7 + n_rows, :] * cw[2:3, :]
    acc = acc + pad_ref[6:6 + n_rows, :] * cw[1:2, :]
    acc = acc + pad_ref[5:5 + n_rows, :] * cw[0:1, :]
    return acc


def _head_norm_rows(h, eps):
    mu = jnp.mean(h, axis=-1, keepdims=True)
    hc = h - mu
    var = jnp.mean(hc * hc, axis=-1, keepdims=True)
    return hc * lax.rsqrt(var + eps)


def _mlstm_seq_kernel(nb, qp_ref, kp_ref, v_ref, ga_ref, if_ref, cw_ref, cb_ref, gb_ref, nw_ref,
                      ya_ref, c_ref, n_ref, m_ref, haloq, halok):
    L = qp_ref.shape[1]
    assert qp_ref.dtype == BF16 and kp_ref.dtype == BF16

    @pl.when(pl.program_id(1) == 0)
    def _():
        c_ref[...] = jnp.zeros_like(c_ref)
        n_ref[...] = jnp.zeros_like(n_ref)
        m_ref[...] = jnp.zeros_like(m_ref)
        haloq[...] = jnp.zeros_like(haloq)
        halok[...] = jnp.zeros_like(halok)

    cw = cw_ref[...]
    cb = cb_ref[...]
    nw = nw_ref[...]
    row = lax.broadcasted_iota(jnp.int32, (L, L), 0)
    col = lax.broadcasted_iota(jnp.int32, (L, L), 1)
    causal = col <= row
    tril = jnp.where(causal, 1.0, 0.0).astype(F32)
    srow = lax.broadcasted_iota(jnp.int32, (3 * L, L), 0)
    scol = lax.broadcasted_iota(jnp.int32, (3 * L, L), 1)
    shift_mat = jnp.where(scol + srow // L + 1 == srow % L, 1.0, 0.0).astype(BF16)
    r8 = lax.broadcasted_iota(jnp.int32, (8, 1), 0)

    def conv(x_bf, halo_ref, cw_, cb_):
        sh = _dot(shift_mat, x_bf)
        x = x_bf.astype(F32)
        acc = cb_ + x * cw_[3:4, :] + sh[0:L] * cw_[2:3, :] + sh[L:2 * L] * cw_[1:2, :] \
            + sh[2 * L:3 * L] * cw_[0:1, :]
        halo = halo_ref[...]
        fix = jnp.zeros((8, DK), F32)
        for j in range(1, CONV_W):
            fix = fix + jnp.where(r8 < j, pltpu.roll(halo, j, 0), 0.0) * cw_[CONV_W - 1 - j:CONV_W - j, :]
        halo_ref[...] = x[L - 8:L, :]
        return jnp.concatenate([acc[0:8] + fix, acc[8:L]], axis=0)

    items = [(bi, h) for bi in range(nb) for h in range(H_A)]
    sl_of = lambda h: slice(h * DK, (h + 1) * DK)
    q_it, k_it = {}, {}
    for (bi, h) in items:
        sl, ksl = sl_of(h), slice(D + h * DK, D + (h + 1) * DK)
        q_it[bi, h] = _silu(conv(qp_ref[bi, :, sl], haloq.at[bi, :, sl], cw[:, sl], cb[:, sl]))
        k_it[bi, h] = _silu(conv(kp_ref[bi, :, sl], halok.at[bi, :, sl], cw[:, ksl], cb[:, ksl])) * (DK ** -0.5)

    gpre_l, bcum_l, gpre_t_l, bcum_t_l = [], [], [], []
    for bi in range(nb):
        gpre = if_ref[bi] + gb_ref[...]
        bcum = _dot(tril, _log_sigmoid(gpre), HIGHEST)
        gpre_l.append(gpre)
        bcum_l.append(bcum)
        gpre_t_l.append(gpre.T)
        bcum_t_l.append(bcum.T)

    st = {}
    for (bi, h) in items:
        ig_col = gpre_l[bi][:, h:h + 1]
        b_col = bcum_l[bi][:, H_A + h:H_A + h + 1]
        ig_row = gpre_t_l[bi][h:h + 1, :]
        b_row = bcum_t_l[bi][H_A + h:H_A + h + 1, :]
        m_prev = m_ref[bi][:, h:h + 1]
        g_col = b_col + m_prev
        dlog = jnp.where(causal, b_col - b_row + ig_row, NEG)
        m_t = jnp.maximum(g_col, jnp.max(dlog, axis=1, keepdims=True))
        b_last = b_col[L - 1:L, :]
        wlog = b_last - b_col + ig_col
        m_new = jnp.maximum(b_last + m_prev, jnp.max(wlog, axis=0, keepdims=True))
        st[bi, h] = dict(m_t=m_t, w_inter=jnp.exp(g_col - m_t), p=jnp.exp(dlog - m_t), m_new=m_new,
                         decay=jnp.exp(b_last + m_prev - m_new), wts=jnp.exp(wlog - m_new))
    qb = {it: q_it[it].astype(BF16) for it in items}
    kb = {it: k_it[it].astype(BF16) for it in items}
    s_l = {it: _dot_nt(qb[it], kb[it]) * st[it]['p'] for it in items}
    qc_l = {(bi, h): _dot_nt(qb[bi, h], c_ref[bi, h].astype(BF16)) for (bi, h) in items}
    sv_l = {(bi, h): _dot(s_l[bi, h].astype(BF16), v_ref[bi, :, sl_of(h)].astype(BF16)) for (bi, h) in items}
    upd_l = {(bi, h): _dot_tn((st[bi, h]['wts'] * v_ref[bi, :, sl_of(h)].astype(F32)).astype(BF16), kb[bi, h])
             for (bi, h) in items}
    for (bi, h) in items:
        sl = sl_of(h)
        d = st[bi, h]
        qh = q_it[bi, h]
        kh = k_it[bi, h]
        nh = n_ref[bi, h:h + 1, :]
        num = d['w_inter'] * qc_l[bi, h] + sv_l[bi, h]
        den = d['w_inter'] * jnp.sum(qh * nh, axis=1, keepdims=True) + jnp.sum(s_l[bi, h], axis=1, keepdims=True)
        hh = num / jnp.maximum(jnp.abs(den), jnp.exp(-d['m_t']))
        ga = _sigmoid(ga_ref[bi, :, sl].astype(F32))
        ya_ref[bi, :, sl] = (ga * _head_norm_rows(hh, MLSTM_EPS) * nw[:, sl]).astype(ya_ref.dtype)
        c_ref[bi, h] = d['decay'] * c_ref[bi, h] + upd_l[bi, h]
        n_ref[bi, h:h + 1, :] = d['decay'] * nh + jnp.sum(d['wts'] * kh, axis=0, keepdims=True)
        m_ref[bi, :, h:h + 1] = d['m_new']


def _mlstm_seq(main3, tail3, conv_w, conv_b, gbias, norm_w, L, nb):
    b, tp, _ = main3.shape
    blk = lambda j: pl.BlockSpec((nb, L, D), lambda i, c, j=j: (i, c, j))
    full = lambda shp: pl.BlockSpec(shp, lambda i, c: (0,) * len(shp))
    return pl.pallas_call(
        functools.partial(_mlstm_seq_kernel, nb),
        grid=(b // nb, tp // L),
        in_specs=[blk(0), blk(1), blk(2), blk(6),
                  pl.BlockSpec((nb, L, 128), lambda i, c: (i, c, TAIL_IF // 128)),
                  full((CONV_W, 2 * D)), full((1, 2 * D)), full((1, 128)), full((1, D))],
        out_specs=[pl.BlockSpec((nb, L, D), lambda i, c: (i, c, 0)),
                   pl.BlockSpec((nb, H_A, DK, DK), lambda i, c: (i, 0, 0, 0)),
                   pl.BlockSpec((nb, 8, DK), lambda i, c: (i, 0, 0)),
                   pl.BlockSpec((nb, 1, 128), lambda i, c: (i, 0, 0))],
        out_shape=[jax.ShapeDtypeStruct((b, tp, D), main3.dtype),
                   jax.ShapeDtypeStruct((b, H_A, DK, DK), F32),
                   jax.ShapeDtypeStruct((b, 8, DK), F32),
                   jax.ShapeDtypeStruct((b, 1, 128), F32)],
        scratch_shapes=[pltpu.VMEM((nb, 8, D), F32), pltpu.VMEM((nb, 8, D), F32)],
        compiler_params=_cp(("parallel", "arbitrary")),
        name="mlstm_seq",
    )(main3, main3, main3, main3, tail3, conv_w, conv_b, gbias, norm_w)


def _mlstm_step_kernel(tv, nb, qp_ref, kp_ref, v_ref, ga_ref, if_ref, conv0_ref, c0_ref, n0_ref, m0_ref,
                       cw_ref, cb_ref, gb_ref, nw_ref,
                       ya_ref, c_ref, n_ref, m_ref, padq, padk, gpad, lpad, kpad, vpad, wvpad):
    @pl.when(pl.program_id(0) == 0)
    def _():
        for r in (gpad, lpad, kpad, vpad, wvpad):
            r[...] = jnp.zeros_like(r)

    cw = cw_ref[...]
    cb = cb_ref[...]
    nw = nw_ref[...]
    gb = gb_ref[...]
    trow = lax.broadcasted_iota(jnp.int32, (8, 128), 0)
    scol = lax.broadcasted_iota(jnp.int32, (8, 128), 1)
    mask = (scol <= trow) & (scol < tv)
    rvalid = lax.broadcasted_iota(jnp.int32, (8, 1), 0) < tv
    r128 = lax.broadcasted_iota(jnp.int32, (128, 128), 0)
    c128 = lax.broadcasted_iota(jnp.int32, (128, 128), 1)
    tril = jnp.where(c128 <= r128, 1.0, 0.0).astype(F32)
    n_ref[...] = jnp.zeros_like(n_ref)
    m_ref[...] = jnp.zeros_like(m_ref)

    batches = range(nb)
    q_l, gpre_l, bcol_l, gt_l, bt_l, ga_l = [], [], [], [], [], []
    for bi in batches:
        padq[bi, 0:8, :] = conv0_ref[bi, :, 0:D]
        padk[bi, 0:8, :] = conv0_ref[bi, :, D:2 * D]
        padq[bi, 8:16, :] = qp_ref[bi]
        padk[bi, 8:16, :] = kp_ref[bi]
        q_l.append(_silu(_conv4(padq.at[bi], 8, cw[:, 0:D], cb[:, 0:D])))
        kpad[bi, 0:8, :] = _silu(_conv4(padk.at[bi], 8, cw[:, D:2 * D], cb[:, D:2 * D])) * (DK ** -0.5)
        vpad[bi, 0:8, :] = v_ref[bi]
        gpre = if_ref[bi] + gb
        gpad[bi, 0:8, :] = gpre
        lpad[bi, 0:8, :] = _log_sigmoid(gpre)
        gpre_l.append(gpre)
        ga_l.append(_sigmoid(ga_ref[bi]))
    for bi in batches:
        bpad = _dot(tril, lpad[bi], HIGHEST)
        bcol_l.append(bpad[0:8, :])
        bt_l.append(bpad.T)
        gt_l.append(gpad[bi].T)

    probs = [(bi, h) for bi in batches for h in range(H_A)]
    sl_of = lambda h: slice(h * DK, (h + 1) * DK)
    st = {}
    for (bi, h) in probs:
        ig_col = gpre_l[bi][:, h:h + 1]
        b_col = bcol_l[bi][:, H_A + h:H_A + h + 1]
        ig_row = gt_l[bi][h:h + 1, :]
        b_row = bt_l[bi][H_A + h:H_A + h + 1, :]
        m_prev = m0_ref[bi][:, h:h + 1]
        g_col = b_col + m_prev
        dlog = jnp.where(mask, b_col - b_row + ig_row, NEG)
        m_t = jnp.maximum(g_col, jnp.max(dlog, axis=1, keepdims=True))
        b_last = b_col[tv - 1:tv, :]
        wlog = jnp.where(rvalid, b_last - b_col + ig_col, NEG)
        m_new = jnp.maximum(b_last + m_prev, jnp.max(wlog, axis=0, keepdims=True))
        wts = jnp.exp(wlog - m_new)
        wvpad[bi, 0:8, sl_of(h)] = wts * vpad[bi, 0:8, sl_of(h)]
        st[bi, h] = dict(m_t=m_t, w_inter=jnp.exp(g_col - m_t), pm=jnp.exp(dlog - m_t), m_new=m_new,
                         decay=jnp.exp(b_last + m_prev - m_new), wts=wts)
    kb = {(bi, h): kpad[bi, :, sl_of(h)].astype(BF16) for (bi, h) in probs}
    qb = {(bi, h): q_l[bi][:, sl_of(h)].astype(BF16) for (bi, h) in probs}
    s_l = {k_: _dot_nt(qb[k_], kb[k_]) * st[k_]['pm'] for k_ in probs}
    qc_l = {(bi, h): _dot_nt(qb[bi, h], c0_ref[bi, h].astype(BF16)) for (bi, h) in probs}
    sv_l = {(bi, h): _dot(s_l[bi, h].astype(BF16), vpad[bi, :, sl_of(h)].astype(BF16)) for (bi, h) in probs}
    upd_l = {(bi, h): _dot(wvpad[bi, :, sl_of(h)].T.astype(BF16), kb[bi, h]) for (bi, h) in probs}
    for (bi, h) in probs:
        sl = sl_of(h)
        d = st[bi, h]
        nh = n0_ref[bi, h:h + 1, :]
        qh = q_l[bi][:, sl]
        num = d['w_inter'] * qc_l[bi, h] + sv_l[bi, h]
        den = d['w_inter'] * jnp.sum(qh * nh, axis=1, keepdims=True) + jnp.sum(s_l[bi, h], axis=1, keepdims=True)
        hh = num / jnp.maximum(jnp.abs(den), jnp.exp(-d['m_t']))
        ya_ref[bi, :, sl] = ga_l[bi][:, sl] * _head_norm_rows(hh, MLSTM_EPS) * nw[:, sl]
        c_ref[bi, h] = d['decay'] * c0_ref[bi, h] + upd_l[bi, h]
        n_ref[bi, h:h + 1, :] = d['decay'] * nh + jnp.sum(d['wts'] * kpad[bi, 0:8, sl], axis=0, keepdims=True)
        m_ref[bi, :, h:h + 1] = d['m_new']


def _mlstm_step(main3, tail3, conv0p, c0, n0p, m0p, conv_w, conv_b, gbias, norm_w, tv, nb):
    b = main3.shape[0]
    blk = lambda j: pl.BlockSpec((nb, 8, D), lambda i, j=j: (i, 0, j))
    full = lambda shp: pl.BlockSpec(shp, lambda i: (0,) * len(shp))
    state_specs = [pl.BlockSpec((nb, H_A, DK, DK), lambda i: (i, 0, 0, 0)),
                   pl.BlockSpec((nb, 8, DK), lambda i: (i, 0, 0)),
                   pl.BlockSpec((nb, 1, 128), lambda i: (i, 0, 0))]
    return pl.pallas_call(
        functools.partial(_mlstm_step_kernel, tv, nb),
        grid=(b // nb,),
        in_specs=[blk(0), blk(1), blk(2), blk(6),
                  pl.BlockSpec((nb, 8, 128), lambda i: (i, 0, TAIL_IF // 128)),
                  pl.BlockSpec((nb, 8, 2 * D), lambda i: (i, 0, 0))] + state_specs +
                 [full((CONV_W, 2 * D)), full((1, 2 * D)), full((1, 128)), full((1, D))],
        out_specs=[pl.BlockSpec((nb, 8, D), lambda i: (i, 0, 0))] + state_specs,
        out_shape=[jax.ShapeDtypeStruct((b, 8, D), F32),
                   jax.ShapeDtypeStruct((b, H_A, DK, DK), F32),
                   jax.ShapeDtypeStruct((b, 8, DK), F32),
                   jax.ShapeDtypeStruct((b, 1, 128), F32)],
        scratch_shapes=[pltpu.VMEM((nb, 16, D), F32), pltpu.VMEM((nb, 16, D), F32),
                        pltpu.VMEM((nb, 128, 128), F32), pltpu.VMEM((nb, 128, 128), F32),
                        pltpu.VMEM((nb, 128, D), F32), pltpu.VMEM((nb, 128, D), F32),
                        pltpu.VMEM((nb, 128, D), F32)],
        compiler_params=_cp(("arbitrary",)),
        name="mlstm_step",
    )(main3, main3, main3, main3, tail3, conv0p, c0, n0p, m0p, conv_w, conv_b, gbias, norm_w)


def _bd(x, lo):
    return jnp.concatenate([jnp.where(lo, x, 0.0), jnp.where(lo, 0.0, x)], axis=0)


def _pair_sum(x, lo):
    s_lo = jnp.sum(jnp.where(lo, x, 0.0), axis=1, keepdims=True)
    s_hi = jnp.sum(jnp.where(lo, 0.0, x), axis=1, keepdims=True)
    return jnp.where(lo, s_lo, s_hi)


PAIR_ROWS = 136


def _state_pairs_kernel(s_ref, o_ref):
    nb = s_ref.shape[-1]
    lane = lax.broadcasted_iota(jnp.int32, (nb, 128), 1)
    lo = lane < HB
    for b in range(nb):
        o_ref[0, b * PAIR_ROWS + 128:(b + 1) * PAIR_ROWS, :] = jnp.zeros((PAIR_ROWS - 128, 128), F32)
    for v in range(HB):
        t = jnp.concatenate([s_ref[0, v], s_ref[1, v]], axis=0).T
        o_ref[0, pl.ds(v, nb, stride=PAIR_ROWS), :] = jnp.where(lo, t, 0.0)
        o_ref[0, pl.ds(HB + v, nb, stride=PAIR_ROWS), :] = jnp.where(lo, 0.0, t)


def _state_pairs(s_hvkb):
    nb = s_hvkb.shape[-1]
    return pl.pallas_call(
        _state_pairs_kernel,
        grid=(N_PAIR,),
        in_specs=[pl.BlockSpec((2, HB, HB, nb), lambda p: (p, 0, 0, 0))],
        out_specs=pl.BlockSpec((1, nb * PAIR_ROWS, 128), lambda p: (p, 0, 0)),
        out_shape=jax.ShapeDtypeStruct((N_PAIR, nb * PAIR_ROWS, 128), F32),
        compiler_params=_cp(("arbitrary",)),
        name="state_pairs",
    )(s_hvkb)


def _rwkv_kernel(nsub, nbg, lb, tv, has_state, *refs):
    (r_ref, k_ref, v_ref, gb_ref, l_ref, ya_ref), refs = refs[:6], refs[6:]
    if has_state:
        (pr_ref, pk_ref, pv_ref, pl_ref, s0_ref), refs = refs[:5], refs[5:]
    (mur_ref, muk_ref, muv_ref, mul_ref, w0_ref, a0_ref, kk_ref, ka_ref, rk_ref,
     lw_ref, lb_ref, w2_ref, a2_ref, g2_ref,
     u_ref, s_ref, sbd, cr, ck, cv, cl) = refs
    L = nbg * lb
    nseq = nsub * nbg
    LT = nsub * L

    @pl.when(pl.program_id(1) == 0)
    def _():
        if has_state:
            for gi in range(nseq):
                for p in range(N_PAIR):
                    sbd[gi, p] = s0_ref[p, gi * PAIR_ROWS:gi * PAIR_ROWS + 128, :]
            cr[...] = pr_ref[...].astype(F32)
            ck[...] = pk_ref[...].astype(F32)
            cv[...] = pv_ref[...].astype(F32)
            cl[...] = pl_ref[...]
        else:
            sbd[...] = jnp.zeros_like(sbd)
            for c_ in (cr, ck, cv, cl):
                c_[...] = jnp.zeros_like(c_)

    def shift_mix(x_ref, carry, mu_ref):
        x3 = x_ref[...].astype(F32)
        width = x3.shape[-1]
        tpos = lax.broadcasted_iota(jnp.int32, x3.shape, 1)
        prev = jnp.where(tpos == 0, carry[...], pltpu.roll(x3, 1, 1))
        carry[...] = x3[:, lb - 1:lb, :]
        return (x3 + (prev - x3) * mu_ref[...]).reshape(LT, width)

    xr = shift_mix(r_ref, cr, mur_ref)
    xk = shift_mix(k_ref, ck, muk_ref)
    xv = shift_mix(v_ref, cv, muv_ref)
    xl = shift_mix(l_ref, cl, mul_ref)

    lane_l = lax.broadcasted_iota(jnp.int32, (LT, LORA), 1)
    act = jnp.where(lane_l < 64, jnp.tanh(xl), jnp.where(lane_l < 128, xl, _sigmoid(xl))).astype(BF16)
    z = w0_ref[...] + _dot(act, w2_ref[...])
    w_log = -(jnp.maximum(-z, 0.0) + jnp.log(1.0 + jnp.exp(-jnp.abs(z)))) - 0.5
    lw = -jnp.exp(w_log)
    a = _sigmoid(a0_ref[...] + _dot(act, a2_ref[...]))
    g = _dot(act, g2_ref[...])
    kk = xk * kk_ref[...]
    kmod = xk * (1.0 + (a - 1.0) * ka_ref[...])
    t_idx = lax.broadcasted_iota(jnp.int32, (LT, 1), 0)
    if tv < lb:
        valid = (t_idx % lb) < tv
        lw = jnp.where(valid, lw, 0.0)
        kk = jnp.where(valid, kk, 0.0)
        kmod = jnp.where(valid, kmod, 0.0)
        xv = jnp.where(valid, xv, 0.0)

    row = lax.broadcasted_iota(jnp.int32, (L, L), 0)
    col = lax.broadcasted_iota(jnp.int32, (L, L), 1)
    tril = jnp.where((col <= row) & (col // lb == row // lb), 1.0, 0.0).astype(F32)
    subs = range(nsub)
    rows = [slice(s * L, (s + 1) * L) for s in subs]
    cum_s = [_cumsum_rows(tril, lw[rows[s]]) for s in subs]

    lane = lax.broadcasted_iota(jnp.int32, (L, 128), 1)
    lo = lane < HB
    src = lane % HB
    trow = lax.broadcasted_iota(jnp.int32, (L, 128), 0)
    same = (src // lb) == (trow // lb)
    strict = same & (src < trow)
    incl = same & (src <= trow)
    r128 = lax.broadcasted_iota(jnp.int32, (128, 128), 0)
    c128 = lax.broadcasted_iota(jnp.int32, (128, 128), 1)
    blockdiag = (r128 < HB) == (c128 < HB)
    eye_pair = jnp.where(src == trow, 1.0, 0.0).astype(F32)

    sls = [slice(p * 128, (p + 1) * 128) for p in range(N_PAIR)]
    items = [(s, p) for s in subs for p in range(N_PAIR)]
    idx = range(len(items))
    groups = range(nbg)

    at_l, rt_l, bt_l, kt_l, win_l, vp_l = [], [], [], [], [], []
    for (s, p) in items:
        rs, sl = rows[s], sls[p]
        kkp = kk[rs, sl]
        kap = kkp * lax.rsqrt(jnp.maximum(_pair_sum(kkp * kkp, lo), 1e-24))
        cum_p = cum_s[s][:, sl]
        w_in = jnp.exp(cum_p)
        w_inv = jnp.exp(-cum_p)
        at_l.append(-kap * jnp.exp(cum_p - lw[rs, sl]))
        rt_l.append(xr[rs, sl] * w_in)
        bt_l.append(kap * a[rs, sl] * w_inv)
        kt_l.append(kmod[rs, sl] * w_inv)
        win_l.append(w_in)
        vp_l.append(xv[rs, sl])
    bdv_l = [_bd(vp_l[i], lo).astype(BF16) for i in idx]

    gm_l = [_dot_nt(jnp.concatenate([at_l[i], rt_l[i]], axis=0).astype(BF16),
                    jnp.concatenate([_bd(bt_l[i], lo), _bd(kt_l[i], lo)], axis=0).astype(BF16))
            for i in idx]
    n_l = [jnp.where(strict, gm_l[i][0:L, 0:128], 0.0) for i in idx]
    aak_l = [jnp.where(strict, gm_l[i][0:L, 128:256], 0.0).astype(BF16) for i in idx]
    ark_l = [jnp.concatenate([jnp.where(incl, gm_l[i][L:2 * L, 0:128], 0.0),
                              jnp.where(incl, gm_l[i][L:2 * L, 128:256], 0.0)], axis=1).astype(BF16)
             for i in idx]

    xs_l = [[_dot_nt(jnp.concatenate([at_l[i][gi * lb:(gi + 1) * lb], rt_l[i][gi * lb:(gi + 1) * lb]],
                                     axis=0).astype(BF16), sbd[items[i][0] * nbg + gi, items[i][1]].astype(BF16))
             for gi in groups] for i in idx]
    if nbg == 1:
        as_l = [xs_l[i][0][0:lb] for i in idx]
        rs_l = [xs_l[i][0][lb:2 * lb] for i in idx]
    else:
        as_l = [jnp.concatenate([xs_l[i][gi][0:lb] for gi in groups], axis=0) for i in idx]
        rs_l = [jnp.concatenate([xs_l[i][gi][lb:2 * lb] for gi in groups], axis=0) for i in idx]

    y0_l = [as_l[i] + _dot(aak_l[i], bdv_l[i]) for i in idx]

    dm_l = [eye_pair for _ in idx]
    s_blk = 1
    while 2 * s_blk <= lb:
        lvl = ((trow // (2 * s_blk)) == (src // (2 * s_blk))) & ((trow % (2 * s_blk)) >= s_blk) \
            & ((src % (2 * s_blk)) < s_blk)
        if s_blk == 1:
            dm_l = [dm_l[i] + jnp.where(lvl, n_l[i], 0.0) for i in idx]
        else:
            t1_l = [_dot(jnp.where(lvl, n_l[i], 0.0).astype(BF16), _bd(dm_l[i], lo).astype(BF16))
                    for i in idx]
            dm_l = [dm_l[i] + _dot(dm_l[i].astype(BF16), _bd(t1_l[i], lo).astype(BF16)) for i in idx]
        s_blk *= 2
    u_l = [_dot(dm_l[i].astype(BF16), _bd(y0_l[i], lo).astype(BF16)) for i in idx]

    gate_b = _sigmoid(gb_ref[...].astype(F32).reshape(LT, D))
    y_a = ya_ref[...].astype(F32).reshape(LT, D)
    bonus_l = [_pair_sum(xr[rows[s], sls[p]] * kmod[rows[s], sls[p]] * rk_ref[:, sls[p]], lo) * vp_l[i]
               for i, (s, p) in enumerate(items)]

    o_l = [rs_l[i] + _dot(ark_l[i], jnp.concatenate([_bd(u_l[i], lo).astype(BF16), bdv_l[i]], axis=0))
           for i in idx]

    w3_l = [win_l[i].reshape(nbg, lb, 128)[:, lb - 1:lb, :] for i in idx]
    rhs_l = []
    for i in idx:
        w_last = jnp.broadcast_to(w3_l[i], (nbg, lb, 128)).reshape(L, 128)
        rhs_l.append(jnp.concatenate([bt_l[i] * w_last, kt_l[i] * w_last], axis=0).astype(BF16))
    uv_l = [jnp.concatenate([u_l[i], vp_l[i]], axis=0) for i in idx]
    if nbg == 1:
        upd_l = [_dot_tn(uv_l[i].astype(BF16), rhs_l[i]) for i in idx]
        for i, (s, p) in enumerate(items):
            sbd[s, p] = sbd[s, p] * w3_l[i][0] + jnp.where(blockdiag, upd_l[i], 0.0)
    else:
        cgrp = (c128 % L) // lb
        uvt_l = [uv_l[i].T for i in idx]
        for i, (s, p) in enumerate(items):
            for gi in groups:
                upd = _dot(jnp.where(cgrp == gi, uvt_l[i], 0.0).astype(BF16), rhs_l[i])
                q_ = s * nbg + gi
                sbd[q_, p] = sbd[q_, p] * w3_l[i][gi] + jnp.where(blockdiag, upd, 0.0)

    out_l = []
    for i, (s, p) in enumerate(items):
        rs, sl = rows[s], sls[p]
        o = o_l[i]
        mu = _pair_sum(o, lo) * (1.0 / HB)
        oc = o - mu
        var = _pair_sum(oc * oc, lo) * (1.0 / HB)
        on = oc * lax.rsqrt(var + RWKV_EPS) * lw_ref[:, sl] + lb_ref[:, sl]
        yb = (on + bonus_l[i]) * g[rs, sl]
        out_l.append(y_a[rs, sl] + gate_b[rs, sl] * yb)
    u_rows = [jnp.concatenate(out_l[s * N_PAIR:(s + 1) * N_PAIR], axis=1) for s in subs]
    u_all = u_rows[0] if nsub == 1 else jnp.concatenate(u_rows, axis=0)
    u_ref[...] = u_all.reshape(nseq, lb, D).astype(u_ref.dtype)

    @pl.when(pl.program_id(1) == pl.num_programs(1) - 1)
    def _():
        for gi in range(nseq):
            for p in range(N_PAIR):
                s_ref[gi, 2 * p] = sbd[gi, p, 0:HB, 0:HB]
                s_ref[gi, 2 * p + 1] = sbd[gi, p, HB:2 * HB, HB:2 * HB]


def _rwkv(main3, cols, tail3, ya3, prev, s0, prm, nsub, nbg, lb, tv):
    b, tp, _ = main3.shape
    has_state = s0 is not None
    nq = nsub * nbg
    blk = lambda j: pl.BlockSpec((nq, lb, D), lambda i, c, j=j: (i, c, j))
    pblk = lambda j: pl.BlockSpec((nq, 1, D), lambda i, c, j=j: (i, 0, j))
    full = lambda a: pl.BlockSpec(a.shape, lambda i, c: (0,) * a.ndim)
    sblk = pl.BlockSpec((nq, H_B, HB, HB), lambda i, c: (i, 0, 0, 0))
    c_r, c_k, c_v, c_gb = cols
    in_specs = [blk(c_r), blk(c_k), blk(c_v), blk(c_gb),
                pl.BlockSpec((nq, lb, LORA), lambda i, c: (i, c, 0)),
                pl.BlockSpec((nq, lb, D), lambda i, c: (i, c, 0))]
    args = [main3, main3, main3, main3, tail3, ya3]
    if has_state:
        in_specs += [pblk(c_r), pblk(c_k), pblk(c_v), pl.BlockSpec((nq, 1, LORA), lambda i, c: (i, 0, 0)),
                     pl.BlockSpec((N_PAIR, nq * PAIR_ROWS, 128), lambda i, c: (0, i, 0))]
        args += [prev[0], prev[0], prev[0], prev[1], s0]
    in_specs += [full(a) for a in prm]
    args += list(prm)
    return pl.pallas_call(
        functools.partial(_rwkv_kernel, nsub, nbg, lb, tv, has_state),
        grid=(b // nq, tp // lb),
        in_specs=in_specs,
        out_specs=[pl.BlockSpec((nq, lb, D), lambda i, c: (i, c, 0)), sblk],
        out_shape=[jax.ShapeDtypeStruct((b, tp, D), main3.dtype),
                   jax.ShapeDtypeStruct((b, H_B, HB, HB), F32)],
        scratch_shapes=[pltpu.VMEM((nq, N_PAIR, 128, 128), F32),
                        pltpu.VMEM((nq, 1, D), F32), pltpu.VMEM((nq, 1, D), F32),
                        pltpu.VMEM((nq, 1, D), F32), pltpu.VMEM((nq, 1, LORA), F32)],
        compiler_params=_cp(("parallel", "arbitrary")),
        name="rwkv",
    )(*args)


def _tail_kernel(u_ref, x_ref, g1_ref, sh_ref, sc_ref, g2_ref, wo_ref, wu_ref, wd_ref,
                 l1g_ref, l1b_ref, l2g_ref, l2b_ref, o_ref, x1_scr, h_scr, acc):
    bb, tt, _ = x_ref.shape
    j = pl.program_id(2)

    @pl.when(j == 0)
    def _():
        u = u_ref[...].reshape(bb * tt, D).astype(BF16)
        y = _dot(u, wo_ref[...]).reshape(bb, tt, D)
        x1 = _layer_norm(ALPHA * x_ref[...] + g1_ref[...] * y, l1g_ref[...], l1b_ref[...])
        x1_scr[...] = x1
        h_scr[...] = (x1 * (1.0 + sc_ref[...]) + sh_ref[...]).reshape(bb * tt, D).astype(BF16)
        acc[...] = jnp.zeros_like(acc)

    up = jnp.maximum(_dot(h_scr[...], wu_ref[...]), 0.0)
    acc[...] += _dot((up * up).astype(BF16), wd_ref[...])

    @pl.when(j == pl.num_programs(2) - 1)
    def _():
        z = ALPHA * x1_scr[...] + g2_ref[...] * acc[...].reshape(bb, tt, D)
        o_ref[...] = _layer_norm(z, l2g_ref[...], l2b_ref[...])


def _tail(u3, x3, mod3, q, bb, tt):
    b, tp, _ = x3.shape
    blk = pl.BlockSpec((bb, tt, D), lambda i, t, j: (i, t, 0))
    mblk = lambda col: pl.BlockSpec((bb, 1, D), lambda i, t, j, col=col: (i, 0, col))
    full = lambda shp: pl.BlockSpec(shp, lambda i, t, j: (0,) * len(shp))
    return pl.pallas_call(
        _tail_kernel,
        grid=(b // bb, tp // tt, D_FF // FF_CHUNK),
        in_specs=[blk, blk, mblk(2), mblk(3), mblk(4), mblk(5),
                  full((D, D)),
                  pl.BlockSpec((D, FF_CHUNK), lambda i, t, j: (0, j)),
                  pl.BlockSpec((FF_CHUNK, D), lambda i, t, j: (j, 0)),
                  full((1, D)), full((1, D)), full((1, D)), full((1, D))],
        out_specs=blk,
        out_shape=jax.ShapeDtypeStruct((b, tp, D), F32),
        scratch_shapes=[pltpu.VMEM((bb, tt, D), F32), pltpu.VMEM((bb * tt, D), BF16),
                        pltpu.VMEM((bb * tt, D), F32)],
        compiler_params=_cp(("parallel", "parallel", "arbitrary")),
        name="outproj_ffn",
    )(u3, x3, mod3, mod3, mod3, mod3, q['w_out'], q['w_up'], q['w_down'],
      q['ln1_g'], q['ln1_b'], q['ln2_g'], q['ln2_b'])


def _relayout_params(p):
    w = p['w_in']
    w_main = jnp.concatenate(
        [w[:, :3 * D], w[:, 3 * D + 8:6 * D + 8], w[:, 6 * D + 8 + LORA:8 * D + 8 + LORA]], axis=1).astype(BF16)
    w_tail = jnp.concatenate(
        [w[:, 6 * D + 8:6 * D + 8 + LORA], w[:, 3 * D:3 * D + 8],
         jnp.zeros((D, N_TAIL - LORA - 8), F32)], axis=1).astype(BF16)
    mu = p['rwkv_mu']
    z64 = jnp.zeros((64, D), F32)
    z128 = jnp.zeros((128, D), F32)
    row = lambda a: a.reshape(1, -1)
    rw = (row(mu[0:D]), row(mu[D:2 * D]), row(mu[2 * D:3 * D]), row(mu[3 * D:3 * D + LORA]),
          row(p['rwkv_w0']), row(p['rwkv_a0']), row(p['rwkv_k_k']), row(p['rwkv_k_a']),
          row(p['rwkv_r_k']), row(p['rwkv_lnx_w']), row(p['rwkv_lnx_b']),
          jnp.concatenate([p['rwkv_w2'], z64, z128], axis=0).astype(BF16),
          jnp.concatenate([z64, p['rwkv_a2'], z128], axis=0).astype(BF16),
          jnp.concatenate([z128, p['rwkv_g2']], axis=0).astype(BF16))
    gbias = jnp.concatenate([p['mlstm_i_bias'], p['mlstm_f_bias'], jnp.zeros((120,), F32)]).reshape(1, 128)
    return dict(w_main=w_main, w_tail=w_tail, rw=rw, gbias=gbias,
                conv_w=p['conv_w'], conv_b=row(p['conv_b']), norm_w=row(p['mlstm_norm_w']),
                w_out=p['w_out'].astype(BF16), w_up=p['w_up'].astype(BF16), w_down=p['w_down'].astype(BF16),
                ln1_g=row(p['ln1_g']), ln1_b=row(p['ln1_b']), ln2_g=row(p['ln2_g']), ln2_b=row(p['ln2_b']))


def _prompt_layer(x, mod, q, seq_tile, mlstm_chunk):
    b, t, _ = x.shape
    mod3 = mod.reshape(b, 1, N_COND)
    main3, tail3 = _inproj(x, mod3, q['w_main'], q['w_tail'], 1, seq_tile, BF16)
    ya3, c1, n1, m1 = _mlstm_seq(main3, tail3, q['conv_w'], q['conv_b'], q['gbias'], q['norm_w'],
                                 mlstm_chunk, min(2, b))
    u3, s1 = _rwkv(main3, RWKV_SECTIONS, tail3, ya3, None, None, q['rw'], min(2, b), 1, RW_L, RW_L)
    y = _tail(u3, x, mod3, q, 1, seq_tile)
    shift = _modulate_rows(x[:, t - 1, :], mod)
    conv = main3[:, t - (CONV_W - 1):, :2 * D].astype(F32)
    return y, (c1, n1[:, :H_A, :], m1[:, 0, :H_A], conv, s1, shift)


def _sample_layer(x, mod, st, q, bb):
    c0, n0, m0, conv0, s0, shift0 = st
    b, t, _ = x.shape
    mod3 = mod.reshape(b, 1, N_COND)
    xp = jnp.pad(x, ((0, 0), (0, 8 - t), (0, 0)))
    main3, tail3 = _inproj(xp, mod3, q['w_main'], q['w_tail'], bb, 8, F32)
    pm, pt = _inproj(shift0.reshape(1, b, D), jnp.zeros((1, 1, N_COND), F32), q['w_main'], q['w_tail'], 1, b, F32)
    prev = (pm.reshape(b, 1, N_MAIN), pt.reshape(b, 1, N_TAIL))
    conv0p = jnp.pad(conv0, ((0, 0), (8 - (CONV_W - 1), 0), (0, 0)))
    n0p = jnp.pad(n0, ((0, 0), (0, 8 - H_A), (0, 0)))
    m0p = jnp.pad(m0, ((0, 0), (0, 128 - H_A))).reshape(b, 1, 128)
    ya3, c1, n1, m1 = _mlstm_step(main3, tail3, conv0p, c0, n0p, m0p, q['conv_w'], q['conv_b'], q['gbias'],
                                  q['norm_w'], t, min(4, b))
    s0_pairs = _state_pairs(jnp.transpose(s0, (1, 2, 3, 0)))
    u3, s1 = _rwkv(main3, RWKV_SECTIONS, tail3, ya3, prev, s0_pairs, q['rw'], 1, RW_L // 8, 8, t)
    y = _tail(u3, xp, mod3, q, bb, 8)
    shift = _modulate_rows(x[:, t - 1, :], mod)
    conv = jnp.concatenate([conv0, main3[:, :t, :2 * D]], axis=1)[:, t:, :]
    return y[:, :t, :], (c1, n1[:, :H_A, :], m1[:, 0, :H_A], conv, s1, shift)


def kernel(x_prompt, x_sample, c_prompt, c_sample, state_mlstm_C, state_mlstm_n, state_mlstm_m, state_mlstm_conv, state_rwkv_S, state_rwkv_shift, w_cond, b_cond, w_in, mlstm_i_bias, mlstm_f_bias, conv_w, conv_b, mlstm_norm_w, rwkv_mu, rwkv_w0, rwkv_w2, rwkv_a0, rwkv_a2, rwkv_g2, rwkv_k_k, rwkv_k_a, rwkv_r_k, rwkv_lnx_w, rwkv_lnx_b, w_out, ln1_g, ln1_b, w_up, w_down, ln2_g, ln2_b):
    depth = w_in.shape[0]
    bp = x_prompt.shape[0]
    yp, ys = x_prompt, x_sample
    new_p = [[] for _ in range(6)]
    new_s = [[] for _ in range(6)]
    for l in range(depth):
        p = {'w_in': w_in[l], 'mlstm_i_bias': mlstm_i_bias[l], 'mlstm_f_bias': mlstm_f_bias[l],
             'conv_w': conv_w[l], 'conv_b': conv_b[l], 'mlstm_norm_w': mlstm_norm_w[l],
             'rwkv_mu': rwkv_mu[l], 'rwkv_w0': rwkv_w0[l], 'rwkv_w2': rwkv_w2[l], 'rwkv_a0': rwkv_a0[l],
             'rwkv_a2': rwkv_a2[l], 'rwkv_g2': rwkv_g2[l], 'rwkv_k_k': rwkv_k_k[l], 'rwkv_k_a': rwkv_k_a[l],
             'rwkv_r_k': rwkv_r_k[l].reshape(-1), 'rwkv_lnx_w': rwkv_lnx_w[l], 'rwkv_lnx_b': rwkv_lnx_b[l],
             'w_out': w_out[l], 'ln1_g': ln1_g[l], 'ln1_b': ln1_b[l], 'w_up': w_up[l], 'w_down': w_down[l],
             'ln2_g': ln2_g[l], 'ln2_b': ln2_b[l]}
        q = _relayout_params(p)
        mod = _cond(jnp.concatenate([c_prompt, c_sample], axis=0), w_cond[l], b_cond[l])
        st_in = (state_mlstm_C[l], state_mlstm_n[l], state_mlstm_m[l], state_mlstm_conv[l],
                 state_rwkv_S[l], state_rwkv_shift[l])
        ys, st_s = _sample_layer(ys, mod[bp:], st_in, q, min(128, ys.shape[0]))
        yp, st_p = _prompt_layer(yp, mod[:bp], q, min(1024, yp.shape[1]), min(256, yp.shape[1]))
        for lst, t in zip(new_p, st_p):
            lst.append(t)
        for lst, t in zip(new_s, st_s):
            lst.append(t)
    outs_p = [jnp.stack(t) for t in new_p]
    outs_s = [jnp.stack(t) for t in new_s]
    return (yp, ys, *outs_p, *outs_s)
```

```python
import functools

import jax
import jax.numpy as jnp
from jax import lax
from jax.experimental import pallas as pl
from jax.experimental.pallas import tpu as pltpu

F32 = jnp.float32
BF16 = jnp.bfloat16
HIGHEST = lax.Precision.HIGHEST

D = 1024
H_A = 4
DK = 256
CONV_W = 4
H_B = 16
HB = 64
N_PAIR = H_B // 2
D_FF = 4096
N_COND = 6 * D
ALPHA = 2.0 ** 0.25
LN_EPS = 1e-5
MLSTM_EPS = 1e-6
RWKV_EPS = 64e-5

N_MAIN = 8 * D
RWKV_SECTIONS = (3, 4, 5, 7)
LORA = 256
TAIL_IF = LORA
N_TAIL = 512
TN_MAIN = 2048
N_MAIN_TILES = N_MAIN // TN_MAIN

RW_L = 64
FF_CHUNK = 1024
NEG = -1e30
VMEM_LIMIT = 56 * 1024 * 1024


def _cp(sem):
    return pltpu.CompilerParams(dimension_semantics=sem, vmem_limit_bytes=VMEM_LIMIT)


def _dot(a, b, prec=None):
    return jnp.dot(a, b, preferred_element_type=F32, precision=prec)


def _dot_nt(a, b, prec=None):
    return lax.dot_general(a, b, (((1,), (1,)), ((), ())), preferred_element_type=F32, precision=prec)


def _dot_tn(a, b, prec=None):
    return lax.dot_general(a, b, (((0,), (0,)), ((), ())), preferred_element_type=F32, precision=prec)


def _cumsum_rows(tril01, x):
    hi = x.astype(BF16)
    r1 = x - hi.astype(F32)
    mid = r1.astype(BF16)
    lo = (r1 - mid.astype(F32)).astype(BF16)
    t = tril01.astype(BF16)
    return _dot(t, hi) + _dot(t, mid) + _dot(t, lo)


def _log_sigmoid(x):
    return jnp.minimum(x, 0.0) - jnp.log1p(jnp.exp(-jnp.abs(x)))


def _sigmoid(x):
    return 0.5 * jnp.tanh(0.5 * x) + 0.5


def _silu(x):
    h = 0.5 * x
    return h + h * jnp.tanh(h)


def _layer_norm(z, g, b):
    mu = jnp.mean(z, axis=-1, keepdims=True)
    zc = z - mu
    var = jnp.mean(zc * zc, axis=-1, keepdims=True)
    return zc * lax.rsqrt(var + LN_EPS) * g + b


def _cond_kernel(c_ref, w_ref, b_ref, o_ref):
    s = _silu(c_ref[...]).astype(BF16)
    o_ref[...] = _dot(s, w_ref[...].astype(BF16)) + b_ref[...]


def _cond(c, w_cond, b_cond):
    n = c.shape[0]
    tn = 1536
    return pl.pallas_call(
        _cond_kernel,
        grid=(N_COND // tn,),
        in_specs=[pl.BlockSpec((n, D), lambda j: (0, 0)),
                  pl.BlockSpec((D, tn), lambda j: (0, j)),
                  pl.BlockSpec((1, tn), lambda j: (0, j))],
        out_specs=pl.BlockSpec((n, tn), lambda j: (0, j)),
        out_shape=jax.ShapeDtypeStruct((n, N_COND), F32),
        compiler_params=_cp(("arbitrary",)),
        name="cond",
    )(c, w_cond, b_cond.reshape(1, N_COND))


def _inproj_kernel(x_ref, sh_ref, sc_ref, wm_ref, wt_ref, main_ref, tail_ref, h_scr):
    bb, tt, _ = x_ref.shape
    j = pl.program_id(2)

    @pl.when(j == 0)
    def _():
        h = x_ref[...] * (1.0 + sc_ref[...]) + sh_ref[...]
        h_scr[...] = h.reshape(bb * tt, D).astype(BF16)

    main_ref[...] = _dot_nt(h_scr[...], wm_ref[...]).reshape(bb, tt, TN_MAIN).astype(main_ref.dtype)

    @pl.when(j == N_MAIN_TILES - 1)
    def _():
        tail_ref[...] = _dot_nt(h_scr[...], wt_ref[...]).reshape(bb, tt, N_TAIL)


def _inproj(x3, mod3, w_main, w_tail, bb, tt, main_dtype):
    b, tp, _ = x3.shape
    return pl.pallas_call(
        _inproj_kernel,
        grid=(b // bb, tp // tt, N_MAIN_TILES),
        in_specs=[pl.BlockSpec((bb, tt, D), lambda i, t, j: (i, t, 0)),
                  pl.BlockSpec((bb, 1, D), lambda i, t, j: (i, 0, 0)),
                  pl.BlockSpec((bb, 1, D), lambda i, t, j: (i, 0, 1)),
                  pl.BlockSpec((TN_MAIN, D), lambda i, t, j: (j, 0)),
                  pl.BlockSpec((N_TAIL, D), lambda i, t, j: (0, 0))],
        out_specs=[pl.BlockSpec((bb, tt, TN_MAIN), lambda i, t, j: (i, t, j)),
                   pl.BlockSpec((bb, tt, N_TAIL), lambda i, t, j: (i, t, 0))],
        out_shape=[jax.ShapeDtypeStruct((b, tp, N_MAIN), main_dtype),
                   jax.ShapeDtypeStruct((b, tp, N_TAIL), F32)],
        scratch_shapes=[pltpu.VMEM((bb * tt, D), BF16)],
        compiler_params=_cp(("parallel", "parallel", "arbitrary")),
        name="inproj",
    )(x3, mod3, mod3, w_main, w_tail)


def _modulate_kernel(x_ref, sh_ref, sc_ref, o_ref):
    o_ref[...] = x_ref[...] * (1.0 + sc_ref[...]) + sh_ref[...]


def _modulate_rows(x2, mod2):
    n = x2.shape[0]
    return pl.pallas_call(
        _modulate_kernel,
        grid=(1,),
        in_specs=[pl.BlockSpec((n, D), lambda i: (0, 0)),
                  pl.BlockSpec((n, D), lambda i: (0, 0)),
                  pl.BlockSpec((n, D), lambda i: (0, 1))],
        out_specs=pl.BlockSpec((n, D), lambda i: (0, 0)),
        out_shape=jax.ShapeDtypeStruct((n, D), F32),
        name="modulate_last",
    )(x2, mod2, mod2)


def _conv4(pad_ref, n_rows, cw, cb):
    acc = cb + pad_ref[8:8 + n_rows, :] * cw[3:4, :]
    acc = acc + pad_ref[7:7 + n_rows, :] * cw[2:3, :]
    acc = acc + pad_ref[6:6 + n_rows, :] * cw[1:2, :]
    acc = acc + pad_ref[5:5 + n_rows, :] * cw[0:1, :]
    return acc


def _head_norm_rows(h, eps):
    mu = jnp.mean(h, axis=-1, keepdims=True)
    hc = h - mu
    var = jnp.mean(hc * hc, axis=-1, keepdims=True)
    return hc * lax.rsqrt(var + eps)


def _mlstm_seq_kernel(nb, qp_ref, kp_ref, v_ref, ga_ref, if_ref, cw_ref, cb_ref, gb_ref, nw_ref,
                      ya_ref, c_ref, n_ref, m_ref, haloq, halok):
    L = qp_ref.shape[1]
    assert qp_ref.dtype == BF16 and kp_ref.dtype == BF16

    @pl.when(pl.program_id(1) == 0)
    def _():
        c_ref[...] = jnp.zeros_like(c_ref)
        n_ref[...] = jnp.zeros_like(n_ref)
        m_ref[...] = jnp.zeros_like(m_ref)
        haloq[...] = jnp.zeros_like(haloq)
        halok[...] = jnp.zeros_like(halok)

    cw = cw_ref[...]
    cb = cb_ref[...]
    nw = nw_ref[...]
    row = lax.broadcasted_iota(jnp.int32, (L, L), 0)
    col = lax.broadcasted_iota(jnp.int32, (L, L), 1)
    causal = col <= row
    tril = jnp.where(causal, 1.0, 0.0).astype(F32)
    srow = lax.broadcasted_iota(jnp.int32, (3 * L, L), 0)
    scol = lax.broadcasted_iota(jnp.int32, (3 * L, L), 1)
    shift_mat = jnp.where(scol + srow // L + 1 == srow % L, 1.0, 0.0).astype(BF16)
    r8 = lax.broadcasted_iota(jnp.int32, (8, 1), 0)

    def conv(x_bf, halo_ref, cw_, cb_):
        sh = _dot(shift_mat, x_bf)
        x = x_bf.astype(F32)
        acc = cb_ + x * cw_[3:4, :] + sh[0:L] * cw_[2:3, :] + sh[L:2 * L] * cw_[1:2, :] \
            + sh[2 * L:3 * L] * cw_[0:1, :]
        halo = halo_ref[...]
        fix = jnp.zeros((8, DK), F32)
        for j in range(1, CONV_W):
            fix = fix + jnp.where(r8 < j, pltpu.roll(halo, j, 0), 0.0) * cw_[CONV_W - 1 - j:CONV_W - j, :]
        halo_ref[...] = x[L - 8:L, :]
        return jnp.concatenate([acc[0:8] + fix, acc[8:L]], axis=0)

    items = [(bi, h) for bi in range(nb) for h in range(H_A)]
    sl_of = lambda h: slice(h * DK, (h + 1) * DK)
    q_it, k_it = {}, {}
    for (bi, h) in items:
        sl, ksl = sl_of(h), slice(D + h * DK, D + (h + 1) * DK)
        q_it[bi, h] = _silu(conv(qp_ref[bi, :, sl], haloq.at[bi, :, sl], cw[:, sl], cb[:, sl]))
        k_it[bi, h] = _silu(conv(kp_ref[bi, :, sl], halok.at[bi, :, sl], cw[:, ksl], cb[:, ksl])) * (DK ** -0.5)

    gpre_l, bcum_l, gpre_t_l, bcum_t_l = [], [], [], []
    for bi in range(nb):
        gpre = if_ref[bi] + gb_ref[...]
        bcum = _dot(tril, _log_sigmoid(gpre), HIGHEST)
        gpre_l.append(gpre)
        bcum_l.append(bcum)
        gpre_t_l.append(gpre.T)
        bcum_t_l.append(bcum.T)

    st = {}
    for (bi, h) in items:
        ig_col = gpre_l[bi][:, h:h + 1]
        b_col = bcum_l[bi][:, H_A + h:H_A + h + 1]
        ig_row = gpre_t_l[bi][h:h + 1, :]
        b_row = bcum_t_l[bi][H_A + h:H_A + h + 1, :]
        m_prev = m_ref[bi][:, h:h + 1]
        g_col = b_col + m_prev
        dlog = jnp.where(causal, b_col - b_row + ig_row, NEG)
        m_t = jnp.maximum(g_col, jnp.max(dlog, axis=1, keepdims=True))
        b_last = b_col[L - 1:L, :]
        wlog = b_last - b_col + ig_col
        m_new = jnp.maximum(b_last + m_prev, jnp.max(wlog, axis=0, keepdims=True))
        st[bi, h] = dict(m_t=m_t, w_inter=jnp.exp(g_col - m_t), p=jnp.exp(dlog - m_t), m_new=m_new,
                         decay=jnp.exp(b_last + m_prev - m_new), wts=jnp.exp(wlog - m_new))
    qb = {it: q_it[it].astype(BF16) for it in items}
    kb = {it: k_it[it].astype(BF16) for it in items}
    s_l = {it: _dot_nt(qb[it], kb[it]) * st[it]['p'] for it in items}
    qc_l = {(bi, h): _dot_nt(qb[bi, h], c_ref[bi, h].astype(BF16)) for (bi, h) in items}
    sv_l = {(bi, h): _dot(s_l[bi, h].astype(BF16), v_ref[bi, :, sl_of(h)].astype(BF16)) for (bi, h) in items}
    upd_l = {(bi, h): _dot_tn((st[bi, h]['wts'] * v_ref[bi, :, sl_of(h)].astype(F32)).astype(BF16), kb[bi, h])
             for (bi, h) in items}
    for (bi, h) in items:
        sl = sl_of(h)
        d = st[bi, h]
        qh = q_it[bi, h]
        kh = k_it[bi, h]
        nh = n_ref[bi, h:h + 1, :]
        num = d['w_inter'] * qc_l[bi, h] + sv_l[bi, h]
        den = d['w_inter'] * jnp.sum(qh * nh, axis=1, keepdims=True) + jnp.sum(s_l[bi, h], axis=1, keepdims=True)
        hh = num / jnp.maximum(jnp.abs(den), jnp.exp(-d['m_t']))
        ga = _sigmoid(ga_ref[bi, :, sl].astype(F32))
        ya_ref[bi, :, sl] = (ga * _head_norm_rows(hh, MLSTM_EPS) * nw[:, sl]).astype(ya_ref.dtype)
        c_ref[bi, h] = d['decay'] * c_ref[bi, h] + upd_l[bi, h]
        n_ref[bi, h:h + 1, :] = d['decay'] * nh + jnp.sum(d['wts'] * kh, axis=0, keepdims=True)
        m_ref[bi, :, h:h + 1] = d['m_new']


def _mlstm_seq(main3, tail3, conv_w, conv_b, gbias, norm_w, L, nb):
    b, tp, _ = main3.shape
    blk = lambda j: pl.BlockSpec((nb, L, D), lambda i, c, j=j: (i, c, j))
    full = lambda shp: pl.BlockSpec(shp, lambda i, c: (0,) * len(shp))
    return pl.pallas_call(
        functools.partial(_mlstm_seq_kernel, nb),
        grid=(b // nb, tp // L),
        in_specs=[blk(0), blk(1), blk(2), blk(6),
                  pl.BlockSpec((nb, L, 128), lambda i, c: (i, c, TAIL_IF // 128)),
                  full((CONV_W, 2 * D)), full((1, 2 * D)), full((1, 128)), full((1, D))],
        out_specs=[pl.BlockSpec((nb, L, D), lambda i, c: (i, c, 0)),
                   pl.BlockSpec((nb, H_A, DK, DK), lambda i, c: (i, 0, 0, 0)),
                   pl.BlockSpec((nb, 8, DK), lambda i, c: (i, 0, 0)),
                   pl.BlockSpec((nb, 1, 128), lambda i, c: (i, 0, 0))],
        out_shape=[jax.ShapeDtypeStruct((b, tp, D), main3.dtype),
                   jax.ShapeDtypeStruct((b, H_A, DK, DK), F32),
                   jax.ShapeDtypeStruct((b, 8, DK), F32),
                   jax.ShapeDtypeStruct((b, 1, 128), F32)],
        scratch_shapes=[pltpu.VMEM((nb, 8, D), F32), pltpu.VMEM((nb, 8, D), F32)],
        compiler_params=_cp(("parallel", "arbitrary")),
        name="mlstm_seq",
    )(main3, main3, main3, main3, tail3, conv_w, conv_b, gbias, norm_w)


def _mlstm_step_kernel(tv, nb, qp_ref, kp_ref, v_ref, ga_ref, if_ref, conv0_ref, c0_ref, n0_ref, m0_ref,
                       cw_ref, cb_ref, gb_ref, nw_ref,
                       ya_ref, c_ref, n_ref, m_ref, padq, padk, gpad, lpad, kpad, vpad, wvpad):
    @pl.when(pl.program_id(0) == 0)
    def _():
        for r in (gpad, lpad, kpad, vpad, wvpad):
            r[...] = jnp.zeros_like(r)

    cw = cw_ref[...]
    cb = cb_ref[...]
    nw = nw_ref[...]
    gb = gb_ref[...]
    trow = lax.broadcasted_iota(jnp.int32, (8, 128), 0)
    scol = lax.broadcasted_iota(jnp.int32, (8, 128), 1)
    mask = (scol <= trow) & (scol < tv)
    rvalid = lax.broadcasted_iota(jnp.int32, (8, 1), 0) < tv
    r128 = lax.broadcasted_iota(jnp.int32, (128, 128), 0)
    c128 = lax.broadcasted_iota(jnp.int32, (128, 128), 1)
    tril = jnp.where(c128 <= r128, 1.0, 0.0).astype(F32)
    n_ref[...] = jnp.zeros_like(n_ref)
    m_ref[...] = jnp.zeros_like(m_ref)

    batches = range(nb)
    q_l, gpre_l, bcol_l, gt_l, bt_l, ga_l = [], [], [], [], [], []
    for bi in batches:
        padq[bi, 0:8, :] = conv0_ref[bi, :, 0:D]
        padk[bi, 0:8, :] = conv0_ref[bi, :, D:2 * D]
        padq[bi, 8:16, :] = qp_ref[bi]
        padk[bi, 8:16, :] = kp_ref[bi]
        q_l.append(_silu(_conv4(padq.at[bi], 8, cw[:, 0:D], cb[:, 0:D])))
        kpad[bi, 0:8, :] = _silu(_conv4(padk.at[bi], 8, cw[:, D:2 * D], cb[:, D:2 * D])) * (DK ** -0.5)
        vpad[bi, 0:8, :] = v_ref[bi]
        gpre = if_ref[bi] + gb
        gpad[bi, 0:8, :] = gpre
        lpad[bi, 0:8, :] = _log_sigmoid(gpre)
        gpre_l.append(gpre)
        ga_l.append(_sigmoid(ga_ref[bi]))
    for bi in batches:
        bpad = _dot(tril, lpad[bi], HIGHEST)
        bcol_l.append(bpad[0:8, :])
        bt_l.append(bpad.T)
        gt_l.append(gpad[bi].T)

    probs = [(bi, h) for bi in batches for h in range(H_A)]
    sl_of = lambda h: slice(h * DK, (h + 1) * DK)
    st = {}
    for (bi, h) in probs:
        ig_col = gpre_l[bi][:, h:h + 1]
        b_col = bcol_l[bi][:, H_A + h:H_A + h + 1]
        ig_row = gt_l[bi][h:h + 1, :]
        b_row = bt_l[bi][H_A + h:H_A + h + 1, :]
        m_prev = m0_ref[bi][:, h:h + 1]
        g_col = b_col + m_prev
        dlog = jnp.where(mask, b_col - b_row + ig_row, NEG)
        m_t = jnp.maximum(g_col, jnp.max(dlog, axis=1, keepdims=True))
        b_last = b_col[tv - 1:tv, :]
        wlog = jnp.where(rvalid, b_last - b_col + ig_col, NEG)
        m_new = jnp.maximum(b_last + m_prev, jnp.max(wlog, axis=0, keepdims=True))
        wts = jnp.exp(wlog - m_new)
        wvpad[bi, 0:8, sl_of(h)] = wts * vpad[bi, 0:8, sl_of(h)]
        st[bi, h] = dict(m_t=m_t, w_inter=jnp.exp(g_col - m_t), pm=jnp.exp(dlog - m_t), m_new=m_new,
                         decay=jnp.exp(b_last + m_prev - m_new), wts=wts)
    kb = {(bi, h): kpad[bi, :, sl_of(h)].astype(BF16) for (bi, h) in probs}
    qb = {(bi, h): q_l[bi][:, sl_of(h)].astype(BF16) for (bi, h) in probs}
    s_l = {k_: _dot_nt(qb[k_], kb[k_]) * st[k_]['pm'] for k_ in probs}
    qc_l = {(bi, h): _dot_nt(qb[bi, h], c0_ref[bi, h].astype(BF16)) for (bi, h) in probs}
    sv_l = {(bi, h): _dot(s_l[bi, h].astype(BF16), vpad[bi, :, sl_of(h)].astype(BF16)) for (bi, h) in probs}
    upd_l = {(bi, h): _dot(wvpad[bi, :, sl_of(h)].T.astype(BF16), kb[bi, h]) for (bi, h) in probs}
    for (bi, h) in probs:
        sl = sl_of(h)
        d = st[bi, h]
        nh = n0_ref[bi, h:h + 1, :]
        qh = q_l[bi][:, sl]
        num = d['w_inter'] * qc_l[bi, h] + sv_l[bi, h]
        den = d['w_inter'] * jnp.sum(qh * nh, axis=1, keepdims=True) + jnp.sum(s_l[bi, h], axis=1, keepdims=True)
        hh = num / jnp.maximum(jnp.abs(den), jnp.exp(-d['m_t']))
        ya_ref[bi, :, sl] = ga_l[bi][:, sl] * _head_norm_rows(hh, MLSTM_EPS) * nw[:, sl]
        c_ref[bi, h] = d['decay'] * c0_ref[bi, h] + upd_l[bi, h]
        n_ref[bi, h:h + 1, :] = d['decay'] * nh + jnp.sum(d['wts'] * kpad[bi, 0:8, sl], axis=0, keepdims=True)
        m_ref[bi, :, h:h + 1] = d['m_new']


def _mlstm_step(main3, tail3, conv0p, c0, n0p, m0p, conv_w, conv_b, gbias, norm_w, tv, nb):
    b = main3.shape[0]
    blk = lambda j: pl.BlockSpec((nb, 8, D), lambda i, j=j: (i, 0, j))
    full = lambda shp: pl.BlockSpec(shp, lambda i: (0,) * len(shp))
    state_specs = [pl.BlockSpec((nb, H_A, DK, DK), lambda i: (i, 0, 0, 0)),
                   pl.BlockSpec((nb, 8, DK), lambda i: (i, 0, 0)),
                   pl.BlockSpec((nb, 1, 128), lambda i: (i, 0, 0))]
    return pl.pallas_call(
        functools.partial(_mlstm_step_kernel, tv, nb),
        grid=(b // nb,),
        in_specs=[blk(0), blk(1), blk(2), blk(6),
                  pl.BlockSpec((nb, 8, 128), lambda i: (i, 0, TAIL_IF // 128)),
                  pl.BlockSpec((nb, 8, 2 * D), lambda i: (i, 0, 0))] + state_specs +
                 [full((CONV_W, 2 * D)), full((1, 2 * D)), full((1, 128)), full((1, D))],
        out_specs=[pl.BlockSpec((nb, 8, D), lambda i: (i, 0, 0))] + state_specs,
        out_shape=[jax.ShapeDtypeStruct((b, 8, D), F32),
                   jax.ShapeDtypeStruct((b, H_A, DK, DK), F32),
                   jax.ShapeDtypeStruct((b, 8, DK), F32),
                   jax.ShapeDtypeStruct((b, 1, 128), F32)],
        scratch_shapes=[pltpu.VMEM((nb, 16, D), F32), pltpu.VMEM((nb, 16, D), F32),
                        pltpu.VMEM((nb, 128, 128), F32), pltpu.VMEM((nb, 128, 128), F32),
                        pltpu.VMEM((nb, 128, D), F32), pltpu.VMEM((nb, 128, D), F32),
                        pltpu.VMEM((nb, 128, D), F32)],
        compiler_params=_cp(("arbitrary",)),
        name="mlstm_step",
    )(main3, main3, main3, main3, tail3, conv0p, c0, n0p, m0p, conv_w, conv_b, gbias, norm_w)


def _bd(x, lo):
    return jnp.concatenate([jnp.where(lo, x, 0.0), jnp.where(lo, 0.0, x)], axis=0)


def _pair_sum(x, lo):
    s_lo = jnp.sum(jnp.where(lo, x, 0.0), axis=1, keepdims=True)
    s_hi = jnp.sum(jnp.where(lo, 0.0, x), axis=1, keepdims=True)
    return jnp.where(lo, s_lo, s_hi)


PAIR_ROWS = 136


def _state_pairs_kernel(s_ref, o_ref):
    nb = s_ref.shape[-1]
    lane = lax.broadcasted_iota(jnp.int32, (nb, 128), 1)
    lo = lane < HB
    for b in range(nb):
        o_ref[0, b * PAIR_ROWS + 128:(b + 1) * PAIR_ROWS, :] = jnp.zeros((PAIR_ROWS - 128, 128), F32)
    for v in range(HB):
        t = jnp.concatenate([s_ref[0, v], s_ref[1, v]], axis=0).T
        o_ref[0, pl.ds(v, nb, stride=PAIR_ROWS), :] = jnp.where(lo, t, 0.0)
        o_ref[0, pl.ds(HB + v, nb, stride=PAIR_ROWS), :] = jnp.where(lo, 0.0, t)


def _state_pairs(s_hvkb):
    nb = s_hvkb.shape[-1]
    return pl.pallas_call(
        _state_pairs_kernel,
        grid=(N_PAIR,),
        in_specs=[pl.BlockSpec((2, HB, HB, nb), lambda p: (p, 0, 0, 0))],
        out_specs=pl.BlockSpec((1, nb * PAIR_ROWS, 128), lambda p: (p, 0, 0)),
        out_shape=jax.ShapeDtypeStruct((N_PAIR, nb * PAIR_ROWS, 128), F32),
        compiler_params=_cp(("arbitrary",)),
        name="state_pairs",
    )(s_hvkb)


def _rwkv_kernel(nsub, nbg, lb, tv, has_state, *refs):
    (r_ref, k_ref, v_ref, gb_ref, l_ref, ya_ref), refs = refs[:6], refs[6:]
    if has_state:
        (pr_ref, pk_ref, pv_ref, pl_ref, s0_ref), refs = refs[:5], refs[5:]
    (mur_ref, muk_ref, muv_ref, mul_ref, w0_ref, a0_ref, kk_ref, ka_ref, rk_ref,
     lw_ref, lb_ref, w2_ref, a2_ref, g2_ref,
     u_ref, s_ref, sbd, cr, ck, cv, cl) = refs
    L = nbg * lb
    nseq = nsub * nbg
    LT = nsub * L

    @pl.when(pl.program_id(1) == 0)
    def _():
        if has_state:
            for gi in range(nseq):
                for p in range(N_PAIR):
                    sbd[gi, p] = s0_ref[p, gi * PAIR_ROWS:gi * PAIR_ROWS + 128, :]
            cr[...] = pr_ref[...].astype(F32)
            ck[...] = pk_ref[...].astype(F32)
            cv[...] = pv_ref[...].astype(F32)
            cl[...] = pl_ref[...]
        else:
            sbd[...] = jnp.zeros_like(sbd)
            for c_ in (cr, ck, cv, cl):
                c_[...] = jnp.zeros_like(c_)

    def shift_mix(x_ref, carry, mu_ref):
        x3 = x_ref[...].astype(F32)
        width = x3.shape[-1]
        tpos = lax.broadcasted_iota(jnp.int32, x3.shape, 1)
        prev = jnp.where(tpos == 0, carry[...], pltpu.roll(x3, 1, 1))
        carry[...] = x3[:, lb - 1:lb, :]
        return (x3 + (prev - x3) * mu_ref[...]).reshape(LT, width)

    xr = shift_mix(r_ref, cr, mur_ref)
    xk = shift_mix(k_ref, ck, muk_ref)
    xv = shift_mix(v_ref, cv, muv_ref)
    xl = shift_mix(l_ref, cl, mul_ref)

    lane_l = lax.broadcasted_iota(jnp.int32, (LT, LORA), 1)
    act = jnp.where(lane_l < 64, jnp.tanh(xl), jnp.where(lane_l < 128, xl, _sigmoid(xl))).astype(BF16)
    z = w0_ref[...] + _dot(act, w2_ref[...])
    w_log = -(jnp.maximum(-z, 0.0) + jnp.log(1.0 + jnp.exp(-jnp.abs(z)))) - 0.5
    lw = -jnp.exp(w_log)
    a = _sigmoid(a0_ref[...] + _dot(act, a2_ref[...]))
    g = _dot(act, g2_ref[...])
    kk = xk * kk_ref[...]
    kmod = xk * (1.0 + (a - 1.0) * ka_ref[...])
    t_idx = lax.broadcasted_iota(jnp.int32, (LT, 1), 0)
    if tv < lb:
        valid = (t_idx % lb) < tv
        lw = jnp.where(valid, lw, 0.0)
        kk = jnp.where(valid, kk, 0.0)
        kmod = jnp.where(valid, kmod, 0.0)
        xv = jnp.where(valid, xv, 0.0)

    row = lax.broadcasted_iota(jnp.int32, (L, L), 0)
    col = lax.broadcasted_iota(jnp.int32, (L, L), 1)
    tril = jnp.where((col <= row) & (col // lb == row // lb), 1.0, 0.0).astype(F32)
    subs = range(nsub)
    rows = [slice(s * L, (s + 1) * L) for s in subs]
    cum_s = [_cumsum_rows(tril, lw[rows[s]]) for s in subs]

    lane = lax.broadcasted_iota(jnp.int32, (L, 128), 1)
    lo = lane < HB
    src = lane % HB
    trow = lax.broadcasted_iota(jnp.int32, (L, 128), 0)
    same = (src // lb) == (trow // lb)
    strict = same & (src < trow)
    incl = same & (src <= trow)
    r128 = lax.broadcasted_iota(jnp.int32, (128, 128), 0)
    c128 = lax.broadcasted_iota(jnp.int32, (128, 128), 1)
    blockdiag = (r128 < HB) == (c128 < HB)
    eye_pair = jnp.where(src == trow, 1.0, 0.0).astype(F32)

    sls = [slice(p * 128, (p + 1) * 128) for p in range(N_PAIR)]
    items = [(s, p) for s in subs for p in range(N_PAIR)]
    idx = range(len(items))
    groups = range(nbg)

    at_l, rt_l, bt_l, kt_l, win_l, vp_l = [], [], [], [], [], []
    for (s, p) in items:
        rs, sl = rows[s], sls[p]
        kkp = kk[rs, sl]
        kap = kkp * lax.rsqrt(jnp.maximum(_pair_sum(kkp * kkp, lo), 1e-24))
        cum_p = cum_s[s][:, sl]
        w_in = jnp.exp(cum_p)
        w_inv = jnp.exp(-cum_p)
        at_l.append(-kap * jnp.exp(cum_p - lw[rs, sl]))
        rt_l.append(xr[rs, sl] * w_in)
        bt_l.append(kap * a[rs, sl] * w_inv)
        kt_l.append(kmod[rs, sl] * w_inv)
        win_l.append(w_in)
        vp_l.append(xv[rs, sl])
    bdv_l = [_bd(vp_l[i], lo).astype(BF16) for i in idx]

    gm_l = [_dot_nt(jnp.concatenate([at_l[i], rt_l[i]], axis=0).astype(BF16),
                    jnp.concatenate([_bd(bt_l[i], lo), _bd(kt_l[i], lo)], axis=0).astype(BF16))
            for i in idx]
    n_l = [jnp.where(strict, gm_l[i][0:L, 0:128], 0.0) for i in idx]
    aak_l = [jnp.where(strict, gm_l[i][0:L, 128:256], 0.0).astype(BF16) for i in idx]
    ark_l = [jnp.concatenate([jnp.where(incl, gm_l[i][L:2 * L, 0:128], 0.0),
                              jnp.where(incl, gm_l[i][L:2 * L, 128:256], 0.0)], axis=1).astype(BF16)
             for i in idx]

    xs_l = [[_dot_nt(jnp.concatenate([at_l[i][gi * lb:(gi + 1) * lb], rt_l[i][gi * lb:(gi + 1) * lb]],
                                     axis=0).astype(BF16), sbd[items[i][0] * nbg + gi, items[i][1]].astype(BF16))
             for gi in groups] for i in idx]
    if nbg == 1:
        as_l = [xs_l[i][0][0:lb] for i in idx]
        rs_l = [xs_l[i][0][lb:2 * lb] for i in idx]
    else:
        as_l = [jnp.concatenate([xs_l[i][gi][0:lb] for gi in groups], axis=0) for i in idx]
        rs_l = [jnp.concatenate([xs_l[i][gi][lb:2 * lb] for gi in groups], axis=0) for i in idx]

    y0_l = [as_l[i] + _dot(aak_l[i], bdv_l[i]) for i in idx]

    dm_l = [eye_pair for _ in idx]
    s_blk = 1
    while 2 * s_blk <= lb:
        lvl = ((trow // (2 * s_blk)) == (src // (2 * s_blk))) & ((trow % (2 * s_blk)) >= s_blk) \
            & ((src % (2 * s_blk)) < s_blk)
        if s_blk == 1:
            dm_l = [dm_l[i] + jnp.where(lvl, n_l[i], 0.0) for i in idx]
        else:
            t1_l = [_dot(jnp.where(lvl, n_l[i], 0.0).astype(BF16), _bd(dm_l[i], lo).astype(BF16))
                    for i in idx]
            dm_l = [dm_l[i] + _dot(dm_l[i].astype(BF16), _bd(t1_l[i], lo).astype(BF16)) for i in idx]
        s_blk *= 2
    u_l = [_dot(dm_l[i].astype(BF16), _bd(y0_l[i], lo).astype(BF16)) for i in idx]

    gate_b = _sigmoid(gb_ref[...].astype(F32).reshape(LT, D))
    y_a = ya_ref[...].astype(F32).reshape(LT, D)
    bonus_l = [_pair_sum(xr[rows[s], sls[p]] * kmod[rows[s], sls[p]] * rk_ref[:, sls[p]], lo) * vp_l[i]
               for i, (s, p) in enumerate(items)]

    o_l = [rs_l[i] + _dot(ark_l[i], jnp.concatenate([_bd(u_l[i], lo).astype(BF16), bdv_l[i]], axis=0))
           for i in idx]

    w3_l = [win_l[i].reshape(nbg, lb, 128)[:, lb - 1:lb, :] for i in idx]
    rhs_l = []
    for i in idx:
        w_last = jnp.broadcast_to(w3_l[i], (nbg, lb, 128)).reshape(L, 128)
        rhs_l.append(jnp.concatenate([bt_l[i] * w_last, kt_l[i] * w_last], axis=0).astype(BF16))
    uv_l = [jnp.concatenate([u_l[i], vp_l[i]], axis=0) for i in idx]
    if nbg == 1:
        upd_l = [_dot_tn(uv_l[i].astype(BF16), rhs_l[i]) for i in idx]
        for i, (s, p) in enumerate(items):
            sbd[s, p] = sbd[s, p] * w3_l[i][0] + jnp.where(blockdiag, upd_l[i], 0.0)
    else:
        cgrp = (c128 % L) // lb
        uvt_l = [uv_l[i].T for i in idx]
        for i, (s, p) in enumerate(items):
            for gi in groups:
                upd = _dot(jnp.where(cgrp == gi, uvt_l[i], 0.0).astype(BF16), rhs_l[i])
                q_ = s * nbg + gi
                sbd[q_, p] = sbd[q_, p] * w3_l[i][gi] + jnp.where(blockdiag, upd, 0.0)

    out_l = []
    for i, (s, p) in enumerate(items):
        rs, sl = rows[s], sls[p]
        o = o_l[i]
        mu = _pair_sum(o, lo) * (1.0 / HB)
        oc = o - mu
        var = _pair_sum(oc * oc, lo) * (1.0 / HB)
        on = oc * lax.rsqrt(var + RWKV_EPS) * lw_ref[:, sl] + lb_ref[:, sl]
        yb = (on + bonus_l[i]) * g[rs, sl]
        out_l.append(y_a[rs, sl] + gate_b[rs, sl] * yb)
    u_rows = [jnp.concatenate(out_l[s * N_PAIR:(s + 1) * N_PAIR], axis=1) for s in subs]
    u_all = u_rows[0] if nsub == 1 else jnp.concatenate(u_rows, axis=0)
    u_ref[...] = u_all.reshape(nseq, lb, D).astype(u_ref.dtype)

    @pl.when(pl.program_id(1) == pl.num_programs(1) - 1)
    def _():
        for gi in range(nseq):
            for p in range(N_PAIR):
                s_ref[gi, 2 * p] = sbd[gi, p, 0:HB, 0:HB]
                s_ref[gi, 2 * p + 1] = sbd[gi, p, HB:2 * HB, HB:2 * HB]


def _rwkv(main3, cols, tail3, ya3, prev, s0, prm, nsub, nbg, lb, tv):
    b, tp, _ = main3.shape
    has_state = s0 is not None
    nq = nsub * nbg
    blk = lambda j: pl.BlockSpec((nq, lb, D), lambda i, c, j=j: (i, c, j))
    pblk = lambda j: pl.BlockSpec((nq, 1, D), lambda i, c, j=j: (i, 0, j))
    full = lambda a: pl.BlockSpec(a.shape, lambda i, c: (0,) * a.ndim)
    sblk = pl.BlockSpec((nq, H_B, HB, HB), lambda i, c: (i, 0, 0, 0))
    c_r, c_k, c_v, c_gb = cols
    in_specs = [blk(c_r), blk(c_k), blk(c_v), blk(c_gb),
                pl.BlockSpec((nq, lb, LORA), lambda i, c: (i, c, 0)),
                pl.BlockSpec((nq, lb, D), lambda i, c: (i, c, 0))]
    args = [main3, main3, main3, main3, tail3, ya3]
    if has_state:
        in_specs += [pblk(c_r), pblk(c_k), pblk(c_v), pl.BlockSpec((nq, 1, LORA), lambda i, c: (i, 0, 0)),
                     pl.BlockSpec((N_PAIR, nq * PAIR_ROWS, 128), lambda i, c: (0, i, 0))]
        args += [prev[0], prev[0], prev[0], prev[1], s0]
    in_specs += [full(a) for a in prm]
    args += list(prm)
    return pl.pallas_call(
        functools.partial(_rwkv_kernel, nsub, nbg, lb, tv, has_state),
        grid=(b // nq, tp // lb),
        in_specs=in_specs,
        out_specs=[pl.BlockSpec((nq, lb, D), lambda i, c: (i, c, 0)), sblk],
        out_shape=[jax.ShapeDtypeStruct((b, tp, D), main3.dtype),
                   jax.ShapeDtypeStruct((b, H_B, HB, HB), F32)],
        scratch_shapes=[pltpu.VMEM((nq, N_PAIR, 128, 128), F32),
                        pltpu.VMEM((nq, 1, D), F32), pltpu.VMEM((nq, 1, D), F32),
                        pltpu.VMEM((nq, 1, D), F32), pltpu.VMEM((nq, 1, LORA), F32)],
        compiler_params=_cp(("parallel", "arbitrary")),
        name="rwkv",
    )(*args)


def _tail_kernel(u_ref, x_ref, g1_ref, sh_ref, sc_ref, g2_ref, wo_ref, wu_ref, wd_ref,
                 l1g_ref, l1b_ref, l2g_ref, l2b_ref, o_ref, x1_scr, h_scr, acc):
    bb, tt, _ = x_ref.shape
    j = pl.program_id(2)

    @pl.when(j == 0)
    def _():
        u = u_ref[...].reshape(bb * tt, D).astype(BF16)
        y = _dot(u, wo_ref[...]).reshape(bb, tt, D)
        x1 = _layer_norm(ALPHA * x_ref[...] + g1_ref[...] * y, l1g_ref[...], l1b_ref[...])
        x1_scr[...] = x1
        h_scr[...] = (x1 * (1.0 + sc_ref[...]) + sh_ref[...]).reshape(bb * tt, D).astype(BF16)
        acc[...] = jnp.zeros_like(acc)

    up = jnp.maximum(_dot(h_scr[...], wu_ref[...]), 0.0)
    acc[...] += _dot((up * up).astype(BF16), wd_ref[...])

    @pl.when(j == pl.num_programs(2) - 1)
    def _():
        z = ALPHA * x1_scr[...] + g2_ref[...] * acc[...].reshape(bb, tt, D)
        o_ref[...] = _layer_norm(z, l2g_ref[...], l2b_ref[...])


def _tail(u3, x3, mod3, q, bb, tt):
    b, tp, _ = x3.shape
    blk = pl.BlockSpec((bb, tt, D), lambda i, t, j: (i, t, 0))
    mblk = lambda col: pl.BlockSpec((bb, 1, D), lambda i, t, j, col=col: (i, 0, col))
    full = lambda shp: pl.BlockSpec(shp, lambda i, t, j: (0,) * len(shp))
    return pl.pallas_call(
        _tail_kernel,
        grid=(b // bb, tp // tt, D_FF // FF_CHUNK),
        in_specs=[blk, blk, mblk(2), mblk(3), mblk(4), mblk(5),
                  full((D, D)),
                  pl.BlockSpec((D, FF_CHUNK), lambda i, t, j: (0, j)),
                  pl.BlockSpec((FF_CHUNK, D), lambda i, t, j: (j, 0)),
                  full((1, D)), full((1, D)), full((1, D)), full((1, D))],
        out_specs=blk,
        out_shape=jax.ShapeDtypeStruct((b, tp, D), F32),
        scratch_shapes=[pltpu.VMEM((bb, tt, D), F32), pltpu.VMEM((bb * tt, D), BF16),
                        pltpu.VMEM((bb * tt, D), F32)],
        compiler_params=_cp(("parallel", "parallel", "arbitrary")),
        name="outproj_ffn",
    )(u3, x3, mod3, mod3, mod3, mod3, q['w_out'], q['w_up'], q['w_down'],
      q['ln1_g'], q['ln1_b'], q['ln2_g'], q['ln2_b'])


def _relayout_params(p):
    wt = p['w_in'].T
    w_main = jnp.concatenate(
        [wt[:3 * D], wt[3 * D + 8:6 * D + 8], wt[6 * D + 8 + LORA:8 * D + 8 + LORA]], axis=0).astype(BF16)
    w_tail = jnp.concatenate(
        [wt[6 * D + 8:6 * D + 8 + LORA], wt[3 * D:3 * D + 8],
         jnp.zeros((N_TAIL - LORA - 8, D), F32)], axis=0).astype(BF16)
    mu = p['rwkv_mu']
    z64 = jnp.zeros((64, D), F32)
    z128 = jnp.zeros((128, D), F32)
    row = lambda a: a.reshape(1, -1)
    rw = (row(mu[0:D]), row(mu[D:2 * D]), row(mu[2 * D:3 * D]), row(mu[3 * D:3 * D + LORA]),
          row(p['rwkv_w0']), row(p['rwkv_a0']), row(p['rwkv_k_k']), row(p['rwkv_k_a']),
          row(p['rwkv_r_k']), row(p['rwkv_lnx_w']), row(p['rwkv_lnx_b']),
          jnp.concatenate([p['rwkv_w2'], z64, z128], axis=0).astype(BF16),
          jnp.concatenate([z64, p['rwkv_a2'], z128], axis=0).astype(BF16),
          jnp.concatenate([z128, p['rwkv_g2']], axis=0).astype(BF16))
    gbias = jnp.concatenate([p['mlstm_i_bias'], p['mlstm_f_bias'], jnp.zeros((120,), F32)]).reshape(1, 128)
    return dict(w_main=w_main, w_tail=w_tail, rw=rw, gbias=gbias,
                conv_w=p['conv_w'], conv_b=row(p['conv_b']), norm_w=row(p['mlstm_norm_w']),
                w_out=p['w_out'].astype(BF16), w_up=p['w_up'].astype(BF16), w_down=p['w_down'].astype(BF16),
                ln1_g=row(p['ln1_g']), ln1_b=row(p['ln1_b']), ln2_g=row(p['ln2_g']), ln2_b=row(p['ln2_b']))


def _prompt_layer(x, mod, q, seq_tile, mlstm_chunk):
    b, t, _ = x.shape
    mod3 = mod.reshape(b, 1, N_COND)
    main3, tail3 = _inproj(x, mod3, q['w_main'], q['w_tail'], 1, seq_tile, BF16)
    ya3, c1, n1, m1 = _mlstm_seq(main3, tail3, q['conv_w'], q['conv_b'], q['gbias'], q['norm_w'],
                                 mlstm_chunk, min(2, b))
    u3, s1 = _rwkv(main3, RWKV_SECTIONS, tail3, ya3, None, None, q['rw'], min(2, b), 1, RW_L, RW_L)
    y = _tail(u3, x, mod3, q, 1, seq_tile)
    shift = _modulate_rows(x[:, t - 1, :], mod)
    conv = main3[:, t - (CONV_W - 1):, :2 * D].astype(F32)
    return y, (c1, n1[:, :H_A, :], m1[:, 0, :H_A], conv, s1, shift)


def _sample_layer(x, mod, st, q, bb):
    c0, n0, m0, conv0, s0, shift0 = st
    b, t, _ = x.shape
    mod3 = mod.reshape(b, 1, N_COND)
    xp = jnp.pad(x, ((0, 0), (0, 8 - t), (0, 0)))
    main3, tail3 = _inproj(xp, mod3, q['w_main'], q['w_tail'], bb, 8, F32)
    pm, pt = _inproj(shift0.reshape(1, b, D), jnp.zeros((1, 1, N_COND), F32), q['w_main'], q['w_tail'], 1, b, F32)
    prev = (pm.reshape(b, 1, N_MAIN), pt.reshape(b, 1, N_TAIL))
    conv0p = jnp.pad(conv0, ((0, 0), (8 - (CONV_W - 1), 0), (0, 0)))
    n0p = jnp.pad(n0, ((0, 0), (0, 8 - H_A), (0, 0)))
    m0p = jnp.pad(m0, ((0, 0), (0, 128 - H_A))).reshape(b, 1, 128)
    ya3, c1, n1, m1 = _mlstm_step(main3, tail3, conv0p, c0, n0p, m0p, q['conv_w'], q['conv_b'], q['gbias'],
                                  q['norm_w'], t, min(4, b))
    s0_pairs = _state_pairs(jnp.transpose(s0, (1, 2, 3, 0)))
    u3, s1 = _rwkv(main3, RWKV_SECTIONS, tail3, ya3, prev, s0_pairs, q['rw'], 1, RW_L // 8, 8, t)
    y = _tail(u3, xp, mod3, q, bb, 8)
    shift = _modulate_rows(x[:, t - 1, :], mod)
    conv = jnp.concatenate([conv0, main3[:, :t, :2 * D]], axis=1)[:, t:, :]
    return y[:, :t, :], (c1, n1[:, :H_A, :], m1[:, 0, :H_A], conv, s1, shift)


def kernel(x_prompt, x_sample, c_prompt, c_sample, state_mlstm_C, state_mlstm_n, state_mlstm_m, state_mlstm_conv, state_rwkv_S, state_rwkv_shift, w_cond, b_cond, w_in, mlstm_i_bias, mlstm_f_bias, conv_w, conv_b, mlstm_norm_w, rwkv_mu, rwkv_w0, rwkv_w2, rwkv_a0, rwkv_a2, rwkv_g2, rwkv_k_k, rwkv_k_a, rwkv_r_k, rwkv_lnx_w, rwkv_lnx_b, w_out, ln1_g, ln1_b, w_up, w_down, ln2_g, ln2_b):
    depth = w_in.shape[0]
    bp = x_prompt.shape[0]
    yp, ys = x_prompt, x_sample
    new_p = [[] for _ in range(6)]
    new_s = [[] for _ in range(6)]
    for l in range(depth):
        p = {'w_in': w_in[l], 'mlstm_i_bias': mlstm_i_bias[l], 'mlstm_f_bias': mlstm_f_bias[l],
             'conv_w': conv_w[l], 'conv_b': conv_b[l], 'mlstm_norm_w': mlstm_norm_w[l],
             'rwkv_mu': rwkv_mu[l], 'rwkv_w0': rwkv_w0[l], 'rwkv_w2': rwkv_w2[l], 'rwkv_a0': rwkv_a0[l],
             'rwkv_a2': rwkv_a2[l], 'rwkv_g2': rwkv_g2[l], 'rwkv_k_k': rwkv_k_k[l], 'rwkv_k_a': rwkv_k_a[l],
             'rwkv_r_k': rwkv_r_k[l].reshape(-1), 'rwkv_lnx_w': rwkv_lnx_w[l], 'rwkv_lnx_b': rwkv_lnx_b[l],
             'w_out': w_out[l], 'ln1_g': ln1_g[l], 'ln1_b': ln1_b[l], 'w_up': w_up[l], 'w_down': w_down[l],
             'ln2_g': ln2_g[l], 'ln2_b': ln2_b[l]}
        q = _relayout_params(p)
        mod = _cond(jnp.concatenate([c_prompt, c_sample], axis=0), w_cond[l], b_cond[l])
        st_in = (state_mlstm_C[l], state_mlstm_n[l], state_mlstm_m[l], state_mlstm_conv[l],
                 state_rwkv_S[l], state_rwkv_shift[l])
        ys, st_s = _sample_layer(ys, mod[bp:], st_in, q, min(128, ys.shape[0]))
        yp, st_p = _prompt_layer(yp, mod[:bp], q, min(1024, yp.shape[1]), min(256, yp.shape[1]))
        for lst, t in zip(new_p, st_p):
            lst.append(t)
        for lst, t in zip(new_s, st_s):
            lst.append(t)
    outs_p = [jnp.stack(t) for t in new_p]
    outs_s = [jnp.stack(t) for t in new_s]
    return (yp, ys, *outs_p, *outs_s)
```

```python
import functools

import jax
import jax.numpy as jnp
from jax import lax
from jax.experimental import pallas as pl
from jax.experimental.pallas import tpu as pltpu

F32 = jnp.float32
BF16 = jnp.bfloat16
HIGHEST = lax.Precision.HIGHEST

D = 1024
H_A = 4
DK = 256
CONV_W = 4
H_B = 16
HB = 64
N_PAIR = H_B // 2
D_FF = 4096
N_COND = 6 * D
ALPHA = 2.0 ** 0.25
LN_EPS = 1e-5
MLSTM_EPS = 1e-6
RWKV_EPS = 64e-5
DECAY_SCALE = 0.6065306597126334

N_MAIN = 8 * D
RWKV_SECTIONS = (3, 4, 5, 7)
LORA = 256
TAIL_IF = LORA
N_TAIL = 512
TN_MAIN = 2048
N_MAIN_TILES = N_MAIN // TN_MAIN

RW_L = 64
FF_CHUNK = 1024
NEG = -1e30
VMEM_LIMIT = 56 * 1024 * 1024


def _cp(sem):
    return pltpu.CompilerParams(dimension_semantics=sem, vmem_limit_bytes=VMEM_LIMIT)


def _dot(a, b, prec=None):
    return jnp.dot(a, b, preferred_element_type=F32, precision=prec)


def _dot_nt(a, b, prec=None):
    return lax.dot_general(a, b, (((1,), (1,)), ((), ())), preferred_element_type=F32, precision=prec)


def _dot_tn(a, b, prec=None):
    return lax.dot_general(a, b, (((0,), (0,)), ((), ())), preferred_element_type=F32, precision=prec)


def _cumsum_rows(tril01, x):
    hi = x.astype(BF16)
    r1 = x - hi.astype(F32)
    mid = r1.astype(BF16)
    lo = (r1 - mid.astype(F32)).astype(BF16)
    t = tril01.astype(BF16)
    return _dot(t, hi) + _dot(t, mid) + _dot(t, lo)


def _log_sigmoid(x):
    return jnp.minimum(x, 0.0) - jnp.log1p(jnp.exp(-jnp.abs(x)))


def _sigmoid(x):
    return 0.5 * jnp.tanh(0.5 * x) + 0.5


def _silu(x):
    h = 0.5 * x
    return h + h * jnp.tanh(h)


def _layer_norm(z, g, b):
    mu = jnp.mean(z, axis=-1, keepdims=True)
    zc = z - mu
    var = jnp.mean(zc * zc, axis=-1, keepdims=True)
    return zc * lax.rsqrt(var + LN_EPS) * g + b


def _cond_kernel(c_ref, w_ref, b_ref, o_ref):
    s = _silu(c_ref[...]).astype(BF16)
    o_ref[...] = _dot(s, w_ref[...].astype(BF16)) + b_ref[...]


def _cond(c, w_cond, b_cond):
    n = c.shape[0]
    tn = 1536
    return pl.pallas_call(
        _cond_kernel,
        grid=(N_COND // tn,),
        in_specs=[pl.BlockSpec((n, D), lambda j: (0, 0)),
                  pl.BlockSpec((D, tn), lambda j: (0, j)),
                  pl.BlockSpec((1, tn), lambda j: (0, j))],
        out_specs=pl.BlockSpec((n, tn), lambda j: (0, j)),
        out_shape=jax.ShapeDtypeStruct((n, N_COND), F32),
        compiler_params=_cp(("arbitrary",)),
        name="cond",
    )(c, w_cond, b_cond.reshape(1, N_COND))


def _inproj_kernel(x_ref, sh_ref, sc_ref, wm_ref, wt_ref, main_ref, tail_ref, h_scr):
    bb, tt, _ = x_ref.shape
    j = pl.program_id(2)

    @pl.when(j == 0)
    def _():
        h = x_ref[...] * (1.0 + sc_ref[...]) + sh_ref[...]
        h_scr[...] = h.reshape(bb * tt, D).astype(BF16)

    main_ref[...] = _dot_nt(h_scr[...], wm_ref[...]).reshape(bb, tt, TN_MAIN).astype(main_ref.dtype)

    @pl.when(j == N_MAIN_TILES - 1)
    def _():
        tail_ref[...] = _dot_nt(h_scr[...], wt_ref[...]).reshape(bb, tt, N_TAIL)


def _inproj(x3, mod3, w_main, w_tail, bb, tt, main_dtype):
    b, tp, _ = x3.shape
    return pl.pallas_call(
        _inproj_kernel,
        grid=(b // bb, tp // tt, N_MAIN_TILES),
        in_specs=[pl.BlockSpec((bb, tt, D), lambda i, t, j: (i, t, 0)),
                  pl.BlockSpec((bb, 1, D), lambda i, t, j: (i, 0, 0)),
                  pl.BlockSpec((bb, 1, D), lambda i, t, j: (i, 0, 1)),
                  pl.BlockSpec((TN_MAIN, D), lambda i, t, j: (j, 0)),
                  pl.BlockSpec((N_TAIL, D), lambda i, t, j: (0, 0))],
        out_specs=[pl.BlockSpec((bb, tt, TN_MAIN), lambda i, t, j: (i, t, j)),
                   pl.BlockSpec((bb, tt, N_TAIL), lambda i, t, j: (i, t, 0))],
        out_shape=[jax.ShapeDtypeStruct((b, tp, N_MAIN), main_dtype),
                   jax.ShapeDtypeStruct((b, tp, N_TAIL), F32)],
        scratch_shapes=[pltpu.VMEM((bb * tt, D), BF16)],
        compiler_params=_cp(("parallel", "parallel", "arbitrary")),
        name="inproj",
    )(x3, mod3, mod3, w_main, w_tail)


def _modulate_kernel(x_ref, sh_ref, sc_ref, o_ref):
    o_ref[...] = x_ref[...] * (1.0 + sc_ref[...]) + sh_ref[...]


def _modulate_rows(x2, mod2):
    n = x2.shape[0]
    return pl.pallas_call(
        _modulate_kernel,
        grid=(1,),
        in_specs=[pl.BlockSpec((n, D), lambda i: (0, 0)),
                  pl.BlockSpec((n, D), lambda i: (0, 0)),
                  pl.BlockSpec((n, D), lambda i: (0, 1))],
        out_specs=pl.BlockSpec((n, D), lambda i: (0, 0)),
        out_shape=jax.ShapeDtypeStruct((n, D), F32),
        name="modulate_last",
    )(x2, mod2, mod2)


def _conv4(pad_ref, n_rows, cw, cb):
    acc = cb + pad_ref[8:8 + n_rows, :] * cw[3:4, :]
    acc = acc + pad_ref[7:7 + n_rows, :] * cw[2:3, :]
    acc = acc + pad_ref[6:6 + n_rows, :] * cw[1:2, :]
    acc = acc + pad_ref[5:5 + n_rows, :] * cw[0:1, :]
    return acc


def _head_norm_rows(h, eps):
    mu = jnp.mean(h, axis=-1, keepdims=True)
    hc = h - mu
    var = jnp.mean(hc * hc, axis=-1, keepdims=True)
    return hc * lax.rsqrt(var + eps)


def _mlstm_seq_kernel(nb, qp_ref, kp_ref, v_ref, ga_ref, if_ref, cw_ref, cb_ref, gb_ref, nw_ref,
                      ya_ref, c_ref, n_ref, m_ref, haloq, halok):
    L = qp_ref.shape[1]
    assert qp_ref.dtype == BF16 and kp_ref.dtype == BF16

    @pl.when(pl.program_id(1) == 0)
    def _():
        c_ref[...] = jnp.zeros_like(c_ref)
        n_ref[...] = jnp.zeros_like(n_ref)
        m_ref[...] = jnp.zeros_like(m_ref)
        haloq[...] = jnp.zeros_like(haloq)
        halok[...] = jnp.zeros_like(halok)

    cw = cw_ref[...]
    cb = cb_ref[...]
    nw = nw_ref[...]
    row = lax.broadcasted_iota(jnp.int32, (L, L), 0)
    col = lax.broadcasted_iota(jnp.int32, (L, L), 1)
    causal = col <= row
    tril = jnp.where(causal, 1.0, 0.0).astype(F32)
    srow = lax.broadcasted_iota(jnp.int32, (3 * L, L), 0)
    scol = lax.broadcasted_iota(jnp.int32, (3 * L, L), 1)
    shift_mat = jnp.where(scol + srow // L + 1 == srow % L, 1.0, 0.0).astype(BF16)
    r8 = lax.broadcasted_iota(jnp.int32, (8, 1), 0)

    def conv(x_bf, halo_ref, cw_, cb_):
        sh = _dot(shift_mat, x_bf)
        x = x_bf.astype(F32)
        acc = cb_ + x * cw_[3:4, :] + sh[0:L] * cw_[2:3, :] + sh[L:2 * L] * cw_[1:2, :] \
            + sh[2 * L:3 * L] * cw_[0:1, :]
        halo = halo_ref[...]
        fix = jnp.zeros((8, DK), F32)
        for j in range(1, CONV_W):
            fix = fix + jnp.where(r8 < j, pltpu.roll(halo, j, 0), 0.0) * cw_[CONV_W - 1 - j:CONV_W - j, :]
        halo_ref[...] = x[L - 8:L, :]
        return jnp.concatenate([acc[0:8] + fix, acc[8:L]], axis=0)

    items = [(bi, h) for bi in range(nb) for h in range(H_A)]
    sl_of = lambda h: slice(h * DK, (h + 1) * DK)
    q_it, k_it = {}, {}
    for (bi, h) in items:
        sl, ksl = sl_of(h), slice(D + h * DK, D + (h + 1) * DK)
        q_it[bi, h] = _silu(conv(qp_ref[bi, :, sl], haloq.at[bi, :, sl], cw[:, sl], cb[:, sl]))
        k_it[bi, h] = _silu(conv(kp_ref[bi, :, sl], halok.at[bi, :, sl], cw[:, ksl], cb[:, ksl])) * (DK ** -0.5)

    gpre_l, bcum_l, gpre_t_l, bcum_t_l = [], [], [], []
    for bi in range(nb):
        gpre = if_ref[bi] + gb_ref[...]
        bcum = _dot(tril, _log_sigmoid(gpre), HIGHEST)
        gpre_l.append(gpre)
        bcum_l.append(bcum)
        gpre_t_l.append(gpre.T)
        bcum_t_l.append(bcum.T)

    st = {}
    for (bi, h) in items:
        ig_col = gpre_l[bi][:, h:h + 1]
        b_col = bcum_l[bi][:, H_A + h:H_A + h + 1]
        ig_row = gpre_t_l[bi][h:h + 1, :]
        b_row = bcum_t_l[bi][H_A + h:H_A + h + 1, :]
        m_prev = m_ref[bi][:, h:h + 1]
        g_col = b_col + m_prev
        dlog = jnp.where(causal, b_col - b_row + ig_row, NEG)
        m_t = jnp.maximum(g_col, jnp.max(dlog, axis=1, keepdims=True))
        b_last = b_col[L - 1:L, :]
        wlog = b_last - b_col + ig_col
        m_new = jnp.maximum(b_last + m_prev, jnp.max(wlog, axis=0, keepdims=True))
        st[bi, h] = dict(m_t=m_t, w_inter=jnp.exp(g_col - m_t), p=jnp.exp(dlog - m_t), m_new=m_new,
                         decay=jnp.exp(b_last + m_prev - m_new), wts=jnp.exp(wlog - m_new))
    qb = {it: q_it[it].astype(BF16) for it in items}
    kb = {it: k_it[it].astype(BF16) for it in items}
    s_l = {it: _dot_nt(qb[it], kb[it]) * st[it]['p'] for it in items}
    qc_l = {(bi, h): _dot_nt(qb[bi, h], c_ref[bi, h].astype(BF16)) for (bi, h) in items}
    sv_l = {(bi, h): _dot(s_l[bi, h].astype(BF16), v_ref[bi, :, sl_of(h)].astype(BF16)) for (bi, h) in items}
    upd_l = {(bi, h): _dot_tn((st[bi, h]['wts'] * v_ref[bi, :, sl_of(h)].astype(F32)).astype(BF16), kb[bi, h])
             for (bi, h) in items}
    for (bi, h) in items:
        sl = sl_of(h)
        d = st[bi, h]
        qh = q_it[bi, h]
        kh = k_it[bi, h]
        nh = n_ref[bi, h:h + 1, :]
        num = d['w_inter'] * qc_l[bi, h] + sv_l[bi, h]
        den = d['w_inter'] * jnp.sum(qh * nh, axis=1, keepdims=True) + jnp.sum(s_l[bi, h], axis=1, keepdims=True)
        hh = num / jnp.maximum(jnp.abs(den), jnp.exp(-d['m_t']))
        ga = _sigmoid(ga_ref[bi, :, sl].astype(F32))
        ya_ref[bi, :, sl] = (ga * _head_norm_rows(hh, MLSTM_EPS) * nw[:, sl]).astype(ya_ref.dtype)
        c_ref[bi, h] = d['decay'] * c_ref[bi, h] + upd_l[bi, h]
        n_ref[bi, h:h + 1, :] = d['decay'] * nh + jnp.sum(d['wts'] * kh, axis=0, keepdims=True)
        m_ref[bi, :, h:h + 1] = d['m_new']


def _mlstm_seq(main3, tail3, conv_w, conv_b, gbias, norm_w, L, nb):
    b, tp, _ = main3.shape
    blk = lambda j: pl.BlockSpec((nb, L, D), lambda i, c, j=j: (i, c, j))
    full = lambda shp: pl.BlockSpec(shp, lambda i, c: (0,) * len(shp))
    return pl.pallas_call(
        functools.partial(_mlstm_seq_kernel, nb),
        grid=(b // nb, tp // L),
        in_specs=[blk(0), blk(1), blk(2), blk(6),
                  pl.BlockSpec((nb, L, 128), lambda i, c: (i, c, TAIL_IF // 128)),
                  full((CONV_W, 2 * D)), full((1, 2 * D)), full((1, 128)), full((1, D))],
        out_specs=[pl.BlockSpec((nb, L, D), lambda i, c: (i, c, 0)),
                   pl.BlockSpec((nb, H_A, DK, DK), lambda i, c: (i, 0, 0, 0)),
                   pl.BlockSpec((nb, 8, DK), lambda i, c: (i, 0, 0)),
                   pl.BlockSpec((nb, 1, 128), lambda i, c: (i, 0, 0))],
        out_shape=[jax.ShapeDtypeStruct((b, tp, D), main3.dtype),
                   jax.ShapeDtypeStruct((b, H_A, DK, DK), F32),
                   jax.ShapeDtypeStruct((b, 8, DK), F32),
                   jax.ShapeDtypeStruct((b, 1, 128), F32)],
        scratch_shapes=[pltpu.VMEM((nb, 8, D), F32), pltpu.VMEM((nb, 8, D), F32)],
        compiler_params=_cp(("parallel", "arbitrary")),
        name="mlstm_seq",
    )(main3, main3, main3, main3, tail3, conv_w, conv_b, gbias, norm_w)


def _mlstm_step_kernel(tv, nb, qp_ref, kp_ref, v_ref, ga_ref, if_ref, conv0_ref, c0_ref, n0_ref, m0_ref,
                       cw_ref, cb_ref, gb_ref, nw_ref,
                       ya_ref, c_ref, n_ref, m_ref, padq, padk, gpad, lpad, kpad, vpad, wvpad):
    @pl.when(pl.program_id(0) == 0)
    def _():
        for r in (gpad, lpad, kpad, vpad, wvpad):
            r[...] = jnp.zeros_like(r)

    cw = cw_ref[...]
    cb = cb_ref[...]
    nw = nw_ref[...]
    gb = gb_ref[...]
    trow = lax.broadcasted_iota(jnp.int32, (8, 128), 0)
    scol = lax.broadcasted_iota(jnp.int32, (8, 128), 1)
    mask = (scol <= trow) & (scol < tv)
    rvalid = lax.broadcasted_iota(jnp.int32, (8, 1), 0) < tv
    r128 = lax.broadcasted_iota(jnp.int32, (128, 128), 0)
    c128 = lax.broadcasted_iota(jnp.int32, (128, 128), 1)
    tril = jnp.where(c128 <= r128, 1.0, 0.0).astype(F32)
    n_ref[...] = jnp.zeros_like(n_ref)
    m_ref[...] = jnp.zeros_like(m_ref)

    batches = range(nb)
    q_l, gpre_l, bcol_l, gt_l, bt_l, ga_l = [], [], [], [], [], []
    for bi in batches:
        padq[bi, 0:8, :] = conv0_ref[bi, :, 0:D]
        padk[bi, 0:8, :] = conv0_ref[bi, :, D:2 * D]
        padq[bi, 8:16, :] = qp_ref[bi]
        padk[bi, 8:16, :] = kp_ref[bi]
        q_l.append(_silu(_conv4(padq.at[bi], 8, cw[:, 0:D], cb[:, 0:D])))
        kpad[bi, 0:8, :] = _silu(_conv4(padk.at[bi], 8, cw[:, D:2 * D], cb[:, D:2 * D])) * (DK ** -0.5)
        vpad[bi, 0:8, :] = v_ref[bi]
        gpre = if_ref[bi] + gb
        gpad[bi, 0:8, :] = gpre
        lpad[bi, 0:8, :] = _log_sigmoid(gpre)
        gpre_l.append(gpre)
        ga_l.append(_sigmoid(ga_ref[bi]))
    for bi in batches:
        bpad = _dot(tril, lpad[bi], HIGHEST)
        bcol_l.append(bpad[0:8, :])
        bt_l.append(bpad.T)
        gt_l.append(gpad[bi].T)

    probs = [(bi, h) for bi in batches for h in range(H_A)]
    sl_of = lambda h: slice(h * DK, (h + 1) * DK)
    st = {}
    for (bi, h) in probs:
        ig_col = gpre_l[bi][:, h:h + 1]
        b_col = bcol_l[bi][:, H_A + h:H_A + h + 1]
        ig_row = gt_l[bi][h:h + 1, :]
        b_row = bt_l[bi][H_A + h:H_A + h + 1, :]
        m_prev = m0_ref[bi][:, h:h + 1]
        g_col = b_col + m_prev
        dlog = jnp.where(mask, b_col - b_row + ig_row, NEG)
        m_t = jnp.maximum(g_col, jnp.max(dlog, axis=1, keepdims=True))
        b_last = b_col[tv - 1:tv, :]
        wlog = jnp.where(rvalid, b_last - b_col + ig_col, NEG)
        m_new = jnp.maximum(b_last + m_prev, jnp.max(wlog, axis=0, keepdims=True))
        wts = jnp.exp(wlog - m_new)
        wvpad[bi, 0:8, sl_of(h)] = wts * vpad[bi, 0:8, sl_of(h)]
        st[bi, h] = dict(m_t=m_t, w_inter=jnp.exp(g_col - m_t), pm=jnp.exp(dlog - m_t), m_new=m_new,
                         decay=jnp.exp(b_last + m_prev - m_new), wts=wts)
    kb = {(bi, h): kpad[bi, :, sl_of(h)].astype(BF16) for (bi, h) in probs}
    qb = {(bi, h): q_l[bi][:, sl_of(h)].astype(BF16) for (bi, h) in probs}
    s_l = {k_: _dot_nt(qb[k_], kb[k_]) * st[k_]['pm'] for k_ in probs}
    qc_l = {(bi, h): _dot_nt(qb[bi, h], c0_ref[bi, h].astype(BF16)) for (bi, h) in probs}
    sv_l = {(bi, h): _dot(s_l[bi, h].astype(BF16), vpad[bi, :, sl_of(h)].astype(BF16)) for (bi, h) in probs}
    upd_l = {(bi, h): _dot(wvpad[bi, :, sl_of(h)].T.astype(BF16), kb[bi, h]) for (bi, h) in probs}
    for (bi, h) in probs:
        sl = sl_of(h)
        d = st[bi, h]
        nh = n0_ref[bi, h:h + 1, :]
        qh = q_l[bi][:, sl]
        num = d['w_inter'] * qc_l[bi, h] + sv_l[bi, h]
        den = d['w_inter'] * jnp.sum(qh * nh, axis=1, keepdims=True) + jnp.sum(s_l[bi, h], axis=1, keepdims=True)
        hh = num / jnp.maximum(jnp.abs(den), jnp.exp(-d['m_t']))
        ya_ref[bi, :, sl] = ga_l[bi][:, sl] * _head_norm_rows(hh, MLSTM_EPS) * nw[:, sl]
        c_ref[bi, h] = d['decay'] * c0_ref[bi, h] + upd_l[bi, h]
        n_ref[bi, h:h + 1, :] = d['decay'] * nh + jnp.sum(d['wts'] * kpad[bi, 0:8, sl], axis=0, keepdims=True)
        m_ref[bi, :, h:h + 1] = d['m_new']


def _mlstm_step(main3, tail3, conv0p, c0, n0p, m0p, conv_w, conv_b, gbias, norm_w, tv, nb):
    b = main3.shape[0]
    blk = lambda j: pl.BlockSpec((nb, 8, D), lambda i, j=j: (i, 0, j))
    full = lambda shp: pl.BlockSpec(shp, lambda i: (0,) * len(shp))
    state_specs = [pl.BlockSpec((nb, H_A, DK, DK), lambda i: (i, 0, 0, 0)),
                   pl.BlockSpec((nb, 8, DK), lambda i: (i, 0, 0)),
                   pl.BlockSpec((nb, 1, 128), lambda i: (i, 0, 0))]
    return pl.pallas_call(
        functools.partial(_mlstm_step_kernel, tv, nb),
        grid=(b // nb,),
        in_specs=[blk(0), blk(1), blk(2), blk(6),
                  pl.BlockSpec((nb, 8, 128), lambda i: (i, 0, TAIL_IF // 128)),
                  pl.BlockSpec((nb, 8, 2 * D), lambda i: (i, 0, 0))] + state_specs +
                 [full((CONV_W, 2 * D)), full((1, 2 * D)), full((1, 128)), full((1, D))],
        out_specs=[pl.BlockSpec((nb, 8, D), lambda i: (i, 0, 0))] + state_specs,
        out_shape=[jax.ShapeDtypeStruct((b, 8, D), F32),
                   jax.ShapeDtypeStruct((b, H_A, DK, DK), F32),
                   jax.ShapeDtypeStruct((b, 8, DK), F32),
                   jax.ShapeDtypeStruct((b, 1, 128), F32)],
        scratch_shapes=[pltpu.VMEM((nb, 16, D), F32), pltpu.VMEM((nb, 16, D), F32),
                        pltpu.VMEM((nb, 128, 128), F32), pltpu.VMEM((nb, 128, 128), F32),
                        pltpu.VMEM((nb, 128, D), F32), pltpu.VMEM((nb, 128, D), F32),
                        pltpu.VMEM((nb, 128, D), F32)],
        compiler_params=_cp(("arbitrary",)),
        name="mlstm_step",
    )(main3, main3, main3, main3, tail3, conv0p, c0, n0p, m0p, conv_w, conv_b, gbias, norm_w)


def _bd(x, lo):
    return jnp.concatenate([jnp.where(lo, x, 0.0), jnp.where(lo, 0.0, x)], axis=0)


def _pair_sum(x, lo):
    s_lo = jnp.sum(jnp.where(lo, x, 0.0), axis=1, keepdims=True)
    s_hi = jnp.sum(jnp.where(lo, 0.0, x), axis=1, keepdims=True)
    return jnp.where(lo, s_lo, s_hi)


PAIR_ROWS = 136


def _state_pairs_kernel(s_ref, o_ref):
    nb = s_ref.shape[-1]
    lane = lax.broadcasted_iota(jnp.int32, (nb, 128), 1)
    lo = lane < HB
    for b in range(nb):
        o_ref[0, b * PAIR_ROWS + 128:(b + 1) * PAIR_ROWS, :] = jnp.zeros((PAIR_ROWS - 128, 128), F32)
    for v in range(HB):
        t = jnp.concatenate([s_ref[0, v], s_ref[1, v]], axis=0).T
        o_ref[0, pl.ds(v, nb, stride=PAIR_ROWS), :] = jnp.where(lo, t, 0.0)
        o_ref[0, pl.ds(HB + v, nb, stride=PAIR_ROWS), :] = jnp.where(lo, 0.0, t)


def _state_pairs(s_hvkb):
    nb = s_hvkb.shape[-1]
    return pl.pallas_call(
        _state_pairs_kernel,
        grid=(N_PAIR,),
        in_specs=[pl.BlockSpec((2, HB, HB, nb), lambda p: (p, 0, 0, 0))],
        out_specs=pl.BlockSpec((1, nb * PAIR_ROWS, 128), lambda p: (p, 0, 0)),
        out_shape=jax.ShapeDtypeStruct((N_PAIR, nb * PAIR_ROWS, 128), F32),
        compiler_params=_cp(("arbitrary",)),
        name="state_pairs",
    )(s_hvkb)


def _rwkv_kernel(nsub, nbg, lb, tv, has_state, *refs):
    (r_ref, k_ref, v_ref, gb_ref, l_ref, ya_ref), refs = refs[:6], refs[6:]
    if has_state:
        (pr_ref, pk_ref, pv_ref, pl_ref, s0_ref), refs = refs[:5], refs[5:]
    (mur_ref, muk_ref, muv_ref, mul_ref, w0_ref, a0_ref, kk_ref, ka_ref, rk_ref,
     lw_ref, lb_ref, w2_ref, a2_ref, g2_ref,
     u_ref, s_ref, sbd, cr, ck, cv, cl) = refs
    L = nbg * lb
    nseq = nsub * nbg
    LT = nsub * L

    @pl.when(pl.program_id(1) == 0)
    def _():
        if has_state:
            for gi in range(nseq):
                for p in range(N_PAIR):
                    sbd[gi, p] = s0_ref[p, gi * PAIR_ROWS:gi * PAIR_ROWS + 128, :]
            cr[...] = pr_ref[...].astype(F32)
            ck[...] = pk_ref[...].astype(F32)
            cv[...] = pv_ref[...].astype(F32)
            cl[...] = pl_ref[...]
        else:
            sbd[...] = jnp.zeros_like(sbd)
            for c_ in (cr, ck, cv, cl):
                c_[...] = jnp.zeros_like(c_)

    def shift_mix(x_ref, carry, mu_ref):
        x3 = x_ref[...].astype(F32)
        width = x3.shape[-1]
        rolled = pltpu.roll(x3, 1, 1)
        first_row = lax.broadcasted_iota(jnp.int32, (1, 8, 1), 1) == 0
        head = jnp.where(first_row, carry[...], rolled[:, 0:8, :])
        prev = head if lb == 8 else jnp.concatenate([head, rolled[:, 8:, :]], axis=1)
        carry[...] = x3[:, lb - 1:lb, :]
        return (x3 + (prev - x3) * mu_ref[...]).reshape(LT, width)

    xr = shift_mix(r_ref, cr, mur_ref)
    xk = shift_mix(k_ref, ck, muk_ref)
    xv = shift_mix(v_ref, cv, muv_ref)
    xl = shift_mix(l_ref, cl, mul_ref)

    lane_l = lax.broadcasted_iota(jnp.int32, (LT, LORA), 1)
    act = jnp.where(lane_l < 64, jnp.tanh(xl), jnp.where(lane_l < 128, xl, _sigmoid(xl))).astype(BF16)
    z = w0_ref[...] + _dot(act, w2_ref[...])
    lw = -DECAY_SCALE * _sigmoid(z)
    a = _sigmoid(a0_ref[...] + _dot(act, a2_ref[...]))
    g = _dot(act, g2_ref[...])
    kk = xk * kk_ref[...]
    kmod = xk * (1.0 + (a - 1.0) * ka_ref[...])
    t_idx = lax.broadcasted_iota(jnp.int32, (LT, 1), 0)
    if tv < lb:
        valid = (t_idx % lb) < tv
        lw = jnp.where(valid, lw, 0.0)
        kk = jnp.where(valid, kk, 0.0)
        kmod = jnp.where(valid, kmod, 0.0)
        xv = jnp.where(valid, xv, 0.0)

    row = lax.broadcasted_iota(jnp.int32, (L, L), 0)
    col = lax.broadcasted_iota(jnp.int32, (L, L), 1)
    tril = jnp.where((col <= row) & (col // lb == row // lb), 1.0, 0.0).astype(F32)
    subs = range(nsub)
    rows = [slice(s * L, (s + 1) * L) for s in subs]
    cum_s = [_cumsum_rows(tril, lw[rows[s]]) for s in subs]

    lane = lax.broadcasted_iota(jnp.int32, (L, 128), 1)
    lo = lane < HB
    src = lane % HB
    trow = lax.broadcasted_iota(jnp.int32, (L, 128), 0)
    same = (src // lb) == (trow // lb)
    strict = same & (src < trow)
    incl = same & (src <= trow)
    r128 = lax.broadcasted_iota(jnp.int32, (128, 128), 0)
    c128 = lax.broadcasted_iota(jnp.int32, (128, 128), 1)
    blockdiag = (r128 < HB) == (c128 < HB)
    eye_pair = jnp.where(src == trow, 1.0, 0.0).astype(F32)

    sls = [slice(p * 128, (p + 1) * 128) for p in range(N_PAIR)]
    items = [(s, p) for s in subs for p in range(N_PAIR)]
    idx = range(len(items))
    groups = range(nbg)

    at_l, rt_l, bt_l, kt_l, win_l, vp_l = [], [], [], [], [], []
    for (s, p) in items:
        rs, sl = rows[s], sls[p]
        kkp = kk[rs, sl]
        kap = kkp * lax.rsqrt(jnp.maximum(_pair_sum(kkp * kkp, lo), 1e-24))
        cum_p = cum_s[s][:, sl]
        w_in = jnp.exp(cum_p)
        w_inv = jnp.exp(-cum_p)
        at_l.append(-kap * jnp.exp(cum_p - lw[rs, sl]))
        rt_l.append(xr[rs, sl] * w_in)
        bt_l.append(kap * a[rs, sl] * w_inv)
        kt_l.append(kmod[rs, sl] * w_inv)
        win_l.append(w_in)
        vp_l.append(xv[rs, sl])
    bdv_l = [_bd(vp_l[i], lo).astype(BF16) for i in idx]

    gm_l = [_dot_nt(jnp.concatenate([at_l[i], rt_l[i]], axis=0).astype(BF16),
                    jnp.concatenate([_bd(bt_l[i], lo), _bd(kt_l[i], lo)], axis=0).astype(BF16))
            for i in idx]
    n_l = [jnp.where(strict, gm_l[i][0:L, 0:128], 0.0) for i in idx]
    aak_l = [jnp.where(strict, gm_l[i][0:L, 128:256], 0.0).astype(BF16) for i in idx]
    ark_l = [jnp.concatenate([jnp.where(incl, gm_l[i][L:2 * L, 0:128], 0.0),
                              jnp.where(incl, gm_l[i][L:2 * L, 128:256], 0.0)], axis=1).astype(BF16)
             for i in idx]

    xs_l = [[_dot_nt(jnp.concatenate([at_l[i][gi * lb:(gi + 1) * lb], rt_l[i][gi * lb:(gi + 1) * lb]],
                                     axis=0).astype(BF16), sbd[items[i][0] * nbg + gi, items[i][1]].astype(BF16))
             for gi in groups] for i in idx]
    if nbg == 1:
        as_l = [xs_l[i][0][0:lb] for i in idx]
        rs_l = [xs_l[i][0][lb:2 * lb] for i in idx]
    else:
        as_l = [jnp.concatenate([xs_l[i][gi][0:lb] for gi in groups], axis=0) for i in idx]
        rs_l = [jnp.concatenate([xs_l[i][gi][lb:2 * lb] for gi in groups], axis=0) for i in idx]

    y0_l = [as_l[i] + _dot(aak_l[i], bdv_l[i]) for i in idx]

    dm_l = [eye_pair for _ in idx]
    s_blk = 1
    while 2 * s_blk <= lb:
        lvl = ((trow // (2 * s_blk)) == (src // (2 * s_blk))) & ((trow % (2 * s_blk)) >= s_blk) \
            & ((src % (2 * s_blk)) < s_blk)
        if s_blk == 1:
            dm_l = [dm_l[i] + jnp.where(lvl, n_l[i], 0.0) for i in idx]
        else:
            t1_l = [_dot(jnp.where(lvl, n_l[i], 0.0).astype(BF16), _bd(dm_l[i], lo).astype(BF16))
                    for i in idx]
            dm_l = [dm_l[i] + _dot(dm_l[i].astype(BF16), _bd(t1_l[i], lo).astype(BF16)) for i in idx]
        s_blk *= 2
    u_l = [_dot(dm_l[i].astype(BF16), _bd(y0_l[i], lo).astype(BF16)) for i in idx]

    gate_b = _sigmoid(gb_ref[...].astype(F32).reshape(LT, D))
    y_a = ya_ref[...].astype(F32).reshape(LT, D)
    bonus_l = [_pair_sum(xr[rows[s], sls[p]] * kmod[rows[s], sls[p]] * rk_ref[:, sls[p]], lo) * vp_l[i]
               for i, (s, p) in enumerate(items)]

    o_l = [rs_l[i] + _dot(ark_l[i], jnp.concatenate([_bd(u_l[i], lo).astype(BF16), bdv_l[i]], axis=0))
           for i in idx]

    w3_l = [win_l[i].reshape(nbg, lb, 128)[:, lb - 1:lb, :] for i in idx]
    rhs_l = []
    for i in idx:
        w_last = jnp.broadcast_to(w3_l[i], (nbg, lb, 128)).reshape(L, 128)
        rhs_l.append(jnp.concatenate([bt_l[i] * w_last, kt_l[i] * w_last], axis=0).astype(BF16))
    uv_l = [jnp.concatenate([u_l[i], vp_l[i]], axis=0) for i in idx]
    if nbg == 1:
        upd_l = [_dot_tn(uv_l[i].astype(BF16), rhs_l[i]) for i in idx]
        for i, (s, p) in enumerate(items):
            sbd[s, p] = sbd[s, p] * w3_l[i][0] + jnp.where(blockdiag, upd_l[i], 0.0)
    else:
        cgrp = (c128 % L) // lb
        uvt_l = [uv_l[i].T for i in idx]
        for i, (s, p) in enumerate(items):
            for gi in groups:
                upd = _dot(jnp.where(cgrp == gi, uvt_l[i], 0.0).astype(BF16), rhs_l[i])
                q_ = s * nbg + gi
                sbd[q_, p] = sbd[q_, p] * w3_l[i][gi] + jnp.where(blockdiag, upd, 0.0)

    out_l = []
    for i, (s, p) in enumerate(items):
        rs, sl = rows[s], sls[p]
        o = o_l[i]
        mu = _pair_sum(o, lo) * (1.0 / HB)
        oc = o - mu
        var = _pair_sum(oc * oc, lo) * (1.0 / HB)
        on = oc * lax.rsqrt(var + RWKV_EPS) * lw_ref[:, sl] + lb_ref[:, sl]
        yb = (on + bonus_l[i]) * g[rs, sl]
        out_l.append(y_a[rs, sl] + gate_b[rs, sl] * yb)
    u_rows = [jnp.concatenate(out_l[s * N_PAIR:(s + 1) * N_PAIR], axis=1) for s in subs]
    u_all = u_rows[0] if nsub == 1 else jnp.concatenate(u_rows, axis=0)
    u_ref[...] = u_all.reshape(nseq, lb, D).astype(u_ref.dtype)

    @pl.when(pl.program_id(1) == pl.num_programs(1) - 1)
    def _():
        for gi in range(nseq):
            for p in range(N_PAIR):
                s_ref[gi, 2 * p] = sbd[gi, p, 0:HB, 0:HB]
                s_ref[gi, 2 * p + 1] = sbd[gi, p, HB:2 * HB, HB:2 * HB]


def _rwkv(main3, cols, tail3, ya3, prev, s0, prm, nsub, nbg, lb, tv):
    b, tp, _ = main3.shape
    has_state = s0 is not None
    nq = nsub * nbg
    blk = lambda j: pl.BlockSpec((nq, lb, D), lambda i, c, j=j: (i, c, j))
    pblk = lambda j: pl.BlockSpec((nq, 1, D), lambda i, c, j=j: (i, 0, j))
    full = lambda a: pl.BlockSpec(a.shape, lambda i, c: (0,) * a.ndim)
    sblk = pl.BlockSpec((nq, H_B, HB, HB), lambda i, c: (i, 0, 0, 0))
    c_r, c_k, c_v, c_gb = cols
    in_specs = [blk(c_r), blk(c_k), blk(c_v), blk(c_gb),
                pl.BlockSpec((nq, lb, LORA), lambda i, c: (i, c, 0)),
                pl.BlockSpec((nq, lb, D), lambda i, c: (i, c, 0))]
    args = [main3, main3, main3, main3, tail3, ya3]
    if has_state:
        in_specs += [pblk(c_r), pblk(c_k), pblk(c_v), pl.BlockSpec((nq, 1, LORA), lambda i, c: (i, 0, 0)),
                     pl.BlockSpec((N_PAIR, nq * PAIR_ROWS, 128), lambda i, c: (0, i, 0))]
        args += [prev[0], prev[0], prev[0], prev[1], s0]
    in_specs += [full(a) for a in prm]
    args += list(prm)
    return pl.pallas_call(
        functools.partial(_rwkv_kernel, nsub, nbg, lb, tv, has_state),
        grid=(b // nq, tp // lb),
        in_specs=in_specs,
        out_specs=[pl.BlockSpec((nq, lb, D), lambda i, c: (i, c, 0)), sblk],
        out_shape=[jax.ShapeDtypeStruct((b, tp, D), main3.dtype),
                   jax.ShapeDtypeStruct((b, H_B, HB, HB), F32)],
        scratch_shapes=[pltpu.VMEM((nq, N_PAIR, 128, 128), F32),
                        pltpu.VMEM((nq, 1, D), F32), pltpu.VMEM((nq, 1, D), F32),
                        pltpu.VMEM((nq, 1, D), F32), pltpu.VMEM((nq, 1, LORA), F32)],
        compiler_params=_cp(("parallel", "arbitrary")),
        name="rwkv",
    )(*args)


def _tail_kernel(u_ref, x_ref, g1_ref, sh_ref, sc_ref, g2_ref, wo_ref, wu_ref, wd_ref,
                 l1g_ref, l1b_ref, l2g_ref, l2b_ref, o_ref, x1_scr, h_scr, acc):
    bb, tt, _ = x_ref.shape
    j = pl.program_id(2)

    @pl.when(j == 0)
    def _():
        u = u_ref[...].reshape(bb * tt, D).astype(BF16)
        y = _dot(u, wo_ref[...]).reshape(bb, tt, D)
        x1 = _layer_norm(ALPHA * x_ref[...] + g1_ref[...] * y, l1g_ref[...], l1b_ref[...])
        x1_scr[...] = x1
        h_scr[...] = (x1 * (1.0 + sc_ref[...]) + sh_ref[...]).reshape(bb * tt, D).astype(BF16)
        acc[...] = jnp.zeros_like(acc)

    up = jnp.maximum(_dot(h_scr[...], wu_ref[...]), 0.0)
    acc[...] += _dot((up * up).astype(BF16), wd_ref[...])

    @pl.when(j == pl.num_programs(2) - 1)
    def _():
        z = ALPHA * x1_scr[...] + g2_ref[...] * acc[...].reshape(bb, tt, D)
        o_ref[...] = _layer_norm(z, l2g_ref[...], l2b_ref[...])


def _tail(u3, x3, mod3, q, bb, tt):
    b, tp, _ = x3.shape
    blk = pl.BlockSpec((bb, tt, D), lambda i, t, j: (i, t, 0))
    mblk = lambda col: pl.BlockSpec((bb, 1, D), lambda i, t, j, col=col: (i, 0, col))
    full = lambda shp: pl.BlockSpec(shp, lambda i, t, j: (0,) * len(shp))
    return pl.pallas_call(
        _tail_kernel,
        grid=(b // bb, tp // tt, D_FF // FF_CHUNK),
        in_specs=[blk, blk, mblk(2), mblk(3), mblk(4), mblk(5),
                  full((D, D)),
                  pl.BlockSpec((D, FF_CHUNK), lambda i, t, j: (0, j)),
                  pl.BlockSpec((FF_CHUNK, D), lambda i, t, j: (j, 0)),
                  full((1, D)), full((1, D)), full((1, D)), full((1, D))],
        out_specs=blk,
        out_shape=jax.ShapeDtypeStruct((b, tp, D), F32),
        scratch_shapes=[pltpu.VMEM((bb, tt, D), F32), pltpu.VMEM((bb * tt, D), BF16),
                        pltpu.VMEM((bb * tt, D), F32)],
        compiler_params=_cp(("parallel", "parallel", "arbitrary")),
        name="outproj_ffn",
    )(u3, x3, mod3, mod3, mod3, mod3, q['w_out'], q['w_up'], q['w_down'],
      q['ln1_g'], q['ln1_b'], q['ln2_g'], q['ln2_b'])


def _relayout_params(p):
    wt = p['w_in'].T
    w_main = jnp.concatenate(
        [wt[:3 * D], wt[3 * D + 8:6 * D + 8], wt[6 * D + 8 + LORA:8 * D + 8 + LORA]], axis=0).astype(BF16)
    w_tail = jnp.concatenate(
        [wt[6 * D + 8:6 * D + 8 + LORA], wt[3 * D:3 * D + 8],
         jnp.zeros((N_TAIL - LORA - 8, D), F32)], axis=0).astype(BF16)
    mu = p['rwkv_mu']
    z64 = jnp.zeros((64, D), F32)
    z128 = jnp.zeros((128, D), F32)
    row = lambda a: a.reshape(1, -1)
    rw = (row(mu[0:D]), row(mu[D:2 * D]), row(mu[2 * D:3 * D]), row(mu[3 * D:3 * D + LORA]),
          row(p['rwkv_w0']), row(p['rwkv_a0']), row(p['rwkv_k_k']), row(p['rwkv_k_a']),
          row(p['rwkv_r_k']), row(p['rwkv_lnx_w']), row(p['rwkv_lnx_b']),
          jnp.concatenate([p['rwkv_w2'], z64, z128], axis=0).astype(BF16),
          jnp.concatenate([z64, p['rwkv_a2'], z128], axis=0).astype(BF16),
          jnp.concatenate([z128, p['rwkv_g2']], axis=0).astype(BF16))
    gbias = jnp.concatenate([p['mlstm_i_bias'], p['mlstm_f_bias'], jnp.zeros((120,), F32)]).reshape(1, 128)
    return dict(w_main=w_main, w_tail=w_tail, rw=rw, gbias=gbias,
                conv_w=p['conv_w'], conv_b=row(p['conv_b']), norm_w=row(p['mlstm_norm_w']),
                w_out=p['w_out'].astype(BF16), w_up=p['w_up'].astype(BF16), w_down=p['w_down'].astype(BF16),
                ln1_g=row(p['ln1_g']), ln1_b=row(p['ln1_b']), ln2_g=row(p['ln2_g']), ln2_b=row(p['ln2_b']))


def _prompt_layer(x, mod, q, seq_tile, mlstm_chunk):
    b, t, _ = x.shape
    mod3 = mod.reshape(b, 1, N_COND)
    main3, tail3 = _inproj(x, mod3, q['w_main'], q['w_tail'], 1, seq_tile, BF16)
    ya3, c1, n1, m1 = _mlstm_seq(main3, tail3, q['conv_w'], q['conv_b'], q['gbias'], q['norm_w'],
                                 mlstm_chunk, min(2, b))
    u3, s1 = _rwkv(main3, RWKV_SECTIONS, tail3, ya3, None, None, q['rw'], min(2, b), 1, RW_L, RW_L)
    y = _tail(u3, x, mod3, q, 1, seq_tile)
    shift = _modulate_rows(x[:, t - 1, :], mod)
    conv = main3[:, t - (CONV_W - 1):, :2 * D].astype(F32)
    return y, (c1, n1[:, :H_A, :], m1[:, 0, :H_A], conv, s1, shift)


def _sample_layer(x, mod, st, q, bb):
    c0, n0, m0, conv0, s0, shift0 = st
    b, t, _ = x.shape
    mod3 = mod.reshape(b, 1, N_COND)
    xp = jnp.pad(x, ((0, 0), (0, 8 - t), (0, 0)))
    main3, tail3 = _inproj(xp, mod3, q['w_main'], q['w_tail'], bb, 8, F32)
    pm, pt = _inproj(shift0.reshape(1, b, D), jnp.zeros((1, 1, N_COND), F32), q['w_main'], q['w_tail'], 1, b, F32)
    prev = (pm.reshape(b, 1, N_MAIN), pt.reshape(b, 1, N_TAIL))
    conv0p = jnp.pad(conv0, ((0, 0), (8 - (CONV_W - 1), 0), (0, 0)))
    n0p = jnp.pad(n0, ((0, 0), (0, 8 - H_A), (0, 0)))
    m0p = jnp.pad(m0, ((0, 0), (0, 128 - H_A))).reshape(b, 1, 128)
    ya3, c1, n1, m1 = _mlstm_step(main3, tail3, conv0p, c0, n0p, m0p, q['conv_w'], q['conv_b'], q['gbias'],
                                  q['norm_w'], t, min(4, b))
    s0_pairs = _state_pairs(jnp.transpose(s0, (1, 2, 3, 0)))
    u3, s1 = _rwkv(main3, RWKV_SECTIONS, tail3, ya3, prev, s0_pairs, q['rw'], 1, RW_L // 8, 8, t)
    y = _tail(u3, xp, mod3, q, bb, 8)
    shift = _modulate_rows(x[:, t - 1, :], mod)
    conv = jnp.concatenate([conv0, main3[:, :t, :2 * D]], axis=1)[:, t:, :]
    return y[:, :t, :], (c1, n1[:, :H_A, :], m1[:, 0, :H_A], conv, s1, shift)


def kernel(x_prompt, x_sample, c_prompt, c_sample, state_mlstm_C, state_mlstm_n, state_mlstm_m, state_mlstm_conv, state_rwkv_S, state_rwkv_shift, w_cond, b_cond, w_in, mlstm_i_bias, mlstm_f_bias, conv_w, conv_b, mlstm_norm_w, rwkv_mu, rwkv_w0, rwkv_w2, rwkv_a0, rwkv_a2, rwkv_g2, rwkv_k_k, rwkv_k_a, rwkv_r_k, rwkv_lnx_w, rwkv_lnx_b, w_out, ln1_g, ln1_b, w_up, w_down, ln2_g, ln2_b):
    depth = w_in.shape[0]
    bp = x_prompt.shape[0]
    yp, ys = x_prompt, x_sample
    new_p = [[] for _ in range(6)]
    new_s = [[] for _ in range(6)]
    for l in range(depth):
        p = {'w_in': w_in[l], 'mlstm_i_bias': mlstm_i_bias[l], 'mlstm_f_bias': mlstm_f_bias[l],
             'conv_w': conv_w[l], 'conv_b': conv_b[l], 'mlstm_norm_w': mlstm_norm_w[l],
             'rwkv_mu': rwkv_mu[l], 'rwkv_w0': rwkv_w0[l], 'rwkv_w2': rwkv_w2[l], 'rwkv_a0': rwkv_a0[l],
             'rwkv_a2': rwkv_a2[l], 'rwkv_g2': rwkv_g2[l], 'rwkv_k_k': rwkv_k_k[l], 'rwkv_k_a': rwkv_k_a[l],
             'rwkv_r_k': rwkv_r_k[l].reshape(-1), 'rwkv_lnx_w': rwkv_lnx_w[l], 'rwkv_lnx_b': rwkv_lnx_b[l],
             'w_out': w_out[l], 'ln1_g': ln1_g[l], 'ln1_b': ln1_b[l], 'w_up': w_up[l], 'w_down': w_down[l],
             'ln2_g': ln2_g[l], 'ln2_b': ln2_b[l]}
        q = _relayout_params(p)
        mod = _cond(jnp.concatenate([c_prompt, c_sample], axis=0), w_cond[l], b_cond[l])
        st_in = (state_mlstm_C[l], state_mlstm_n[l], state_mlstm_m[l], state_mlstm_conv[l],
                 state_rwkv_S[l], state_rwkv_shift[l])
        ys, st_s = _sample_layer(ys, mod[bp:], st_in, q, min(128, ys.shape[0]))
        yp, st_p = _prompt_layer(yp, mod[:bp], q, min(1024, yp.shape[1]), min(256, yp.shape[1]))
        for lst, t in zip(new_p, st_p):
            lst.append(t)
        for lst, t in zip(new_s, st_s):
            lst.append(t)
    outs_p = [jnp.stack(t) for t in new_p]
    outs_s = [jnp.stack(t) for t in new_s]
    return (yp, ys, *outs_p, *outs_s)
```

```python
import functools

import jax
import jax.numpy as jnp
from jax import lax
from jax.experimental import pallas as pl
from jax.experimental.pallas import tpu as pltpu

F32 = jnp.float32
BF16 = jnp.bfloat16
HIGHEST = lax.Precision.HIGHEST

D = 1024
H_A = 4
DK = 256
CONV_W = 4
H_B = 16
HB = 64
N_PAIR = H_B // 2
D_FF = 4096
N_COND = 6 * D
ALPHA = 2.0 ** 0.25
LN_EPS = 1e-5
MLSTM_EPS = 1e-6
RWKV_EPS = 64e-5
DECAY_SCALE = 0.6065306597126334

N_MAIN = 8 * D
RWKV_SECTIONS = (3, 4, 5, 7)
LORA = 256
TAIL_IF = LORA
N_TAIL = 512
TN_MAIN = 2048
N_MAIN_TILES = N_MAIN // TN_MAIN

RW_L = 64
FF_CHUNK = 1024
NEG = -1e30
VMEM_LIMIT = 56 * 1024 * 1024


def _cp(sem):
    return pltpu.CompilerParams(dimension_semantics=sem, vmem_limit_bytes=VMEM_LIMIT)


def _dot(a, b, prec=None):
    return jnp.dot(a, b, preferred_element_type=F32, precision=prec)


def _dot_nt(a, b, prec=None):
    return lax.dot_general(a, b, (((1,), (1,)), ((), ())), preferred_element_type=F32, precision=prec)


def _dot_tn(a, b, prec=None):
    return lax.dot_general(a, b, (((0,), (0,)), ((), ())), preferred_element_type=F32, precision=prec)


def _cumsum_rows(tril01, x):
    hi = x.astype(BF16)
    r1 = x - hi.astype(F32)
    mid = r1.astype(BF16)
    lo = (r1 - mid.astype(F32)).astype(BF16)
    t = tril01.astype(BF16)
    return _dot(t, hi) + _dot(t, mid) + _dot(t, lo)


def _log_sigmoid(x):
    return jnp.minimum(x, 0.0) - jnp.log1p(jnp.exp(-jnp.abs(x)))


def _sigmoid(x):
    return 0.5 * jnp.tanh(0.5 * x) + 0.5


def _silu(x):
    h = 0.5 * x
    return h + h * jnp.tanh(h)


def _layer_norm(z, g, b):
    mu = jnp.mean(z, axis=-1, keepdims=True)
    zc = z - mu
    var = jnp.mean(zc * zc, axis=-1, keepdims=True)
    return zc * lax.rsqrt(var + LN_EPS) * g + b


def _cond_kernel(c_ref, w_ref, b_ref, o_ref):
    s = _silu(c_ref[...]).astype(BF16)
    o_ref[...] = _dot(s, w_ref[...].astype(BF16)) + b_ref[...]


def _cond(c, w_cond, b_cond):
    n = c.shape[0]
    tn = 1536
    return pl.pallas_call(
        _cond_kernel,
        grid=(N_COND // tn,),
        in_specs=[pl.BlockSpec((n, D), lambda j: (0, 0)),
                  pl.BlockSpec((D, tn), lambda j: (0, j)),
                  pl.BlockSpec((1, tn), lambda j: (0, j))],
        out_specs=pl.BlockSpec((n, tn), lambda j: (0, j)),
        out_shape=jax.ShapeDtypeStruct((n, N_COND), F32),
        compiler_params=_cp(("arbitrary",)),
        name="cond",
    )(c, w_cond, b_cond.reshape(1, N_COND))


def _inproj_kernel(x_ref, sh_ref, sc_ref, wm_ref, wt_ref, main_ref, tail_ref, h_scr):
    bb, tt, _ = x_ref.shape
    j = pl.program_id(2)

    @pl.when(j == 0)
    def _():
        h = x_ref[...] * (1.0 + sc_ref[...]) + sh_ref[...]
        h_scr[...] = h.reshape(bb * tt, D).astype(BF16)

    main_ref[...] = _dot_nt(h_scr[...], wm_ref[...]).reshape(bb, tt, TN_MAIN).astype(main_ref.dtype)

    @pl.when(j == N_MAIN_TILES - 1)
    def _():
        tail_ref[...] = _dot_nt(h_scr[...], wt_ref[...]).reshape(bb, tt, N_TAIL)


def _inproj(x3, mod3, w_main, w_tail, bb, tt, main_dtype):
    b, tp, _ = x3.shape
    return pl.pallas_call(
        _inproj_kernel,
        grid=(b // bb, tp // tt, N_MAIN_TILES),
        in_specs=[pl.BlockSpec((bb, tt, D), lambda i, t, j: (i, t, 0)),
                  pl.BlockSpec((bb, 1, D), lambda i, t, j: (i, 0, 0)),
                  pl.BlockSpec((bb, 1, D), lambda i, t, j: (i, 0, 1)),
                  pl.BlockSpec((TN_MAIN, D), lambda i, t, j: (j, 0)),
                  pl.BlockSpec((N_TAIL, D), lambda i, t, j: (0, 0))],
        out_specs=[pl.BlockSpec((bb, tt, TN_MAIN), lambda i, t, j: (i, t, j)),
                   pl.BlockSpec((bb, tt, N_TAIL), lambda i, t, j: (i, t, 0))],
        out_shape=[jax.ShapeDtypeStruct((b, tp, N_MAIN), main_dtype),
                   jax.ShapeDtypeStruct((b, tp, N_TAIL), F32)],
        scratch_shapes=[pltpu.VMEM((bb * tt, D), BF16)],
        compiler_params=_cp(("parallel", "parallel", "arbitrary")),
        name="inproj",
    )(x3, mod3, mod3, w_main, w_tail)


def _modulate_kernel(x_ref, sh_ref, sc_ref, o_ref):
    o_ref[...] = x_ref[...] * (1.0 + sc_ref[...]) + sh_ref[...]


def _modulate_rows(x2, mod2):
    n = x2.shape[0]
    return pl.pallas_call(
        _modulate_kernel,
        grid=(1,),
        in_specs=[pl.BlockSpec((n, D), lambda i: (0, 0)),
                  pl.BlockSpec((n, D), lambda i: (0, 0)),
                  pl.BlockSpec((n, D), lambda i: (0, 1))],
        out_specs=pl.BlockSpec((n, D), lambda i: (0, 0)),
        out_shape=jax.ShapeDtypeStruct((n, D), F32),
        name="modulate_last",
    )(x2, mod2, mod2)


def _conv4(pad_ref, n_rows, cw, cb):
    acc = cb + pad_ref[8:8 + n_rows, :] * cw[3:4, :]
    acc = acc + pad_ref[7:7 + n_rows, :] * cw[2:3, :]
    acc = acc + pad_ref[6:6 + n_rows, :] * cw[1:2, :]
    acc = acc + pad_ref[5:5 + n_rows, :] * cw[0:1, :]
    return acc


def _head_norm_rows(h, eps):
    mu = jnp.mean(h, axis=-1, keepdims=True)
    hc = h - mu
    var = jnp.mean(hc * hc, axis=-1, keepdims=True)
    return hc * lax.rsqrt(var + eps)


def _mlstm_seq_kernel(nb, qp_ref, kp_ref, v_ref, ga_ref, if_ref, cw_ref, cb_ref, gb_ref, nw_ref,
                      ya_ref, c_ref, n_ref, m_ref, haloq, halok):
    L = qp_ref.shape[1]
    assert qp_ref.dtype == BF16 and kp_ref.dtype == BF16

    @pl.when(pl.program_id(1) == 0)
    def _():
        c_ref[...] = jnp.zeros_like(c_ref)
        n_ref[...] = jnp.zeros_like(n_ref)
        m_ref[...] = jnp.zeros_like(m_ref)
        haloq[...] = jnp.zeros_like(haloq)
        halok[...] = jnp.zeros_like(halok)

    cw = cw_ref[...]
    cb = cb_ref[...]
    nw = nw_ref[...]
    row = lax.broadcasted_iota(jnp.int32, (L, L), 0)
    col = lax.broadcasted_iota(jnp.int32, (L, L), 1)
    causal = col <= row
    tril = jnp.where(causal, 1.0, 0.0).astype(F32)
    srow = lax.broadcasted_iota(jnp.int32, (CONV_W * L, L), 0)
    scol = lax.broadcasted_iota(jnp.int32, (CONV_W * L, L), 1)
    shift_mat = jnp.where(scol + srow // L == srow % L, 1.0, 0.0).astype(BF16)
    r8 = lax.broadcasted_iota(jnp.int32, (8, 1), 0)

    def conv(x_bf, halo_ref, cw_, cb_):
        sh = _dot(shift_mat, x_bf)
        acc = cb_ + sh[0:L] * cw_[3:4, :] + sh[L:2 * L] * cw_[2:3, :] + sh[2 * L:3 * L] * cw_[1:2, :] \
            + sh[3 * L:4 * L] * cw_[0:1, :]
        halo = halo_ref[...]
        fix = jnp.zeros((8, DK), F32)
        for j in range(1, CONV_W):
            fix = fix + jnp.where(r8 < j, pltpu.roll(halo, j, 0), 0.0) * cw_[CONV_W - 1 - j:CONV_W - j, :]
        halo_ref[...] = sh[L - 8:L, :]
        return jnp.concatenate([acc[0:8] + fix, acc[8:L]], axis=0)

    items = [(bi, h) for bi in range(nb) for h in range(H_A)]
    sl_of = lambda h: slice(h * DK, (h + 1) * DK)
    q_it, k_it = {}, {}
    for (bi, h) in items:
        sl, ksl = sl_of(h), slice(D + h * DK, D + (h + 1) * DK)
        q_it[bi, h] = _silu(conv(qp_ref[bi, :, sl], haloq.at[bi, :, sl], cw[:, sl], cb[:, sl]))
        k_it[bi, h] = _silu(conv(kp_ref[bi, :, sl], halok.at[bi, :, sl], cw[:, ksl], cb[:, ksl])) * (DK ** -0.5)

    gpre_l, bcum_l, gpre_t_l, bcum_t_l = [], [], [], []
    for bi in range(nb):
        gpre = if_ref[bi] + gb_ref[...]
        bcum = _dot(tril, _log_sigmoid(gpre), HIGHEST)
        gpre_l.append(gpre)
        bcum_l.append(bcum)
        gpre_t_l.append(gpre.T)
        bcum_t_l.append(bcum.T)

    st = {}
    for (bi, h) in items:
        ig_col = gpre_l[bi][:, h:h + 1]
        b_col = bcum_l[bi][:, H_A + h:H_A + h + 1]
        ig_row = gpre_t_l[bi][h:h + 1, :]
        b_row = bcum_t_l[bi][H_A + h:H_A + h + 1, :]
        m_prev = m_ref[bi][:, h:h + 1]
        g_col = b_col + m_prev
        src_log = jnp.where(causal, ig_row - b_row, NEG)
        m_t = jnp.maximum(g_col, b_col + jnp.max(src_log, axis=1, keepdims=True))
        b_last = b_col[L - 1:L, :]
        wlog = b_last - b_col + ig_col
        m_new = jnp.maximum(b_last + m_prev, jnp.max(wlog, axis=0, keepdims=True))
        st[bi, h] = dict(m_t=m_t, w_inter=jnp.exp(g_col - m_t), p=jnp.exp(src_log + (b_col - m_t)), m_new=m_new,
                         decay=jnp.exp(b_last + m_prev - m_new), wts=jnp.exp(wlog - m_new))
    qb = {it: q_it[it].astype(BF16) for it in items}
    kb = {it: k_it[it].astype(BF16) for it in items}
    s_l = {it: _dot_nt(qb[it], kb[it]) * st[it]['p'] for it in items}
    qc_l = {(bi, h): _dot_nt(qb[bi, h], c_ref[bi, h].astype(BF16)) for (bi, h) in items}
    sv_l = {(bi, h): _dot(s_l[bi, h].astype(BF16), v_ref[bi, :, sl_of(h)].astype(BF16)) for (bi, h) in items}
    upd_l = {(bi, h): _dot_tn((st[bi, h]['wts'] * v_ref[bi, :, sl_of(h)].astype(F32)).astype(BF16), kb[bi, h])
             for (bi, h) in items}
    for (bi, h) in items:
        sl = sl_of(h)
        d = st[bi, h]
        qh = q_it[bi, h]
        kh = k_it[bi, h]
        nh = n_ref[bi, h:h + 1, :]
        num = d['w_inter'] * qc_l[bi, h] + sv_l[bi, h]
        den = d['w_inter'] * jnp.sum(qh * nh, axis=1, keepdims=True) + jnp.sum(s_l[bi, h], axis=1, keepdims=True)
        hh = num / jnp.maximum(jnp.abs(den), jnp.exp(-d['m_t']))
        ga = _sigmoid(ga_ref[bi, :, sl].astype(F32))
        ya_ref[bi, :, sl] = (ga * _head_norm_rows(hh, MLSTM_EPS) * nw[:, sl]).astype(ya_ref.dtype)
        c_ref[bi, h] = d['decay'] * c_ref[bi, h] + upd_l[bi, h]
        n_ref[bi, h:h + 1, :] = d['decay'] * nh + jnp.sum(d['wts'] * kh, axis=0, keepdims=True)
        m_ref[bi, :, h:h + 1] = d['m_new']


def _mlstm_seq(main3, tail3, conv_w, conv_b, gbias, norm_w, L, nb):
    b, tp, _ = main3.shape
    blk = lambda j: pl.BlockSpec((nb, L, D), lambda i, c, j=j: (i, c, j))
    full = lambda shp: pl.BlockSpec(shp, lambda i, c: (0,) * len(shp))
    return pl.pallas_call(
        functools.partial(_mlstm_seq_kernel, nb),
        grid=(b // nb, tp // L),
        in_specs=[blk(0), blk(1), blk(2), blk(6),
                  pl.BlockSpec((nb, L, 128), lambda i, c: (i, c, TAIL_IF // 128)),
                  full((CONV_W, 2 * D)), full((1, 2 * D)), full((1, 128)), full((1, D))],
        out_specs=[pl.BlockSpec((nb, L, D), lambda i, c: (i, c, 0)),
                   pl.BlockSpec((nb, H_A, DK, DK), lambda i, c: (i, 0, 0, 0)),
                   pl.BlockSpec((nb, 8, DK), lambda i, c: (i, 0, 0)),
                   pl.BlockSpec((nb, 1, 128), lambda i, c: (i, 0, 0))],
        out_shape=[jax.ShapeDtypeStruct((b, tp, D), main3.dtype),
                   jax.ShapeDtypeStruct((b, H_A, DK, DK), F32),
                   jax.ShapeDtypeStruct((b, 8, DK), F32),
                   jax.ShapeDtypeStruct((b, 1, 128), F32)],
        scratch_shapes=[pltpu.VMEM((nb, 8, D), F32), pltpu.VMEM((nb, 8, D), F32)],
        compiler_params=_cp(("parallel", "arbitrary")),
        name="mlstm_seq",
    )(main3, main3, main3, main3, tail3, conv_w, conv_b, gbias, norm_w)


def _mlstm_step_kernel(tv, nb, qp_ref, kp_ref, v_ref, ga_ref, if_ref, conv0_ref, c0_ref, n0_ref, m0_ref,
                       cw_ref, cb_ref, gb_ref, nw_ref,
                       ya_ref, c_ref, n_ref, m_ref, padq, padk, gpad, lpad, kpad, vpad, wvpad):
    @pl.when(pl.program_id(0) == 0)
    def _():
        for r in (gpad, lpad, kpad, vpad, wvpad):
            r[...] = jnp.zeros_like(r)

    cw = cw_ref[...]
    cb = cb_ref[...]
    nw = nw_ref[...]
    gb = gb_ref[...]
    trow = lax.broadcasted_iota(jnp.int32, (8, 128), 0)
    scol = lax.broadcasted_iota(jnp.int32, (8, 128), 1)
    mask = (scol <= trow) & (scol < tv)
    rvalid = lax.broadcasted_iota(jnp.int32, (8, 1), 0) < tv
    r128 = lax.broadcasted_iota(jnp.int32, (128, 128), 0)
    c128 = lax.broadcasted_iota(jnp.int32, (128, 128), 1)
    tril = jnp.where(c128 <= r128, 1.0, 0.0).astype(F32)
    n_ref[...] = jnp.zeros_like(n_ref)
    m_ref[...] = jnp.zeros_like(m_ref)

    batches = range(nb)
    q_l, gpre_l, bcol_l, gt_l, bt_l, ga_l = [], [], [], [], [], []
    for bi in batches:
        padq[bi, 0:8, :] = conv0_ref[bi, :, 0:D]
        padk[bi, 0:8, :] = conv0_ref[bi, :, D:2 * D]
        padq[bi, 8:16, :] = qp_ref[bi]
        padk[bi, 8:16, :] = kp_ref[bi]
        q_l.append(_silu(_conv4(padq.at[bi], 8, cw[:, 0:D], cb[:, 0:D])))
        kpad[bi, 0:8, :] = _silu(_conv4(padk.at[bi], 8, cw[:, D:2 * D], cb[:, D:2 * D])) * (DK ** -0.5)
        vpad[bi, 0:8, :] = v_ref[bi]
        gpre = if_ref[bi] + gb
        gpad[bi, 0:8, :] = gpre
        lpad[bi, 0:8, :] = _log_sigmoid(gpre)
        gpre_l.append(gpre)
        ga_l.append(_sigmoid(ga_ref[bi]))
    for bi in batches:
        bpad = _dot(tril, lpad[bi], HIGHEST)
        bcol_l.append(bpad[0:8, :])
        bt_l.append(bpad.T)
        gt_l.append(gpad[bi].T)

    probs = [(bi, h) for bi in batches for h in range(H_A)]
    sl_of = lambda h: slice(h * DK, (h + 1) * DK)
    st = {}
    for (bi, h) in probs:
        ig_col = gpre_l[bi][:, h:h + 1]
        b_col = bcol_l[bi][:, H_A + h:H_A + h + 1]
        ig_row = gt_l[bi][h:h + 1, :]
        b_row = bt_l[bi][H_A + h:H_A + h + 1, :]
        m_prev = m0_ref[bi][:, h:h + 1]
        g_col = b_col + m_prev
        dlog = jnp.where(mask, b_col - b_row + ig_row, NEG)
        m_t = jnp.maximum(g_col, jnp.max(dlog, axis=1, keepdims=True))
        b_last = b_col[tv - 1:tv, :]
        wlog = jnp.where(rvalid, b_last - b_col + ig_col, NEG)
        m_new = jnp.maximum(b_last + m_prev, jnp.max(wlog, axis=0, keepdims=True))
        wts = jnp.exp(wlog - m_new)
        wvpad[bi, 0:8, sl_of(h)] = wts * vpad[bi, 0:8, sl_of(h)]
        st[bi, h] = dict(m_t=m_t, w_inter=jnp.exp(g_col - m_t), pm=jnp.exp(dlog - m_t), m_new=m_new,
                         decay=jnp.exp(b_last + m_prev - m_new), wts=wts)
    kb = {(bi, h): kpad[bi, :, sl_of(h)].astype(BF16) for (bi, h) in probs}
    qb = {(bi, h): q_l[bi][:, sl_of(h)].astype(BF16) for (bi, h) in probs}
    s_l = {k_: _dot_nt(qb[k_], kb[k_]) * st[k_]['pm'] for k_ in probs}
    qc_l = {(bi, h): _dot_nt(qb[bi, h], c0_ref[bi, h].astype(BF16)) for (bi, h) in probs}
    sv_l = {(bi, h): _dot(s_l[bi, h].astype(BF16), vpad[bi, :, sl_of(h)].astype(BF16)) for (bi, h) in probs}
    upd_l = {(bi, h): _dot(wvpad[bi, :, sl_of(h)].T.astype(BF16), kb[bi, h]) for (bi, h) in probs}
    for (bi, h) in probs:
        sl = sl_of(h)
        d = st[bi, h]
        nh = n0_ref[bi, h:h + 1, :]
        qh = q_l[bi][:, sl]
        num = d['w_inter'] * qc_l[bi, h] + sv_l[bi, h]
        den = d['w_inter'] * jnp.sum(qh * nh, axis=1, keepdims=True) + jnp.sum(s_l[bi, h], axis=1, keepdims=True)
        hh = num / jnp.maximum(jnp.abs(den), jnp.exp(-d['m_t']))
        ya_ref[bi, :, sl] = ga_l[bi][:, sl] * _head_norm_rows(hh, MLSTM_EPS) * nw[:, sl]
        c_ref[bi, h] = d['decay'] * c0_ref[bi, h] + upd_l[bi, h]
        n_ref[bi, h:h + 1, :] = d['decay'] * nh + jnp.sum(d['wts'] * kpad[bi, 0:8, sl], axis=0, keepdims=True)
        m_ref[bi, :, h:h + 1] = d['m_new']


def _mlstm_step(main3, tail3, conv0p, c0, n0p, m0p, conv_w, conv_b, gbias, norm_w, tv, nb):
    b = main3.shape[0]
    blk = lambda j: pl.BlockSpec((nb, 8, D), lambda i, j=j: (i, 0, j))
    full = lambda shp: pl.BlockSpec(shp, lambda i: (0,) * len(shp))
    state_specs = [pl.BlockSpec((nb, H_A, DK, DK), lambda i: (i, 0, 0, 0)),
                   pl.BlockSpec((nb, 8, DK), lambda i: (i, 0, 0)),
                   pl.BlockSpec((nb, 1, 128), lambda i: (i, 0, 0))]
    return pl.pallas_call(
        functools.partial(_mlstm_step_kernel, tv, nb),
        grid=(b // nb,),
        in_specs=[blk(0), blk(1), blk(2), blk(6),
                  pl.BlockSpec((nb, 8, 128), lambda i: (i, 0, TAIL_IF // 128)),
                  pl.BlockSpec((nb, 8, 2 * D), lambda i: (i, 0, 0))] + state_specs +
                 [full((CONV_W, 2 * D)), full((1, 2 * D)), full((1, 128)), full((1, D))],
        out_specs=[pl.BlockSpec((nb, 8, D), lambda i: (i, 0, 0))] + state_specs,
        out_shape=[jax.ShapeDtypeStruct((b, 8, D), F32),
                   jax.ShapeDtypeStruct((b, H_A, DK, DK), F32),
                   jax.ShapeDtypeStruct((b, 8, DK), F32),
                   jax.ShapeDtypeStruct((b, 1, 128), F32)],
        scratch_shapes=[pltpu.VMEM((nb, 16, D), F32), pltpu.VMEM((nb, 16, D), F32),
                        pltpu.VMEM((nb, 128, 128), F32), pltpu.VMEM((nb, 128, 128), F32),
                        pltpu.VMEM((nb, 128, D), F32), pltpu.VMEM((nb, 128, D), F32),
                        pltpu.VMEM((nb, 128, D), F32)],
        compiler_params=_cp(("arbitrary",)),
        name="mlstm_step",
    )(main3, main3, main3, main3, tail3, conv0p, c0, n0p, m0p, conv_w, conv_b, gbias, norm_w)


def _bd(x, lo):
    return jnp.concatenate([jnp.where(lo, x, 0.0), jnp.where(lo, 0.0, x)], axis=0)


def _pair_sum(x, lo):
    s_lo = jnp.sum(jnp.where(lo, x, 0.0), axis=1, keepdims=True)
    s_hi = jnp.sum(jnp.where(lo, 0.0, x), axis=1, keepdims=True)
    return jnp.where(lo, s_lo, s_hi)


PAIR_ROWS = 136


def _state_pairs_kernel(s_ref, o_ref):
    nb = s_ref.shape[-1]
    lane = lax.broadcasted_iota(jnp.int32, (nb, 128), 1)
    lo = lane < HB
    for b in range(nb):
        o_ref[0, b * PAIR_ROWS + 128:(b + 1) * PAIR_ROWS, :] = jnp.zeros((PAIR_ROWS - 128, 128), F32)
    for v in range(HB):
        t = jnp.concatenate([s_ref[0, v], s_ref[1, v]], axis=0).T
        o_ref[0, pl.ds(v, nb, stride=PAIR_ROWS), :] = jnp.where(lo, t, 0.0)
        o_ref[0, pl.ds(HB + v, nb, stride=PAIR_ROWS), :] = jnp.where(lo, 0.0, t)


def _state_pairs(s_hvkb):
    nb = s_hvkb.shape[-1]
    return pl.pallas_call(
        _state_pairs_kernel,
        grid=(N_PAIR,),
        in_specs=[pl.BlockSpec((2, HB, HB, nb), lambda p: (p, 0, 0, 0))],
        out_specs=pl.BlockSpec((1, nb * PAIR_ROWS, 128), lambda p: (p, 0, 0)),
        out_shape=jax.ShapeDtypeStruct((N_PAIR, nb * PAIR_ROWS, 128), F32),
        compiler_params=_cp(("arbitrary",)),
        name="state_pairs",
    )(s_hvkb)


def _rwkv_kernel(nsub, nbg, lb, tv, has_state, *refs):
    (r_ref, k_ref, v_ref, gb_ref, l_ref, ya_ref), refs = refs[:6], refs[6:]
    if has_state:
        (pr_ref, pk_ref, pv_ref, pl_ref, s0_ref), refs = refs[:5], refs[5:]
    (mur_ref, muk_ref, muv_ref, mul_ref, w0_ref, a0_ref, kk_ref, ka_ref, rk_ref,
     lw_ref, lb_ref, w2_ref, a2_ref, g2_ref,
     u_ref, s_ref, sbd, cr, ck, cv, cl) = refs
    L = nbg * lb
    nseq = nsub * nbg
    LT = nsub * L

    @pl.when(pl.program_id(1) == 0)
    def _():
        if has_state:
            for gi in range(nseq):
                for p in range(N_PAIR):
                    sbd[gi, p] = s0_ref[p, gi * PAIR_ROWS:gi * PAIR_ROWS + 128, :]
            cr[...] = pr_ref[...].astype(F32)
            ck[...] = pk_ref[...].astype(F32)
            cv[...] = pv_ref[...].astype(F32)
            cl[...] = pl_ref[...]
        else:
            sbd[...] = jnp.zeros_like(sbd)
            for c_ in (cr, ck, cv, cl):
                c_[...] = jnp.zeros_like(c_)

    def shift_mix(x_ref, carry, mu_ref):
        x3 = x_ref[...].astype(F32)
        width = x3.shape[-1]
        rolled = pltpu.roll(x3, 1, 1)
        first_row = lax.broadcasted_iota(jnp.int32, (1, 8, 1), 1) == 0
        head = jnp.where(first_row, carry[...], rolled[:, 0:8, :])
        prev = head if lb == 8 else jnp.concatenate([head, rolled[:, 8:, :]], axis=1)
        carry[...] = x3[:, lb - 1:lb, :]
        return (x3 + (prev - x3) * mu_ref[...]).reshape(LT, width)

    xr = shift_mix(r_ref, cr, mur_ref)
    xk = shift_mix(k_ref, ck, muk_ref)
    xv = shift_mix(v_ref, cv, muv_ref)
    xl = shift_mix(l_ref, cl, mul_ref)

    lane_l = lax.broadcasted_iota(jnp.int32, (LT, LORA), 1)
    act = jnp.where(lane_l < 64, jnp.tanh(xl), jnp.where(lane_l < 128, xl, _sigmoid(xl))).astype(BF16)
    z = w0_ref[...] + _dot(act, w2_ref[...])
    lw = -DECAY_SCALE * _sigmoid(z)
    a = _sigmoid(a0_ref[...] + _dot(act, a2_ref[...]))
    g = _dot(act, g2_ref[...])
    kk = xk * kk_ref[...]
    kmod = xk * (1.0 + (a - 1.0) * ka_ref[...])
    t_idx = lax.broadcasted_iota(jnp.int32, (LT, 1), 0)
    if tv < lb:
        valid = (t_idx % lb) < tv
        lw = jnp.where(valid, lw, 0.0)
        kk = jnp.where(valid, kk, 0.0)
        kmod = jnp.where(valid, kmod, 0.0)
        xv = jnp.where(valid, xv, 0.0)

    row = lax.broadcasted_iota(jnp.int32, (L, L), 0)
    col = lax.broadcasted_iota(jnp.int32, (L, L), 1)
    tril = jnp.where((col <= row) & (col // lb == row // lb), 1.0, 0.0).astype(F32)
    subs = range(nsub)
    rows = [slice(s * L, (s + 1) * L) for s in subs]
    cum_s = [_cumsum_rows(tril, lw[rows[s]]) for s in subs]

    lane = lax.broadcasted_iota(jnp.int32, (L, 128), 1)
    lo = lane < HB
    src = lane % HB
    trow = lax.broadcasted_iota(jnp.int32, (L, 128), 0)
    same = (src // lb) == (trow // lb)
    strict = same & (src < trow)
    incl = same & (src <= trow)
    r128 = lax.broadcasted_iota(jnp.int32, (128, 128), 0)
    c128 = lax.broadcasted_iota(jnp.int32, (128, 128), 1)
    blockdiag = (r128 < HB) == (c128 < HB)
    eye_pair = jnp.where(src == trow, 1.0, 0.0).astype(F32)

    sls = [slice(p * 128, (p + 1) * 128) for p in range(N_PAIR)]
    items = [(s, p) for s in subs for p in range(N_PAIR)]
    idx = range(len(items))
    groups = range(nbg)

    at_l, rt_l, bt_l, kt_l, win_l, vp_l = [], [], [], [], [], []
    for (s, p) in items:
        rs, sl = rows[s], sls[p]
        kkp = kk[rs, sl]
        kap = kkp * lax.rsqrt(jnp.maximum(_pair_sum(kkp * kkp, lo), 1e-24))
        cum_p = cum_s[s][:, sl]
        w_in = jnp.exp(cum_p)
        w_inv = jnp.exp(-cum_p)
        at_l.append(-kap * jnp.exp(cum_p - lw[rs, sl]))
        rt_l.append(xr[rs, sl] * w_in)
        bt_l.append(kap * a[rs, sl] * w_inv)
        kt_l.append(kmod[rs, sl] * w_inv)
        win_l.append(w_in)
        vp_l.append(xv[rs, sl])
    bdv_l = [_bd(vp_l[i], lo).astype(BF16) for i in idx]

    gm_l = [_dot_nt(jnp.concatenate([at_l[i], rt_l[i]], axis=0).astype(BF16),
                    jnp.concatenate([_bd(bt_l[i], lo), _bd(kt_l[i], lo)], axis=0).astype(BF16))
            for i in idx]
    n_l = [jnp.where(strict, gm_l[i][0:L, 0:128], 0.0) for i in idx]
    aak_l = [jnp.where(strict, gm_l[i][0:L, 128:256], 0.0).astype(BF16) for i in idx]
    ark_l = [jnp.concatenate([jnp.where(incl, gm_l[i][L:2 * L, 0:128], 0.0),
                              jnp.where(incl, gm_l[i][L:2 * L, 128:256], 0.0)], axis=1).astype(BF16)
             for i in idx]

    xs_l = [[_dot_nt(jnp.concatenate([at_l[i][gi * lb:(gi + 1) * lb], rt_l[i][gi * lb:(gi + 1) * lb]],
                                     axis=0).astype(BF16), sbd[items[i][0] * nbg + gi, items[i][1]].astype(BF16))
             for gi in groups] for i in idx]
    if nbg == 1:
        as_l = [xs_l[i][0][0:lb] for i in idx]
        rs_l = [xs_l[i][0][lb:2 * lb] for i in idx]
    else:
        as_l = [jnp.concatenate([xs_l[i][gi][0:lb] for gi in groups], axis=0) for i in idx]
        rs_l = [jnp.concatenate([xs_l[i][gi][lb:2 * lb] for gi in groups], axis=0) for i in idx]

    y0_l = [as_l[i] + _dot(aak_l[i], bdv_l[i]) for i in idx]

    dm_l = [eye_pair for _ in idx]
    s_blk = 1
    while 2 * s_blk <= lb:
        lvl = ((trow // (2 * s_blk)) == (src // (2 * s_blk))) & ((trow % (2 * s_blk)) >= s_blk) \
            & ((src % (2 * s_blk)) < s_blk)
        if s_blk == 1:
            dm_l = [dm_l[i] + jnp.where(lvl, n_l[i], 0.0) for i in idx]
        else:
            t1_l = [_dot(jnp.where(lvl, n_l[i], 0.0).astype(BF16), _bd(dm_l[i], lo).astype(BF16))
                    for i in idx]
            dm_l = [dm_l[i] + _dot(dm_l[i].astype(BF16), _bd(t1_l[i], lo).astype(BF16)) for i in idx]
        s_blk *= 2
    u_l = [_dot(dm_l[i].astype(BF16), _bd(y0_l[i], lo).astype(BF16)) for i in idx]

    gate_b = _sigmoid(gb_ref[...].astype(F32).reshape(LT, D))
    y_a = ya_ref[...].astype(F32).reshape(LT, D)
    bonus_l = [_pair_sum(xr[rows[s], sls[p]] * kmod[rows[s], sls[p]] * rk_ref[:, sls[p]], lo) * vp_l[i]
               for i, (s, p) in enumerate(items)]

    o_l = [rs_l[i] + _dot(ark_l[i], jnp.concatenate([_bd(u_l[i], lo).astype(BF16), bdv_l[i]], axis=0))
           for i in idx]

    w3_l = [win_l[i].reshape(nbg, lb, 128)[:, lb - 1:lb, :] for i in idx]
    rhs_l = []
    for i in idx:
        w_last = jnp.broadcast_to(w3_l[i], (nbg, lb, 128)).reshape(L, 128)
        rhs_l.append(jnp.concatenate([bt_l[i] * w_last, kt_l[i] * w_last], axis=0).astype(BF16))
    uv_l = [jnp.concatenate([u_l[i], vp_l[i]], axis=0) for i in idx]
    if nbg == 1:
        upd_l = [_dot_tn(uv_l[i].astype(BF16), rhs_l[i]) for i in idx]
        for i, (s, p) in enumerate(items):
            sbd[s, p] = sbd[s, p] * w3_l[i][0] + jnp.where(blockdiag, upd_l[i], 0.0)
    else:
        cgrp = (c128 % L) // lb
        uvt_l = [uv_l[i].T for i in idx]
        for i, (s, p) in enumerate(items):
            for gi in groups:
                upd = _dot(jnp.where(cgrp == gi, uvt_l[i], 0.0).astype(BF16), rhs_l[i])
                q_ = s * nbg + gi
                sbd[q_, p] = sbd[q_, p] * w3_l[i][gi] + jnp.where(blockdiag, upd, 0.0)

    out_l = []
    for i, (s, p) in enumerate(items):
        rs, sl = rows[s], sls[p]
        o = o_l[i]
        mu = _pair_sum(o, lo) * (1.0 / HB)
        oc = o - mu
        var = _pair_sum(oc * oc, lo) * (1.0 / HB)
        on = oc * lax.rsqrt(var + RWKV_EPS) * lw_ref[:, sl] + lb_ref[:, sl]
        yb = (on + bonus_l[i]) * g[rs, sl]
        out_l.append(y_a[rs, sl] + gate_b[rs, sl] * yb)
    u_rows = [jnp.concatenate(out_l[s * N_PAIR:(s + 1) * N_PAIR], axis=1) for s in subs]
    u_all = u_rows[0] if nsub == 1 else jnp.concatenate(u_rows, axis=0)
    u_ref[...] = u_all.reshape(nseq, lb, D).astype(u_ref.dtype)

    @pl.when(pl.program_id(1) == pl.num_programs(1) - 1)
    def _():
        for gi in range(nseq):
            for p in range(N_PAIR):
                s_ref[gi, 2 * p] = sbd[gi, p, 0:HB, 0:HB]
                s_ref[gi, 2 * p + 1] = sbd[gi, p, HB:2 * HB, HB:2 * HB]


def _rwkv(main3, cols, tail3, ya3, prev, s0, prm, nsub, nbg, lb, tv):
    b, tp, _ = main3.shape
    has_state = s0 is not None
    nq = nsub * nbg
    blk = lambda j: pl.BlockSpec((nq, lb, D), lambda i, c, j=j: (i, c, j))
    pblk = lambda j: pl.BlockSpec((nq, 1, D), lambda i, c, j=j: (i, 0, j))
    full = lambda a: pl.BlockSpec(a.shape, lambda i, c: (0,) * a.ndim)
    sblk = pl.BlockSpec((nq, H_B, HB, HB), lambda i, c: (i, 0, 0, 0))
    c_r, c_k, c_v, c_gb = cols
    in_specs = [blk(c_r), blk(c_k), blk(c_v), blk(c_gb),
                pl.BlockSpec((nq, lb, LORA), lambda i, c: (i, c, 0)),
                pl.BlockSpec((nq, lb, D), lambda i, c: (i, c, 0))]
    args = [main3, main3, main3, main3, tail3, ya3]
    if has_state:
        in_specs += [pblk(c_r), pblk(c_k), pblk(c_v), pl.BlockSpec((nq, 1, LORA), lambda i, c: (i, 0, 0)),
                     pl.BlockSpec((N_PAIR, nq * PAIR_ROWS, 128), lambda i, c: (0, i, 0))]
        args += [prev[0], prev[0], prev[0], prev[1], s0]
    in_specs += [full(a) for a in prm]
    args += list(prm)
    return pl.pallas_call(
        functools.partial(_rwkv_kernel, nsub, nbg, lb, tv, has_state),
        grid=(b // nq, tp // lb),
        in_specs=in_specs,
        out_specs=[pl.BlockSpec((nq, lb, D), lambda i, c: (i, c, 0)), sblk],
        out_shape=[jax.ShapeDtypeStruct((b, tp, D), main3.dtype),
                   jax.ShapeDtypeStruct((b, H_B, HB, HB), F32)],
        scratch_shapes=[pltpu.VMEM((nq, N_PAIR, 128, 128), F32),
                        pltpu.VMEM((nq, 1, D), F32), pltpu.VMEM((nq, 1, D), F32),
                        pltpu.VMEM((nq, 1, D), F32), pltpu.VMEM((nq, 1, LORA), F32)],
        compiler_params=_cp(("parallel", "arbitrary")),
        name="rwkv",
    )(*args)


def _tail_kernel(u_ref, x_ref, g1_ref, sh_ref, sc_ref, g2_ref, wo_ref, wu_ref, wd_ref,
                 l1g_ref, l1b_ref, l2g_ref, l2b_ref, o_ref, x1_scr, h_scr, acc):
    bb, tt, _ = x_ref.shape
    j = pl.program_id(2)

    @pl.when(j == 0)
    def _():
        u = u_ref[...].reshape(bb * tt, D).astype(BF16)
        y = _dot(u, wo_ref[...]).reshape(bb, tt, D)
        x1 = _layer_norm(ALPHA * x_ref[...] + g1_ref[...] * y, l1g_ref[...], l1b_ref[...])
        x1_scr[...] = x1
        h_scr[...] = (x1 * (1.0 + sc_ref[...]) + sh_ref[...]).reshape(bb * tt, D).astype(BF16)
        acc[...] = jnp.zeros_like(acc)

    up = jnp.maximum(_dot(h_scr[...], wu_ref[...]), 0.0)
    acc[...] += _dot((up * up).astype(BF16), wd_ref[...])

    @pl.when(j == pl.num_programs(2) - 1)
    def _():
        z = ALPHA * x1_scr[...] + g2_ref[...] * acc[...].reshape(bb, tt, D)
        o_ref[...] = _layer_norm(z, l2g_ref[...], l2b_ref[...])


def _tail(u3, x3, mod3, q, bb, tt):
    b, tp, _ = x3.shape
    blk = pl.BlockSpec((bb, tt, D), lambda i, t, j: (i, t, 0))
    mblk = lambda col: pl.BlockSpec((bb, 1, D), lambda i, t, j, col=col: (i, 0, col))
    full = lambda shp: pl.BlockSpec(shp, lambda i, t, j: (0,) * len(shp))
    return pl.pallas_call(
        _tail_kernel,
        grid=(b // bb, tp // tt, D_FF // FF_CHUNK),
        in_specs=[blk, blk, mblk(2), mblk(3), mblk(4), mblk(5),
                  full((D, D)),
                  pl.BlockSpec((D, FF_CHUNK), lambda i, t, j: (0, j)),
                  pl.BlockSpec((FF_CHUNK, D), lambda i, t, j: (j, 0)),
                  full((1, D)), full((1, D)), full((1, D)), full((1, D))],
        out_specs=blk,
        out_shape=jax.ShapeDtypeStruct((b, tp, D), F32),
        scratch_shapes=[pltpu.VMEM((bb, tt, D), F32), pltpu.VMEM((bb * tt, D), BF16),
                        pltpu.VMEM((bb * tt, D), F32)],
        compiler_params=_cp(("parallel", "parallel", "arbitrary")),
        name="outproj_ffn",
    )(u3, x3, mod3, mod3, mod3, mod3, q['w_out'], q['w_up'], q['w_down'],
      q['ln1_g'], q['ln1_b'], q['ln2_g'], q['ln2_b'])


def _relayout_params(p):
    wt = p['w_in'].T
    w_main = jnp.concatenate(
        [wt[:3 * D], wt[3 * D + 8:6 * D + 8], wt[6 * D + 8 + LORA:8 * D + 8 + LORA]], axis=0).astype(BF16)
    w_tail = jnp.concatenate(
        [wt[6 * D + 8:6 * D + 8 + LORA], wt[3 * D:3 * D + 8],
         jnp.zeros((N_TAIL - LORA - 8, D), F32)], axis=0).astype(BF16)
    mu = p['rwkv_mu']
    z64 = jnp.zeros((64, D), F32)
    z128 = jnp.zeros((128, D), F32)
    row = lambda a: a.reshape(1, -1)
    rw = (row(mu[0:D]), row(mu[D:2 * D]), row(mu[2 * D:3 * D]), row(mu[3 * D:3 * D + LORA]),
          row(p['rwkv_w0']), row(p['rwkv_a0']), row(p['rwkv_k_k']), row(p['rwkv_k_a']),
          row(p['rwkv_r_k']), row(p['rwkv_lnx_w']), row(p['rwkv_lnx_b']),
          jnp.concatenate([p['rwkv_w2'], z64, z128], axis=0).astype(BF16),
          jnp.concatenate([z64, p['rwkv_a2'], z128], axis=0).astype(BF16),
          jnp.concatenate([z128, p['rwkv_g2']], axis=0).astype(BF16))
    gbias = jnp.concatenate([p['mlstm_i_bias'], p['mlstm_f_bias'], jnp.zeros((120,), F32)]).reshape(1, 128)
    return dict(w_main=w_main, w_tail=w_tail, rw=rw, gbias=gbias,
                conv_w=p['conv_w'], conv_b=row(p['conv_b']), norm_w=row(p['mlstm_norm_w']),
                w_out=p['w_out'].astype(BF16), w_up=p['w_up'].astype(BF16), w_down=p['w_down'].astype(BF16),
                ln1_g=row(p['ln1_g']), ln1_b=row(p['ln1_b']), ln2_g=row(p['ln2_g']), ln2_b=row(p['ln2_b']))


def _prompt_layer(x, mod, q, seq_tile, mlstm_chunk):
    b, t, _ = x.shape
    mod3 = mod.reshape(b, 1, N_COND)
    main3, tail3 = _inproj(x, mod3, q['w_main'], q['w_tail'], 1, seq_tile, BF16)
    ya3, c1, n1, m1 = _mlstm_seq(main3, tail3, q['conv_w'], q['conv_b'], q['gbias'], q['norm_w'],
                                 mlstm_chunk, min(2, b))
    u3, s1 = _rwkv(main3, RWKV_SECTIONS, tail3, ya3, None, None, q['rw'], min(2, b), 1, RW_L, RW_L)
    y = _tail(u3, x, mod3, q, 1, seq_tile)
    shift = _modulate_rows(x[:, t - 1, :], mod)
    conv = main3[:, t - (CONV_W - 1):, :2 * D].astype(F32)
    return y, (c1, n1[:, :H_A, :], m1[:, 0, :H_A], conv, s1, shift)


def _sample_layer(x, mod, st, q, bb):
    c0, n0, m0, conv0, s0, shift0 = st
    b, t, _ = x.shape
    mod3 = mod.reshape(b, 1, N_COND)
    xp = jnp.pad(x, ((0, 0), (0, 8 - t), (0, 0)))
    main3, tail3 = _inproj(xp, mod3, q['w_main'], q['w_tail'], bb, 8, F32)
    pm, pt = _inproj(shift0.reshape(1, b, D), jnp.zeros((1, 1, N_COND), F32), q['w_main'], q['w_tail'], 1, b, F32)
    prev = (pm.reshape(b, 1, N_MAIN), pt.reshape(b, 1, N_TAIL))
    conv0p = jnp.pad(conv0, ((0, 0), (8 - (CONV_W - 1), 0), (0, 0)))
    n0p = jnp.pad(n0, ((0, 0), (0, 8 - H_A), (0, 0)))
    m0p = jnp.pad(m0, ((0, 0), (0, 128 - H_A))).reshape(b, 1, 128)
    ya3, c1, n1, m1 = _mlstm_step(main3, tail3, conv0p, c0, n0p, m0p, q['conv_w'], q['conv_b'], q['gbias'],
                                  q['norm_w'], t, min(4, b))
    s0_pairs = _state_pairs(jnp.transpose(s0, (1, 2, 3, 0)))
    u3, s1 = _rwkv(main3, RWKV_SECTIONS, tail3, ya3, prev, s0_pairs, q['rw'], 1, RW_L // 8, 8, t)
    y = _tail(u3, xp, mod3, q, bb, 8)
    shift = _modulate_rows(x[:, t - 1, :], mod)
    conv = jnp.concatenate([conv0, main3[:, :t, :2 * D]], axis=1)[:, t:, :]
    return y[:, :t, :], (c1, n1[:, :H_A, :], m1[:, 0, :H_A], conv, s1, shift)


def kernel(x_prompt, x_sample, c_prompt, c_sample, state_mlstm_C, state_mlstm_n, state_mlstm_m, state_mlstm_conv, state_rwkv_S, state_rwkv_shift, w_cond, b_cond, w_in, mlstm_i_bias, mlstm_f_bias, conv_w, conv_b, mlstm_norm_w, rwkv_mu, rwkv_w0, rwkv_w2, rwkv_a0, rwkv_a2, rwkv_g2, rwkv_k_k, rwkv_k_a, rwkv_r_k, rwkv_lnx_w, rwkv_lnx_b, w_out, ln1_g, ln1_b, w_up, w_down, ln2_g, ln2_b):
    depth = w_in.shape[0]
    bp = x_prompt.shape[0]
    yp, ys = x_prompt, x_sample
    new_p = [[] for _ in range(6)]
    new_s = [[] for _ in range(6)]
    for l in range(depth):
        p = {'w_in': w_in[l], 'mlstm_i_bias': mlstm_i_bias[l], 'mlstm_f_bias': mlstm_f_bias[l],
             'conv_w': conv_w[l], 'conv_b': conv_b[l], 'mlstm_norm_w': mlstm_norm_w[l],
             'rwkv_mu': rwkv_mu[l], 'rwkv_w0': rwkv_w0[l], 'rwkv_w2': rwkv_w2[l], 'rwkv_a0': rwkv_a0[l],
             'rwkv_a2': rwkv_a2[l], 'rwkv_g2': rwkv_g2[l], 'rwkv_k_k': rwkv_k_k[l], 'rwkv_k_a': rwkv_k_a[l],
             'rwkv_r_k': rwkv_r_k[l].reshape(-1), 'rwkv_lnx_w': rwkv_lnx_w[l], 'rwkv_lnx_b': rwkv_lnx_b[l],
             'w_out': w_out[l], 'ln1_g': ln1_g[l], 'ln1_b': ln1_b[l], 'w_up': w_up[l], 'w_down': w_down[l],
             'ln2_g': ln2_g[l], 'ln2_b': ln2_b[l]}
        q = _relayout_params(p)
        mod = _cond(jnp.concatenate([c_prompt, c_sample], axis=0), w_cond[l], b_cond[l])
        st_in = (state_mlstm_C[l], state_mlstm_n[l], state_mlstm_m[l], state_mlstm_conv[l],
                 state_rwkv_S[l], state_rwkv_shift[l])
        ys, st_s = _sample_layer(ys, mod[bp:], st_in, q, min(128, ys.shape[0]))
        yp, st_p = _prompt_layer(yp, mod[:bp], q, min(1024, yp.shape[1]), min(256, yp.shape[1]))
        for lst, t in zip(new_p, st_p):
            lst.append(t)
        for lst, t in zip(new_s, st_s):
            lst.append(t)
    outs_p = [jnp.stack(t) for t in new_p]
    outs_s = [jnp.stack(t) for t in new_s]
    return (yp, ys, *outs_p, *outs_s)
```

```python
import functools

import jax
import jax.numpy as jnp
from jax import lax
from jax.experimental import pallas as pl
from jax.experimental.pallas import tpu as pltpu

F32 = jnp.float32
BF16 = jnp.bfloat16
HIGHEST = lax.Precision.HIGHEST

D = 1024
H_A = 4
DK = 256
CONV_W = 4
H_B = 16
HB = 64
N_PAIR = H_B // 2
D_FF = 4096
N_COND = 6 * D
ALPHA = 2.0 ** 0.25
LN_EPS = 1e-5
MLSTM_EPS = 1e-6
RWKV_EPS = 64e-5
DECAY_SCALE = 0.6065306597126334

N_MAIN = 8 * D
RWKV_SECTIONS = (3, 4, 5, 7)
LORA = 256
TAIL_IF = LORA
N_TAIL = 512
TN_MAIN = 2048
N_MAIN_TILES = N_MAIN // TN_MAIN

RW_L = 64
FF_CHUNK = 1024
NEG = -1e30
VMEM_LIMIT = 56 * 1024 * 1024


def _cp(sem):
    return pltpu.CompilerParams(dimension_semantics=sem, vmem_limit_bytes=VMEM_LIMIT)


def _dot(a, b, prec=None):
    return jnp.dot(a, b, preferred_element_type=F32, precision=prec)


def _dot_nt(a, b, prec=None):
    return lax.dot_general(a, b, (((1,), (1,)), ((), ())), preferred_element_type=F32, precision=prec)


def _dot_tn(a, b, prec=None):
    return lax.dot_general(a, b, (((0,), (0,)), ((), ())), preferred_element_type=F32, precision=prec)


def _cumsum_rows(tril01, x):
    hi = x.astype(BF16)
    r1 = x - hi.astype(F32)
    mid = r1.astype(BF16)
    lo = (r1 - mid.astype(F32)).astype(BF16)
    t = tril01.astype(BF16)
    return _dot(t, hi) + _dot(t, mid) + _dot(t, lo)


def _log_sigmoid(x):
    return jnp.minimum(x, 0.0) - jnp.log1p(jnp.exp(-jnp.abs(x)))


def _sigmoid(x):
    return 0.5 * jnp.tanh(0.5 * x) + 0.5


def _silu(x):
    h = 0.5 * x
    return h + h * jnp.tanh(h)


def _layer_norm(z, g, b):
    mu = jnp.mean(z, axis=-1, keepdims=True)
    zc = z - mu
    var = jnp.mean(zc * zc, axis=-1, keepdims=True)
    return zc * lax.rsqrt(var + LN_EPS) * g + b


def _cond_kernel(c_ref, w_ref, b_ref, o_ref):
    s = _silu(c_ref[...]).astype(BF16)
    o_ref[...] = _dot(s, w_ref[...].astype(BF16)) + b_ref[...]


def _cond(c, w_cond, b_cond):
    n = c.shape[0]
    tn = 1536
    return pl.pallas_call(
        _cond_kernel,
        grid=(N_COND // tn,),
        in_specs=[pl.BlockSpec((n, D), lambda j: (0, 0)),
                  pl.BlockSpec((D, tn), lambda j: (0, j)),
                  pl.BlockSpec((1, tn), lambda j: (0, j))],
        out_specs=pl.BlockSpec((n, tn), lambda j: (0, j)),
        out_shape=jax.ShapeDtypeStruct((n, N_COND), F32),
        compiler_params=_cp(("arbitrary",)),
        name="cond",
    )(c, w_cond, b_cond.reshape(1, N_COND))


def _inproj_kernel(x_ref, sh_ref, sc_ref, wm_ref, wt_ref, main_ref, tail_ref, h_scr):
    bb, tt, _ = x_ref.shape
    j = pl.program_id(2)

    @pl.when(j == 0)
    def _():
        h = x_ref[...] * (1.0 + sc_ref[...]) + sh_ref[...]
        h_scr[...] = h.reshape(bb * tt, D).astype(BF16)

    main_ref[...] = _dot_nt(h_scr[...], wm_ref[...]).reshape(bb, tt, TN_MAIN).astype(main_ref.dtype)

    @pl.when(j == N_MAIN_TILES - 1)
    def _():
        tail_ref[...] = _dot_nt(h_scr[...], wt_ref[...]).reshape(bb, tt, N_TAIL)


def _inproj(x3, mod3, w_main, w_tail, bb, tt, main_dtype):
    b, tp, _ = x3.shape
    return pl.pallas_call(
        _inproj_kernel,
        grid=(b // bb, tp // tt, N_MAIN_TILES),
        in_specs=[pl.BlockSpec((bb, tt, D), lambda i, t, j: (i, t, 0)),
                  pl.BlockSpec((bb, 1, D), lambda i, t, j: (i, 0, 0)),
                  pl.BlockSpec((bb, 1, D), lambda i, t, j: (i, 0, 1)),
                  pl.BlockSpec((TN_MAIN, D), lambda i, t, j: (j, 0)),
                  pl.BlockSpec((N_TAIL, D), lambda i, t, j: (0, 0))],
        out_specs=[pl.BlockSpec((bb, tt, TN_MAIN), lambda i, t, j: (i, t, j)),
                   pl.BlockSpec((bb, tt, N_TAIL), lambda i, t, j: (i, t, 0))],
        out_shape=[jax.ShapeDtypeStruct((b, tp, N_MAIN), main_dtype),
                   jax.ShapeDtypeStruct((b, tp, N_TAIL), F32)],
        scratch_shapes=[pltpu.VMEM((bb * tt, D), BF16)],
        compiler_params=_cp(("parallel", "parallel", "arbitrary")),
        name="inproj",
    )(x3, mod3, mod3, w_main, w_tail)


def _modulate_kernel(x_ref, sh_ref, sc_ref, o_ref):
    o_ref[...] = x_ref[...] * (1.0 + sc_ref[...]) + sh_ref[...]


def _modulate_rows(x2, mod2):
    n = x2.shape[0]
    return pl.pallas_call(
        _modulate_kernel,
        grid=(1,),
        in_specs=[pl.BlockSpec((n, D), lambda i: (0, 0)),
                  pl.BlockSpec((n, D), lambda i: (0, 0)),
                  pl.BlockSpec((n, D), lambda i: (0, 1))],
        out_specs=pl.BlockSpec((n, D), lambda i: (0, 0)),
        out_shape=jax.ShapeDtypeStruct((n, D), F32),
        name="modulate_last",
    )(x2, mod2, mod2)


def _conv4(pad_ref, n_rows, cw, cb):
    acc = cb + pad_ref[8:8 + n_rows, :] * cw[3:4, :]
    acc = acc + pad_ref[7:7 + n_rows, :] * cw[2:3, :]
    acc = acc + pad_ref[6:6 + n_rows, :] * cw[1:2, :]
    acc = acc + pad_ref[5:5 + n_rows, :] * cw[0:1, :]
    return acc


def _head_norm_rows(h, eps):
    mu = jnp.mean(h, axis=-1, keepdims=True)
    hc = h - mu
    var = jnp.mean(hc * hc, axis=-1, keepdims=True)
    return hc * lax.rsqrt(var + eps)


def _mlstm_seq_kernel(nb, qp_ref, kp_ref, v_ref, ga_ref, if_ref, cw_ref, cb_ref, gb_ref, nw_ref,
                      ya_ref, c_ref, n_ref, m_ref, haloq, halok):
    L = qp_ref.shape[1]
    assert qp_ref.dtype == BF16 and kp_ref.dtype == BF16

    @pl.when(pl.program_id(1) == 0)
    def _():
        c_ref[...] = jnp.zeros_like(c_ref)
        n_ref[...] = jnp.zeros_like(n_ref)
        m_ref[...] = jnp.zeros_like(m_ref)
        haloq[...] = jnp.zeros_like(haloq)
        halok[...] = jnp.zeros_like(halok)

    cw = cw_ref[...]
    cb = cb_ref[...]
    nw = nw_ref[...]
    row = lax.broadcasted_iota(jnp.int32, (L, L), 0)
    col = lax.broadcasted_iota(jnp.int32, (L, L), 1)
    causal = col <= row
    tril = jnp.where(causal, 1.0, 0.0).astype(F32)
    srow = lax.broadcasted_iota(jnp.int32, (CONV_W * L, L), 0)
    scol = lax.broadcasted_iota(jnp.int32, (CONV_W * L, L), 1)
    shift_mat = jnp.where(scol + srow // L == srow % L, 1.0, 0.0).astype(BF16)
    r8 = lax.broadcasted_iota(jnp.int32, (8, 1), 0)

    def conv(x_bf, halo_ref, cw_, cb_):
        sh = _dot(shift_mat, x_bf)
        acc = cb_ + sh[0:L] * cw_[3:4, :] + sh[L:2 * L] * cw_[2:3, :] + sh[2 * L:3 * L] * cw_[1:2, :] \
            + sh[3 * L:4 * L] * cw_[0:1, :]
        halo = halo_ref[...]
        fix = jnp.zeros((8, DK), F32)
        for j in range(1, CONV_W):
            fix = fix + jnp.where(r8 < j, pltpu.roll(halo, j, 0), 0.0) * cw_[CONV_W - 1 - j:CONV_W - j, :]
        halo_ref[...] = sh[L - 8:L, :]
        return jnp.concatenate([acc[0:8] + fix, acc[8:L]], axis=0)

    items = [(bi, h) for bi in range(nb) for h in range(H_A)]
    sl_of = lambda h: slice(h * DK, (h + 1) * DK)
    q_it, k_it = {}, {}
    for (bi, h) in items:
        sl, ksl = sl_of(h), slice(D + h * DK, D + (h + 1) * DK)
        q_it[bi, h] = _silu(conv(qp_ref[bi, :, sl], haloq.at[bi, :, sl], cw[:, sl], cb[:, sl]))
        k_it[bi, h] = _silu(conv(kp_ref[bi, :, sl], halok.at[bi, :, sl], cw[:, ksl], cb[:, ksl])) * (DK ** -0.5)

    gpre_l, bcum_l, gpre_t_l, bcum_t_l = [], [], [], []
    for bi in range(nb):
        gpre = if_ref[bi] + gb_ref[...]
        bcum = _dot(tril, _log_sigmoid(gpre), HIGHEST)
        gpre_l.append(gpre)
        bcum_l.append(bcum)
        gpre_t_l.append(gpre.T)
        bcum_t_l.append(bcum.T)

    st = {}
    for (bi, h) in items:
        ig_col = gpre_l[bi][:, h:h + 1]
        b_col = bcum_l[bi][:, H_A + h:H_A + h + 1]
        ig_row = gpre_t_l[bi][h:h + 1, :]
        b_row = bcum_t_l[bi][H_A + h:H_A + h + 1, :]
        m_prev = m_ref[bi][:, h:h + 1]
        g_col = b_col + m_prev
        src_log = jnp.where(causal, ig_row - b_row, NEG)
        m_t = jnp.maximum(g_col, b_col + jnp.max(src_log, axis=1, keepdims=True))
        b_last = b_col[L - 1:L, :]
        wlog = b_last - b_col + ig_col
        m_new = jnp.maximum(b_last + m_prev, jnp.max(wlog, axis=0, keepdims=True))
        st[bi, h] = dict(m_t=m_t, w_inter=jnp.exp(g_col - m_t), p=jnp.exp(src_log + (b_col - m_t)), m_new=m_new,
                         decay=jnp.exp(b_last + m_prev - m_new), wts=jnp.exp(wlog - m_new))
    qb = {it: q_it[it].astype(BF16) for it in items}
    kb = {it: k_it[it].astype(BF16) for it in items}
    s_l = {it: _dot_nt(qb[it], kb[it]) * st[it]['p'] for it in items}
    qc_l = {(bi, h): _dot_nt(qb[bi, h], c_ref[bi, h].astype(BF16)) for (bi, h) in items}
    sv_l = {(bi, h): _dot(s_l[bi, h].astype(BF16), v_ref[bi, :, sl_of(h)].astype(BF16)) for (bi, h) in items}
    wk_l = {it: st[it]['wts'] * k_it[it] for it in items}
    upd_l = {(bi, h): _dot_tn(v_ref[bi, :, sl_of(h)], wk_l[bi, h].astype(BF16)) for (bi, h) in items}
    for (bi, h) in items:
        sl = sl_of(h)
        d = st[bi, h]
        qh = q_it[bi, h]
        nh = n_ref[bi, h:h + 1, :]
        num = d['w_inter'] * qc_l[bi, h] + sv_l[bi, h]
        den = d['w_inter'] * jnp.sum(qh * nh, axis=1, keepdims=True) + jnp.sum(s_l[bi, h], axis=1, keepdims=True)
        hh = num / jnp.maximum(jnp.abs(den), jnp.exp(-d['m_t']))
        ga = _sigmoid(ga_ref[bi, :, sl].astype(F32))
        ya_ref[bi, :, sl] = (ga * _head_norm_rows(hh, MLSTM_EPS) * nw[:, sl]).astype(ya_ref.dtype)
        c_ref[bi, h] = d['decay'] * c_ref[bi, h] + upd_l[bi, h]
        n_ref[bi, h:h + 1, :] = d['decay'] * nh + jnp.sum(wk_l[bi, h], axis=0, keepdims=True)
        m_ref[bi, :, h:h + 1] = d['m_new']


def _mlstm_seq(main3, tail3, conv_w, conv_b, gbias, norm_w, L, nb):
    b, tp, _ = main3.shape
    blk = lambda j: pl.BlockSpec((nb, L, D), lambda i, c, j=j: (i, c, j))
    full = lambda shp: pl.BlockSpec(shp, lambda i, c: (0,) * len(shp))
    return pl.pallas_call(
        functools.partial(_mlstm_seq_kernel, nb),
        grid=(b // nb, tp // L),
        in_specs=[blk(0), blk(1), blk(2), blk(6),
                  pl.BlockSpec((nb, L, 128), lambda i, c: (i, c, TAIL_IF // 128)),
                  full((CONV_W, 2 * D)), full((1, 2 * D)), full((1, 128)), full((1, D))],
        out_specs=[pl.BlockSpec((nb, L, D), lambda i, c: (i, c, 0)),
                   pl.BlockSpec((nb, H_A, DK, DK), lambda i, c: (i, 0, 0, 0)),
                   pl.BlockSpec((nb, 8, DK), lambda i, c: (i, 0, 0)),
                   pl.BlockSpec((nb, 1, 128), lambda i, c: (i, 0, 0))],
        out_shape=[jax.ShapeDtypeStruct((b, tp, D), main3.dtype),
                   jax.ShapeDtypeStruct((b, H_A, DK, DK), F32),
                   jax.ShapeDtypeStruct((b, 8, DK), F32),
                   jax.ShapeDtypeStruct((b, 1, 128), F32)],
        scratch_shapes=[pltpu.VMEM((nb, 8, D), F32), pltpu.VMEM((nb, 8, D), F32)],
        compiler_params=_cp(("parallel", "arbitrary")),
        name="mlstm_seq",
    )(main3, main3, main3, main3, tail3, conv_w, conv_b, gbias, norm_w)


def _mlstm_step_kernel(tv, nb, qp_ref, kp_ref, v_ref, ga_ref, if_ref, conv0_ref, c0_ref, n0_ref, m0_ref,
                       cw_ref, cb_ref, gb_ref, nw_ref,
                       ya_ref, c_ref, n_ref, m_ref, padq, padk, gpad, lpad, kpad, vpad, wvpad):
    @pl.when(pl.program_id(0) == 0)
    def _():
        for r in (gpad, lpad, kpad, vpad, wvpad):
            r[...] = jnp.zeros_like(r)

    cw = cw_ref[...]
    cb = cb_ref[...]
    nw = nw_ref[...]
    gb = gb_ref[...]
    trow = lax.broadcasted_iota(jnp.int32, (8, 128), 0)
    scol = lax.broadcasted_iota(jnp.int32, (8, 128), 1)
    mask = (scol <= trow) & (scol < tv)
    rvalid = lax.broadcasted_iota(jnp.int32, (8, 1), 0) < tv
    r128 = lax.broadcasted_iota(jnp.int32, (128, 128), 0)
    c128 = lax.broadcasted_iota(jnp.int32, (128, 128), 1)
    tril = jnp.where(c128 <= r128, 1.0, 0.0).astype(F32)
    n_ref[...] = jnp.zeros_like(n_ref)
    m_ref[...] = jnp.zeros_like(m_ref)

    batches = range(nb)
    q_l, gpre_l, bcol_l, gt_l, bt_l, ga_l = [], [], [], [], [], []
    for bi in batches:
        padq[bi, 0:8, :] = conv0_ref[bi, :, 0:D]
        padk[bi, 0:8, :] = conv0_ref[bi, :, D:2 * D]
        padq[bi, 8:16, :] = qp_ref[bi]
        padk[bi, 8:16, :] = kp_ref[bi]
        q_l.append(_silu(_conv4(padq.at[bi], 8, cw[:, 0:D], cb[:, 0:D])))
        kpad[bi, 0:8, :] = _silu(_conv4(padk.at[bi], 8, cw[:, D:2 * D], cb[:, D:2 * D])) * (DK ** -0.5)
        vpad[bi, 0:8, :] = v_ref[bi]
        gpre = if_ref[bi] + gb
        gpad[bi, 0:8, :] = gpre
        lpad[bi, 0:8, :] = _log_sigmoid(gpre)
        gpre_l.append(gpre)
        ga_l.append(_sigmoid(ga_ref[bi]))
    for bi in batches:
        bpad = _dot(tril, lpad[bi], HIGHEST)
        bcol_l.append(bpad[0:8, :])
        bt_l.append(bpad.T)
        gt_l.append(gpad[bi].T)

    probs = [(bi, h) for bi in batches for h in range(H_A)]
    sl_of = lambda h: slice(h * DK, (h + 1) * DK)
    st = {}
    for (bi, h) in probs:
        ig_col = gpre_l[bi][:, h:h + 1]
        b_col = bcol_l[bi][:, H_A + h:H_A + h + 1]
        ig_row = gt_l[bi][h:h + 1, :]
        b_row = bt_l[bi][H_A + h:H_A + h + 1, :]
        m_prev = m0_ref[bi][:, h:h + 1]
        g_col = b_col + m_prev
        dlog = jnp.where(mask, b_col - b_row + ig_row, NEG)
        m_t = jnp.maximum(g_col, jnp.max(dlog, axis=1, keepdims=True))
        b_last = b_col[tv - 1:tv, :]
        wlog = jnp.where(rvalid, b_last - b_col + ig_col, NEG)
        m_new = jnp.maximum(b_last + m_prev, jnp.max(wlog, axis=0, keepdims=True))
        wts = jnp.exp(wlog - m_new)
        wvpad[bi, 0:8, sl_of(h)] = wts * vpad[bi, 0:8, sl_of(h)]
        st[bi, h] = dict(m_t=m_t, w_inter=jnp.exp(g_col - m_t), pm=jnp.exp(dlog - m_t), m_new=m_new,
                         decay=jnp.exp(b_last + m_prev - m_new), wts=wts)
    kb = {(bi, h): kpad[bi, :, sl_of(h)].astype(BF16) for (bi, h) in probs}
    qb = {(bi, h): q_l[bi][:, sl_of(h)].astype(BF16) for (bi, h) in probs}
    s_l = {k_: _dot_nt(qb[k_], kb[k_]) * st[k_]['pm'] for k_ in probs}
    qc_l = {(bi, h): _dot_nt(qb[bi, h], c0_ref[bi, h].astype(BF16)) for (bi, h) in probs}
    sv_l = {(bi, h): _dot(s_l[bi, h].astype(BF16), vpad[bi, :, sl_of(h)].astype(BF16)) for (bi, h) in probs}
    upd_l = {(bi, h): _dot(wvpad[bi, :, sl_of(h)].T.astype(BF16), kb[bi, h]) for (bi, h) in probs}
    for (bi, h) in probs:
        sl = sl_of(h)
        d = st[bi, h]
        nh = n0_ref[bi, h:h + 1, :]
        qh = q_l[bi][:, sl]
        num = d['w_inter'] * qc_l[bi, h] + sv_l[bi, h]
        den = d['w_inter'] * jnp.sum(qh * nh, axis=1, keepdims=True) + jnp.sum(s_l[bi, h], axis=1, keepdims=True)
        hh = num / jnp.maximum(jnp.abs(den), jnp.exp(-d['m_t']))
        ya_ref[bi, :, sl] = ga_l[bi][:, sl] * _head_norm_rows(hh, MLSTM_EPS) * nw[:, sl]
        c_ref[bi, h] = d['decay'] * c0_ref[bi, h] + upd_l[bi, h]
        n_ref[bi, h:h + 1, :] = d['decay'] * nh + jnp.sum(d['wts'] * kpad[bi, 0:8, sl], axis=0, keepdims=True)
        m_ref[bi, :, h:h + 1] = d['m_new']


def _mlstm_step(main3, tail3, conv0p, c0, n0p, m0p, conv_w, conv_b, gbias, norm_w, tv, nb):
    b = main3.shape[0]
    blk = lambda j: pl.BlockSpec((nb, 8, D), lambda i, j=j: (i, 0, j))
    full = lambda shp: pl.BlockSpec(shp, lambda i: (0,) * len(shp))
    state_specs = [pl.BlockSpec((nb, H_A, DK, DK), lambda i: (i, 0, 0, 0)),
                   pl.BlockSpec((nb, 8, DK), lambda i: (i, 0, 0)),
                   pl.BlockSpec((nb, 1, 128), lambda i: (i, 0, 0))]
    return pl.pallas_call(
        functools.partial(_mlstm_step_kernel, tv, nb),
        grid=(b // nb,),
        in_specs=[blk(0), blk(1), blk(2), blk(6),
                  pl.BlockSpec((nb, 8, 128), lambda i: (i, 0, TAIL_IF // 128)),
                  pl.BlockSpec((nb, 8, 2 * D), lambda i: (i, 0, 0))] + state_specs +
                 [full((CONV_W, 2 * D)), full((1, 2 * D)), full((1, 128)), full((1, D))],
        out_specs=[pl.BlockSpec((nb, 8, D), lambda i: (i, 0, 0))] + state_specs,
        out_shape=[jax.ShapeDtypeStruct((b, 8, D), F32),
                   jax.ShapeDtypeStruct((b, H_A, DK, DK), F32),
                   jax.ShapeDtypeStruct((b, 8, DK), F32),
                   jax.ShapeDtypeStruct((b, 1, 128), F32)],
        scratch_shapes=[pltpu.VMEM((nb, 16, D), F32), pltpu.VMEM((nb, 16, D), F32),
                        pltpu.VMEM((nb, 128, 128), F32), pltpu.VMEM((nb, 128, 128), F32),
                        pltpu.VMEM((nb, 128, D), F32), pltpu.VMEM((nb, 128, D), F32),
                        pltpu.VMEM((nb, 128, D), F32)],
        compiler_params=_cp(("arbitrary",)),
        name="mlstm_step",
    )(main3, main3, main3, main3, tail3, conv0p, c0, n0p, m0p, conv_w, conv_b, gbias, norm_w)


def _bd(x, lo):
    return jnp.concatenate([jnp.where(lo, x, 0.0), jnp.where(lo, 0.0, x)], axis=0)


def _pair_sum(x, lo):
    s_lo = jnp.sum(jnp.where(lo, x, 0.0), axis=1, keepdims=True)
    s_hi = jnp.sum(jnp.where(lo, 0.0, x), axis=1, keepdims=True)
    return jnp.where(lo, s_lo, s_hi)


PAIR_ROWS = 136


def _state_pairs_kernel(s_ref, o_ref):
    nb = s_ref.shape[-1]
    lane = lax.broadcasted_iota(jnp.int32, (nb, 128), 1)
    lo = lane < HB
    for b in range(nb):
        o_ref[0, b * PAIR_ROWS + 128:(b + 1) * PAIR_ROWS, :] = jnp.zeros((PAIR_ROWS - 128, 128), F32)
    for v in range(HB):
        t = jnp.concatenate([s_ref[0, v], s_ref[1, v]], axis=0).T
        o_ref[0, pl.ds(v, nb, stride=PAIR_ROWS), :] = jnp.where(lo, t, 0.0)
        o_ref[0, pl.ds(HB + v, nb, stride=PAIR_ROWS), :] = jnp.where(lo, 0.0, t)


def _state_pairs(s_hvkb):
    nb = s_hvkb.shape[-1]
    return pl.pallas_call(
        _state_pairs_kernel,
        grid=(N_PAIR,),
        in_specs=[pl.BlockSpec((2, HB, HB, nb), lambda p: (p, 0, 0, 0))],
        out_specs=pl.BlockSpec((1, nb * PAIR_ROWS, 128), lambda p: (p, 0, 0)),
        out_shape=jax.ShapeDtypeStruct((N_PAIR, nb * PAIR_ROWS, 128), F32),
        compiler_params=_cp(("arbitrary",)),
        name="state_pairs",
    )(s_hvkb)


def _rwkv_kernel(nsub, nbg, lb, tv, has_state, *refs):
    (r_ref, k_ref, v_ref, gb_ref, l_ref, ya_ref), refs = refs[:6], refs[6:]
    if has_state:
        (pr_ref, pk_ref, pv_ref, pl_ref, s0_ref), refs = refs[:5], refs[5:]
    (mur_ref, muk_ref, muv_ref, mul_ref, w0_ref, a0_ref, kk_ref, ka_ref, rk_ref,
     lw_ref, lb_ref, w2_ref, a2_ref, g2_ref,
     u_ref, s_ref, sbd, cr, ck, cv, cl) = refs
    L = nbg * lb
    nseq = nsub * nbg
    LT = nsub * L

    @pl.when(pl.program_id(1) == 0)
    def _():
        if has_state:
            for gi in range(nseq):
                for p in range(N_PAIR):
                    sbd[gi, p] = s0_ref[p, gi * PAIR_ROWS:gi * PAIR_ROWS + 128, :]
            cr[...] = pr_ref[...].astype(F32)
            ck[...] = pk_ref[...].astype(F32)
            cv[...] = pv_ref[...].astype(F32)
            cl[...] = pl_ref[...]
        else:
            sbd[...] = jnp.zeros_like(sbd)
            for c_ in (cr, ck, cv, cl):
                c_[...] = jnp.zeros_like(c_)

    def shift_mix(x_ref, carry, mu_ref):
        x3 = x_ref[...].astype(F32)
        width = x3.shape[-1]
        rolled = pltpu.roll(x3, 1, 1)
        first_row = lax.broadcasted_iota(jnp.int32, (1, 8, 1), 1) == 0
        head = jnp.where(first_row, carry[...], rolled[:, 0:8, :])
        prev = head if lb == 8 else jnp.concatenate([head, rolled[:, 8:, :]], axis=1)
        carry[...] = x3[:, lb - 1:lb, :]
        return (x3 + (prev - x3) * mu_ref[...]).reshape(LT, width)

    xr = shift_mix(r_ref, cr, mur_ref)
    xk = shift_mix(k_ref, ck, muk_ref)
    xv = shift_mix(v_ref, cv, muv_ref)
    xl = shift_mix(l_ref, cl, mul_ref)

    lane_l = lax.broadcasted_iota(jnp.int32, (LT, LORA), 1)
    act = jnp.where(lane_l < 64, jnp.tanh(xl), jnp.where(lane_l < 128, xl, _sigmoid(xl))).astype(BF16)
    z = w0_ref[...] + _dot(act, w2_ref[...])
    lw = -DECAY_SCALE * _sigmoid(z)
    a = _sigmoid(a0_ref[...] + _dot(act, a2_ref[...]))
    g = _dot(act, g2_ref[...])
    kk = xk * kk_ref[...]
    kmod = xk * (1.0 + (a - 1.0) * ka_ref[...])
    t_idx = lax.broadcasted_iota(jnp.int32, (LT, 1), 0)
    if tv < lb:
        valid = (t_idx % lb) < tv
        lw = jnp.where(valid, lw, 0.0)
        kk = jnp.where(valid, kk, 0.0)
        kmod = jnp.where(valid, kmod, 0.0)
        xv = jnp.where(valid, xv, 0.0)

    row = lax.broadcasted_iota(jnp.int32, (L, L), 0)
    col = lax.broadcasted_iota(jnp.int32, (L, L), 1)
    tril = jnp.where((col <= row) & (col // lb == row // lb), 1.0, 0.0).astype(F32)
    subs = range(nsub)
    rows = [slice(s * L, (s + 1) * L) for s in subs]
    cum_s = [_cumsum_rows(tril, lw[rows[s]]) for s in subs]

    lane = lax.broadcasted_iota(jnp.int32, (L, 128), 1)
    lo = lane < HB
    src = lane % HB
    trow = lax.broadcasted_iota(jnp.int32, (L, 128), 0)
    same = (src // lb) == (trow // lb)
    strict = same & (src < trow)
    incl = same & (src <= trow)
    r128 = lax.broadcasted_iota(jnp.int32, (128, 128), 0)
    c128 = lax.broadcasted_iota(jnp.int32, (128, 128), 1)
    blockdiag = (r128 < HB) == (c128 < HB)
    eye_pair = jnp.where(src == trow, 1.0, 0.0).astype(F32)

    sls = [slice(p * 128, (p + 1) * 128) for p in range(N_PAIR)]
    items = [(s, p) for s in subs for p in range(N_PAIR)]
    idx = range(len(items))
    groups = range(nbg)

    at_l, rt_l, bt_l, kt_l, win_l, vp_l = [], [], [], [], [], []
    for (s, p) in items:
        rs, sl = rows[s], sls[p]
        kkp = kk[rs, sl]
        kap = kkp * lax.rsqrt(jnp.maximum(_pair_sum(kkp * kkp, lo), 1e-24))
        cum_p = cum_s[s][:, sl]
        w_in = jnp.exp(cum_p)
        w_inv = jnp.exp(-cum_p)
        at_l.append(-kap * jnp.exp(cum_p - lw[rs, sl]))
        rt_l.append(xr[rs, sl] * w_in)
        bt_l.append(kap * a[rs, sl] * w_inv)
        kt_l.append(kmod[rs, sl] * w_inv)
        win_l.append(w_in)
        vp_l.append(xv[rs, sl])
    bdv_l = [_bd(vp_l[i], lo).astype(BF16) for i in idx]

    gm_l = [_dot_nt(jnp.concatenate([at_l[i], rt_l[i]], axis=0).astype(BF16),
                    jnp.concatenate([_bd(bt_l[i], lo), _bd(kt_l[i], lo)], axis=0).astype(BF16))
            for i in idx]
    n_l = [jnp.where(strict, gm_l[i][0:L, 0:128], 0.0) for i in idx]
    aak_l = [jnp.where(strict, gm_l[i][0:L, 128:256], 0.0).astype(BF16) for i in idx]
    ark_l = [jnp.concatenate([jnp.where(incl, gm_l[i][L:2 * L, 0:128], 0.0),
                              jnp.where(incl, gm_l[i][L:2 * L, 128:256], 0.0)], axis=1).astype(BF16)
             for i in idx]

    xs_l = [[_dot_nt(jnp.concatenate([at_l[i][gi * lb:(gi + 1) * lb], rt_l[i][gi * lb:(gi + 1) * lb]],
                                     axis=0).astype(BF16), sbd[items[i][0] * nbg + gi, items[i][1]].astype(BF16))
             for gi in groups] for i in idx]
    if nbg == 1:
        as_l = [xs_l[i][0][0:lb] for i in idx]
        rs_l = [xs_l[i][0][lb:2 * lb] for i in idx]
    else:
        as_l = [jnp.concatenate([xs_l[i][gi][0:lb] for gi in groups], axis=0) for i in idx]
        rs_l = [jnp.concatenate([xs_l[i][gi][lb:2 * lb] for gi in groups], axis=0) for i in idx]

    y0_l = [as_l[i] + _dot(aak_l[i], bdv_l[i]) for i in idx]

    dm_l = [eye_pair for _ in idx]
    s_blk = 1
    while 2 * s_blk <= lb:
        lvl = ((trow // (2 * s_blk)) == (src // (2 * s_blk))) & ((trow % (2 * s_blk)) >= s_blk) \
            & ((src % (2 * s_blk)) < s_blk)
        if s_blk == 1:
            dm_l = [dm_l[i] + jnp.where(lvl, n_l[i], 0.0) for i in idx]
        else:
            t1_l = [_dot(jnp.where(lvl, n_l[i], 0.0).astype(BF16), _bd(dm_l[i], lo).astype(BF16))
                    for i in idx]
            dm_l = [dm_l[i] + _dot(dm_l[i].astype(BF16), _bd(t1_l[i], lo).astype(BF16)) for i in idx]
        s_blk *= 2
    u_l = [_dot(dm_l[i].astype(BF16), _bd(y0_l[i], lo).astype(BF16)) for i in idx]

    gate_b = _sigmoid(gb_ref[...].astype(F32).reshape(LT, D))
    y_a = ya_ref[...].astype(F32).reshape(LT, D)
    bonus_l = [_pair_sum(xr[rows[s], sls[p]] * kmod[rows[s], sls[p]] * rk_ref[:, sls[p]], lo) * vp_l[i]
               for i, (s, p) in enumerate(items)]

    o_l = [rs_l[i] + _dot(ark_l[i], jnp.concatenate([_bd(u_l[i], lo).astype(BF16), bdv_l[i]], axis=0))
           for i in idx]

    w3_l = [win_l[i].reshape(nbg, lb, 128)[:, lb - 1:lb, :] for i in idx]
    rhs_l = []
    for i in idx:
        w_last = jnp.broadcast_to(w3_l[i], (nbg, lb, 128)).reshape(L, 128)
        rhs_l.append(jnp.concatenate([bt_l[i] * w_last, kt_l[i] * w_last], axis=0).astype(BF16))
    uv_l = [jnp.concatenate([u_l[i], vp_l[i]], axis=0) for i in idx]
    if nbg == 1:
        upd_l = [_dot_tn(uv_l[i].astype(BF16), rhs_l[i]) for i in idx]
        for i, (s, p) in enumerate(items):
            sbd[s, p] = sbd[s, p] * w3_l[i][0] + jnp.where(blockdiag, upd_l[i], 0.0)
    else:
        cgrp = (c128 % L) // lb
        uvt_l = [uv_l[i].T for i in idx]
        for i, (s, p) in enumerate(items):
            for gi in groups:
                upd = _dot(jnp.where(cgrp == gi, uvt_l[i], 0.0).astype(BF16), rhs_l[i])
                q_ = s * nbg + gi
                sbd[q_, p] = sbd[q_, p] * w3_l[i][gi] + jnp.where(blockdiag, upd, 0.0)

    out_l = []
    for i, (s, p) in enumerate(items):
        rs, sl = rows[s], sls[p]
        o = o_l[i]
        mu = _pair_sum(o, lo) * (1.0 / HB)
        oc = o - mu
        var = _pair_sum(oc * oc, lo) * (1.0 / HB)
        on = oc * lax.rsqrt(var + RWKV_EPS) * lw_ref[:, sl] + lb_ref[:, sl]
        yb = (on + bonus_l[i]) * g[rs, sl]
        out_l.append(y_a[rs, sl] + gate_b[rs, sl] * yb)
    u_rows = [jnp.concatenate(out_l[s * N_PAIR:(s + 1) * N_PAIR], axis=1) for s in subs]
    u_all = u_rows[0] if nsub == 1 else jnp.concatenate(u_rows, axis=0)
    u_ref[...] = u_all.reshape(nseq, lb, D).astype(u_ref.dtype)

    @pl.when(pl.program_id(1) == pl.num_programs(1) - 1)
    def _():
        for gi in range(nseq):
            for p in range(N_PAIR):
                s_ref[gi, 2 * p] = sbd[gi, p, 0:HB, 0:HB]
                s_ref[gi, 2 * p + 1] = sbd[gi, p, HB:2 * HB, HB:2 * HB]


def _rwkv(main3, cols, tail3, ya3, prev, s0, prm, nsub, nbg, lb, tv):
    b, tp, _ = main3.shape
    has_state = s0 is not None
    nq = nsub * nbg
    blk = lambda j: pl.BlockSpec((nq, lb, D), lambda i, c, j=j: (i, c, j))
    pblk = lambda j: pl.BlockSpec((nq, 1, D), lambda i, c, j=j: (i, 0, j))
    full = lambda a: pl.BlockSpec(a.shape, lambda i, c: (0,) * a.ndim)
    sblk = pl.BlockSpec((nq, H_B, HB, HB), lambda i, c: (i, 0, 0, 0))
    c_r, c_k, c_v, c_gb = cols
    in_specs = [blk(c_r), blk(c_k), blk(c_v), blk(c_gb),
                pl.BlockSpec((nq, lb, LORA), lambda i, c: (i, c, 0)),
                pl.BlockSpec((nq, lb, D), lambda i, c: (i, c, 0))]
    args = [main3, main3, main3, main3, tail3, ya3]
    if has_state:
        in_specs += [pblk(c_r), pblk(c_k), pblk(c_v), pl.BlockSpec((nq, 1, LORA), lambda i, c: (i, 0, 0)),
                     pl.BlockSpec((N_PAIR, nq * PAIR_ROWS, 128), lambda i, c: (0, i, 0))]
        args += [prev[0], prev[0], prev[0], prev[1], s0]
    in_specs += [full(a) for a in prm]
    args += list(prm)
    return pl.pallas_call(
        functools.partial(_rwkv_kernel, nsub, nbg, lb, tv, has_state),
        grid=(b // nq, tp // lb),
        in_specs=in_specs,
        out_specs=[pl.BlockSpec((nq, lb, D), lambda i, c: (i, c, 0)), sblk],
        out_shape=[jax.ShapeDtypeStruct((b, tp, D), main3.dtype),
                   jax.ShapeDtypeStruct((b, H_B, HB, HB), F32)],
        scratch_shapes=[pltpu.VMEM((nq, N_PAIR, 128, 128), F32),
                        pltpu.VMEM((nq, 1, D), F32), pltpu.VMEM((nq, 1, D), F32),
                        pltpu.VMEM((nq, 1, D), F32), pltpu.VMEM((nq, 1, LORA), F32)],
        compiler_params=_cp(("parallel", "arbitrary")),
        name="rwkv",
    )(*args)


def _tail_kernel(u_ref, x_ref, g1_ref, sh_ref, sc_ref, g2_ref, wo_ref, wu_ref, wd_ref,
                 l1g_ref, l1b_ref, l2g_ref, l2b_ref, o_ref, x1_scr, h_scr, acc):
    bb, tt, _ = x_ref.shape
    j = pl.program_id(2)

    @pl.when(j == 0)
    def _():
        u = u_ref[...].reshape(bb * tt, D).astype(BF16)
        y = _dot(u, wo_ref[...]).reshape(bb, tt, D)
        x1 = _layer_norm(ALPHA * x_ref[...] + g1_ref[...] * y, l1g_ref[...], l1b_ref[...])
        x1_scr[...] = x1
        h_scr[...] = (x1 * (1.0 + sc_ref[...]) + sh_ref[...]).reshape(bb * tt, D).astype(BF16)
        acc[...] = jnp.zeros_like(acc)

    up = jnp.maximum(_dot(h_scr[...], wu_ref[...]), 0.0)
    acc[...] += _dot((up * up).astype(BF16), wd_ref[...])

    @pl.when(j == pl.num_programs(2) - 1)
    def _():
        z = ALPHA * x1_scr[...] + g2_ref[...] * acc[...].reshape(bb, tt, D)
        o_ref[...] = _layer_norm(z, l2g_ref[...], l2b_ref[...])


def _tail(u3, x3, mod3, q, bb, tt):
    b, tp, _ = x3.shape
    blk = pl.BlockSpec((bb, tt, D), lambda i, t, j: (i, t, 0))
    mblk = lambda col: pl.BlockSpec((bb, 1, D), lambda i, t, j, col=col: (i, 0, col))
    full = lambda shp: pl.BlockSpec(shp, lambda i, t, j: (0,) * len(shp))
    return pl.pallas_call(
        _tail_kernel,
        grid=(b // bb, tp // tt, D_FF // FF_CHUNK),
        in_specs=[blk, blk, mblk(2), mblk(3), mblk(4), mblk(5),
                  full((D, D)),
                  pl.BlockSpec((D, FF_CHUNK), lambda i, t, j: (0, j)),
                  pl.BlockSpec((FF_CHUNK, D), lambda i, t, j: (j, 0)),
                  full((1, D)), full((1, D)), full((1, D)), full((1, D))],
        out_specs=blk,
        out_shape=jax.ShapeDtypeStruct((b, tp, D), F32),
        scratch_shapes=[pltpu.VMEM((bb, tt, D), F32), pltpu.VMEM((bb * tt, D), BF16),
                        pltpu.VMEM((bb * tt, D), F32)],
        compiler_params=_cp(("parallel", "parallel", "arbitrary")),
        name="outproj_ffn",
    )(u3, x3, mod3, mod3, mod3, mod3, q['w_out'], q['w_up'], q['w_down'],
      q['ln1_g'], q['ln1_b'], q['ln2_g'], q['ln2_b'])


def _relayout_params(p):
    wt = p['w_in'].T
    w_main = jnp.concatenate(
        [wt[:3 * D], wt[3 * D + 8:6 * D + 8], wt[6 * D + 8 + LORA:8 * D + 8 + LORA]], axis=0).astype(BF16)
    w_tail = jnp.concatenate(
        [wt[6 * D + 8:6 * D + 8 + LORA], wt[3 * D:3 * D + 8],
         jnp.zeros((N_TAIL - LORA - 8, D), F32)], axis=0).astype(BF16)
    mu = p['rwkv_mu']
    z64 = jnp.zeros((64, D), F32)
    z128 = jnp.zeros((128, D), F32)
    row = lambda a: a.reshape(1, -1)
    rw = (row(mu[0:D]), row(mu[D:2 * D]), row(mu[2 * D:3 * D]), row(mu[3 * D:3 * D + LORA]),
          row(p['rwkv_w0']), row(p['rwkv_a0']), row(p['rwkv_k_k']), row(p['rwkv_k_a']),
          row(p['rwkv_r_k']), row(p['rwkv_lnx_w']), row(p['rwkv_lnx_b']),
          jnp.concatenate([p['rwkv_w2'], z64, z128], axis=0).astype(BF16),
          jnp.concatenate([z64, p['rwkv_a2'], z128], axis=0).astype(BF16),
          jnp.concatenate([z128, p['rwkv_g2']], axis=0).astype(BF16))
    gbias = jnp.concatenate([p['mlstm_i_bias'], p['mlstm_f_bias'], jnp.zeros((120,), F32)]).reshape(1, 128)
    return dict(w_main=w_main, w_tail=w_tail, rw=rw, gbias=gbias,
                conv_w=p['conv_w'], conv_b=row(p['conv_b']), norm_w=row(p['mlstm_norm_w']),
                w_out=p['w_out'].astype(BF16), w_up=p['w_up'].astype(BF16), w_down=p['w_down'].astype(BF16),
                ln1_g=row(p['ln1_g']), ln1_b=row(p['ln1_b']), ln2_g=row(p['ln2_g']), ln2_b=row(p['ln2_b']))


def _prompt_layer(x, mod, q, seq_tile, mlstm_chunk):
    b, t, _ = x.shape
    mod3 = mod.reshape(b, 1, N_COND)
    main3, tail3 = _inproj(x, mod3, q['w_main'], q['w_tail'], 1, seq_tile, BF16)
    ya3, c1, n1, m1 = _mlstm_seq(main3, tail3, q['conv_w'], q['conv_b'], q['gbias'], q['norm_w'],
                                 mlstm_chunk, min(2, b))
    u3, s1 = _rwkv(main3, RWKV_SECTIONS, tail3, ya3, None, None, q['rw'], min(2, b), 1, RW_L, RW_L)
    y = _tail(u3, x, mod3, q, 1, seq_tile)
    shift = _modulate_rows(x[:, t - 1, :], mod)
    conv = main3[:, t - (CONV_W - 1):, :2 * D].astype(F32)
    return y, (c1, n1[:, :H_A, :], m1[:, 0, :H_A], conv, s1, shift)


def _sample_layer(x, mod, st, q, bb):
    c0, n0, m0, conv0, s0, shift0 = st
    b, t, _ = x.shape
    mod3 = mod.reshape(b, 1, N_COND)
    xp = jnp.pad(x, ((0, 0), (0, 8 - t), (0, 0)))
    main3, tail3 = _inproj(xp, mod3, q['w_main'], q['w_tail'], bb, 8, F32)
    pm, pt = _inproj(shift0.reshape(1, b, D), jnp.zeros((1, 1, N_COND), F32), q['w_main'], q['w_tail'], 1, b, F32)
    prev = (pm.reshape(b, 1, N_MAIN), pt.reshape(b, 1, N_TAIL))
    conv0p = jnp.pad(conv0, ((0, 0), (8 - (CONV_W - 1), 0), (0, 0)))
    n0p = jnp.pad(n0, ((0, 0), (0, 8 - H_A), (0, 0)))
    m0p = jnp.pad(m0, ((0, 0), (0, 128 - H_A))).reshape(b, 1, 128)
    ya3, c1, n1, m1 = _mlstm_step(main3, tail3, conv0p, c0, n0p, m0p, q['conv_w'], q['conv_b'], q['gbias'],
                                  q['norm_w'], t, min(4, b))
    s0_pairs = _state_pairs(jnp.transpose(s0, (1, 2, 3, 0)))
    u3, s1 = _rwkv(main3, RWKV_SECTIONS, tail3, ya3, prev, s0_pairs, q['rw'], 1, RW_L // 8, 8, t)
    y = _tail(u3, xp, mod3, q, bb, 8)
    shift = _modulate_rows(x[:, t - 1, :], mod)
    conv = jnp.concatenate([conv0, main3[:, :t, :2 * D]], axis=1)[:, t:, :]
    return y[:, :t, :], (c1, n1[:, :H_A, :], m1[:, 0, :H_A], conv, s1, shift)


def kernel(x_prompt, x_sample, c_prompt, c_sample, state_mlstm_C, state_mlstm_n, state_mlstm_m, state_mlstm_conv, state_rwkv_S, state_rwkv_shift, w_cond, b_cond, w_in, mlstm_i_bias, mlstm_f_bias, conv_w, conv_b, mlstm_norm_w, rwkv_mu, rwkv_w0, rwkv_w2, rwkv_a0, rwkv_a2, rwkv_g2, rwkv_k_k, rwkv_k_a, rwkv_r_k, rwkv_lnx_w, rwkv_lnx_b, w_out, ln1_g, ln1_b, w_up, w_down, ln2_g, ln2_b):
    depth = w_in.shape[0]
    bp = x_prompt.shape[0]
    yp, ys = x_prompt, x_sample
    new_p = [[] for _ in range(6)]
    new_s = [[] for _ in range(6)]
    for l in range(depth):
        p = {'w_in': w_in[l], 'mlstm_i_bias': mlstm_i_bias[l], 'mlstm_f_bias': mlstm_f_bias[l],
             'conv_w': conv_w[l], 'conv_b': conv_b[l], 'mlstm_norm_w': mlstm_norm_w[l],
             'rwkv_mu': rwkv_mu[l], 'rwkv_w0': rwkv_w0[l], 'rwkv_w2': rwkv_w2[l], 'rwkv_a0': rwkv_a0[l],
             'rwkv_a2': rwkv_a2[l], 'rwkv_g2': rwkv_g2[l], 'rwkv_k_k': rwkv_k_k[l], 'rwkv_k_a': rwkv_k_a[l],
             'rwkv_r_k': rwkv_r_k[l].reshape(-1), 'rwkv_lnx_w': rwkv_lnx_w[l], 'rwkv_lnx_b': rwkv_lnx_b[l],
             'w_out': w_out[l], 'ln1_g': ln1_g[l], 'ln1_b': ln1_b[l], 'w_up': w_up[l], 'w_down': w_down[l],
             'ln2_g': ln2_g[l], 'ln2_b': ln2_b[l]}
        q = _relayout_params(p)
        mod = _cond(jnp.concatenate([c_prompt, c_sample], axis=0), w_cond[l], b_cond[l])
        st_in = (state_mlstm_C[l], state_mlstm_n[l], state_mlstm_m[l], state_mlstm_conv[l],
                 state_rwkv_S[l], state_rwkv_shift[l])
        ys, st_s = _sample_layer(ys, mod[bp:], st_in, q, min(128, ys.shape[0]))
        yp, st_p = _prompt_layer(yp, mod[:bp], q, min(1024, yp.shape[1]), min(256, yp.shape[1]))
        for lst, t in zip(new_p, st_p):
            lst.append(t)
        for lst, t in zip(new_s, st_s):
            lst.append(t)
    outs_p = [jnp.stack(t) for t in new_p]
    outs_s = [jnp.stack(t) for t in new_s]
    return (yp, ys, *outs_p, *outs_s)
```
